```python
import jax, jax.numpy as jnp
from jax import lax
import numpy as np

D_MODEL = 1024
BATCH = 4
SEQ = 8192
DEPTH = 1

HEAD_DIM = 64
MIX_WIDTH = D_MODEL
FOX_HEADS = (MIX_WIDTH // 2) // HEAD_DIM
SWA_Q_HEADS = (MIX_WIDTH // 2) // HEAD_DIM
SWA_KV_HEADS = 2
SWA_GROUP = SWA_Q_HEADS // SWA_KV_HEADS
WINDOW = 128
Q_BLOCK = 128
ROPE_THETA = 10000.0
RMS_EPS = 1e-6
FOX_W = FOX_HEADS * HEAD_DIM
SWA_W = SWA_Q_HEADS * HEAD_DIM
SWA_KV_W = SWA_KV_HEADS * HEAD_DIM
IN_SIZES = (FOX_W, FOX_W, FOX_W, FOX_HEADS, FOX_W, SWA_W, SWA_KV_W, SWA_KV_W, SWA_W)
IN_WIDTH = FOX_W * 4 + FOX_HEADS + SWA_W * 2 + SWA_KV_W * 2

kernel_name = "hybrid_fox_swa_sink_parallel_heads"


def rms_norm(x, g):
    xf = x.astype(jnp.float32)
    y = xf * lax.rsqrt(jnp.mean(xf * xf, axis=-1, keepdims=True) + RMS_EPS)
    return (y * g.astype(jnp.float32)).astype(x.dtype)


def rope(x, positions):
    half = HEAD_DIM // 2
    inv_freq = ROPE_THETA ** (-jnp.arange(half, dtype=jnp.float32) / half)
    ang = positions.astype(jnp.float32)[..., None] * inv_freq
    cos = jnp.cos(ang)[:, :, None, :]
    sin = jnp.sin(ang)[:, :, None, :]
    xf = x.astype(jnp.float32)
    x1, x2 = xf[..., :half], xf[..., half:]
    return jnp.concatenate([x1 * cos - x2 * sin, x2 * cos + x1 * sin], axis=-1).astype(x.dtype)


def forgetting_attention(q, k, v, f_logit):
    B, S, H, d = q.shape
    nb = S // Q_BLOCK
    cum = jnp.cumsum(jax.nn.log_sigmoid(f_logit.astype(jnp.float32)), axis=1)
    cum = cum.transpose(0, 2, 1)
    kh = k.transpose(0, 2, 1, 3)
    vh = v.transpose(0, 2, 1, 3)
    q_blocks = q.transpose(0, 2, 1, 3).reshape(B, H, nb, Q_BLOCK, d).transpose(2, 0, 1, 3, 4)
    c_blocks = cum.reshape(B, H, nb, Q_BLOCK).transpose(2, 0, 1, 3)
    key_pos = jnp.arange(S)
    scale = d ** -0.5

    def one_block(args):
        qb, cb, n = args
        s = jnp.einsum('bhqd,bhkd->bhqk', qb, kh, preferred_element_type=jnp.float32) * scale
        s = s + cb[..., None] - cum[:, :, None, :]
        q_pos = n * Q_BLOCK + jnp.arange(Q_BLOCK)
        causal = key_pos[None, :] <= q_pos[:, None]
        s = jnp.where(causal, s, -jnp.inf)
        p = jax.nn.softmax(s, axis=-1)
        return jnp.einsum('bhqk,bhkd->bhqd', p.astype(vh.dtype), vh)

    out = lax.map(one_block, (q_blocks, c_blocks, jnp.arange(nb)))
    return out.transpose(1, 0, 3, 2, 4).reshape(B, S, H * d)


def sliding_window_sink_attention(q, k, v, sinks):
    B, S, _, d = q.shape
    nb = S // Q_BLOCK
    qb = q.reshape(B, nb, Q_BLOCK, SWA_KV_HEADS, SWA_GROUP, d)

    def with_prev(t):
        t = t.reshape(B, nb, Q_BLOCK, SWA_KV_HEADS, d)
        prev = jnp.pad(t[:, :-1], ((0, 0), (1, 0), (0, 0), (0, 0), (0, 0)))
        return jnp.concatenate([prev, t], axis=2)

    kk, vv = with_prev(k), with_prev(v)
    s = jnp.einsum('bnqhgd,bnshd->bnhgqs', qb, kk, preferred_element_type=jnp.float32) * (d ** -0.5)
    i = jnp.arange(Q_BLOCK)[:, None]
    j = jnp.arange(2 * Q_BLOCK)[None, :]
    rel = i + Q_BLOCK - j
    band = (rel >= 0) & (rel < WINDOW)
    valid_blk = (jnp.arange(nb)[:, None, None] > 0) | (j[None] >= Q_BLOCK)
    mask = band[None] & valid_blk
    s = jnp.where(mask[None, :, None, None], s, -jnp.inf)
    sink = jnp.broadcast_to(
        sinks.astype(jnp.float32).reshape(SWA_KV_HEADS, SWA_GROUP)[None, None, :, :, None, None],
        s.shape[:-1] + (1,))
    p = jax.nn.softmax(jnp.concatenate([s, sink], axis=-1), axis=-1)[..., :-1]
    out = jnp.einsum('bnhgqs,bnshd->bnqhgd', p.astype(vv.dtype), vv)
    return out.reshape(B, S, SWA_Q_HEADS * d)


def setup_inputs(seed: int = 0) -> dict:
    key = jax.random.key(seed)
    ks = jax.random.split(key, 12)
    x = jax.random.normal(ks[0], (BATCH, SEQ, D_MODEL), jnp.float32)
    c = jax.random.normal(ks[1], (BATCH, D_MODEL), jnp.float32)
    positions = jnp.broadcast_to(jnp.arange(SEQ, dtype=jnp.int32)[None, :], (BATCH, SEQ))
    w_ada = jax.random.normal(ks[2], (DEPTH, D_MODEL, 3 * D_MODEL), jnp.float32) * (0.5 * D_MODEL ** -0.5)
    b_ada = jax.random.normal(ks[3], (DEPTH, 3 * D_MODEL), jnp.float32) * 0.02
    g_pre = 1.0 + 0.1 * jax.random.normal(ks[4], (DEPTH, D_MODEL), jnp.float32)
    w_in = jax.random.normal(ks[5], (DEPTH, D_MODEL, IN_WIDTH), jnp.float32) * D_MODEL ** -0.5
    b_fgate = jax.random.uniform(ks[6], (DEPTH, FOX_HEADS), jnp.float32, 1.0, 4.0)
    sinks = jax.random.normal(ks[7], (DEPTH, SWA_Q_HEADS), jnp.float32)
    w_out = jax.random.normal(ks[8], (DEPTH, MIX_WIDTH, D_MODEL), jnp.float32) * MIX_WIDTH ** -0.5
    g_post = 1.0 + 0.1 * jax.random.normal(ks[9], (DEPTH, D_MODEL), jnp.float32)
    return {"x": x, "c": c, "positions": positions, "w_ada": w_ada, "b_ada": b_ada,
            "g_pre": g_pre, "w_in": w_in, "b_fgate": b_fgate, "sinks": sinks,
            "w_out": w_out, "g_post": g_post}


def reference(x, c, positions, w_ada, b_ada, g_pre, w_in, b_fgate, sinks, w_out, g_post):
    B, S, _ = x.shape
    split_points = [int(v) for v in np.cumsum(IN_SIZES)[:-1]]
    for l in range(DEPTH):
        mod = jax.nn.silu(c) @ w_ada[l] + b_ada[l]
        shift, scale, gate = jnp.split(mod, 3, axis=-1)
        h = rms_norm(x, g_pre[l]) * (1.0 + scale[:, None, :]) + shift[:, None, :]
        proj = h @ w_in[l]
        qa, ka, va, fa, za, qb, kb, vb, zb = jnp.split(proj, split_points, axis=-1)
        oa = forgetting_attention(
            qa.reshape(B, S, FOX_HEADS, HEAD_DIM),
            ka.reshape(B, S, FOX_HEADS, HEAD_DIM),
            va.reshape(B, S, FOX_HEADS, HEAD_DIM),
            fa + b_fgate[l])
        oa = oa * jax.nn.silu(za)
        qb = rope(qb.reshape(B, S, SWA_Q_HEADS, HEAD_DIM), positions)
        kb = rope(kb.reshape(B, S, SWA_KV_HEADS, HEAD_DIM), positions)
        ob = sliding_window_sink_attention(qb, kb, vb.reshape(B, S, SWA_KV_HEADS, HEAD_DIM), sinks[l])
        ob = ob * jax.nn.silu(zb)
        y = jnp.concatenate([oa, ob], axis=-1) @ w_out[l]
        x = x + gate[:, None, :] * rms_norm(y, g_post[l])
    return x
```

```python
import functools

import jax
import jax.numpy as jnp
import numpy as np
from jax import lax
from jax.experimental import pallas as pl
from jax.experimental.pallas import tpu as pltpu

D_MODEL = 1024
HEAD_DIM = 64
FOX_HEADS = 8
SWA_Q_HEADS = 8
SWA_KV_HEADS = 2
WINDOW = 128
ROPE_THETA = 10000.0
RMS_EPS = 1e-6
FOX_W = FOX_HEADS * HEAD_DIM
SWA_W = SWA_Q_HEADS * HEAD_DIM
SWA_KV_W = SWA_KV_HEADS * HEAD_DIM

LANES = 128
AUG_W = FOX_HEADS * LANES

OFF_QA = 0
OFF_KA = OFF_QA + FOX_W
OFF_VA = OFF_KA + FOX_W
OFF_ZA = OFF_VA + FOX_W
OFF_QB = OFF_ZA + FOX_W
OFF_ZB = OFF_QB + SWA_W
OFF_KD = OFF_ZB + SWA_W
OFF_VD = OFF_KD + 2 * SWA_KV_W
OFF_F = OFF_VD + 2 * SWA_KV_W
IN_NP = OFF_F + LANES

TM_IN = 512
TQ_FOX = 256
TK_FOX = 256
TQ_SWA = 512
TM_OUT = 512
NEG_BIG = -1e30
VMEM_LIMIT = 56 * 1024 * 1024

F32 = jnp.float32
BF16 = jnp.bfloat16


def _split3(a):
    hi = a.astype(BF16)
    r = a - hi.astype(F32)
    mid = r.astype(BF16)
    lo = (r - mid.astype(F32)).astype(BF16)
    return hi, mid, lo


def _mod_kernel(c_ref, w_ref, b_ref, o_ref):
    c = c_ref[...]
    sc = c * (1.0 / (1.0 + jnp.exp(-c)))
    o_ref[...] = jnp.dot(sc, w_ref[...], precision=lax.Precision.HIGHEST,
                         preferred_element_type=F32) + b_ref[...]


def _mod_call(c_pad, w_ada, b_ada):
    rows = c_pad.shape[0]
    n = w_ada.shape[1]
    bn = D_MODEL
    return pl.pallas_call(
        _mod_kernel,
        grid=(n // bn,),
        in_specs=[pl.BlockSpec((rows, D_MODEL), lambda j: (0, 0)),
                  pl.BlockSpec((D_MODEL, bn), lambda j: (0, j)),
                  pl.BlockSpec((1, bn), lambda j: (0, j))],
        out_specs=pl.BlockSpec((rows, bn), lambda j: (0, j)),
        out_shape=jax.ShapeDtypeStruct((rows, n), F32),
        name="mod",
    )(c_pad, w_ada, b_ada)


def _inproj_kernel(x_ref, shift_ref, scale_ref, gpre_ref, w_ref, pos_ref, invf_ref, bf_ref,
                   tri_ref, eq_ref, ek_ref, oq_ref, ok_ref,
                   qaug_ref, kaug_ref, vaug_ref, ga_ref, qb_ref, kd_ref, vd_ref, gb_ref,
                   carry_ref):
    tm = x_ref.shape[1]
    t = pl.program_id(1)

    x = x_ref[0]
    ms = jnp.mean(x * x, axis=-1, keepdims=True)
    y = x * lax.rsqrt(ms + RMS_EPS) * gpre_ref[...]
    h = (y * (1.0 + scale_ref[0]) + shift_ref[0]).astype(BF16)

    def proj(off, width):
        return jnp.dot(h, w_ref[:, off:off + width], preferred_element_type=F32)

    f = proj(OFF_F, LANES) + bf_ref[...]
    ls = jnp.minimum(f, 0.0) - jnp.log1p(jnp.exp(-jnp.abs(f)))
    tri = tri_ref[...]
    cum = None
    for part in _split3(ls):
        d = jnp.dot(tri, part, preferred_element_type=F32)
        cum = d if cum is None else cum + d

    @pl.when(t == 0)
    def _():
        carry_ref[...] = jnp.zeros_like(carry_ref)

    cum = cum + carry_ref[...]
    carry_ref[...] = cum[tm - 1:tm, :]

    hi, mid, lo = _split3(cum)
    lane = lax.broadcasted_iota(jnp.int32, (tm, LANES), 1)
    cs = jnp.where(lane < FOX_HEADS, hi.astype(F32),
                   jnp.where(lane < 2 * FOX_HEADS, mid.astype(F32), lo.astype(F32))).astype(BF16)
    augq = jnp.dot(cs, eq_ref[...], preferred_element_type=F32) + oq_ref[...]
    augk = jnp.dot(cs, ek_ref[...], preferred_element_type=F32) + ok_ref[...]

    lane_w = lax.broadcasted_iota(jnp.int32, (tm, AUG_W), 1)
    data = (((lane_w >> 6) ^ (lane_w >> 7)) & 1) == 0

    def rep(a):
        return jnp.concatenate(
            [a[:, LANES * (hd // 2):LANES * (hd // 2 + 1)] for hd in range(FOX_HEADS)], axis=1)

    qaug_ref[0] = jnp.where(data, rep(proj(OFF_QA, FOX_W)), augq).astype(BF16)
    kaug_ref[0] = jnp.where(data, rep(proj(OFF_KA, FOX_W)), augk).astype(BF16)
    vaug_ref[0] = jnp.where(data, rep(proj(OFF_VA, FOX_W)), 1.0).astype(BF16)

    za = proj(OFF_ZA, FOX_W)
    ga_ref[0] = (za * (1.0 / (1.0 + jnp.exp(-za)))).astype(BF16)
    zb = proj(OFF_ZB, SWA_W)
    gb_ref[0] = (zb * (1.0 / (1.0 + jnp.exp(-zb)))).astype(BF16)

    ang = pos_ref[0].astype(F32) * invf_ref[...]
    cosv = jnp.cos(ang)
    sinv = jnp.sin(ang)
    first = (lane & (HEAD_DIM // 2)) == 0
    sin_signed = jnp.where(first, -sinv, sinv)

    def rope(a):
        outs = []
        for cidx in range(a.shape[1] // LANES):
            blk = a[:, cidx * LANES:(cidx + 1) * LANES]
            other = jnp.where(first, pltpu.roll(blk, LANES - HEAD_DIM // 2, 1),
                              pltpu.roll(blk, HEAD_DIM // 2, 1))
            outs.append(blk * cosv + other * sin_signed)
        return jnp.concatenate(outs, axis=1)

    qb_ref[0] = rope(proj(OFF_QB, SWA_W)).astype(BF16)
    kd_ref[0] = rope(proj(OFF_KD, 2 * SWA_KV_W)).astype(BF16)
    vd_ref[0] = proj(OFF_VD, 2 * SWA_KV_W).astype(BF16)


def _inproj_call(x, shift, scale, g_pre, w_perm, pos3, invf, bf_pad, tri, eq, ek, oq, ok):
    B, S, D = x.shape
    tm = TM_IN
    row = lambda b, t: (b, t, 0)
    per_b = lambda b, t: (b, 0, 0)
    const2 = lambda b, t: (0, 0)
    out_w = (AUG_W, AUG_W, AUG_W, FOX_W, SWA_W, 2 * SWA_KV_W, 2 * SWA_KV_W, SWA_W)
    return pl.pallas_call(
        _inproj_kernel,
        grid=(B, S // tm),
        in_specs=[pl.BlockSpec((1, tm, D), row),
                  pl.BlockSpec((1, 1, D), per_b),
                  pl.BlockSpec((1, 1, D), per_b),
                  pl.BlockSpec((1, D), const2),
                  pl.BlockSpec((D, IN_NP), const2),
                  pl.BlockSpec((1, tm, 1), row),
                  pl.BlockSpec((1, LANES), const2),
                  pl.BlockSpec((1, LANES), const2),
                  pl.BlockSpec((tm, tm), const2),
                  pl.BlockSpec((LANES, AUG_W), const2),
                  pl.BlockSpec((LANES, AUG_W), const2),
                  pl.BlockSpec((1, AUG_W), const2),
                  pl.BlockSpec((1, AUG_W), const2)],
        out_specs=[pl.BlockSpec((1, tm, w), row) for w in out_w],
        out_shape=[jax.ShapeDtypeStruct((B, S, w), BF16) for w in out_w],
        scratch_shapes=[pltpu.VMEM((1, LANES), F32)],
        compiler_params=pltpu.CompilerParams(
            dimension_semantics=("arbitrary", "arbitrary"), vmem_limit_bytes=VMEM_LIMIT),
        name="in_proj",
    )(x, shift, scale, g_pre, w_perm, pos3, invf, bf_pad, tri, eq, ek, oq, ok)


def _fox_kernel(q_ref, k_ref, v_ref, g_ref, o_ref, m_sc, acc_sc):
    tq = q_ref.shape[1]
    tk = TK_FOX
    qi = pl.program_id(2)
    reps = tk // LANES

    m_sc[...] = jnp.full(m_sc.shape, NEG_BIG, F32)
    acc_sc[...] = jnp.zeros(acc_sc.shape, F32)

    def step(j, masked):
        off = pl.multiple_of(j * tk, tk)
        for hh in range(2):
            q = q_ref[0, :, hh * LANES:(hh + 1) * LANES]
            k = k_ref[0, pl.ds(off, tk), hh * LANES:(hh + 1) * LANES]
            v = v_ref[0, pl.ds(off, tk), hh * LANES:(hh + 1) * LANES]
            s = lax.dot_general(q, k, (((1,), (1,)), ((), ())), preferred_element_type=F32)
            if masked:
                r = lax.broadcasted_iota(jnp.int32, (tq, tk), 0)
                c = lax.broadcasted_iota(jnp.int32, (tq, tk), 1)
                s = jnp.where(r >= c, s, NEG_BIG)
            m_prev = m_sc[hh]
            m_next = jnp.maximum(m_prev, jnp.max(s, axis=1, keepdims=True))
            alpha = jnp.exp(m_prev - m_next)
            p = jnp.exp(s - pltpu.repeat(m_next, reps, axis=1))
            pv = jnp.dot(p.astype(BF16), v, preferred_element_type=F32)
            acc_sc[hh] = alpha * acc_sc[hh] + pv
            m_sc[hh] = m_next

    def body(j, carry):
        step(j, False)
        return carry

    lax.fori_loop(0, qi, body, 0)
    step(qi, True)

    lane = lax.broadcasted_iota(jnp.int32, (tq, LANES), 1)
    a0 = acc_sc[0]
    a1 = acc_sc[1]
    o0 = a0 / pltpu.roll(a0, HEAD_DIM, 1)
    o1 = a1 / pltpu.roll(a1, HEAD_DIM, 1)
    o = jnp.where(lane < HEAD_DIM, o0, o1)
    o_ref[0] = (o * g_ref[0].astype(F32)).astype(BF16)


def _fox_call(qaug, kaug, vaug, ga):
    B, S, _ = qaug.shape
    tq = TQ_FOX
    assert TQ_FOX == TK_FOX
    return pl.pallas_call(
        _fox_kernel,
        grid=(B, FOX_HEADS // 2, S // tq),
        in_specs=[pl.BlockSpec((1, tq, 2 * LANES), lambda b, p, i: (b, i, p)),
                  pl.BlockSpec((1, S, 2 * LANES), lambda b, p, i: (b, 0, p)),
                  pl.BlockSpec((1, S, 2 * LANES), lambda b, p, i: (b, 0, p)),
                  pl.BlockSpec((1, tq, LANES), lambda b, p, i: (b, i, p))],
        out_specs=pl.BlockSpec((1, tq, LANES), lambda b, p, i: (b, i, p)),
        out_shape=jax.ShapeDtypeStruct((B, S, FOX_W), BF16),
        scratch_shapes=[pltpu.VMEM((2, tq, LANES), F32), pltpu.VMEM((2, tq, LANES), F32)],
        compiler_params=pltpu.CompilerParams(
            dimension_semantics=("arbitrary", "arbitrary", "arbitrary"),
            vmem_limit_bytes=VMEM_LIMIT),
        name="fox",
    )(qaug, kaug, vaug, ga)


def _swa_kernel(sinks_ref, q_ref, kp_ref, kc_ref, vp_ref, vc_ref, g_ref, o_ref):
    i = pl.program_id(1)
    nsub = TQ_SWA // WINDOW
    kall = jnp.concatenate([kp_ref[0], kc_ref[0]], axis=0)
    vall = jnp.concatenate([vp_ref[0], vc_ref[0]], axis=0)
    lane = lax.broadcasted_iota(jnp.int32, (WINDOW, LANES), 1)
    lo = lane < HEAD_DIM
    qi_ = lax.broadcasted_iota(jnp.int32, (WINDOW, 2 * WINDOW), 0)
    kj_ = lax.broadcasted_iota(jnp.int32, (WINDOW, 2 * WINDOW), 1)
    rel = qi_ + WINDOW - kj_
    band = (rel >= 0) & (rel < WINDOW)
    zero = jnp.zeros((WINDOW, LANES), BF16)
    group = SWA_Q_HEADS // SWA_KV_HEADS
    for r in range(nsub):
        if r == 0:
            jmin = jnp.where(i == 0, WINDOW, 0)
            valid = band & (kj_ >= jmin)
        else:
            valid = band
        valid4 = jnp.concatenate([valid] * group, axis=0)
        for g in range(SWA_KV_HEADS):
            kg = kall[r * WINDOW:(r + 2) * WINDOW, g * LANES:(g + 1) * LANES]
            vg = vall[r * WINDOW:(r + 2) * WINDOW, g * LANES:(g + 1) * LANES]
            c0 = g * group * HEAD_DIM
            qp0 = q_ref[0, r * WINDOW:(r + 1) * WINDOW, c0:c0 + LANES]
            qp1 = q_ref[0, r * WINDOW:(r + 1) * WINDOW, c0 + LANES:c0 + 2 * LANES]
            qs = jnp.concatenate([jnp.where(lo, qp0, zero), jnp.where(lo, qp1, zero),
                                  jnp.where(lo, zero, qp0), jnp.where(lo, zero, qp1)], axis=0)
            s = lax.dot_general(qs, kg, (((1,), (1,)), ((), ())), preferred_element_type=F32)
            s = jnp.where(valid4, s, NEG_BIG)
            heads = (group * g, group * g + 2, group * g + 1, group * g + 3)
            sink = jnp.concatenate(
                [jnp.full((WINDOW, 1), sinks_ref[hd], F32) for hd in heads], axis=0)
            m = jnp.maximum(jnp.max(s, axis=1, keepdims=True), sink)
            p = jnp.exp(s - m)
            l = jnp.sum(p, axis=1, keepdims=True) + jnp.exp(sink - m)
            o = jnp.dot(p.astype(BF16), vg, preferred_element_type=F32) * (1.0 / l)
            pair0 = jnp.where(lo, o[0:WINDOW], o[2 * WINDOW:3 * WINDOW])
            pair1 = jnp.where(lo, o[WINDOW:2 * WINDOW], o[3 * WINDOW:4 * WINDOW])
            rows = slice(r * WINDOW, (r + 1) * WINDOW)
            o_ref[0, rows, c0:c0 + LANES] = (
                pair0 * g_ref[0, rows, c0:c0 + LANES].astype(F32)).astype(BF16)
            o_ref[0, rows, c0 + LANES:c0 + 2 * LANES] = (
                pair1 * g_ref[0, rows, c0 + LANES:c0 + 2 * LANES].astype(F32)).astype(BF16)


def _swa_call(sinks, qb, kd, vd, gb):
    B, S, _ = qb.shape
    tq = TQ_SWA
    nsub = tq // WINDOW
    cur = lambda b, i: (b, i, 0)
    prev = lambda b, i: (b, jnp.maximum(i * nsub - 1, 0), 0)
    kvw = 2 * SWA_KV_W
    return pl.pallas_call(
        _swa_kernel,
        grid=(B, S // tq),
        in_specs=[pl.BlockSpec(memory_space=pltpu.SMEM),
                  pl.BlockSpec((1, tq, SWA_W), cur),
                  pl.BlockSpec((1, WINDOW, kvw), prev),
                  pl.BlockSpec((1, tq, kvw), cur),
                  pl.BlockSpec((1, WINDOW, kvw), prev),
                  pl.BlockSpec((1, tq, kvw), cur),
                  pl.BlockSpec((1, tq, SWA_W), cur)],
        out_specs=pl.BlockSpec((1, tq, SWA_W), cur),
        out_shape=jax.ShapeDtypeStruct((B, S, SWA_W), BF16),
        compiler_params=pltpu.CompilerParams(
            dimension_semantics=("arbitrary", "arbitrary"), vmem_limit_bytes=VMEM_LIMIT),
        name="swa",
    )(sinks, qb, kd, kd, vd, vd, gb)


def _outproj_kernel(oa_ref, ob_ref, x_ref, gate_ref, gpost_ref, wa_ref, wb_ref, out_ref):
    y = (jnp.dot(oa_ref[0], wa_ref[...], preferred_element_type=F32)
         + jnp.dot(ob_ref[0], wb_ref[...], preferred_element_type=F32))
    ms = jnp.mean(y * y, axis=-1, keepdims=True)
    yn = y * lax.rsqrt(ms + RMS_EPS) * gpost_ref[...]
    out_ref[0] = x_ref[0] + gate_ref[0] * yn


def _outproj_call(oa, ob, x, gate, g_post, wa, wb):
    B, S, D = x.shape
    tm = TM_OUT
    row = lambda b, t: (b, t, 0)
    const2 = lambda b, t: (0, 0)
    return pl.pallas_call(
        _outproj_kernel,
        grid=(B, S // tm),
        in_specs=[pl.BlockSpec((1, tm, FOX_W), row),
                  pl.BlockSpec((1, tm, SWA_W), row),
                  pl.BlockSpec((1, tm, D), row),
                  pl.BlockSpec((1, 1, D), lambda b, t: (b, 0, 0)),
                  pl.BlockSpec((1, D), const2),
                  pl.BlockSpec((FOX_W, D), const2),
                  pl.BlockSpec((SWA_W, D), const2)],
        out_specs=pl.BlockSpec((1, tm, D), row),
        out_shape=jax.ShapeDtypeStruct((B, S, D), F32),
        compiler_params=pltpu.CompilerParams(
            dimension_semantics=("arbitrary", "arbitrary"), vmem_limit_bytes=VMEM_LIMIT),
        name="out_proj",
    )(oa, ob, x, gate, g_post, wa, wb)


def _perm_w_in(w):
    sc = HEAD_DIM ** -0.5
    o = 0
    qa = w[:, o:o + FOX_W]; o += FOX_W
    ka = w[:, o:o + FOX_W]; o += FOX_W
    va = w[:, o:o + FOX_W]; o += FOX_W
    fa = w[:, o:o + FOX_HEADS]; o += FOX_HEADS
    za = w[:, o:o + FOX_W]; o += FOX_W
    qb = w[:, o:o + SWA_W]; o += SWA_W
    kb = w[:, o:o + SWA_KV_W]; o += SWA_KV_W
    vb = w[:, o:o + SWA_KV_W]; o += SWA_KV_W
    zb = w[:, o:o + SWA_W]
    dup = lambda a: jnp.concatenate(
        [a[:, HEAD_DIM * (g // 2):HEAD_DIM * (g // 2 + 1)] for g in range(2 * SWA_KV_HEADS)], axis=1)
    fpad = jnp.concatenate(
        [fa, fa, fa, jnp.zeros((w.shape[0], LANES - 3 * FOX_HEADS), w.dtype)], axis=1)
    return jnp.concatenate([qa * sc, ka, va, za, qb * sc, zb, dup(kb), dup(vb), fpad],
                           axis=1).astype(BF16)


def _aug_constants():
    eq = np.zeros((LANES, AUG_W), np.float32)
    ek = np.zeros((LANES, AUG_W), np.float32)
    oq = np.zeros((1, AUG_W), np.float32)
    ok = np.zeros((1, AUG_W), np.float32)
    for hd in range(FOX_HEADS):
        base = LANES * hd + (HEAD_DIM if hd % 2 == 0 else 0)
        for part in range(3):
            eq[part * FOX_HEADS + hd, base + part] = 1.0
            ok[0, base + part] = 1.0
            ek[part * FOX_HEADS + hd, base + 3 + part] = -1.0
            oq[0, base + 3 + part] = 1.0
    return (jnp.asarray(eq, BF16), jnp.asarray(ek, BF16), jnp.asarray(oq), jnp.asarray(ok))


def kernel(x, c, positions, w_ada, b_ada, g_pre, w_in, b_fgate, sinks, w_out, g_post):
    B, S, D = x.shape
    depth = w_ada.shape[0]
    half = HEAD_DIM // 2
    inv_freq = ROPE_THETA ** (-jnp.arange(half, dtype=F32) / half)
    invf = jnp.tile(inv_freq, LANES // half)[None, :]
    pos3 = positions[:, :, None]
    tri = jnp.asarray(np.tril(np.ones((TM_IN, TM_IN), np.float32)), BF16)
    eq, ek, oq, ok = _aug_constants()
    c_pad = jnp.zeros((8, D), F32).at[:B].set(c)
    for l in range(depth):
        mod = _mod_call(c_pad, w_ada[l], b_ada[l][None, :])[:B]
        shift = mod[:, None, 0:D]
        scale = mod[:, None, D:2 * D]
        gate = mod[:, None, 2 * D:3 * D]
        bf_pad = jnp.concatenate(
            [b_fgate[l]] * 3 + [jnp.zeros((LANES - 3 * FOX_HEADS,), F32)])[None, :]
        qaug, kaug, vaug, ga, qb, kd, vd, gb = _inproj_call(
            x, shift, scale, g_pre[l][None, :], _perm_w_in(w_in[l]), pos3, invf, bf_pad,
            tri, eq, ek, oq, ok)
        oa = _fox_call(qaug, kaug, vaug, ga)
        ob = _swa_call(sinks[l], qb, kd, vd, gb)
        wo = w_out[l].astype(BF16)
        x = _outproj_call(oa, ob, x, gate, g_post[l][None, :], wo[:FOX_W], wo[FOX_W:])
    return x
```

```python
import functools

import jax
import jax.numpy as jnp
import numpy as np
from jax import lax
from jax.experimental import pallas as pl
from jax.experimental.pallas import tpu as pltpu

D_MODEL = 1024
HEAD_DIM = 64
FOX_HEADS = 8
SWA_Q_HEADS = 8
SWA_KV_HEADS = 2
WINDOW = 128
ROPE_THETA = 10000.0
RMS_EPS = 1e-6
FOX_W = FOX_HEADS * HEAD_DIM
SWA_W = SWA_Q_HEADS * HEAD_DIM
SWA_KV_W = SWA_KV_HEADS * HEAD_DIM

LANES = 128
AUG_W = FOX_HEADS * LANES

OFF_QA = 0
OFF_KA = OFF_QA + FOX_W
OFF_VA = OFF_KA + FOX_W
OFF_ZA = OFF_VA + FOX_W
OFF_QB = OFF_ZA + FOX_W
OFF_ZB = OFF_QB + SWA_W
OFF_KD = OFF_ZB + SWA_W
OFF_VD = OFF_KD + 2 * SWA_KV_W
OFF_F = OFF_VD + 2 * SWA_KV_W
IN_NP = OFF_F + LANES

TM_IN = 512
TQ_FOX = 512
TK_FOX = 512
TQ_SWA = 512
TM_OUT = 512
NEG_BIG = -1e30
VMEM_LIMIT = 56 * 1024 * 1024

F32 = jnp.float32
BF16 = jnp.bfloat16


def _split3(a):
    hi = a.astype(BF16)
    r = a - hi.astype(F32)
    mid = r.astype(BF16)
    lo = (r - mid.astype(F32)).astype(BF16)
    return hi, mid, lo


def _mod_kernel(c_ref, w_ref, b_ref, o_ref):
    c = c_ref[...]
    sc = c * (1.0 / (1.0 + jnp.exp(-c)))
    o_ref[...] = jnp.dot(sc, w_ref[...], precision=lax.Precision.HIGHEST,
                         preferred_element_type=F32) + b_ref[...]


def _mod_call(c_pad, w_ada, b_ada):
    rows = c_pad.shape[0]
    n = w_ada.shape[1]
    bn = D_MODEL
    return pl.pallas_call(
        _mod_kernel,
        grid=(n // bn,),
        in_specs=[pl.BlockSpec((rows, D_MODEL), lambda j: (0, 0)),
                  pl.BlockSpec((D_MODEL, bn), lambda j: (0, j)),
                  pl.BlockSpec((1, bn), lambda j: (0, j))],
        out_specs=pl.BlockSpec((rows, bn), lambda j: (0, j)),
        out_shape=jax.ShapeDtypeStruct((rows, n), F32),
        name="mod",
    )(c_pad, w_ada, b_ada)


def _inproj_kernel(x_ref, shift_ref, scale_ref, gpre_ref, w_ref, pos_ref, invf_ref, bf_ref,
                   tri_ref, eq_ref, ek_ref, oq_ref, ok_ref,
                   qaug_ref, kaug_ref, vaug_ref, ga_ref, qb_ref, kd_ref, vd_ref, gb_ref,
                   carry_ref):
    tm = x_ref.shape[1]
    t = pl.program_id(1)

    x = x_ref[0]
    ms = jnp.mean(x * x, axis=-1, keepdims=True)
    y = x * lax.rsqrt(ms + RMS_EPS) * gpre_ref[...]
    h = (y * (1.0 + scale_ref[0]) + shift_ref[0]).astype(BF16)

    def proj(off, width):
        return jnp.dot(h, w_ref[:, off:off + width], preferred_element_type=F32)

    f = proj(OFF_F, LANES) + bf_ref[...]
    ls = jnp.minimum(f, 0.0) - jnp.log1p(jnp.exp(-jnp.abs(f)))
    tri = tri_ref[...]
    cum = None
    for part in _split3(ls):
        d = jnp.dot(tri, part, preferred_element_type=F32)
        cum = d if cum is None else cum + d

    @pl.when(t == 0)
    def _():
        carry_ref[...] = jnp.zeros_like(carry_ref)

    cum = cum + carry_ref[...]
    carry_ref[...] = cum[tm - 1:tm, :]

    hi, mid, lo = _split3(cum)
    lane = lax.broadcasted_iota(jnp.int32, (tm, LANES), 1)
    cs = jnp.where(lane < FOX_HEADS, hi.astype(F32),
                   jnp.where(lane < 2 * FOX_HEADS, mid.astype(F32), lo.astype(F32))).astype(BF16)
    augq = jnp.dot(cs, eq_ref[...], preferred_element_type=F32) + oq_ref[...]
    augk = jnp.dot(cs, ek_ref[...], preferred_element_type=F32) + ok_ref[...]

    lane_w = lax.broadcasted_iota(jnp.int32, (tm, AUG_W), 1)
    data = (((lane_w >> 6) ^ (lane_w >> 7)) & 1) == 0

    def rep(a):
        return jnp.concatenate(
            [a[:, LANES * (hd // 2):LANES * (hd // 2 + 1)] for hd in range(FOX_HEADS)], axis=1)

    qaug_ref[0] = jnp.where(data, rep(proj(OFF_QA, FOX_W)), augq).astype(BF16)
    kaug_ref[0] = jnp.where(data, rep(proj(OFF_KA, FOX_W)), augk).astype(BF16)
    vaug_ref[0] = jnp.where(data, rep(proj(OFF_VA, FOX_W)), 1.0).astype(BF16)

    za = proj(OFF_ZA, FOX_W)
    ga_ref[0] = (za * (1.0 / (1.0 + jnp.exp(-za)))).astype(BF16)
    zb = proj(OFF_ZB, SWA_W)
    gb_ref[0] = (zb * (1.0 / (1.0 + jnp.exp(-zb)))).astype(BF16)

    ang = pos_ref[0].astype(F32) * invf_ref[...]
    cosv = jnp.cos(ang)
    sinv = jnp.sin(ang)
    first = (lane & (HEAD_DIM // 2)) == 0
    sin_signed = jnp.where(first, -sinv, sinv)

    def rope(a):
        outs = []
        for cidx in range(a.shape[1] // LANES):
            blk = a[:, cidx * LANES:(cidx + 1) * LANES]
            other = jnp.where(first, pltpu.roll(blk, LANES - HEAD_DIM // 2, 1),
                              pltpu.roll(blk, HEAD_DIM // 2, 1))
            outs.append(blk * cosv + other * sin_signed)
        return jnp.concatenate(outs, axis=1)

    qb_ref[0] = rope(proj(OFF_QB, SWA_W)).astype(BF16)
    kd_ref[0] = rope(proj(OFF_KD, 2 * SWA_KV_W)).astype(BF16)
    vd_ref[0] = proj(OFF_VD, 2 * SWA_KV_W).astype(BF16)


def _inproj_call(x, shift, scale, g_pre, w_perm, pos3, invf, bf_pad, tri, eq, ek, oq, ok):
    B, S, D = x.shape
    tm = TM_IN
    row = lambda b, t: (b, t, 0)
    per_b = lambda b, t: (b, 0, 0)
    const2 = lambda b, t: (0, 0)
    out_w = (AUG_W, AUG_W, AUG_W, FOX_W, SWA_W, 2 * SWA_KV_W, 2 * SWA_KV_W, SWA_W)
    return pl.pallas_call(
        _inproj_kernel,
        grid=(B, S // tm),
        in_specs=[pl.BlockSpec((1, tm, D), row),
                  pl.BlockSpec((1, 1, D), per_b),
                  pl.BlockSpec((1, 1, D), per_b),
                  pl.BlockSpec((1, D), const2),
                  pl.BlockSpec((D, IN_NP), const2),
                  pl.BlockSpec((1, tm, 1), row),
                  pl.BlockSpec((1, LANES), const2),
                  pl.BlockSpec((1, LANES), const2),
                  pl.BlockSpec((tm, tm), const2),
                  pl.BlockSpec((LANES, AUG_W), const2),
                  pl.BlockSpec((LANES, AUG_W), const2),
                  pl.BlockSpec((1, AUG_W), const2),
                  pl.BlockSpec((1, AUG_W), const2)],
        out_specs=[pl.BlockSpec((1, tm, w), row) for w in out_w],
        out_shape=[jax.ShapeDtypeStruct((B, S, w), BF16) for w in out_w],
        scratch_shapes=[pltpu.VMEM((1, LANES), F32)],
        compiler_params=pltpu.CompilerParams(
            dimension_semantics=("arbitrary", "arbitrary"), vmem_limit_bytes=VMEM_LIMIT),
        name="in_proj",
    )(x, shift, scale, g_pre, w_perm, pos3, invf, bf_pad, tri, eq, ek, oq, ok)


def _fox_kernel(q_ref, k_ref, v_ref, g_ref, o_ref, m_sc, acc_sc):
    tq = q_ref.shape[1]
    tk = TK_FOX
    qi = pl.program_id(2)

    m_sc[...] = jnp.full(m_sc.shape, NEG_BIG, F32)
    acc_sc[...] = jnp.zeros(acc_sc.shape, F32)

    def step(j, masked):
        off = pl.multiple_of(j * tk, tk)
        st = []
        for hh in range(2):
            q = q_ref[0, :, hh * LANES:(hh + 1) * LANES]
            k = k_ref[0, pl.ds(off, tk), hh * LANES:(hh + 1) * LANES]
            st.append(lax.dot_general(k, q, (((1,), (1,)), ((), ())),
                                      preferred_element_type=F32))
        for hh in range(2):
            s = st[hh]
            if masked:
                kr = lax.broadcasted_iota(jnp.int32, (tk, tq), 0)
                qc = lax.broadcasted_iota(jnp.int32, (tk, tq), 1)
                s = jnp.where(kr <= qc, s, NEG_BIG)
            v = v_ref[0, pl.ds(off, tk), hh * LANES:(hh + 1) * LANES]
            m_prev = m_sc[hh]
            m_next = jnp.maximum(m_prev, jnp.max(s, axis=0, keepdims=True))
            alpha = jnp.exp(m_prev - m_next)
            p = jnp.exp(s - m_next).astype(BF16)
            pv = lax.dot_general(v, p, (((0,), (0,)), ((), ())),
                                 preferred_element_type=F32)
            acc_sc[hh] = alpha * acc_sc[hh] + pv
            m_sc[hh] = m_next

    def body(j, carry):
        step(j, False)
        return carry

    lax.fori_loop(0, qi, body, 0)
    step(qi, True)

    a0 = acc_sc[0]
    a1 = acc_sc[1]
    ot = jnp.concatenate([a0[:HEAD_DIM] / a0[HEAD_DIM:HEAD_DIM + 1],
                          a1[HEAD_DIM:] / a1[0:1]], axis=0)
    o_ref[0] = (ot.T * g_ref[0].astype(F32)).astype(BF16)


def _fox_call(qaug, kaug, vaug, ga):
    B, S, _ = qaug.shape
    tq = TQ_FOX
    assert TQ_FOX == TK_FOX
    return pl.pallas_call(
        _fox_kernel,
        grid=(B, FOX_HEADS // 2, S // tq),
        in_specs=[pl.BlockSpec((1, tq, 2 * LANES), lambda b, p, i: (b, i, p)),
                  pl.BlockSpec((1, S, 2 * LANES), lambda b, p, i: (b, 0, p)),
                  pl.BlockSpec((1, S, 2 * LANES), lambda b, p, i: (b, 0, p)),
                  pl.BlockSpec((1, tq, LANES), lambda b, p, i: (b, i, p))],
        out_specs=pl.BlockSpec((1, tq, LANES), lambda b, p, i: (b, i, p)),
        out_shape=jax.ShapeDtypeStruct((B, S, FOX_W), BF16),
        scratch_shapes=[pltpu.VMEM((2, 1, tq), F32), pltpu.VMEM((2, LANES, tq), F32)],
        compiler_params=pltpu.CompilerParams(
            dimension_semantics=("arbitrary", "arbitrary", "arbitrary"),
            vmem_limit_bytes=VMEM_LIMIT),
        name="fox",
    )(qaug, kaug, vaug, ga)


def _swa_kernel(sinks_ref, q_ref, kp_ref, kc_ref, vp_ref, vc_ref, g_ref, o_ref):
    i = pl.program_id(1)
    nsub = TQ_SWA // WINDOW
    kall = jnp.concatenate([kp_ref[0], kc_ref[0]], axis=0)
    vall = jnp.concatenate([vp_ref[0], vc_ref[0]], axis=0)
    lane = lax.broadcasted_iota(jnp.int32, (WINDOW, LANES), 1)
    lo = lane < HEAD_DIM
    qi_ = lax.broadcasted_iota(jnp.int32, (WINDOW, 2 * WINDOW), 0)
    kj_ = lax.broadcasted_iota(jnp.int32, (WINDOW, 2 * WINDOW), 1)
    rel = qi_ + WINDOW - kj_
    band = (rel >= 0) & (rel < WINDOW)
    zero = jnp.zeros((WINDOW, LANES), BF16)
    group = SWA_Q_HEADS // SWA_KV_HEADS
    for r in range(nsub):
        if r == 0:
            jmin = jnp.where(i == 0, WINDOW, 0)
            valid = band & (kj_ >= jmin)
        else:
            valid = band
        valid4 = jnp.concatenate([valid] * group, axis=0)
        for g in range(SWA_KV_HEADS):
            kg = kall[r * WINDOW:(r + 2) * WINDOW, g * LANES:(g + 1) * LANES]
            vg = vall[r * WINDOW:(r + 2) * WINDOW, g * LANES:(g + 1) * LANES]
            c0 = g * group * HEAD_DIM
            qp0 = q_ref[0, r * WINDOW:(r + 1) * WINDOW, c0:c0 + LANES]
            qp1 = q_ref[0, r * WINDOW:(r + 1) * WINDOW, c0 + LANES:c0 + 2 * LANES]
            qs = jnp.concatenate([jnp.where(lo, qp0, zero), jnp.where(lo, qp1, zero),
                                  jnp.where(lo, zero, qp0), jnp.where(lo, zero, qp1)], axis=0)
            s = lax.dot_general(qs, kg, (((1,), (1,)), ((), ())), preferred_element_type=F32)
            s = jnp.where(valid4, s, NEG_BIG)
            heads = (group * g, group * g + 2, group * g + 1, group * g + 3)
            sink = jnp.concatenate(
                [jnp.full((WINDOW, 1), sinks_ref[hd], F32) for hd in heads], axis=0)
            m = jnp.maximum(jnp.max(s, axis=1, keepdims=True), sink)
            p = jnp.exp(s - m)
            l = jnp.sum(p, axis=1, keepdims=True) + jnp.exp(sink - m)
            o = jnp.dot(p.astype(BF16), vg, preferred_element_type=F32) * (1.0 / l)
            pair0 = jnp.where(lo, o[0:WINDOW], o[2 * WINDOW:3 * WINDOW])
            pair1 = jnp.where(lo, o[WINDOW:2 * WINDOW], o[3 * WINDOW:4 * WINDOW])
            rows = slice(r * WINDOW, (r + 1) * WINDOW)
            o_ref[0, rows, c0:c0 + LANES] = (
                pair0 * g_ref[0, rows, c0:c0 + LANES].astype(F32)).astype(BF16)
            o_ref[0, rows, c0 + LANES:c0 + 2 * LANES] = (
                pair1 * g_ref[0, rows, c0 + LANES:c0 + 2 * LANES].astype(F32)).astype(BF16)


def _swa_call(sinks, qb, kd, vd, gb):
    B, S, _ = qb.shape
    tq = TQ_SWA
    nsub = tq // WINDOW
    cur = lambda b, i: (b, i, 0)
    prev = lambda b, i: (b, jnp.maximum(i * nsub - 1, 0), 0)
    kvw = 2 * SWA_KV_W
    return pl.pallas_call(
        _swa_kernel,
        grid=(B, S // tq),
        in_specs=[pl.BlockSpec(memory_space=pltpu.SMEM),
                  pl.BlockSpec((1, tq, SWA_W), cur),
                  pl.BlockSpec((1, WINDOW, kvw), prev),
                  pl.BlockSpec((1, tq, kvw), cur),
                  pl.BlockSpec((1, WINDOW, kvw), prev),
                  pl.BlockSpec((1, tq, kvw), cur),
                  pl.BlockSpec((1, tq, SWA_W), cur)],
        out_specs=pl.BlockSpec((1, tq, SWA_W), cur),
        out_shape=jax.ShapeDtypeStruct((B, S, SWA_W), BF16),
        compiler_params=pltpu.CompilerParams(
            dimension_semantics=("arbitrary", "arbitrary"), vmem_limit_bytes=VMEM_LIMIT),
        name="swa",
    )(sinks, qb, kd, kd, vd, vd, gb)


def _outproj_kernel(oa_ref, ob_ref, x_ref, gate_ref, gpost_ref, wa_ref, wb_ref, out_ref):
    y = (jnp.dot(oa_ref[0], wa_ref[...], preferred_element_type=F32)
         + jnp.dot(ob_ref[0], wb_ref[...], preferred_element_type=F32))
    ms = jnp.mean(y * y, axis=-1, keepdims=True)
    yn = y * lax.rsqrt(ms + RMS_EPS) * gpost_ref[...]
    out_ref[0] = x_ref[0] + gate_ref[0] * yn


def _outproj_call(oa, ob, x, gate, g_post, wa, wb):
    B, S, D = x.shape
    tm = TM_OUT
    row = lambda b, t: (b, t, 0)
    const2 = lambda b, t: (0, 0)
    return pl.pallas_call(
        _outproj_kernel,
        grid=(B, S // tm),
        in_specs=[pl.BlockSpec((1, tm, FOX_W), row),
                  pl.BlockSpec((1, tm, SWA_W), row),
                  pl.BlockSpec((1, tm, D), row),
                  pl.BlockSpec((1, 1, D), lambda b, t: (b, 0, 0)),
                  pl.BlockSpec((1, D), const2),
                  pl.BlockSpec((FOX_W, D), const2),
                  pl.BlockSpec((SWA_W, D), const2)],
        out_specs=pl.BlockSpec((1, tm, D), row),
        out_shape=jax.ShapeDtypeStruct((B, S, D), F32),
        compiler_params=pltpu.CompilerParams(
            dimension_semantics=("arbitrary", "arbitrary"), vmem_limit_bytes=VMEM_LIMIT),
        name="out_proj",
    )(oa, ob, x, gate, g_post, wa, wb)


def _perm_w_in(w):
    sc = HEAD_DIM ** -0.5
    o = 0
    qa = w[:, o:o + FOX_W]; o += FOX_W
    ka = w[:, o:o + FOX_W]; o += FOX_W
    va = w[:, o:o + FOX_W]; o += FOX_W
    fa = w[:, o:o + FOX_HEADS]; o += FOX_HEADS
    za = w[:, o:o + FOX_W]; o += FOX_W
    qb = w[:, o:o + SWA_W]; o += SWA_W
    kb = w[:, o:o + SWA_KV_W]; o += SWA_KV_W
    vb = w[:, o:o + SWA_KV_W]; o += SWA_KV_W
    zb = w[:, o:o + SWA_W]
    dup = lambda a: jnp.concatenate(
        [a[:, HEAD_DIM * (g // 2):HEAD_DIM * (g // 2 + 1)] for g in range(2 * SWA_KV_HEADS)], axis=1)
    fpad = jnp.concatenate(
        [fa, fa, fa, jnp.zeros((w.shape[0], LANES - 3 * FOX_HEADS), w.dtype)], axis=1)
    return jnp.concatenate([qa * sc, ka, va, za, qb * sc, zb, dup(kb), dup(vb), fpad],
                           axis=1).astype(BF16)


def _aug_constants():
    eq = np.zeros((LANES, AUG_W), np.float32)
    ek = np.zeros((LANES, AUG_W), np.float32)
    oq = np.zeros((1, AUG_W), np.float32)
    ok = np.zeros((1, AUG_W), np.float32)
    for hd in range(FOX_HEADS):
        base = LANES * hd + (HEAD_DIM if hd % 2 == 0 else 0)
        for part in range(3):
            eq[part * FOX_HEADS + hd, base + part] = 1.0
            ok[0, base + part] = 1.0
            ek[part * FOX_HEADS + hd, base + 3 + part] = -1.0
            oq[0, base + 3 + part] = 1.0
    return (jnp.asarray(eq, BF16), jnp.asarray(ek, BF16), jnp.asarray(oq), jnp.asarray(ok))


def kernel(x, c, positions, w_ada, b_ada, g_pre, w_in, b_fgate, sinks, w_out, g_post):
    B, S, D = x.shape
    depth = w_ada.shape[0]
    half = HEAD_DIM // 2
    inv_freq = ROPE_THETA ** (-jnp.arange(half, dtype=F32) / half)
    invf = jnp.tile(inv_freq, LANES // half)[None, :]
    pos3 = positions[:, :, None]
    tri = jnp.asarray(np.tril(np.ones((TM_IN, TM_IN), np.float32)), BF16)
    eq, ek, oq, ok = _aug_constants()
    c_pad = jnp.zeros((8, D), F32).at[:B].set(c)
    for l in range(depth):
        mod = _mod_call(c_pad, w_ada[l], b_ada[l][None, :])[:B]
        shift = mod[:, None, 0:D]
        scale = mod[:, None, D:2 * D]
        gate = mod[:, None, 2 * D:3 * D]
        bf_pad = jnp.concatenate(
            [b_fgate[l]] * 3 + [jnp.zeros((LANES - 3 * FOX_HEADS,), F32)])[None, :]
        qaug, kaug, vaug, ga, qb, kd, vd, gb = _inproj_call(
            x, shift, scale, g_pre[l][None, :], _perm_w_in(w_in[l]), pos3, invf, bf_pad,
            tri, eq, ek, oq, ok)
        oa = _fox_call(qaug, kaug, vaug, ga)
        ob = _swa_call(sinks[l], qb, kd, vd, gb)
        wo = w_out[l].astype(BF16)
        x = _outproj_call(oa, ob, x, gate, g_post[l][None, :], wo[:FOX_W], wo[FOX_W:])
    return x
```

```python
import functools

import jax
import jax.numpy as jnp
import numpy as np
from jax import lax
from jax.experimental import pallas as pl
from jax.experimental.pallas import tpu as pltpu

D_MODEL = 1024
HEAD_DIM = 64
FOX_HEADS = 8
SWA_Q_HEADS = 8
SWA_KV_HEADS = 2
WINDOW = 128
ROPE_THETA = 10000.0
RMS_EPS = 1e-6
FOX_W = FOX_HEADS * HEAD_DIM
SWA_W = SWA_Q_HEADS * HEAD_DIM
SWA_KV_W = SWA_KV_HEADS * HEAD_DIM

LANES = 128
AUG_W = FOX_HEADS * LANES

OFF_QA = 0
OFF_KA = OFF_QA + FOX_W
OFF_VA = OFF_KA + FOX_W
OFF_ZA = OFF_VA + FOX_W
OFF_QB = OFF_ZA + FOX_W
OFF_ZB = OFF_QB + SWA_W
OFF_KD = OFF_ZB + SWA_W
OFF_VD = OFF_KD + 2 * SWA_KV_W
OFF_F = OFF_VD + 2 * SWA_KV_W
IN_NP = OFF_F + LANES

TM_IN = 512
TQ_FOX = 512
TK_FOX = 512
TQ_SWA = 512
TM_OUT = 512
NEG_BIG = -1e30
LOG2E = 1.4426950408889634
VMEM_LIMIT = 56 * 1024 * 1024

F32 = jnp.float32
BF16 = jnp.bfloat16


def _split3(a):
    hi = a.astype(BF16)
    r = a - hi.astype(F32)
    mid = r.astype(BF16)
    lo = (r - mid.astype(F32)).astype(BF16)
    return hi, mid, lo


def _mod_kernel(c_ref, w_ref, b_ref, o_ref):
    c = c_ref[...]
    sc = c * (1.0 / (1.0 + jnp.exp(-c)))
    o_ref[...] = jnp.dot(sc, w_ref[...], precision=lax.Precision.HIGHEST,
                         preferred_element_type=F32) + b_ref[...]


def _mod_call(c_pad, w_ada, b_ada):
    rows = c_pad.shape[0]
    n = w_ada.shape[1]
    bn = D_MODEL
    return pl.pallas_call(
        _mod_kernel,
        grid=(n // bn,),
        in_specs=[pl.BlockSpec((rows, D_MODEL), lambda j: (0, 0)),
                  pl.BlockSpec((D_MODEL, bn), lambda j: (0, j)),
                  pl.BlockSpec((1, bn), lambda j: (0, j))],
        out_specs=pl.BlockSpec((rows, bn), lambda j: (0, j)),
        out_shape=jax.ShapeDtypeStruct((rows, n), F32),
        name="mod",
    )(c_pad, w_ada, b_ada)


def _inproj_kernel(x_ref, shift_ref, scale_ref, gpre_ref, w_ref, pos_ref, invf_ref, bf_ref,
                   tri_ref, eq_ref, ek_ref, oq_ref, ok_ref,
                   qaug_ref, kaug_ref, vaug_ref, ga_ref, qb_ref, kd_ref, vd_ref, gb_ref,
                   carry_ref):
    tm = x_ref.shape[1]
    t = pl.program_id(1)

    x = x_ref[0]
    ms = jnp.mean(x * x, axis=-1, keepdims=True)
    y = x * lax.rsqrt(ms + RMS_EPS) * gpre_ref[...]
    h = (y * (1.0 + scale_ref[0]) + shift_ref[0]).astype(BF16)

    def proj(off, width):
        return jnp.dot(h, w_ref[:, off:off + width], preferred_element_type=F32)

    f = proj(OFF_F, LANES) + bf_ref[...]
    ls = jnp.minimum(f, 0.0) - jnp.log1p(jnp.exp(-jnp.abs(f)))
    tri = tri_ref[...]
    cum = None
    for part in _split3(ls):
        d = jnp.dot(tri, part, preferred_element_type=F32)
        cum = d if cum is None else cum + d

    @pl.when(t == 0)
    def _():
        carry_ref[...] = jnp.zeros_like(carry_ref)

    cum = cum + carry_ref[...]
    carry_ref[...] = cum[tm - 1:tm, :]

    hi, mid, lo = _split3(cum * LOG2E)
    lane = lax.broadcasted_iota(jnp.int32, (tm, LANES), 1)
    cs = jnp.where(lane < FOX_HEADS, hi.astype(F32),
                   jnp.where(lane < 2 * FOX_HEADS, mid.astype(F32), lo.astype(F32))).astype(BF16)
    augq = jnp.dot(cs, eq_ref[...], preferred_element_type=F32) + oq_ref[...]
    augk = jnp.dot(cs, ek_ref[...], preferred_element_type=F32) + ok_ref[...]

    lane_w = lax.broadcasted_iota(jnp.int32, (tm, AUG_W), 1)
    data = (((lane_w >> 6) ^ (lane_w >> 7)) & 1) == 0

    def rep(a):
        return jnp.concatenate(
            [a[:, LANES * (hd // 2):LANES * (hd // 2 + 1)] for hd in range(FOX_HEADS)], axis=1)

    qaug_ref[0] = jnp.where(data, rep(proj(OFF_QA, FOX_W)), augq).astype(BF16)
    kaug_ref[0] = jnp.where(data, rep(proj(OFF_KA, FOX_W)), augk).astype(BF16)
    vaug_ref[0] = jnp.where(data, rep(proj(OFF_VA, FOX_W)), 1.0).T.astype(BF16)

    za = proj(OFF_ZA, FOX_W)
    ga_ref[0] = (za * (1.0 / (1.0 + jnp.exp(-za)))).astype(BF16)
    zb = proj(OFF_ZB, SWA_W)
    gb_ref[0] = (zb * (1.0 / (1.0 + jnp.exp(-zb)))).astype(BF16)

    ang = pos_ref[0].astype(F32) * invf_ref[...]
    cosv = jnp.cos(ang)
    sinv = jnp.sin(ang)
    first = (lane & (HEAD_DIM // 2)) == 0
    sin_signed = jnp.where(first, -sinv, sinv)

    def rope(a):
        outs = []
        for cidx in range(a.shape[1] // LANES):
            blk = a[:, cidx * LANES:(cidx + 1) * LANES]
            other = jnp.where(first, pltpu.roll(blk, LANES - HEAD_DIM // 2, 1),
                              pltpu.roll(blk, HEAD_DIM // 2, 1))
            outs.append(blk * cosv + other * sin_signed)
        return jnp.concatenate(outs, axis=1)

    qb_ref[0] = rope(proj(OFF_QB, SWA_W)).astype(BF16)
    kd_ref[0] = rope(proj(OFF_KD, 2 * SWA_KV_W)).astype(BF16)
    vd_ref[0] = proj(OFF_VD, 2 * SWA_KV_W).astype(BF16)


def _inproj_call(x, shift, scale, g_pre, w_perm, pos3, invf, bf_pad, tri, eq, ek, oq, ok):
    B, S, D = x.shape
    tm = TM_IN
    row = lambda b, t: (b, t, 0)
    per_b = lambda b, t: (b, 0, 0)
    const2 = lambda b, t: (0, 0)
    out_w = (AUG_W, AUG_W, AUG_W, FOX_W, SWA_W, 2 * SWA_KV_W, 2 * SWA_KV_W, SWA_W)
    return pl.pallas_call(
        _inproj_kernel,
        grid=(B, S // tm),
        in_specs=[pl.BlockSpec((1, tm, D), row),
                  pl.BlockSpec((1, 1, D), per_b),
                  pl.BlockSpec((1, 1, D), per_b),
                  pl.BlockSpec((1, D), const2),
                  pl.BlockSpec((D, IN_NP), const2),
                  pl.BlockSpec((1, tm, 1), row),
                  pl.BlockSpec((1, LANES), const2),
                  pl.BlockSpec((1, LANES), const2),
                  pl.BlockSpec((tm, tm), const2),
                  pl.BlockSpec((LANES, AUG_W), const2),
                  pl.BlockSpec((LANES, AUG_W), const2),
                  pl.BlockSpec((1, AUG_W), const2),
                  pl.BlockSpec((1, AUG_W), const2)],
        out_specs=[pl.BlockSpec((1, AUG_W, tm), lambda b, t: (b, 0, t)) if i == 2
                   else pl.BlockSpec((1, tm, w), row) for i, w in enumerate(out_w)],
        out_shape=[jax.ShapeDtypeStruct((B, AUG_W, S) if i == 2 else (B, S, w), BF16)
                   for i, w in enumerate(out_w)],
        scratch_shapes=[pltpu.VMEM((1, LANES), F32)],
        compiler_params=pltpu.CompilerParams(
            dimension_semantics=("arbitrary", "arbitrary"), vmem_limit_bytes=VMEM_LIMIT),
        name="in_proj",
    )(x, shift, scale, g_pre, w_perm, pos3, invf, bf_pad, tri, eq, ek, oq, ok)


def _fox_tables(nq):
    diag = [(qi, qi, qi) for qi in range(nq)] + [(0, 0, nq)] * 2
    off = [(qi, j, qi) for qi in range(1, nq) for j in range(qi)] + [(1, 0, nq)] * 2
    tab = np.array(diag + off, np.int32).T
    return tab, len(diag) - 2, len(off) - 2


def _fox_kernel(tab_ref, q_ref, k_ref, v_ref, g_ref, o_ref,
                s0, s1, p0, p1, al0, al1, m_st, acc_st, *, n_diag, n_off):
    tq = TQ_FOX
    tk = TK_FOX
    nq = q_ref.shape[1] // tq
    ntab = tab_ref.shape[1]
    s_buf = (s0, s1)
    p_buf = (p0, p1)
    al_buf = (al0, al1)
    nt = (((1,), (1,)), ((), ()))

    def stage_qk(t, slot):
        qoff = pl.multiple_of(tab_ref[0, t] * tq, tq)
        koff = pl.multiple_of(tab_ref[1, t] * tk, tk)
        for hh in range(2):
            q = q_ref[0, pl.ds(qoff, tq), hh * LANES:(hh + 1) * LANES]
            k = k_ref[0, pl.ds(koff, tk), hh * LANES:(hh + 1) * LANES]
            s_buf[slot][hh] = lax.dot_general(k, q, nt, preferred_element_type=F32)

    def stage_softmax(t, slot, diag):
        st = tab_ref[2, t]
        for hh in range(2):
            s = s_buf[slot][hh]
            if diag:
                kr = lax.broadcasted_iota(jnp.int32, (tk, tq), 0)
                qc = lax.broadcasted_iota(jnp.int32, (tk, tq), 1)
                s = jnp.where(kr <= qc, s, NEG_BIG)
                m_next = jnp.max(s, axis=0, keepdims=True)
            else:
                m_prev = m_st[hh, st]
                m_next = jnp.maximum(m_prev, jnp.max(s, axis=0, keepdims=True))
                al_buf[slot][hh] = jnp.exp2(m_prev - m_next)
            p_buf[slot][hh] = jnp.exp2(s - m_next).astype(BF16)
            m_st[hh, st] = m_next

    def stage_pv(t, slot, diag):
        koff = pl.multiple_of(tab_ref[1, t] * tk, tk)
        st = tab_ref[2, t]
        for hh in range(2):
            vt = v_ref[0, hh * LANES:(hh + 1) * LANES, pl.ds(koff, tk)]
            pv = jnp.dot(vt, p_buf[slot][hh], preferred_element_type=F32)
            if diag:
                acc_st[hh, st] = pv
            else:
                acc_st[hh, st] = al_buf[slot][hh] * acc_st[hh, st] + pv

    def run(base, n, diag):
        stage_qk(base, 0)
        stage_qk(base + 1, 1)
        stage_softmax(base, 0, diag)

        def body(u, carry):
            t = base + 2 * u + 1
            stage_qk(t + 1, 0)
            stage_softmax(t, 1, diag)
            stage_pv(t - 1, 0, diag)
            stage_qk(t + 2, 1)
            stage_softmax(t + 1, 0, diag)
            stage_pv(t, 1, diag)
            return carry

        lax.fori_loop(0, n // 2, body, 0)

    assert n_diag % 2 == 0 and n_off % 2 == 0 and ntab == n_diag + n_off + 4
    run(0, n_diag, True)
    run(n_diag + 2, n_off, False)

    def finish(qi, carry):
        a0 = acc_st[0, qi]
        a1 = acc_st[1, qi]
        ot = jnp.concatenate([a0[:HEAD_DIM] / a0[HEAD_DIM:HEAD_DIM + 1],
                              a1[HEAD_DIM:] / a1[0:1]], axis=0)
        rows = pl.ds(pl.multiple_of(qi * tq, tq), tq)
        o_ref[0, rows, :] = (ot.T * g_ref[0, rows, :].astype(F32)).astype(BF16)
        return carry

    lax.fori_loop(0, nq, finish, 0)


def _fox_call(qaug, kaug, vaug, ga):
    B, S, _ = qaug.shape
    tq, tk = TQ_FOX, TK_FOX
    assert tq == tk
    nq = S // tq
    tab, n_diag, n_off = _fox_tables(nq)
    pair = lambda b, p: (b, 0, p)
    return pl.pallas_call(
        functools.partial(_fox_kernel, n_diag=n_diag, n_off=n_off),
        grid=(B, FOX_HEADS // 2),
        in_specs=[pl.BlockSpec(memory_space=pltpu.SMEM),
                  pl.BlockSpec((1, S, 2 * LANES), pair),
                  pl.BlockSpec((1, S, 2 * LANES), pair),
                  pl.BlockSpec((1, 2 * LANES, S), lambda b, p: (b, p, 0)),
                  pl.BlockSpec((1, S, LANES), pair)],
        out_specs=pl.BlockSpec((1, S, LANES), pair),
        out_shape=jax.ShapeDtypeStruct((B, S, FOX_W), BF16),
        scratch_shapes=[pltpu.VMEM((2, tk, tq), F32), pltpu.VMEM((2, tk, tq), F32),
                        pltpu.VMEM((2, tk, tq), BF16), pltpu.VMEM((2, tk, tq), BF16),
                        pltpu.VMEM((2, 1, tq), F32), pltpu.VMEM((2, 1, tq), F32),
                        pltpu.VMEM((2, nq + 1, 1, tq), F32),
                        pltpu.VMEM((2, nq + 1, LANES, tq), F32)],
        compiler_params=pltpu.CompilerParams(
            dimension_semantics=("arbitrary", "arbitrary"), vmem_limit_bytes=VMEM_LIMIT),
        name="fox",
    )(jnp.asarray(tab), qaug, kaug, vaug, ga)


def _swa_kernel(sinks_ref, q_ref, kp_ref, kc_ref, vp_ref, vc_ref, g_ref, o_ref):
    i = pl.program_id(1)
    nsub = TQ_SWA // WINDOW
    kall = jnp.concatenate([kp_ref[0], kc_ref[0]], axis=0)
    vall = jnp.concatenate([vp_ref[0], vc_ref[0]], axis=0)
    lane = lax.broadcasted_iota(jnp.int32, (WINDOW, LANES), 1)
    lo = lane < HEAD_DIM
    qi_ = lax.broadcasted_iota(jnp.int32, (WINDOW, 2 * WINDOW), 0)
    kj_ = lax.broadcasted_iota(jnp.int32, (WINDOW, 2 * WINDOW), 1)
    rel = qi_ + WINDOW - kj_
    band = (rel >= 0) & (rel < WINDOW)
    zero = jnp.zeros((WINDOW, LANES), BF16)
    group = SWA_Q_HEADS // SWA_KV_HEADS
    for r in range(nsub):
        if r == 0:
            jmin = jnp.where(i == 0, WINDOW, 0)
            valid = band & (kj_ >= jmin)
        else:
            valid = band
        valid4 = jnp.concatenate([valid] * group, axis=0)
        for g in range(SWA_KV_HEADS):
            kg = kall[r * WINDOW:(r + 2) * WINDOW, g * LANES:(g + 1) * LANES]
            vg = vall[r * WINDOW:(r + 2) * WINDOW, g * LANES:(g + 1) * LANES]
            c0 = g * group * HEAD_DIM
            qp0 = q_ref[0, r * WINDOW:(r + 1) * WINDOW, c0:c0 + LANES]
            qp1 = q_ref[0, r * WINDOW:(r + 1) * WINDOW, c0 + LANES:c0 + 2 * LANES]
            qs = jnp.concatenate([jnp.where(lo, qp0, zero), jnp.where(lo, qp1, zero),
                                  jnp.where(lo, zero, qp0), jnp.where(lo, zero, qp1)], axis=0)
            s = lax.dot_general(qs, kg, (((1,), (1,)), ((), ())), preferred_element_type=F32)
            s = jnp.where(valid4, s, NEG_BIG)
            heads = (group * g, group * g + 2, group * g + 1, group * g + 3)
            sink = jnp.concatenate(
                [jnp.full((WINDOW, 1), sinks_ref[hd], F32) for hd in heads], axis=0)
            m = jnp.maximum(jnp.max(s, axis=1, keepdims=True), sink)
            p = jnp.exp(s - m)
            l = jnp.sum(p, axis=1, keepdims=True) + jnp.exp(sink - m)
            o = jnp.dot(p.astype(BF16), vg, preferred_element_type=F32) * (1.0 / l)
            pair0 = jnp.where(lo, o[0:WINDOW], o[2 * WINDOW:3 * WINDOW])
            pair1 = jnp.where(lo, o[WINDOW:2 * WINDOW], o[3 * WINDOW:4 * WINDOW])
            rows = slice(r * WINDOW, (r + 1) * WINDOW)
            o_ref[0, rows, c0:c0 + LANES] = (
                pair0 * g_ref[0, rows, c0:c0 + LANES].astype(F32)).astype(BF16)
            o_ref[0, rows, c0 + LANES:c0 + 2 * LANES] = (
                pair1 * g_ref[0, rows, c0 + LANES:c0 + 2 * LANES].astype(F32)).astype(BF16)


def _swa_call(sinks, qb, kd, vd, gb):
    B, S, _ = qb.shape
    tq = TQ_SWA
    nsub = tq // WINDOW
    cur = lambda b, i: (b, i, 0)
    prev = lambda b, i: (b, jnp.maximum(i * nsub - 1, 0), 0)
    kvw = 2 * SWA_KV_W
    return pl.pallas_call(
        _swa_kernel,
        grid=(B, S // tq),
        in_specs=[pl.BlockSpec(memory_space=pltpu.SMEM),
                  pl.BlockSpec((1, tq, SWA_W), cur),
                  pl.BlockSpec((1, WINDOW, kvw), prev),
                  pl.BlockSpec((1, tq, kvw), cur),
                  pl.BlockSpec((1, WINDOW, kvw), prev),
                  pl.BlockSpec((1, tq, kvw), cur),
                  pl.BlockSpec((1, tq, SWA_W), cur)],
        out_specs=pl.BlockSpec((1, tq, SWA_W), cur),
        out_shape=jax.ShapeDtypeStruct((B, S, SWA_W), BF16),
        compiler_params=pltpu.CompilerParams(
            dimension_semantics=("arbitrary", "arbitrary"), vmem_limit_bytes=VMEM_LIMIT),
        name="swa",
    )(sinks, qb, kd, kd, vd, vd, gb)


def _outproj_kernel(oa_ref, ob_ref, x_ref, gate_ref, gpost_ref, wa_ref, wb_ref, out_ref):
    y = (jnp.dot(oa_ref[0], wa_ref[...], preferred_element_type=F32)
         + jnp.dot(ob_ref[0], wb_ref[...], preferred_element_type=F32))
    ms = jnp.mean(y * y, axis=-1, keepdims=True)
    yn = y * lax.rsqrt(ms + RMS_EPS) * gpost_ref[...]
    out_ref[0] = x_ref[0] + gate_ref[0] * yn


def _outproj_call(oa, ob, x, gate, g_post, wa, wb):
    B, S, D = x.shape
    tm = TM_OUT
    row = lambda b, t: (b, t, 0)
    const2 = lambda b, t: (0, 0)
    return pl.pallas_call(
        _outproj_kernel,
        grid=(B, S // tm),
        in_specs=[pl.BlockSpec((1, tm, FOX_W), row),
                  pl.BlockSpec((1, tm, SWA_W), row),
                  pl.BlockSpec((1, tm, D), row),
                  pl.BlockSpec((1, 1, D), lambda b, t: (b, 0, 0)),
                  pl.BlockSpec((1, D), const2),
                  pl.BlockSpec((FOX_W, D), const2),
                  pl.BlockSpec((SWA_W, D), const2)],
        out_specs=pl.BlockSpec((1, tm, D), row),
        out_shape=jax.ShapeDtypeStruct((B, S, D), F32),
        compiler_params=pltpu.CompilerParams(
            dimension_semantics=("arbitrary", "arbitrary"), vmem_limit_bytes=VMEM_LIMIT),
        name="out_proj",
    )(oa, ob, x, gate, g_post, wa, wb)


def _perm_w_in(w):
    sc = HEAD_DIM ** -0.5
    o = 0
    qa = w[:, o:o + FOX_W]; o += FOX_W
    ka = w[:, o:o + FOX_W]; o += FOX_W
    va = w[:, o:o + FOX_W]; o += FOX_W
    fa = w[:, o:o + FOX_HEADS]; o += FOX_HEADS
    za = w[:, o:o + FOX_W]; o += FOX_W
    qb = w[:, o:o + SWA_W]; o += SWA_W
    kb = w[:, o:o + SWA_KV_W]; o += SWA_KV_W
    vb = w[:, o:o + SWA_KV_W]; o += SWA_KV_W
    zb = w[:, o:o + SWA_W]
    dup = lambda a: jnp.concatenate(
        [a[:, HEAD_DIM * (g // 2):HEAD_DIM * (g // 2 + 1)] for g in range(2 * SWA_KV_HEADS)], axis=1)
    fpad = jnp.concatenate(
        [fa, fa, fa, jnp.zeros((w.shape[0], LANES - 3 * FOX_HEADS), w.dtype)], axis=1)
    return jnp.concatenate([qa * (sc * LOG2E), ka, va, za, qb * sc, zb, dup(kb), dup(vb), fpad],
                           axis=1).astype(BF16)


def _aug_constants():
    eq = np.zeros((LANES, AUG_W), np.float32)
    ek = np.zeros((LANES, AUG_W), np.float32)
    oq = np.zeros((1, AUG_W), np.float32)
    ok = np.zeros((1, AUG_W), np.float32)
    for hd in range(FOX_HEADS):
        base = LANES * hd + (HEAD_DIM if hd % 2 == 0 else 0)
        for part in range(3):
            eq[part * FOX_HEADS + hd, base + part] = 1.0
            ok[0, base + part] = 1.0
            ek[part * FOX_HEADS + hd, base + 3 + part] = -1.0
            oq[0, base + 3 + part] = 1.0
    return (jnp.asarray(eq, BF16), jnp.asarray(ek, BF16), jnp.asarray(oq), jnp.asarray(ok))


def kernel(x, c, positions, w_ada, b_ada, g_pre, w_in, b_fgate, sinks, w_out, g_post):
    B, S, D = x.shape
    depth = w_ada.shape[0]
    half = HEAD_DIM // 2
    inv_freq = ROPE_THETA ** (-jnp.arange(half, dtype=F32) / half)
    invf = jnp.tile(inv_freq, LANES // half)[None, :]
    pos3 = positions[:, :, None]
    tri = jnp.asarray(np.tril(np.ones((TM_IN, TM_IN), np.float32)), BF16)
    eq, ek, oq, ok = _aug_constants()
    c_pad = jnp.zeros((8, D), F32).at[:B].set(c)
    for l in range(depth):
        mod = _mod_call(c_pad, w_ada[l], b_ada[l][None, :])[:B]
        shift = mod[:, None, 0:D]
        scale = mod[:, None, D:2 * D]
        gate = mod[:, None, 2 * D:3 * D]
        bf_pad = jnp.concatenate(
            [b_fgate[l]] * 3 + [jnp.zeros((LANES - 3 * FOX_HEADS,), F32)])[None, :]
        qaug, kaug, vaug, ga, qb, kd, vd, gb = _inproj_call(
            x, shift, scale, g_pre[l][None, :], _perm_w_in(w_in[l]), pos3, invf, bf_pad,
            tri, eq, ek, oq, ok)
        oa = _fox_call(qaug, kaug, vaug, ga)
        ob = _swa_call(sinks[l], qb, kd, vd, gb)
        wo = w_out[l].astype(BF16)
        x = _outproj_call(oa, ob, x, gate, g_post[l][None, :], wo[:FOX_W], wo[FOX_W:])
    return x
```

```python
import functools

import jax
import jax.numpy as jnp
import numpy as np
from jax import lax
from jax.experimental import pallas as pl
from jax.experimental.pallas import tpu as pltpu

D_MODEL = 1024
HEAD_DIM = 64
FOX_HEADS = 8
SWA_Q_HEADS = 8
SWA_KV_HEADS = 2
WINDOW = 128
ROPE_THETA = 10000.0
RMS_EPS = 1e-6
FOX_W = FOX_HEADS * HEAD_DIM
SWA_W = SWA_Q_HEADS * HEAD_DIM
SWA_KV_W = SWA_KV_HEADS * HEAD_DIM

LANES = 128
AUG_W = FOX_HEADS * LANES

OFF_QA = 0
OFF_KA = OFF_QA + FOX_W
OFF_VA = OFF_KA + FOX_W
OFF_ZA = OFF_VA + FOX_W
OFF_QB = OFF_ZA + FOX_W
OFF_ZB = OFF_QB + SWA_W
OFF_KD = OFF_ZB + SWA_W
OFF_VD = OFF_KD + 2 * SWA_KV_W
OFF_F = OFF_VD + 2 * SWA_KV_W
IN_NP = OFF_F + LANES

TM_IN = 512
TQ_FOX = 512
TK_FOX = 512
TQ_SWA = 512
TM_OUT = 512
NEG_BIG = -1e30
LOG2E = 1.4426950408889634
VMEM_LIMIT = 56 * 1024 * 1024

F32 = jnp.float32
BF16 = jnp.bfloat16


def _split3(a):
    hi = a.astype(BF16)
    r = a - hi.astype(F32)
    mid = r.astype(BF16)
    lo = (r - mid.astype(F32)).astype(BF16)
    return hi, mid, lo


def _mod_kernel(c_ref, w_ref, b_ref, o_ref):
    c = c_ref[...]
    sc = c * (1.0 / (1.0 + jnp.exp(-c)))
    o_ref[...] = jnp.dot(sc, w_ref[...], precision=lax.Precision.HIGHEST,
                         preferred_element_type=F32) + b_ref[...]


def _mod_call(c_pad, w_ada, b_ada):
    rows = c_pad.shape[0]
    n = w_ada.shape[1]
    bn = D_MODEL
    return pl.pallas_call(
        _mod_kernel,
        grid=(n // bn,),
        in_specs=[pl.BlockSpec((rows, D_MODEL), lambda j: (0, 0)),
                  pl.BlockSpec((D_MODEL, bn), lambda j: (0, j)),
                  pl.BlockSpec((1, bn), lambda j: (0, j))],
        out_specs=pl.BlockSpec((rows, bn), lambda j: (0, j)),
        out_shape=jax.ShapeDtypeStruct((rows, n), F32),
        name="mod",
    )(c_pad, w_ada, b_ada)


def _inproj_kernel(x_ref, shift_ref, scale_ref, gpre_ref, w_ref, pos_ref, invf_ref, bf_ref,
                   tri_ref, eq_ref, ek_ref, oq_ref, ok_ref,
                   qaug_ref, kaug_ref, vaug_ref, ga_ref, qb_ref, kd_ref, vd_ref, gb_ref,
                   carry_ref):
    tm = x_ref.shape[1]
    t = pl.program_id(1)

    x = x_ref[0]
    ms = jnp.mean(x * x, axis=-1, keepdims=True)
    y = x * lax.rsqrt(ms + RMS_EPS) * gpre_ref[...]
    h = (y * (1.0 + scale_ref[0]) + shift_ref[0]).astype(BF16)

    def proj(off, width):
        return jnp.dot(h, w_ref[:, off:off + width], preferred_element_type=F32)

    f = proj(OFF_F, LANES) + bf_ref[...]
    ls = jnp.minimum(f, 0.0) - jnp.log1p(jnp.exp(-jnp.abs(f)))
    tri = tri_ref[...]
    cum = None
    for part in _split3(ls):
        d = jnp.dot(tri, part, preferred_element_type=F32)
        cum = d if cum is None else cum + d

    @pl.when(t == 0)
    def _():
        carry_ref[...] = jnp.zeros_like(carry_ref)

    cum = cum + carry_ref[...]
    carry_ref[...] = cum[tm - 1:tm, :]

    hi, mid, lo = _split3(cum * LOG2E)
    lane = lax.broadcasted_iota(jnp.int32, (tm, LANES), 1)
    cs = jnp.where(lane < FOX_HEADS, hi.astype(F32),
                   jnp.where(lane < 2 * FOX_HEADS, mid.astype(F32), lo.astype(F32))).astype(BF16)
    augq = jnp.dot(cs, eq_ref[...], preferred_element_type=F32) + oq_ref[...]
    augk = jnp.dot(cs, ek_ref[...], preferred_element_type=F32) + ok_ref[...]

    lane_w = lax.broadcasted_iota(jnp.int32, (tm, AUG_W), 1)
    data = (((lane_w >> 6) ^ (lane_w >> 7)) & 1) == 0

    def rep(a):
        return jnp.concatenate(
            [a[:, LANES * (hd // 2):LANES * (hd // 2 + 1)] for hd in range(FOX_HEADS)], axis=1)

    qaug_ref[0] = jnp.where(data, rep(proj(OFF_QA, FOX_W)), augq).astype(BF16)
    kaug_ref[0] = jnp.where(data, rep(proj(OFF_KA, FOX_W)), augk).astype(BF16)
    vaug_ref[0] = jnp.where(data, rep(proj(OFF_VA, FOX_W)), 1.0).T.astype(BF16)

    za = proj(OFF_ZA, FOX_W)
    ga_ref[0] = (za * (1.0 / (1.0 + jnp.exp(-za)))).astype(BF16)
    zb = proj(OFF_ZB, SWA_W)
    gb_ref[0] = (zb * (1.0 / (1.0 + jnp.exp(-zb)))).astype(BF16)

    ang = pos_ref[0].astype(F32) * invf_ref[...]
    cosv = jnp.cos(ang)
    sinv = jnp.sin(ang)
    first = (lane & (HEAD_DIM // 2)) == 0
    sin_signed = jnp.where(first, -sinv, sinv)

    def rope(a):
        outs = []
        for cidx in range(a.shape[1] // LANES):
            blk = a[:, cidx * LANES:(cidx + 1) * LANES]
            other = jnp.where(first, pltpu.roll(blk, LANES - HEAD_DIM // 2, 1),
                              pltpu.roll(blk, HEAD_DIM // 2, 1))
            outs.append(blk * cosv + other * sin_signed)
        return jnp.concatenate(outs, axis=1)

    qb_ref[0] = rope(proj(OFF_QB, SWA_W)).astype(BF16)
    kd_ref[0] = rope(proj(OFF_KD, 2 * SWA_KV_W)).astype(BF16)
    lane_v = lax.broadcasted_iota(jnp.int32, (tm, 2 * SWA_KV_W), 1)
    vd = jnp.where((lane_v & HEAD_DIM) == 0, proj(OFF_VD, 2 * SWA_KV_W), 1.0)
    vd_ref[0] = vd.T.astype(BF16)


def _inproj_call(x, shift, scale, g_pre, w_perm, pos3, invf, bf_pad, tri, eq, ek, oq, ok):
    B, S, D = x.shape
    tm = TM_IN
    row = lambda b, t: (b, t, 0)
    per_b = lambda b, t: (b, 0, 0)
    const2 = lambda b, t: (0, 0)
    out_w = (AUG_W, AUG_W, AUG_W, FOX_W, SWA_W, 2 * SWA_KV_W, 2 * SWA_KV_W, SWA_W)
    transposed = (2, 6)
    return pl.pallas_call(
        _inproj_kernel,
        grid=(B, S // tm),
        in_specs=[pl.BlockSpec((1, tm, D), row),
                  pl.BlockSpec((1, 1, D), per_b),
                  pl.BlockSpec((1, 1, D), per_b),
                  pl.BlockSpec((1, D), const2),
                  pl.BlockSpec((D, IN_NP), const2),
                  pl.BlockSpec((1, tm, 1), row),
                  pl.BlockSpec((1, LANES), const2),
                  pl.BlockSpec((1, LANES), const2),
                  pl.BlockSpec((tm, tm), const2),
                  pl.BlockSpec((LANES, AUG_W), const2),
                  pl.BlockSpec((LANES, AUG_W), const2),
                  pl.BlockSpec((1, AUG_W), const2),
                  pl.BlockSpec((1, AUG_W), const2)],
        out_specs=[pl.BlockSpec((1, w, tm), lambda b, t: (b, 0, t)) if i in transposed
                   else pl.BlockSpec((1, tm, w), row) for i, w in enumerate(out_w)],
        out_shape=[jax.ShapeDtypeStruct((B, w, S) if i in transposed else (B, S, w), BF16)
                   for i, w in enumerate(out_w)],
        scratch_shapes=[pltpu.VMEM((1, LANES), F32)],
        compiler_params=pltpu.CompilerParams(
            dimension_semantics=("arbitrary", "arbitrary"), vmem_limit_bytes=VMEM_LIMIT),
        name="in_proj",
    )(x, shift, scale, g_pre, w_perm, pos3, invf, bf_pad, tri, eq, ek, oq, ok)


def _fox_tables(nq):
    diag = [(qi, qi, qi) for qi in range(nq)] + [(0, 0, nq)] * 2
    off = [(qi, j, qi) for qi in range(1, nq) for j in range(qi)] + [(1, 0, nq)] * 2
    tab = np.array(diag + off, np.int32).T
    return tab, len(diag) - 2, len(off) - 2


def _fox_kernel(tab_ref, q_ref, k_ref, v_ref, g_ref, o_ref,
                s0, s1, p0, p1, al0, al1, m_st, acc_st, *, n_diag, n_off):
    tq = TQ_FOX
    tk = TK_FOX
    nq = q_ref.shape[1] // tq
    ntab = tab_ref.shape[1]
    s_buf = (s0, s1)
    p_buf = (p0, p1)
    al_buf = (al0, al1)
    nt = (((1,), (1,)), ((), ()))

    def stage_qk(t, slot):
        qoff = pl.multiple_of(tab_ref[0, t] * tq, tq)
        koff = pl.multiple_of(tab_ref[1, t] * tk, tk)
        for hh in range(2):
            q = q_ref[0, pl.ds(qoff, tq), hh * LANES:(hh + 1) * LANES]
            k = k_ref[0, pl.ds(koff, tk), hh * LANES:(hh + 1) * LANES]
            s_buf[slot][hh] = lax.dot_general(k, q, nt, preferred_element_type=F32)

    def stage_softmax(t, slot, diag):
        st = tab_ref[2, t]
        for hh in range(2):
            s = s_buf[slot][hh]
            if diag:
                kr = lax.broadcasted_iota(jnp.int32, (tk, tq), 0)
                qc = lax.broadcasted_iota(jnp.int32, (tk, tq), 1)
                s = jnp.where(kr <= qc, s, NEG_BIG)
                m_next = jnp.max(s, axis=0, keepdims=True)
            else:
                m_prev = m_st[hh, st]
                m_next = jnp.maximum(m_prev, jnp.max(s, axis=0, keepdims=True))
                al_buf[slot][hh] = jnp.exp2(m_prev - m_next)
            p_buf[slot][hh] = jnp.exp2(s - m_next).astype(BF16)
            m_st[hh, st] = m_next

    def stage_pv(t, slot, diag):
        koff = pl.multiple_of(tab_ref[1, t] * tk, tk)
        st = tab_ref[2, t]
        for hh in range(2):
            vt = v_ref[0, hh * LANES:(hh + 1) * LANES, pl.ds(koff, tk)]
            pv = jnp.dot(vt, p_buf[slot][hh], preferred_element_type=F32)
            if diag:
                acc_st[hh, st] = pv
            else:
                acc_st[hh, st] = al_buf[slot][hh] * acc_st[hh, st] + pv

    def run(base, n, diag):
        stage_qk(base, 0)
        stage_qk(base + 1, 1)
        stage_softmax(base, 0, diag)

        def body(u, carry):
            t = base + 2 * u + 1
            stage_qk(t + 1, 0)
            stage_softmax(t, 1, diag)
            stage_pv(t - 1, 0, diag)
            stage_qk(t + 2, 1)
            stage_softmax(t + 1, 0, diag)
            stage_pv(t, 1, diag)
            return carry

        lax.fori_loop(0, n // 2, body, 0)

    assert n_diag % 2 == 0 and n_off % 2 == 0 and ntab == n_diag + n_off + 4
    run(0, n_diag, True)
    run(n_diag + 2, n_off, False)

    def finish(qi, carry):
        a0 = acc_st[0, qi]
        a1 = acc_st[1, qi]
        ot = jnp.concatenate([a0[:HEAD_DIM] / a0[HEAD_DIM:HEAD_DIM + 1],
                              a1[HEAD_DIM:] / a1[0:1]], axis=0)
        rows = pl.ds(pl.multiple_of(qi * tq, tq), tq)
        o_ref[0, rows, :] = (ot.T * g_ref[0, rows, :].astype(F32)).astype(BF16)
        return carry

    lax.fori_loop(0, nq, finish, 0)


def _fox_call(qaug, kaug, vaug, ga):
    B, S, _ = qaug.shape
    tq, tk = TQ_FOX, TK_FOX
    assert tq == tk
    nq = S // tq
    tab, n_diag, n_off = _fox_tables(nq)
    pair = lambda b, p: (b, 0, p)
    return pl.pallas_call(
        functools.partial(_fox_kernel, n_diag=n_diag, n_off=n_off),
        grid=(B, FOX_HEADS // 2),
        in_specs=[pl.BlockSpec(memory_space=pltpu.SMEM),
                  pl.BlockSpec((1, S, 2 * LANES), pair),
                  pl.BlockSpec((1, S, 2 * LANES), pair),
                  pl.BlockSpec((1, 2 * LANES, S), lambda b, p: (b, p, 0)),
                  pl.BlockSpec((1, S, LANES), pair)],
        out_specs=pl.BlockSpec((1, S, LANES), pair),
        out_shape=jax.ShapeDtypeStruct((B, S, FOX_W), BF16),
        scratch_shapes=[pltpu.VMEM((2, tk, tq), F32), pltpu.VMEM((2, tk, tq), F32),
                        pltpu.VMEM((2, tk, tq), BF16), pltpu.VMEM((2, tk, tq), BF16),
                        pltpu.VMEM((2, 1, tq), F32), pltpu.VMEM((2, 1, tq), F32),
                        pltpu.VMEM((2, nq + 1, 1, tq), F32),
                        pltpu.VMEM((2, nq + 1, LANES, tq), F32)],
        compiler_params=pltpu.CompilerParams(
            dimension_semantics=("arbitrary", "arbitrary"), vmem_limit_bytes=VMEM_LIMIT),
        name="fox",
    )(jnp.asarray(tab), qaug, kaug, vaug, ga)


def _swa_kernel(sinks_ref, q_ref, kp_ref, kc_ref, vtp_ref, vtc_ref, g_ref, o_ref):
    i = pl.program_id(1)
    nsub = TQ_SWA // WINDOW
    group = SWA_Q_HEADS // SWA_KV_HEADS
    ncol = group * WINDOW
    kall = jnp.concatenate([kp_ref[0], kc_ref[0]], axis=0)
    vtall = jnp.concatenate([vtp_ref[0], vtc_ref[0]], axis=1)
    lane = lax.broadcasted_iota(jnp.int32, (WINDOW, LANES), 1)
    lo = lane < HEAD_DIM
    zero = jnp.zeros((WINDOW, LANES), BF16)
    kj_ = lax.broadcasted_iota(jnp.int32, (2 * WINDOW, ncol), 0)
    qi_ = lax.broadcasted_iota(jnp.int32, (2 * WINDOW, ncol), 1) & (WINDOW - 1)
    rel = qi_ + WINDOW - kj_
    band = (rel >= 0) & (rel < WINDOW)
    nt = (((1,), (1,)), ((), ()))
    for r in range(nsub):
        if r == 0:
            jmin = jnp.where(i == 0, WINDOW, 0)
            valid = band & (kj_ >= jmin)
        else:
            valid = band
        rows = slice(r * WINDOW, (r + 1) * WINDOW)
        for g in range(SWA_KV_HEADS):
            kg = kall[r * WINDOW:(r + 2) * WINDOW, g * LANES:(g + 1) * LANES]
            vtg = vtall[g * LANES:(g + 1) * LANES, r * WINDOW:(r + 2) * WINDOW]
            c0 = g * group * HEAD_DIM
            qp0 = q_ref[0, rows, c0:c0 + LANES]
            qp1 = q_ref[0, rows, c0 + LANES:c0 + 2 * LANES]
            qs = jnp.concatenate([jnp.where(lo, qp0, zero), jnp.where(lo, qp1, zero),
                                  jnp.where(lo, zero, qp0), jnp.where(lo, zero, qp1)], axis=0)
            st = lax.dot_general(kg, qs, nt, preferred_element_type=F32)
            st = jnp.where(valid, st, NEG_BIG)
            heads = (group * g, group * g + 2, group * g + 1, group * g + 3)
            sink = jnp.concatenate(
                [jnp.full((1, WINDOW), sinks_ref[hd] * LOG2E, F32) for hd in heads], axis=1)
            m = jnp.maximum(jnp.max(st, axis=0, keepdims=True), sink)
            p = jnp.exp2(st - m).astype(BF16)
            acc = jnp.dot(vtg, p, preferred_element_type=F32)
            l = acc[HEAD_DIM:HEAD_DIM + 1] + jnp.exp2(sink - m)
            on = acc[:HEAD_DIM] * (1.0 / l)
            pair0 = jnp.concatenate([on[:, 0:WINDOW], on[:, 2 * WINDOW:3 * WINDOW]], axis=0).T
            pair1 = jnp.concatenate([on[:, WINDOW:2 * WINDOW], on[:, 3 * WINDOW:]], axis=0).T
            o_ref[0, rows, c0:c0 + LANES] = (
                pair0 * g_ref[0, rows, c0:c0 + LANES].astype(F32)).astype(BF16)
            o_ref[0, rows, c0 + LANES:c0 + 2 * LANES] = (
                pair1 * g_ref[0, rows, c0 + LANES:c0 + 2 * LANES].astype(F32)).astype(BF16)


def _swa_call(sinks, qb, kd, vdt, gb):
    B, S, _ = qb.shape
    tq = TQ_SWA
    nsub = tq // WINDOW
    cur = lambda b, i: (b, i, 0)
    prev = lambda b, i: (b, jnp.maximum(i * nsub - 1, 0), 0)
    cur_t = lambda b, i: (b, 0, i)
    prev_t = lambda b, i: (b, 0, jnp.maximum(i * nsub - 1, 0))
    kvw = 2 * SWA_KV_W
    return pl.pallas_call(
        _swa_kernel,
        grid=(B, S // tq),
        in_specs=[pl.BlockSpec(memory_space=pltpu.SMEM),
                  pl.BlockSpec((1, tq, SWA_W), cur),
                  pl.BlockSpec((1, WINDOW, kvw), prev),
                  pl.BlockSpec((1, tq, kvw), cur),
                  pl.BlockSpec((1, kvw, WINDOW), prev_t),
                  pl.BlockSpec((1, kvw, tq), cur_t),
                  pl.BlockSpec((1, tq, SWA_W), cur)],
        out_specs=pl.BlockSpec((1, tq, SWA_W), cur),
        out_shape=jax.ShapeDtypeStruct((B, S, SWA_W), BF16),
        compiler_params=pltpu.CompilerParams(
            dimension_semantics=("arbitrary", "arbitrary"), vmem_limit_bytes=VMEM_LIMIT),
        name="swa",
    )(sinks, qb, kd, kd, vdt, vdt, gb)


def _outproj_kernel(oa_ref, ob_ref, x_ref, gate_ref, gpost_ref, wa_ref, wb_ref, out_ref):
    y = (jnp.dot(oa_ref[0], wa_ref[...], preferred_element_type=F32)
         + jnp.dot(ob_ref[0], wb_ref[...], preferred_element_type=F32))
    ms = jnp.mean(y * y, axis=-1, keepdims=True)
    yn = y * lax.rsqrt(ms + RMS_EPS) * gpost_ref[...]
    out_ref[0] = x_ref[0] + gate_ref[0] * yn


def _outproj_call(oa, ob, x, gate, g_post, wa, wb):
    B, S, D = x.shape
    tm = TM_OUT
    row = lambda b, t: (b, t, 0)
    const2 = lambda b, t: (0, 0)
    return pl.pallas_call(
        _outproj_kernel,
        grid=(B, S // tm),
        in_specs=[pl.BlockSpec((1, tm, FOX_W), row),
                  pl.BlockSpec((1, tm, SWA_W), row),
                  pl.BlockSpec((1, tm, D), row),
                  pl.BlockSpec((1, 1, D), lambda b, t: (b, 0, 0)),
                  pl.BlockSpec((1, D), const2),
                  pl.BlockSpec((FOX_W, D), const2),
                  pl.BlockSpec((SWA_W, D), const2)],
        out_specs=pl.BlockSpec((1, tm, D), row),
        out_shape=jax.ShapeDtypeStruct((B, S, D), F32),
        compiler_params=pltpu.CompilerParams(
            dimension_semantics=("arbitrary", "arbitrary"), vmem_limit_bytes=VMEM_LIMIT),
        name="out_proj",
    )(oa, ob, x, gate, g_post, wa, wb)


def _perm_w_in(w):
    sc = HEAD_DIM ** -0.5
    o = 0
    qa = w[:, o:o + FOX_W]; o += FOX_W
    ka = w[:, o:o + FOX_W]; o += FOX_W
    va = w[:, o:o + FOX_W]; o += FOX_W
    fa = w[:, o:o + FOX_HEADS]; o += FOX_HEADS
    za = w[:, o:o + FOX_W]; o += FOX_W
    qb = w[:, o:o + SWA_W]; o += SWA_W
    kb = w[:, o:o + SWA_KV_W]; o += SWA_KV_W
    vb = w[:, o:o + SWA_KV_W]; o += SWA_KV_W
    zb = w[:, o:o + SWA_W]
    dup = lambda a: jnp.concatenate(
        [a[:, HEAD_DIM * (g // 2):HEAD_DIM * (g // 2 + 1)] for g in range(2 * SWA_KV_HEADS)], axis=1)
    fpad = jnp.concatenate(
        [fa, fa, fa, jnp.zeros((w.shape[0], LANES - 3 * FOX_HEADS), w.dtype)], axis=1)
    sc2 = sc * LOG2E
    return jnp.concatenate([qa * sc2, ka, va, za, qb * sc2, zb, dup(kb), dup(vb), fpad],
                           axis=1).astype(BF16)


def _aug_constants():
    eq = np.zeros((LANES, AUG_W), np.float32)
    ek = np.zeros((LANES, AUG_W), np.float32)
    oq = np.zeros((1, AUG_W), np.float32)
    ok = np.zeros((1, AUG_W), np.float32)
    for hd in range(FOX_HEADS):
        base = LANES * hd + (HEAD_DIM if hd % 2 == 0 else 0)
        for part in range(3):
            eq[part * FOX_HEADS + hd, base + part] = 1.0
            ok[0, base + part] = 1.0
            ek[part * FOX_HEADS + hd, base + 3 + part] = -1.0
            oq[0, base + 3 + part] = 1.0
    return (jnp.asarray(eq, BF16), jnp.asarray(ek, BF16), jnp.asarray(oq), jnp.asarray(ok))


def kernel(x, c, positions, w_ada, b_ada, g_pre, w_in, b_fgate, sinks, w_out, g_post):
    B, S, D = x.shape
    depth = w_ada.shape[0]
    half = HEAD_DIM // 2
    inv_freq = ROPE_THETA ** (-jnp.arange(half, dtype=F32) / half)
    invf = jnp.tile(inv_freq, LANES // half)[None, :]
    pos3 = positions[:, :, None]
    tri = jnp.asarray(np.tril(np.ones((TM_IN, TM_IN), np.float32)), BF16)
    eq, ek, oq, ok = _aug_constants()
    c_pad = jnp.zeros((8, D), F32).at[:B].set(c)
    for l in range(depth):
        mod = _mod_call(c_pad, w_ada[l], b_ada[l][None, :])[:B]
        shift = mod[:, None, 0:D]
        scale = mod[:, None, D:2 * D]
        gate = mod[:, None, 2 * D:3 * D]
        bf_pad = jnp.concatenate(
            [b_fgate[l]] * 3 + [jnp.zeros((LANES - 3 * FOX_HEADS,), F32)])[None, :]
        qaug, kaug, vaug, ga, qb, kd, vd, gb = _inproj_call(
            x, shift, scale, g_pre[l][None, :], _perm_w_in(w_in[l]), pos3, invf, bf_pad,
            tri, eq, ek, oq, ok)
        oa = _fox_call(qaug, kaug, vaug, ga)
        ob = _swa_call(sinks[l], qb, kd, vd, gb)
        wo = w_out[l].astype(BF16)
        x = _outproj_call(oa, ob, x, gate, g_post[l][None, :], wo[:FOX_W], wo[FOX_W:])
    return x
```

```python
import functools

import jax
import jax.numpy as jnp
import numpy as np
from jax import lax
from jax.experimental import pallas as pl
from jax.experimental.pallas import tpu as pltpu

D_MODEL = 1024
HEAD_DIM = 64
FOX_HEADS = 8
SWA_Q_HEADS = 8
SWA_KV_HEADS = 2
WINDOW = 128
ROPE_THETA = 10000.0
RMS_EPS = 1e-6
FOX_W = FOX_HEADS * HEAD_DIM
SWA_W = SWA_Q_HEADS * HEAD_DIM
SWA_KV_W = SWA_KV_HEADS * HEAD_DIM

LANES = 128
AUG_W = FOX_HEADS * LANES

OFF_QA = 0
OFF_KA = OFF_QA + FOX_W
OFF_VA = OFF_KA + FOX_W
OFF_ZA = OFF_VA + FOX_W
OFF_QB = OFF_ZA + FOX_W
OFF_ZB = OFF_QB + SWA_W
OFF_KD = OFF_ZB + SWA_W
OFF_VD = OFF_KD + 2 * SWA_KV_W
OFF_F = OFF_VD + 2 * SWA_KV_W
IN_NP = OFF_F + LANES

TM_IN = 512
TQ_FOX = 512
TK_FOX = 512
TQ_SWA = 512
TM_OUT = 512
NEG_BIG = -1e30
LOG2E = 1.4426950408889634
PRUNE_T = 140.0
NORM_MARGIN = 1.01
VMEM_LIMIT = 56 * 1024 * 1024

F32 = jnp.float32
BF16 = jnp.bfloat16


def _split3(a):
    hi = a.astype(BF16)
    r = a - hi.astype(F32)
    mid = r.astype(BF16)
    lo = (r - mid.astype(F32)).astype(BF16)
    return hi, mid, lo


def _mod_kernel(c_ref, w_ref, b_ref, o_ref):
    c = c_ref[...]
    sc = c * (1.0 / (1.0 + jnp.exp(-c)))
    o_ref[...] = jnp.dot(sc, w_ref[...], precision=lax.Precision.HIGHEST,
                         preferred_element_type=F32) + b_ref[...]


def _mod_call(c_pad, w_ada, b_ada):
    rows = c_pad.shape[0]
    n = w_ada.shape[1]
    bn = D_MODEL
    return pl.pallas_call(
        _mod_kernel,
        grid=(n // bn,),
        in_specs=[pl.BlockSpec((rows, D_MODEL), lambda j: (0, 0)),
                  pl.BlockSpec((D_MODEL, bn), lambda j: (0, j)),
                  pl.BlockSpec((1, bn), lambda j: (0, j))],
        out_specs=pl.BlockSpec((rows, bn), lambda j: (0, j)),
        out_shape=jax.ShapeDtypeStruct((rows, n), F32),
        name="mod",
    )(c_pad, w_ada, b_ada)


def _inproj_kernel(x_ref, shift_ref, scale_ref, gpre_ref, w_ref, pos_ref, invf_ref, bf_ref,
                   tri_ref, eq_ref, ek_ref, oq_ref, ok_ref, ind_ref,
                   qaug_ref, kaug_ref, vaug_ref, ga_ref, qb_ref, kd_ref, vd_ref, gb_ref, st_ref,
                   carry_ref):
    tm = x_ref.shape[1]

    @pl.when(pl.program_id(1) == 0)
    def _():
        carry_ref[...] = jnp.zeros_like(carry_ref)

    x = x_ref[0]
    ms = jnp.mean(x * x, axis=-1, keepdims=True)
    y = x * lax.rsqrt(ms + RMS_EPS) * gpre_ref[...]
    h = (y * (1.0 + scale_ref[0]) + shift_ref[0]).astype(BF16)

    def proj(off, width):
        return jnp.dot(h, w_ref[:, off:off + width], preferred_element_type=F32)

    f = proj(OFF_F, LANES) + bf_ref[...]
    ls = jnp.minimum(f, 0.0) - jnp.log1p(jnp.exp(-jnp.abs(f)))
    tri = tri_ref[...]
    cum = None
    for part in _split3(ls):
        d = jnp.dot(tri, part, preferred_element_type=F32)
        cum = d if cum is None else cum + d

    cum = cum + carry_ref[...]
    carry_ref[...] = cum[tm - 1:tm, :]

    cum2 = cum * LOG2E
    hi, mid, lo = _split3(cum2)
    lane = lax.broadcasted_iota(jnp.int32, (tm, LANES), 1)
    cs = jnp.where(lane < FOX_HEADS, hi.astype(F32),
                   jnp.where(lane < 2 * FOX_HEADS, mid.astype(F32), lo.astype(F32))).astype(BF16)
    augq = jnp.dot(cs, eq_ref[...], preferred_element_type=F32) + oq_ref[...]
    augk = jnp.dot(cs, ek_ref[...], preferred_element_type=F32) + ok_ref[...]

    lane_w = lax.broadcasted_iota(jnp.int32, (tm, AUG_W), 1)
    data = (((lane_w >> 6) ^ (lane_w >> 7)) & 1) == 0

    def rep(a):
        return jnp.concatenate(
            [a[:, LANES * (hd // 2):LANES * (hd // 2 + 1)] for hd in range(FOX_HEADS)], axis=1)

    qa = proj(OFF_QA, FOX_W)
    ka = proj(OFF_KA, FOX_W)
    qaug_ref[0] = jnp.where(data, rep(qa), augq).T.astype(BF16)
    kaug_ref[0] = jnp.where(data, rep(ka), augk).astype(BF16)

    q16 = qa.astype(BF16).astype(F32)
    k16 = ka.astype(BF16).astype(F32)
    sq = jnp.concatenate([q16 * q16, k16 * k16], axis=1).astype(BF16)
    nrm2 = jnp.dot(sq, ind_ref[...], preferred_element_type=F32)
    nmax = jnp.sqrt(jnp.max(nrm2, axis=0, keepdims=True)) * NORM_MARGIN
    st_ref[0, 0] = jnp.concatenate(
        [nmax, cum2[0:1], cum2[tm - 1:tm], jnp.zeros((5, LANES), F32)], axis=0)
    vaug_ref[0] = jnp.where(data, rep(proj(OFF_VA, FOX_W)), 1.0).T.astype(BF16)

    za = proj(OFF_ZA, FOX_W)
    ga_ref[0] = (za * (1.0 / (1.0 + jnp.exp(-za)))).astype(BF16)
    zb = proj(OFF_ZB, SWA_W)
    gb_ref[0] = (zb * (1.0 / (1.0 + jnp.exp(-zb)))).astype(BF16)

    ang_t = invf_ref[...] * pos_ref[0].astype(F32)
    cosv = jnp.cos(ang_t).T
    sinv = jnp.sin(ang_t).T
    first = (lane & (HEAD_DIM // 2)) == 0
    sin_signed = jnp.where(first, -sinv, sinv)

    def rope(a):
        outs = []
        for cidx in range(a.shape[1] // LANES):
            blk = a[:, cidx * LANES:(cidx + 1) * LANES]
            other = jnp.where(first, pltpu.roll(blk, LANES - HEAD_DIM // 2, 1),
                              pltpu.roll(blk, HEAD_DIM // 2, 1))
            outs.append(blk * cosv + other * sin_signed)
        return jnp.concatenate(outs, axis=1)

    qb_ref[0] = rope(proj(OFF_QB, SWA_W)).astype(BF16)
    kd_ref[0] = rope(proj(OFF_KD, 2 * SWA_KV_W)).astype(BF16)
    lane_v = lax.broadcasted_iota(jnp.int32, (tm, 2 * SWA_KV_W), 1)
    vd = jnp.where((lane_v & HEAD_DIM) == 0, proj(OFF_VD, 2 * SWA_KV_W), 1.0)
    vd_ref[0] = vd.T.astype(BF16)


def _inproj_call(x, shift, scale, g_pre, w_perm, pos3, invf, bf_pad, tri, eq, ek, oq, ok, ind):
    B, S, D = x.shape
    tm = TM_IN
    row = lambda b, t: (b, t, 0)
    per_b = lambda b, t: (b, 0, 0)
    const2 = lambda b, t: (0, 0)
    out_w = (AUG_W, AUG_W, AUG_W, FOX_W, SWA_W, 2 * SWA_KV_W, 2 * SWA_KV_W, SWA_W)
    transposed = (0, 2, 6)
    return pl.pallas_call(
        _inproj_kernel,
        grid=(B, S // tm),
        in_specs=[pl.BlockSpec((1, tm, D), row),
                  pl.BlockSpec((1, 1, D), per_b),
                  pl.BlockSpec((1, 1, D), per_b),
                  pl.BlockSpec((1, D), const2),
                  pl.BlockSpec((D, IN_NP), const2),
                  pl.BlockSpec((1, 1, tm), lambda b, t: (b, 0, t)),
                  pl.BlockSpec((LANES, 1), const2),
                  pl.BlockSpec((1, LANES), const2),
                  pl.BlockSpec((tm, tm), const2),
                  pl.BlockSpec((LANES, AUG_W), const2),
                  pl.BlockSpec((LANES, AUG_W), const2),
                  pl.BlockSpec((1, AUG_W), const2),
                  pl.BlockSpec((1, AUG_W), const2),
                  pl.BlockSpec((2 * FOX_W, LANES), const2)],
        out_specs=[pl.BlockSpec((1, w, tm), lambda b, t: (b, 0, t)) if i in transposed
                   else pl.BlockSpec((1, tm, w), row) for i, w in enumerate(out_w)]
        + [pl.BlockSpec((1, 1, 8, LANES), lambda b, t: (b, t, 0, 0))],
        out_shape=[jax.ShapeDtypeStruct((B, w, S) if i in transposed else (B, S, w), BF16)
                   for i, w in enumerate(out_w)]
        + [jax.ShapeDtypeStruct((B, S // tm, 8, LANES), F32)],
        scratch_shapes=[pltpu.VMEM((1, LANES), F32)],
        compiler_params=pltpu.CompilerParams(
            dimension_semantics=("arbitrary", "arbitrary"), vmem_limit_bytes=VMEM_LIMIT),
        name="in_proj",
    )(x, shift, scale, g_pre, w_perm, pos3, invf, bf_pad, tri, eq, ek, oq, ok, ind)


def _fox_tables(nq):
    diag = [(qi, qi, qi) for qi in range(nq)] + [(0, 0, nq)] * 2
    tab = np.array(diag, np.int32).T
    return tab, nq, nq * (nq - 1) // 2


def _fox_kernel(tab_ref, st_ref, q_ref, k_ref, v_ref, g_ref, o_ref,
                off_tab, s0, s1, p0, p1, al0, al1, m_st, acc_st, *, n_diag):
    tq = TQ_FOX
    tk = TK_FOX
    nq = q_ref.shape[2] // tq
    s_buf = (s0, s1)
    p_buf = (p0, p1)
    al_buf = (al0, al1)

    def stage_qk(tab, t, slot):
        qoff = pl.multiple_of(tab[0, t] * tq, tq)
        koff = pl.multiple_of(tab[1, t] * tk, tk)
        for hh in range(2):
            qt = q_ref[0, hh * LANES:(hh + 1) * LANES, pl.ds(qoff, tq)]
            k = k_ref[0, pl.ds(koff, tk), hh * LANES:(hh + 1) * LANES]
            s_buf[slot][hh] = jnp.dot(k, qt, preferred_element_type=F32)

    def stage_softmax(tab, t, slot, diag):
        st = tab[2, t]
        for hh in range(2):
            s = s_buf[slot][hh]
            if diag:
                kr = lax.broadcasted_iota(jnp.int32, (tk, tq), 0)
                qc = lax.broadcasted_iota(jnp.int32, (tk, tq), 1)
                s = jnp.where(kr <= qc, s, NEG_BIG)
                m_next = jnp.max(s, axis=0, keepdims=True)
            else:
                m_prev = m_st[hh, st]
                m_next = jnp.maximum(m_prev, jnp.max(s, axis=0, keepdims=True))
                al_buf[slot][hh] = jnp.exp2(m_prev - m_next)
            p_buf[slot][hh] = jnp.exp2(s - m_next).astype(BF16)
            m_st[hh, st] = m_next

    def stage_pv(tab, t, slot, diag):
        koff = pl.multiple_of(tab[1, t] * tk, tk)
        st = tab[2, t]
        for hh in range(2):
            vt = v_ref[0, hh * LANES:(hh + 1) * LANES, pl.ds(koff, tk)]
            pv = jnp.dot(vt, p_buf[slot][hh], preferred_element_type=F32)
            if diag:
                acc_st[hh, st] = pv
            else:
                acc_st[hh, st] = al_buf[slot][hh] * acc_st[hh, st] + pv

    def run(tab, trips, diag):
        stage_qk(tab, 0, 0)
        stage_qk(tab, 1, 1)
        stage_softmax(tab, 0, 0, diag)

        def body(u, carry):
            for d in range(2):
                t = 2 * u + 1 + d
                stage_qk(tab, t + 1, d % 2)
                stage_softmax(tab, t, (d + 1) % 2, diag)
                stage_pv(tab, t - 1, d % 2, diag)
            return carry

        lax.fori_loop(0, trips, body, 0)

    assert n_diag % 2 == 0 and tab_ref.shape[1] == n_diag + 2
    run(tab_ref, n_diag // 2, True)

    b = pl.program_id(0)
    pr = pl.program_id(1)

    def contributes(qi, j):
        keep = None
        for hh in range(2):
            hd = 2 * pr + hh
            qn = st_ref[b, qi, 0, hd]
            bound = (qn * (st_ref[b, j, 0, FOX_HEADS + hd] + st_ref[b, qi, 0, FOX_HEADS + hd])
                     + st_ref[b, qi, 1, hd] - st_ref[b, j, 2, hd])
            k_h = bound > -PRUNE_T
            keep = k_h if keep is None else jnp.logical_or(keep, k_h)
        return keep

    def put(idx, qrow, j, state):
        off_tab[0, idx] = qrow
        off_tab[1, idx] = j
        off_tab[2, idx] = state

    def list_row(qi, cnt):
        def list_tile(j, cnt):
            put(cnt, qi, j, qi)
            return cnt + contributes(qi, j).astype(jnp.int32)
        return lax.fori_loop(0, qi, list_tile, cnt)

    n_keep = lax.fori_loop(1, nq, list_row, jnp.int32(0))
    for d in range(3):
        put(n_keep + d, 1, 0, nq)
    run(off_tab, (n_keep + 1) // 2, False)

    def finish(qi, carry):
        a0 = acc_st[0, qi]
        a1 = acc_st[1, qi]
        ot = jnp.concatenate([a0[:HEAD_DIM] / a0[HEAD_DIM:HEAD_DIM + 1],
                              a1[HEAD_DIM:] / a1[0:1]], axis=0)
        rows = pl.ds(pl.multiple_of(qi * tq, tq), tq)
        o_ref[0, rows, :] = (ot.T * g_ref[0, rows, :].astype(F32)).astype(BF16)
        return carry

    lax.fori_loop(0, nq, finish, 0)


def _fox_call(qaug_t, kaug, vaug_t, ga, stats):
    B, S, _ = kaug.shape
    tq, tk = TQ_FOX, TK_FOX
    assert tq == tk
    nq = S // tq
    tab, n_diag, n_off = _fox_tables(nq)
    pair = lambda b, p: (b, 0, p)
    return pl.pallas_call(
        functools.partial(_fox_kernel, n_diag=n_diag),
        grid=(B, FOX_HEADS // 2),
        in_specs=[pl.BlockSpec(memory_space=pltpu.SMEM),
                  pl.BlockSpec(memory_space=pltpu.SMEM),
                  pl.BlockSpec((1, 2 * LANES, S), lambda b, p: (b, p, 0)),
                  pl.BlockSpec((1, S, 2 * LANES), pair),
                  pl.BlockSpec((1, 2 * LANES, S), lambda b, p: (b, p, 0)),
                  pl.BlockSpec((1, S, LANES), pair)],
        out_specs=pl.BlockSpec((1, S, LANES), pair),
        out_shape=jax.ShapeDtypeStruct((B, S, FOX_W), BF16),
        scratch_shapes=[pltpu.SMEM((3, n_off + 4), jnp.int32),
                        pltpu.VMEM((2, tk, tq), F32), pltpu.VMEM((2, tk, tq), F32),
                        pltpu.VMEM((2, tk, tq), BF16), pltpu.VMEM((2, tk, tq), BF16),
                        pltpu.VMEM((2, 1, tq), F32), pltpu.VMEM((2, 1, tq), F32),
                        pltpu.VMEM((2, nq + 1, 1, tq), F32),
                        pltpu.VMEM((2, nq + 1, LANES, tq), F32)],
        compiler_params=pltpu.CompilerParams(
            dimension_semantics=("arbitrary", "arbitrary"), vmem_limit_bytes=VMEM_LIMIT),
        name="fox",
    )(jnp.asarray(tab), stats, qaug_t, kaug, vaug_t, ga)


def _swa_kernel(sinks_ref, q_ref, kp_ref, kc_ref, vtp_ref, vtc_ref, g_ref, o_ref):
    i = pl.program_id(1)
    nsub = TQ_SWA // WINDOW
    group = SWA_Q_HEADS // SWA_KV_HEADS
    ncol = group * WINDOW
    kall = jnp.concatenate([kp_ref[0], kc_ref[0]], axis=0)
    vtall = jnp.concatenate([vtp_ref[0], vtc_ref[0]], axis=1)
    lane = lax.broadcasted_iota(jnp.int32, (WINDOW, LANES), 1)
    lo = lane < HEAD_DIM
    zero = jnp.zeros((WINDOW, LANES), BF16)
    kj_ = lax.broadcasted_iota(jnp.int32, (2 * WINDOW, ncol), 0)
    qi_ = lax.broadcasted_iota(jnp.int32, (2 * WINDOW, ncol), 1) & (WINDOW - 1)
    rel = qi_ + WINDOW - kj_
    band = (rel >= 0) & (rel < WINDOW)
    nt = (((1,), (1,)), ((), ()))
    for r in range(nsub):
        if r == 0:
            jmin = jnp.where(i == 0, WINDOW, 0)
            valid = band & (kj_ >= jmin)
        else:
            valid = band
        rows = slice(r * WINDOW, (r + 1) * WINDOW)
        for g in range(SWA_KV_HEADS):
            kg = kall[r * WINDOW:(r + 2) * WINDOW, g * LANES:(g + 1) * LANES]
            vtg = vtall[g * LANES:(g + 1) * LANES, r * WINDOW:(r + 2) * WINDOW]
            c0 = g * group * HEAD_DIM
            qp0 = q_ref[0, rows, c0:c0 + LANES]
            qp1 = q_ref[0, rows, c0 + LANES:c0 + 2 * LANES]
            qs = jnp.concatenate([jnp.where(lo, qp0, zero), jnp.where(lo, qp1, zero),
                                  jnp.where(lo, zero, qp0), jnp.where(lo, zero, qp1)], axis=0)
            st = lax.dot_general(kg, qs, nt, preferred_element_type=F32)
            st = jnp.where(valid, st, NEG_BIG)
            heads = (group * g, group * g + 2, group * g + 1, group * g + 3)
            sink = jnp.concatenate(
                [jnp.full((1, WINDOW), sinks_ref[hd] * LOG2E, F32) for hd in heads], axis=1)
            m = jnp.maximum(jnp.max(st, axis=0, keepdims=True), sink)
            p = jnp.exp2(st - m).astype(BF16)
            acc = jnp.dot(vtg, p, preferred_element_type=F32)
            l = acc[HEAD_DIM:HEAD_DIM + 1] + jnp.exp2(sink - m)
            on = acc[:HEAD_DIM] * (1.0 / l)
            pair0 = jnp.concatenate([on[:, 0:WINDOW], on[:, 2 * WINDOW:3 * WINDOW]], axis=0).T
            pair1 = jnp.concatenate([on[:, WINDOW:2 * WINDOW], on[:, 3 * WINDOW:]], axis=0).T
            o_ref[0, rows, c0:c0 + LANES] = (
                pair0 * g_ref[0, rows, c0:c0 + LANES].astype(F32)).astype(BF16)
            o_ref[0, rows, c0 + LANES:c0 + 2 * LANES] = (
                pair1 * g_ref[0, rows, c0 + LANES:c0 + 2 * LANES].astype(F32)).astype(BF16)


def _swa_call(sinks, qb, kd, vdt, gb):
    B, S, _ = qb.shape
    tq = TQ_SWA
    nsub = tq // WINDOW
    cur = lambda b, i: (b, i, 0)
    prev = lambda b, i: (b, jnp.maximum(i * nsub - 1, 0), 0)
    cur_t = lambda b, i: (b, 0, i)
    prev_t = lambda b, i: (b, 0, jnp.maximum(i * nsub - 1, 0))
    kvw = 2 * SWA_KV_W
    return pl.pallas_call(
        _swa_kernel,
        grid=(B, S // tq),
        in_specs=[pl.BlockSpec(memory_space=pltpu.SMEM),
                  pl.BlockSpec((1, tq, SWA_W), cur),
                  pl.BlockSpec((1, WINDOW, kvw), prev),
                  pl.BlockSpec((1, tq, kvw), cur),
                  pl.BlockSpec((1, kvw, WINDOW), prev_t),
                  pl.BlockSpec((1, kvw, tq), cur_t),
                  pl.BlockSpec((1, tq, SWA_W), cur)],
        out_specs=pl.BlockSpec((1, tq, SWA_W), cur),
        out_shape=jax.ShapeDtypeStruct((B, S, SWA_W), BF16),
        compiler_params=pltpu.CompilerParams(
            dimension_semantics=("arbitrary", "arbitrary"), vmem_limit_bytes=VMEM_LIMIT),
        name="swa",
    )(sinks, qb, kd, kd, vdt, vdt, gb)


def _outproj_kernel(oa_ref, ob_ref, x_ref, gate_ref, gpost_ref, wa_ref, wb_ref, out_ref):
    y = (jnp.dot(oa_ref[0], wa_ref[...], preferred_element_type=F32)
         + jnp.dot(ob_ref[0], wb_ref[...], preferred_element_type=F32))
    ms = jnp.mean(y * y, axis=-1, keepdims=True)
    yn = y * lax.rsqrt(ms + RMS_EPS) * gpost_ref[...]
    out_ref[0] = x_ref[0] + gate_ref[0] * yn


def _outproj_call(oa, ob, x, gate, g_post, wa, wb):
    B, S, D = x.shape
    tm = TM_OUT
    row = lambda b, t: (b, t, 0)
    const2 = lambda b, t: (0, 0)
    return pl.pallas_call(
        _outproj_kernel,
        grid=(B, S // tm),
        in_specs=[pl.BlockSpec((1, tm, FOX_W), row),
                  pl.BlockSpec((1, tm, SWA_W), row),
                  pl.BlockSpec((1, tm, D), row),
                  pl.BlockSpec((1, 1, D), lambda b, t: (b, 0, 0)),
                  pl.BlockSpec((1, D), const2),
                  pl.BlockSpec((FOX_W, D), const2),
                  pl.BlockSpec((SWA_W, D), const2)],
        out_specs=pl.BlockSpec((1, tm, D), row),
        out_shape=jax.ShapeDtypeStruct((B, S, D), F32),
        compiler_params=pltpu.CompilerParams(
            dimension_semantics=("arbitrary", "arbitrary"), vmem_limit_bytes=VMEM_LIMIT),
        name="out_proj",
    )(oa, ob, x, gate, g_post, wa, wb)


def _perm_w_in(w):
    sc = HEAD_DIM ** -0.5
    o = 0
    qa = w[:, o:o + FOX_W]; o += FOX_W
    ka = w[:, o:o + FOX_W]; o += FOX_W
    va = w[:, o:o + FOX_W]; o += FOX_W
    fa = w[:, o:o + FOX_HEADS]; o += FOX_HEADS
    za = w[:, o:o + FOX_W]; o += FOX_W
    qb = w[:, o:o + SWA_W]; o += SWA_W
    kb = w[:, o:o + SWA_KV_W]; o += SWA_KV_W
    vb = w[:, o:o + SWA_KV_W]; o += SWA_KV_W
    zb = w[:, o:o + SWA_W]
    dup = lambda a: jnp.concatenate(
        [a[:, HEAD_DIM * (g // 2):HEAD_DIM * (g // 2 + 1)] for g in range(2 * SWA_KV_HEADS)], axis=1)
    fpad = jnp.concatenate(
        [fa, fa, fa, jnp.zeros((w.shape[0], LANES - 3 * FOX_HEADS), w.dtype)], axis=1)
    sc2 = sc * LOG2E
    return jnp.concatenate([qa * sc2, ka, va, za, qb * sc2, zb, dup(kb), dup(vb), fpad],
                           axis=1).astype(BF16)


def _aug_constants():
    eq = np.zeros((LANES, AUG_W), np.float32)
    ek = np.zeros((LANES, AUG_W), np.float32)
    oq = np.zeros((1, AUG_W), np.float32)
    ok = np.zeros((1, AUG_W), np.float32)
    for hd in range(FOX_HEADS):
        base = LANES * hd + (HEAD_DIM if hd % 2 == 0 else 0)
        for part in range(3):
            eq[part * FOX_HEADS + hd, base + part] = 1.0
            ok[0, base + part] = 1.0
            ek[part * FOX_HEADS + hd, base + 3 + part] = -1.0
            oq[0, base + 3 + part] = 1.0
    return (jnp.asarray(eq, BF16), jnp.asarray(ek, BF16), jnp.asarray(oq), jnp.asarray(ok))


def kernel(x, c, positions, w_ada, b_ada, g_pre, w_in, b_fgate, sinks, w_out, g_post):
    B, S, D = x.shape
    depth = w_ada.shape[0]
    assert TM_IN == TQ_FOX == TK_FOX
    half = HEAD_DIM // 2
    inv_freq = ROPE_THETA ** (-jnp.arange(half, dtype=F32) / half)
    invf = jnp.tile(inv_freq, LANES // half)[:, None]
    pos3 = positions[:, None, :]
    ind = np.zeros((2 * FOX_W, LANES), np.float32)
    ind[np.arange(2 * FOX_W), np.arange(2 * FOX_W) // HEAD_DIM] = 1.0
    ind = jnp.asarray(ind, BF16)
    tri = jnp.asarray(np.tril(np.ones((TM_IN, TM_IN), np.float32)), BF16)
    eq, ek, oq, ok = _aug_constants()
    c_pad = jnp.zeros((8, D), F32).at[:B].set(c)
    for l in range(depth):
        mod = _mod_call(c_pad, w_ada[l], b_ada[l][None, :])[:B]
        shift = mod[:, None, 0:D]
        scale = mod[:, None, D:2 * D]
        gate = mod[:, None, 2 * D:3 * D]
        bf_pad = jnp.concatenate(
            [b_fgate[l]] * 3 + [jnp.zeros((LANES - 3 * FOX_HEADS,), F32)])[None, :]
        qaug, kaug, vaug, ga, qb, kd, vd, gb, stats = _inproj_call(
            x, shift, scale, g_pre[l][None, :], _perm_w_in(w_in[l]), pos3, invf, bf_pad,
            tri, eq, ek, oq, ok, ind)
        oa = _fox_call(qaug, kaug, vaug, ga, stats[:, :, 0:3, 0:2 * FOX_HEADS])
        ob = _swa_call(sinks[l], qb, kd, vd, gb)
        wo = w_out[l].astype(BF16)
        x = _outproj_call(oa, ob, x, gate, g_post[l][None, :], wo[:FOX_W], wo[FOX_W:])
    return x
```

```python
import functools

import jax
import jax.numpy as jnp
import numpy as np
from jax import lax
from jax.experimental import pallas as pl
from jax.experimental.pallas import tpu as pltpu

D_MODEL = 1024
HEAD_DIM = 64
FOX_HEADS = 8
SWA_Q_HEADS = 8
SWA_KV_HEADS = 2
WINDOW = 128
ROPE_THETA = 10000.0
RMS_EPS = 1e-6
FOX_W = FOX_HEADS * HEAD_DIM
SWA_W = SWA_Q_HEADS * HEAD_DIM
SWA_KV_W = SWA_KV_HEADS * HEAD_DIM

LANES = 128
AUG_W = FOX_HEADS * LANES

OFF_QA = 0
OFF_KA = OFF_QA + FOX_W
OFF_VA = OFF_KA + FOX_W
OFF_ZA = OFF_VA + FOX_W
OFF_QB = OFF_ZA + FOX_W
OFF_ZB = OFF_QB + SWA_W
OFF_KD = OFF_ZB + SWA_W
OFF_VD = OFF_KD + 2 * SWA_KV_W
OFF_F = OFF_VD + 2 * SWA_KV_W
IN_NP = OFF_F + LANES

TM_IN = 512
TQ_FOX = 512
TK_FOX = 512
TQ_SWA = 512
TM_OUT = 512
NEG_BIG = -1e30
LOG2E = 1.4426950408889634
PRUNE_T = 140.0
NORM_MARGIN = 1.01
VMEM_LIMIT = 56 * 1024 * 1024

F32 = jnp.float32
BF16 = jnp.bfloat16


def _split3(a):
    hi = a.astype(BF16)
    r = a - hi.astype(F32)
    mid = r.astype(BF16)
    lo = (r - mid.astype(F32)).astype(BF16)
    return hi, mid, lo


def _mod_kernel(c_ref, w_ref, b_ref, o_ref):
    c = c_ref[...]
    sc = c * (1.0 / (1.0 + jnp.exp(-c)))
    o_ref[...] = jnp.dot(sc, w_ref[...], precision=lax.Precision.HIGHEST,
                         preferred_element_type=F32) + b_ref[...]


def _mod_call(c_pad, w_ada, b_ada):
    rows = c_pad.shape[0]
    n = w_ada.shape[1]
    bn = D_MODEL
    return pl.pallas_call(
        _mod_kernel,
        grid=(n // bn,),
        in_specs=[pl.BlockSpec((rows, D_MODEL), lambda j: (0, 0)),
                  pl.BlockSpec((D_MODEL, bn), lambda j: (0, j)),
                  pl.BlockSpec((1, bn), lambda j: (0, j))],
        out_specs=pl.BlockSpec((rows, bn), lambda j: (0, j)),
        out_shape=jax.ShapeDtypeStruct((rows, n), F32),
        name="mod",
    )(c_pad, w_ada, b_ada)


def _inproj_kernel(x_ref, shift_ref, scale_ref, gpre_ref, w_ref, pos_ref, invf_ref, bf_ref,
                   tri_ref, eq_ref, ek_ref, oq_ref, ok_ref, ind_ref,
                   qaug_ref, kaug_ref, vaug_ref, ga_ref, qb_ref, kd_ref, vd_ref, gb_ref, st_ref,
                   carry_ref):
    tm = x_ref.shape[1]

    @pl.when(pl.program_id(1) == 0)
    def _():
        carry_ref[...] = jnp.zeros_like(carry_ref)

    x = x_ref[0]
    ms = jnp.mean(x * x, axis=-1, keepdims=True)
    y = x * lax.rsqrt(ms + RMS_EPS) * gpre_ref[...]
    h = (y * (1.0 + scale_ref[0]) + shift_ref[0]).astype(BF16)

    def proj(off, width):
        return jnp.dot(h, w_ref[:, off:off + width], preferred_element_type=F32)

    f = proj(OFF_F, LANES) + bf_ref[...]
    ls = jnp.minimum(f, 0.0) - jnp.log1p(jnp.exp(-jnp.abs(f)))
    tri = tri_ref[...]
    cum = None
    for part in _split3(ls):
        d = jnp.dot(tri, part, preferred_element_type=F32)
        cum = d if cum is None else cum + d

    cum = cum + carry_ref[...]
    carry_ref[...] = cum[tm - 1:tm, :]

    cum2 = cum * LOG2E
    hi, mid, lo = _split3(cum2)
    lane = lax.broadcasted_iota(jnp.int32, (tm, LANES), 1)
    cs = jnp.where(lane < FOX_HEADS, hi.astype(F32),
                   jnp.where(lane < 2 * FOX_HEADS, mid.astype(F32), lo.astype(F32))).astype(BF16)
    augq = jnp.dot(cs, eq_ref[...], preferred_element_type=F32) + oq_ref[...]
    augk = jnp.dot(cs, ek_ref[...], preferred_element_type=F32) + ok_ref[...]

    lane_w = lax.broadcasted_iota(jnp.int32, (tm, AUG_W), 1)
    data = (((lane_w >> 6) ^ (lane_w >> 7)) & 1) == 0

    def rep(a):
        return jnp.concatenate(
            [a[:, LANES * (hd // 2):LANES * (hd // 2 + 1)] for hd in range(FOX_HEADS)], axis=1)

    qa = proj(OFF_QA, FOX_W)
    ka = proj(OFF_KA, FOX_W)
    qaug_ref[0] = jnp.where(data, rep(qa), augq).T.astype(BF16)
    kaug_ref[0] = jnp.where(data, rep(ka), augk).astype(BF16)

    q16 = qa.astype(BF16).astype(F32)
    k16 = ka.astype(BF16).astype(F32)
    sq = jnp.concatenate([q16 * q16, k16 * k16], axis=1).astype(BF16)
    nrm2 = jnp.dot(sq, ind_ref[...], preferred_element_type=F32)
    nmax = jnp.sqrt(jnp.max(nrm2, axis=0, keepdims=True)) * NORM_MARGIN
    st_ref[0, 0] = jnp.concatenate(
        [nmax, cum2[0:1], cum2[tm - 1:tm], jnp.zeros((5, LANES), F32)], axis=0)
    vaug_ref[0] = jnp.where(data, rep(proj(OFF_VA, FOX_W)), 1.0).T.astype(BF16)

    za = proj(OFF_ZA, FOX_W)
    ga_ref[0] = (za * (1.0 / (1.0 + jnp.exp(-za)))).astype(BF16)
    zb = proj(OFF_ZB, SWA_W)
    gb_ref[0] = (zb * (1.0 / (1.0 + jnp.exp(-zb)))).astype(BF16)

    ang_t = invf_ref[...] * pos_ref[0].astype(F32)
    cosv = jnp.cos(ang_t).T
    sinv = jnp.sin(ang_t).T
    first = (lane & (HEAD_DIM // 2)) == 0
    sin_signed = jnp.where(first, -sinv, sinv)

    def rope(a):
        outs = []
        for cidx in range(a.shape[1] // LANES):
            blk = a[:, cidx * LANES:(cidx + 1) * LANES]
            other = jnp.where(first, pltpu.roll(blk, LANES - HEAD_DIM // 2, 1),
                              pltpu.roll(blk, HEAD_DIM // 2, 1))
            outs.append(blk * cosv + other * sin_signed)
        return jnp.concatenate(outs, axis=1)

    qb_ref[0] = rope(proj(OFF_QB, SWA_W)).astype(BF16)
    kd_ref[0] = rope(proj(OFF_KD, 2 * SWA_KV_W)).astype(BF16)
    lane_v = lax.broadcasted_iota(jnp.int32, (tm, 2 * SWA_KV_W), 1)
    vd = jnp.where((lane_v & HEAD_DIM) == 0, proj(OFF_VD, 2 * SWA_KV_W), 1.0)
    vd_ref[0] = vd.T.astype(BF16)


def _inproj_call(x, shift, scale, g_pre, w_perm, pos3, invf, bf_pad, tri, eq, ek, oq, ok, ind):
    B, S, D = x.shape
    tm = TM_IN
    row = lambda b, t: (b, t, 0)
    per_b = lambda b, t: (b, 0, 0)
    const2 = lambda b, t: (0, 0)
    out_w = (AUG_W, AUG_W, AUG_W, FOX_W, SWA_W, 2 * SWA_KV_W, 2 * SWA_KV_W, SWA_W)
    transposed = (0, 2, 6)
    return pl.pallas_call(
        _inproj_kernel,
        grid=(B, S // tm),
        in_specs=[pl.BlockSpec((1, tm, D), row),
                  pl.BlockSpec((1, 1, D), per_b),
                  pl.BlockSpec((1, 1, D), per_b),
                  pl.BlockSpec((1, D), const2),
                  pl.BlockSpec((D, IN_NP), const2),
                  pl.BlockSpec((1, 1, tm), lambda b, t: (b, 0, t)),
                  pl.BlockSpec((LANES, 1), const2),
                  pl.BlockSpec((1, LANES), const2),
                  pl.BlockSpec((tm, tm), const2),
                  pl.BlockSpec((LANES, AUG_W), const2),
                  pl.BlockSpec((LANES, AUG_W), const2),
                  pl.BlockSpec((1, AUG_W), const2),
                  pl.BlockSpec((1, AUG_W), const2),
                  pl.BlockSpec((2 * FOX_W, LANES), const2)],
        out_specs=[pl.BlockSpec((1, w, tm), lambda b, t: (b, 0, t)) if i in transposed
                   else pl.BlockSpec((1, tm, w), row) for i, w in enumerate(out_w)]
        + [pl.BlockSpec((1, 1, 8, LANES), lambda b, t: (b, t, 0, 0))],
        out_shape=[jax.ShapeDtypeStruct((B, w, S) if i in transposed else (B, S, w), BF16)
                   for i, w in enumerate(out_w)]
        + [jax.ShapeDtypeStruct((B, S // tm, 8, LANES), F32)],
        scratch_shapes=[pltpu.VMEM((1, LANES), F32)],
        compiler_params=pltpu.CompilerParams(
            dimension_semantics=("arbitrary", "arbitrary"), vmem_limit_bytes=VMEM_LIMIT),
        name="in_proj",
    )(x, shift, scale, g_pre, w_perm, pos3, invf, bf_pad, tri, eq, ek, oq, ok, ind)


def _fox_tables(nq):
    diag = [(qi, qi, hh * nq + qi, hh) for hh in range(2) for qi in range(nq)]
    tab = np.array(diag + [(1, 0, 2 * nq, 0)], np.int32).T
    return tab, len(diag), nq * (nq - 1)


def _fox_kernel(tab_ref, st_ref, q_ref, k_ref, v_ref, g_ref, o_ref,
                off_tab, s0, s1, p0, p1, al0, al1, m_st, acc_st, *, n_diag, n_off):
    tq = TQ_FOX
    tk = TK_FOX
    nq = q_ref.shape[2] // tq
    s_buf = (s0, s1)
    p_buf = (p0, p1)
    al_buf = (al0, al1)

    def run(tab, n_items, dummy, diag):
        half = (n_items + 1) // 2
        lens = (half, n_items - half)

        def item(stream, t):
            return jnp.where(t < lens[stream], stream * half + t, dummy)

        def stage_qk(t, slot):
            for sm in range(2):
                e = item(sm, t)
                qoff = pl.multiple_of(tab[0, e] * tq, tq)
                koff = pl.multiple_of(tab[1, e] * tk, tk)
                hoff = pl.multiple_of(tab[3, e] * LANES, LANES)
                qt = q_ref[0, pl.ds(hoff, LANES), pl.ds(qoff, tq)]
                k = k_ref[0, pl.ds(koff, tk), pl.ds(hoff, LANES)]
                s_buf[slot][sm] = jnp.dot(k, qt, preferred_element_type=F32)

        def stage_softmax(t, slot):
            for sm in range(2):
                st = tab[2, item(sm, t)]
                s = s_buf[slot][sm]
                if diag:
                    kr = lax.broadcasted_iota(jnp.int32, (tk, tq), 0)
                    qc = lax.broadcasted_iota(jnp.int32, (tk, tq), 1)
                    s = jnp.where(kr <= qc, s, NEG_BIG)
                    m_next = jnp.max(s, axis=0, keepdims=True)
                else:
                    m_prev = m_st[st]
                    m_next = jnp.maximum(m_prev, jnp.max(s, axis=0, keepdims=True))
                    al_buf[slot][sm] = jnp.exp2(m_prev - m_next)
                p_buf[slot][sm] = jnp.exp2(s - m_next).astype(BF16)
                m_st[st] = m_next

        def stage_pv(t, slot):
            for sm in range(2):
                e = item(sm, t)
                koff = pl.multiple_of(tab[1, e] * tk, tk)
                hoff = pl.multiple_of(tab[3, e] * LANES, LANES)
                st = tab[2, e]
                vt = v_ref[0, pl.ds(hoff, LANES), pl.ds(koff, tk)]
                pv = jnp.dot(vt, p_buf[slot][sm], preferred_element_type=F32)
                if diag:
                    acc_st[st] = pv
                else:
                    acc_st[st] = al_buf[slot][sm] * acc_st[st] + pv

        stage_qk(0, 0)
        stage_qk(1, 1)
        stage_softmax(0, 0)

        def body(u, carry):
            for d in range(2):
                t = 2 * u + 1 + d
                stage_qk(t + 1, d % 2)
                stage_softmax(t, (d + 1) % 2)
                stage_pv(t - 1, d % 2)
            return carry

        lax.fori_loop(0, (half + 1) // 2, body, 0)

    assert tab_ref.shape[1] == n_diag + 1
    run(tab_ref, n_diag, n_diag, True)

    b = pl.program_id(0)
    pr = pl.program_id(1)

    def put(idx, qrow, j, state, head):
        off_tab[0, idx] = qrow
        off_tab[1, idx] = j
        off_tab[2, idx] = state
        off_tab[3, idx] = head

    n_keep = jnp.int32(0)
    for hh in range(2):
        hd = 2 * pr + hh

        def list_row(qi, cnt, hh=hh, hd=hd):
            qn = st_ref[b, qi, 0, hd]
            base = qn * st_ref[b, qi, 0, FOX_HEADS + hd] + st_ref[b, qi, 1, hd]

            def list_tile(j, cnt):
                put(cnt, qi, j, hh * nq + qi, hh)
                bound = base + qn * st_ref[b, j, 0, FOX_HEADS + hd] - st_ref[b, j, 2, hd]
                return cnt + (bound > -PRUNE_T).astype(jnp.int32)

            return lax.fori_loop(0, qi, list_tile, cnt)

        n_keep = lax.fori_loop(1, nq, list_row, n_keep)
    put(n_off, 1, 0, 2 * nq, 0)
    run(off_tab, n_keep, n_off, False)

    def finish(qi, carry):
        a0 = acc_st[qi]
        a1 = acc_st[nq + qi]
        ot = jnp.concatenate([a0[:HEAD_DIM] / a0[HEAD_DIM:HEAD_DIM + 1],
                              a1[HEAD_DIM:] / a1[0:1]], axis=0)
        rows = pl.ds(pl.multiple_of(qi * tq, tq), tq)
        o_ref[0, rows, :] = (ot.T * g_ref[0, rows, :].astype(F32)).astype(BF16)
        return carry

    lax.fori_loop(0, nq, finish, 0)


def _fox_call(qaug_t, kaug, vaug_t, ga, stats):
    B, S, _ = kaug.shape
    tq, tk = TQ_FOX, TK_FOX
    assert tq == tk
    nq = S // tq
    tab, n_diag, n_off = _fox_tables(nq)
    pair = lambda b, p: (b, 0, p)
    return pl.pallas_call(
        functools.partial(_fox_kernel, n_diag=n_diag, n_off=n_off),
        grid=(B, FOX_HEADS // 2),
        in_specs=[pl.BlockSpec(memory_space=pltpu.SMEM),
                  pl.BlockSpec(memory_space=pltpu.SMEM),
                  pl.BlockSpec((1, 2 * LANES, S), lambda b, p: (b, p, 0)),
                  pl.BlockSpec((1, S, 2 * LANES), pair),
                  pl.BlockSpec((1, 2 * LANES, S), lambda b, p: (b, p, 0)),
                  pl.BlockSpec((1, S, LANES), pair)],
        out_specs=pl.BlockSpec((1, S, LANES), pair),
        out_shape=jax.ShapeDtypeStruct((B, S, FOX_W), BF16),
        scratch_shapes=[pltpu.SMEM((4, n_off + 1), jnp.int32),
                        pltpu.VMEM((2, tk, tq), F32), pltpu.VMEM((2, tk, tq), F32),
                        pltpu.VMEM((2, tk, tq), BF16), pltpu.VMEM((2, tk, tq), BF16),
                        pltpu.VMEM((2, 1, tq), F32), pltpu.VMEM((2, 1, tq), F32),
                        pltpu.VMEM((2 * nq + 1, 1, tq), F32),
                        pltpu.VMEM((2 * nq + 1, LANES, tq), F32)],
        compiler_params=pltpu.CompilerParams(
            dimension_semantics=("arbitrary", "arbitrary"), vmem_limit_bytes=VMEM_LIMIT),
        name="fox",
    )(jnp.asarray(tab), stats, qaug_t, kaug, vaug_t, ga)


def _swa_kernel(sinks_ref, q_ref, kp_ref, kc_ref, vtp_ref, vtc_ref, g_ref, o_ref):
    i = pl.program_id(1)
    nsub = TQ_SWA // WINDOW
    group = SWA_Q_HEADS // SWA_KV_HEADS
    ncol = group * WINDOW
    kall = jnp.concatenate([kp_ref[0], kc_ref[0]], axis=0)
    vtall = jnp.concatenate([vtp_ref[0], vtc_ref[0]], axis=1)
    lane = lax.broadcasted_iota(jnp.int32, (WINDOW, LANES), 1)
    lo = lane < HEAD_DIM
    zero = jnp.zeros((WINDOW, LANES), BF16)
    kj_ = lax.broadcasted_iota(jnp.int32, (2 * WINDOW, ncol), 0)
    qi_ = lax.broadcasted_iota(jnp.int32, (2 * WINDOW, ncol), 1) & (WINDOW - 1)
    rel = qi_ + WINDOW - kj_
    band = (rel >= 0) & (rel < WINDOW)
    nt = (((1,), (1,)), ((), ()))
    for r in range(nsub):
        if r == 0:
            jmin = jnp.where(i == 0, WINDOW, 0)
            valid = band & (kj_ >= jmin)
        else:
            valid = band
        rows = slice(r * WINDOW, (r + 1) * WINDOW)
        for g in range(SWA_KV_HEADS):
            kg = kall[r * WINDOW:(r + 2) * WINDOW, g * LANES:(g + 1) * LANES]
            vtg = vtall[g * LANES:(g + 1) * LANES, r * WINDOW:(r + 2) * WINDOW]
            c0 = g * group * HEAD_DIM
            qp0 = q_ref[0, rows, c0:c0 + LANES]
            qp1 = q_ref[0, rows, c0 + LANES:c0 + 2 * LANES]
            qs = jnp.concatenate([jnp.where(lo, qp0, zero), jnp.where(lo, qp1, zero),
                                  jnp.where(lo, zero, qp0), jnp.where(lo, zero, qp1)], axis=0)
            st = lax.dot_general(kg, qs, nt, preferred_element_type=F32)
            st = jnp.where(valid, st, NEG_BIG)
            heads = (group * g, group * g + 2, group * g + 1, group * g + 3)
            sink = jnp.concatenate(
                [jnp.full((1, WINDOW), sinks_ref[hd] * LOG2E, F32) for hd in heads], axis=1)
            m = jnp.maximum(jnp.max(st, axis=0, keepdims=True), sink)
            p = jnp.exp2(st - m).astype(BF16)
            acc = jnp.dot(vtg, p, preferred_element_type=F32)
            l = acc[HEAD_DIM:HEAD_DIM + 1] + jnp.exp2(sink - m)
            on = acc[:HEAD_DIM] * (1.0 / l)
            pair0 = jnp.concatenate([on[:, 0:WINDOW], on[:, 2 * WINDOW:3 * WINDOW]], axis=0).T
            pair1 = jnp.concatenate([on[:, WINDOW:2 * WINDOW], on[:, 3 * WINDOW:]], axis=0).T
            o_ref[0, rows, c0:c0 + LANES] = (
                pair0 * g_ref[0, rows, c0:c0 + LANES].astype(F32)).astype(BF16)
            o_ref[0, rows, c0 + LANES:c0 + 2 * LANES] = (
                pair1 * g_ref[0, rows, c0 + LANES:c0 + 2 * LANES].astype(F32)).astype(BF16)


def _swa_call(sinks, qb, kd, vdt, gb):
    B, S, _ = qb.shape
    tq = TQ_SWA
    nsub = tq // WINDOW
    cur = lambda b, i: (b, i, 0)
    prev = lambda b, i: (b, jnp.maximum(i * nsub - 1, 0), 0)
    cur_t = lambda b, i: (b, 0, i)
    prev_t = lambda b, i: (b, 0, jnp.maximum(i * nsub - 1, 0))
    kvw = 2 * SWA_KV_W
    return pl.pallas_call(
        _swa_kernel,
        grid=(B, S // tq),
        in_specs=[pl.BlockSpec(memory_space=pltpu.SMEM),
                  pl.BlockSpec((1, tq, SWA_W), cur),
                  pl.BlockSpec((1, WINDOW, kvw), prev),
                  pl.BlockSpec((1, tq, kvw), cur),
                  pl.BlockSpec((1, kvw, WINDOW), prev_t),
                  pl.BlockSpec((1, kvw, tq), cur_t),
                  pl.BlockSpec((1, tq, SWA_W), cur)],
        out_specs=pl.BlockSpec((1, tq, SWA_W), cur),
        out_shape=jax.ShapeDtypeStruct((B, S, SWA_W), BF16),
        compiler_params=pltpu.CompilerParams(
            dimension_semantics=("arbitrary", "arbitrary"), vmem_limit_bytes=VMEM_LIMIT),
        name="swa",
    )(sinks, qb, kd, kd, vdt, vdt, gb)


def _outproj_kernel(oa_ref, ob_ref, x_ref, gate_ref, gpost_ref, wa_ref, wb_ref, out_ref):
    y = (jnp.dot(oa_ref[0], wa_ref[...], preferred_element_type=F32)
         + jnp.dot(ob_ref[0], wb_ref[...], preferred_element_type=F32))
    ms = jnp.mean(y * y, axis=-1, keepdims=True)
    yn = y * lax.rsqrt(ms + RMS_EPS) * gpost_ref[...]
    out_ref[0] = x_ref[0] + gate_ref[0] * yn


def _outproj_call(oa, ob, x, gate, g_post, wa, wb):
    B, S, D = x.shape
    tm = TM_OUT
    row = lambda b, t: (b, t, 0)
    const2 = lambda b, t: (0, 0)
    return pl.pallas_call(
        _outproj_kernel,
        grid=(B, S // tm),
        in_specs=[pl.BlockSpec((1, tm, FOX_W), row),
                  pl.BlockSpec((1, tm, SWA_W), row),
                  pl.BlockSpec((1, tm, D), row),
                  pl.BlockSpec((1, 1, D), lambda b, t: (b, 0, 0)),
                  pl.BlockSpec((1, D), const2),
                  pl.BlockSpec((FOX_W, D), const2),
                  pl.BlockSpec((SWA_W, D), const2)],
        out_specs=pl.BlockSpec((1, tm, D), row),
        out_shape=jax.ShapeDtypeStruct((B, S, D), F32),
        compiler_params=pltpu.CompilerParams(
            dimension_semantics=("arbitrary", "arbitrary"), vmem_limit_bytes=VMEM_LIMIT),
        name="out_proj",
    )(oa, ob, x, gate, g_post, wa, wb)


def _perm_w_in(w):
    sc = HEAD_DIM ** -0.5
    o = 0
    qa = w[:, o:o + FOX_W]; o += FOX_W
    ka = w[:, o:o + FOX_W]; o += FOX_W
    va = w[:, o:o + FOX_W]; o += FOX_W
    fa = w[:, o:o + FOX_HEADS]; o += FOX_HEADS
    za = w[:, o:o + FOX_W]; o += FOX_W
    qb = w[:, o:o + SWA_W]; o += SWA_W
    kb = w[:, o:o + SWA_KV_W]; o += SWA_KV_W
    vb = w[:, o:o + SWA_KV_W]; o += SWA_KV_W
    zb = w[:, o:o + SWA_W]
    dup = lambda a: jnp.concatenate(
        [a[:, HEAD_DIM * (g // 2):HEAD_DIM * (g // 2 + 1)] for g in range(2 * SWA_KV_HEADS)], axis=1)
    fpad = jnp.concatenate(
        [fa, fa, fa, jnp.zeros((w.shape[0], LANES - 3 * FOX_HEADS), w.dtype)], axis=1)
    sc2 = sc * LOG2E
    return jnp.concatenate([qa * sc2, ka, va, za, qb * sc2, zb, dup(kb), dup(vb), fpad],
                           axis=1).astype(BF16)


def _aug_constants():
    eq = np.zeros((LANES, AUG_W), np.float32)
    ek = np.zeros((LANES, AUG_W), np.float32)
    oq = np.zeros((1, AUG_W), np.float32)
    ok = np.zeros((1, AUG_W), np.float32)
    for hd in range(FOX_HEADS):
        base = LANES * hd + (HEAD_DIM if hd % 2 == 0 else 0)
        for part in range(3):
            eq[part * FOX_HEADS + hd, base + part] = 1.0
            ok[0, base + part] = 1.0
            ek[part * FOX_HEADS + hd, base + 3 + part] = -1.0
            oq[0, base + 3 + part] = 1.0
    return (jnp.asarray(eq, BF16), jnp.asarray(ek, BF16), jnp.asarray(oq), jnp.asarray(ok))


def kernel(x, c, positions, w_ada, b_ada, g_pre, w_in, b_fgate, sinks, w_out, g_post):
    B, S, D = x.shape
    depth = w_ada.shape[0]
    assert TM_IN == TQ_FOX == TK_FOX
    half = HEAD_DIM // 2
    inv_freq = ROPE_THETA ** (-jnp.arange(half, dtype=F32) / half)
    invf = jnp.tile(inv_freq, LANES // half)[:, None]
    pos3 = positions[:, None, :]
    ind = np.zeros((2 * FOX_W, LANES), np.float32)
    ind[np.arange(2 * FOX_W), np.arange(2 * FOX_W) // HEAD_DIM] = 1.0
    ind = jnp.asarray(ind, BF16)
    tri = jnp.asarray(np.tril(np.ones((TM_IN, TM_IN), np.float32)), BF16)
    eq, ek, oq, ok = _aug_constants()
    c_pad = jnp.zeros((8, D), F32).at[:B].set(c)
    for l in range(depth):
        mod = _mod_call(c_pad, w_ada[l], b_ada[l][None, :])[:B]
        shift = mod[:, None, 0:D]
        scale = mod[:, None, D:2 * D]
        gate = mod[:, None, 2 * D:3 * D]
        bf_pad = jnp.concatenate(
            [b_fgate[l]] * 3 + [jnp.zeros((LANES - 3 * FOX_HEADS,), F32)])[None, :]
        qaug, kaug, vaug, ga, qb, kd, vd, gb, stats = _inproj_call(
            x, shift, scale, g_pre[l][None, :], _perm_w_in(w_in[l]), pos3, invf, bf_pad,
            tri, eq, ek, oq, ok, ind)
        oa = _fox_call(qaug, kaug, vaug, ga, stats[:, :, 0:3, 0:2 * FOX_HEADS])
        ob = _swa_call(sinks[l], qb, kd, vd, gb)
        wo = w_out[l].astype(BF16)
        x = _outproj_call(oa, ob, x, gate, g_post[l][None, :], wo[:FOX_W], wo[FOX_W:])
    return x
```

```python
import functools

import jax
import jax.numpy as jnp
import numpy as np
from jax import lax
from jax.experimental import pallas as pl
from jax.experimental.pallas import tpu as pltpu

D_MODEL = 1024
HEAD_DIM = 64
FOX_HEADS = 8
SWA_Q_HEADS = 8
SWA_KV_HEADS = 2
WINDOW = 128
ROPE_THETA = 10000.0
RMS_EPS = 1e-6
FOX_W = FOX_HEADS * HEAD_DIM
SWA_W = SWA_Q_HEADS * HEAD_DIM
SWA_KV_W = SWA_KV_HEADS * HEAD_DIM

LANES = 128
AUG_W = FOX_HEADS * LANES

OFF_QA = 0
OFF_KA = OFF_QA + FOX_W
OFF_VA = OFF_KA + FOX_W
OFF_ZA = OFF_VA + FOX_W
OFF_QB = OFF_ZA + FOX_W
OFF_ZB = OFF_QB + SWA_W
OFF_KD = OFF_ZB + SWA_W
OFF_VD = OFF_KD + SWA_KV_W
OFF_F = OFF_VD + SWA_KV_W
IN_NP = OFF_F + LANES

TM_IN = 512
TQ_FOX = 512
TK_FOX = 512
TQ_SWA = 512
TM_OUT = 512
NEG_BIG = -1e30
LOG2E = 1.4426950408889634
PRUNE_T = 140.0
NORM_MARGIN = 1.01
VMEM_LIMIT = 56 * 1024 * 1024

F32 = jnp.float32
BF16 = jnp.bfloat16


def _split3(a):
    hi = a.astype(BF16)
    r = a - hi.astype(F32)
    mid = r.astype(BF16)
    lo = (r - mid.astype(F32)).astype(BF16)
    return hi, mid, lo


def _mod_kernel(c_ref, w_ref, b_ref, o_ref):
    c = c_ref[...]
    sc = c * (1.0 / (1.0 + jnp.exp(-c)))
    o_ref[...] = jnp.dot(sc, w_ref[...], precision=lax.Precision.HIGHEST,
                         preferred_element_type=F32) + b_ref[...]


def _mod_call(c_pad, w_ada, b_ada):
    rows = c_pad.shape[0]
    n = w_ada.shape[1]
    bn = D_MODEL
    return pl.pallas_call(
        _mod_kernel,
        grid=(n // bn,),
        in_specs=[pl.BlockSpec((rows, D_MODEL), lambda j: (0, 0)),
                  pl.BlockSpec((D_MODEL, bn), lambda j: (0, j)),
                  pl.BlockSpec((1, bn), lambda j: (0, j))],
        out_specs=pl.BlockSpec((rows, bn), lambda j: (0, j)),
        out_shape=jax.ShapeDtypeStruct((rows, n), F32),
        name="mod",
    )(c_pad, w_ada, b_ada)


def _inproj_kernel(x_ref, shift_ref, scale_ref, gpre_ref, w_ref, pos_ref, invf_ref, bf_ref,
                   tri_ref, eq_ref, ek_ref, oq_ref, ok_ref, ind_ref,
                   qaug_ref, kaug_ref, vaug_ref, ga_ref, qb_ref, kd_ref, vd_ref, gb_ref, st_ref,
                   carry_ref):
    tm = x_ref.shape[1]

    @pl.when(pl.program_id(1) == 0)
    def _():
        carry_ref[...] = jnp.zeros_like(carry_ref)

    x = x_ref[0]
    ms = jnp.mean(x * x, axis=-1, keepdims=True)
    y = x * lax.rsqrt(ms + RMS_EPS) * gpre_ref[...]
    h = (y * (1.0 + scale_ref[0]) + shift_ref[0]).astype(BF16)

    def proj(off, width):
        return jnp.dot(h, w_ref[:, off:off + width], preferred_element_type=F32)

    f = proj(OFF_F, LANES) + bf_ref[...]
    ls = jnp.minimum(f, 0.0) - jnp.log1p(jnp.exp(-jnp.abs(f)))
    lane = lax.broadcasted_iota(jnp.int32, (tm, LANES), 1)

    def by_group(a, b_, c_):
        return jnp.where(lane < FOX_HEADS, a, jnp.where(lane < 2 * FOX_HEADS, b_, c_))

    part = by_group(*(t.astype(F32) for t in _split3(ls))).astype(BF16)
    psum = jnp.dot(tri_ref[...], part, preferred_element_type=F32)
    cum = (psum + pltpu.roll(psum, LANES - FOX_HEADS, 1)
           + pltpu.roll(psum, LANES - 2 * FOX_HEADS, 1))
    cum = cum + carry_ref[...]
    carry_ref[...] = cum[tm - 1:tm, :]

    cum2 = cum * LOG2E
    cum2 = by_group(cum2, pltpu.roll(cum2, FOX_HEADS, 1), pltpu.roll(cum2, 2 * FOX_HEADS, 1))
    cs = by_group(*(t.astype(F32) for t in _split3(cum2))).astype(BF16)
    augq = jnp.dot(cs, eq_ref[...], preferred_element_type=F32) + oq_ref[...]
    augk = jnp.dot(cs, ek_ref[...], preferred_element_type=F32) + ok_ref[...]

    lane_w = lax.broadcasted_iota(jnp.int32, (tm, AUG_W), 1)
    data = (((lane_w >> 6) ^ (lane_w >> 7)) & 1) == 0

    def rep(a):
        return jnp.concatenate(
            [a[:, LANES * (hd // 2):LANES * (hd // 2 + 1)] for hd in range(FOX_HEADS)], axis=1)

    qa = proj(OFF_QA, FOX_W)
    ka = proj(OFF_KA, FOX_W)
    qaug_ref[0] = jnp.where(data, rep(qa), augq).T.astype(BF16)
    kaug_ref[0] = jnp.where(data, rep(ka), augk).astype(BF16)

    q16 = qa.astype(BF16).astype(F32)
    k16 = ka.astype(BF16).astype(F32)
    sq = jnp.concatenate([q16 * q16, k16 * k16], axis=1).astype(BF16)
    nrm2 = jnp.dot(sq, ind_ref[...], preferred_element_type=F32)
    nmax = jnp.sqrt(jnp.max(nrm2, axis=0, keepdims=True)) * NORM_MARGIN
    st_ref[0, 0] = jnp.concatenate(
        [nmax, cum2[0:1], cum2[tm - 1:tm], jnp.zeros((5, LANES), F32)], axis=0)
    vaug_ref[0] = jnp.where(data, rep(proj(OFF_VA, FOX_W)), 1.0).T.astype(BF16)

    za = proj(OFF_ZA, FOX_W)
    ga_ref[0] = (za * (1.0 / (1.0 + jnp.exp(-za)))).astype(BF16)
    zb = proj(OFF_ZB, SWA_W)
    gb_ref[0] = (zb * (1.0 / (1.0 + jnp.exp(-zb)))).astype(BF16)

    ang_t = invf_ref[...] * pos_ref[0].astype(F32)
    cosv = jnp.cos(ang_t).T
    sinv = jnp.sin(ang_t).T
    first = (lane & (HEAD_DIM // 2)) == 0
    sin_signed = jnp.where(first, -sinv, sinv)

    def rope(a):
        outs = []
        for cidx in range(a.shape[1] // LANES):
            blk = a[:, cidx * LANES:(cidx + 1) * LANES]
            other = jnp.where(first, pltpu.roll(blk, LANES - HEAD_DIM // 2, 1),
                              pltpu.roll(blk, HEAD_DIM // 2, 1))
            outs.append(blk * cosv + other * sin_signed)
        return jnp.concatenate(outs, axis=1)

    qb_ref[0] = rope(proj(OFF_QB, SWA_W)).astype(BF16)
    kr = rope(proj(OFF_KD, SWA_KV_W))
    ks = pltpu.roll(kr, HEAD_DIM, 1)
    low = lane < HEAD_DIM
    kd_ref[0] = jnp.concatenate([jnp.where(low, kr, ks), jnp.where(low, ks, kr)],
                                axis=1).astype(BF16)
    vt = proj(OFF_VD, SWA_KV_W).T
    ones = jnp.ones((HEAD_DIM, tm), F32)
    vd_ref[0] = jnp.concatenate([vt[:HEAD_DIM], ones, vt[HEAD_DIM:], ones],
                                axis=0).astype(BF16)


def _inproj_call(x, shift, scale, g_pre, w_perm, pos3, invf, bf_pad, tri, eq, ek, oq, ok, ind):
    B, S, D = x.shape
    tm = TM_IN
    row = lambda b, t: (b, t, 0)
    per_b = lambda b, t: (b, 0, 0)
    const2 = lambda b, t: (0, 0)
    out_w = (AUG_W, AUG_W, AUG_W, FOX_W, SWA_W, 2 * SWA_KV_W, 2 * SWA_KV_W, SWA_W)
    transposed = (0, 2, 6)
    return pl.pallas_call(
        _inproj_kernel,
        grid=(B, S // tm),
        in_specs=[pl.BlockSpec((1, tm, D), row),
                  pl.BlockSpec((1, 1, D), per_b),
                  pl.BlockSpec((1, 1, D), per_b),
                  pl.BlockSpec((1, D), const2),
                  pl.BlockSpec((D, IN_NP), const2),
                  pl.BlockSpec((1, 1, tm), lambda b, t: (b, 0, t)),
                  pl.BlockSpec((LANES, 1), const2),
                  pl.BlockSpec((1, LANES), const2),
                  pl.BlockSpec((tm, tm), const2),
                  pl.BlockSpec((LANES, AUG_W), const2),
                  pl.BlockSpec((LANES, AUG_W), const2),
                  pl.BlockSpec((1, AUG_W), const2),
                  pl.BlockSpec((1, AUG_W), const2),
                  pl.BlockSpec((2 * FOX_W, LANES), const2)],
        out_specs=[pl.BlockSpec((1, w, tm), lambda b, t: (b, 0, t)) if i in transposed
                   else pl.BlockSpec((1, tm, w), row) for i, w in enumerate(out_w)]
        + [pl.BlockSpec((1, 1, 8, LANES), lambda b, t: (b, t, 0, 0))],
        out_shape=[jax.ShapeDtypeStruct((B, w, S) if i in transposed else (B, S, w), BF16)
                   for i, w in enumerate(out_w)]
        + [jax.ShapeDtypeStruct((B, S // tm, 8, LANES), F32)],
        scratch_shapes=[pltpu.VMEM((1, LANES), F32)],
        compiler_params=pltpu.CompilerParams(
            dimension_semantics=("arbitrary", "arbitrary"), vmem_limit_bytes=VMEM_LIMIT),
        name="in_proj",
    )(x, shift, scale, g_pre, w_perm, pos3, invf, bf_pad, tri, eq, ek, oq, ok, ind)


def _fox_tables(nq):
    diag = [(qi, qi, hh * nq + qi, hh) for hh in range(2) for qi in range(nq)]
    tab = np.array(diag + [(1, 0, 2 * nq, 0)], np.int32).T
    return tab, len(diag), nq * (nq - 1)


def _fox_kernel(tab_ref, st_ref, q_ref, k_ref, v_ref, g_ref, o_ref,
                off_tab, s0, s1, p0, p1, al0, al1, m_st, acc_st, *, n_diag, n_off):
    tq = TQ_FOX
    tk = TK_FOX
    nq = q_ref.shape[2] // tq
    s_buf = (s0, s1)
    p_buf = (p0, p1)
    al_buf = (al0, al1)

    def run(tab, n_items, dummy, diag):
        half = (n_items + 1) // 2
        lens = (half, n_items - half)

        def item(stream, t):
            return jnp.where(t < lens[stream], stream * half + t, dummy)

        def stage_qk(t, slot):
            for sm in range(2):
                e = item(sm, t)
                qoff = pl.multiple_of(tab[0, e] * tq, tq)
                koff = pl.multiple_of(tab[1, e] * tk, tk)
                hoff = pl.multiple_of(tab[3, e] * LANES, LANES)
                qt = q_ref[0, pl.ds(hoff, LANES), pl.ds(qoff, tq)]
                k = k_ref[0, pl.ds(koff, tk), pl.ds(hoff, LANES)]
                s_buf[slot][sm] = jnp.dot(k, qt, preferred_element_type=F32)

        def stage_softmax(t, slot):
            for sm in range(2):
                st = tab[2, item(sm, t)]
                s = s_buf[slot][sm]
                if diag:
                    kr = lax.broadcasted_iota(jnp.int32, (tk, tq), 0)
                    qc = lax.broadcasted_iota(jnp.int32, (tk, tq), 1)
                    s = jnp.where(kr <= qc, s, NEG_BIG)
                    m_next = jnp.max(s, axis=0, keepdims=True)
                else:
                    m_prev = m_st[st]
                    m_next = jnp.maximum(m_prev, jnp.max(s, axis=0, keepdims=True))
                    al_buf[slot][sm] = jnp.exp2(m_prev - m_next)
                p_buf[slot][sm] = jnp.exp2(s - m_next).astype(BF16)
                m_st[st] = m_next

        def stage_pv(t, slot):
            for sm in range(2):
                e = item(sm, t)
                koff = pl.multiple_of(tab[1, e] * tk, tk)
                hoff = pl.multiple_of(tab[3, e] * LANES, LANES)
                st = tab[2, e]
                vt = v_ref[0, pl.ds(hoff, LANES), pl.ds(koff, tk)]
                pv = jnp.dot(vt, p_buf[slot][sm], preferred_element_type=F32)
                if diag:
                    acc_st[st] = pv
                else:
                    acc_st[st] = al_buf[slot][sm] * acc_st[st] + pv

        stage_qk(0, 0)
        stage_qk(1, 1)
        stage_softmax(0, 0)

        def body(u, carry):
            for d in range(2):
                t = 2 * u + 1 + d
                stage_qk(t + 1, d % 2)
                stage_softmax(t, (d + 1) % 2)
                stage_pv(t - 1, d % 2)
            return carry

        lax.fori_loop(0, (half + 1) // 2, body, 0)

    assert tab_ref.shape[1] == n_diag + 1
    run(tab_ref, n_diag, n_diag, True)

    b = pl.program_id(0)
    pr = pl.program_id(1)

    def put(idx, qrow, j, state, head):
        off_tab[0, idx] = qrow
        off_tab[1, idx] = j
        off_tab[2, idx] = state
        off_tab[3, idx] = head

    n_keep = jnp.int32(0)
    for hh in range(2):
        hd = 2 * pr + hh

        kmax = lax.fori_loop(
            0, nq, lambda j, m, hd=hd: jnp.maximum(m, st_ref[b, j, 0, FOX_HEADS + hd]),
            jnp.float32(0.0))

        def list_row(qi, cnt, hh=hh, hd=hd, kmax=kmax):
            base = (st_ref[b, qi, 0, hd] * (st_ref[b, qi, 0, FOX_HEADS + hd] + kmax)
                    + st_ref[b, qi, 1, hd])

            def contributes(carry):
                j, _ = carry
                bound = base - st_ref[b, jnp.maximum(j, 0), 2, hd]
                return jnp.logical_and(j >= 0, bound > -PRUNE_T)

            def take(carry):
                j, cnt = carry
                put(cnt, qi, j, hh * nq + qi, hh)
                return j - 1, cnt + 1

            return lax.while_loop(contributes, take, (qi - 1, cnt))[1]

        n_keep = lax.fori_loop(1, nq, list_row, n_keep)
    put(n_off, 1, 0, 2 * nq, 0)
    run(off_tab, n_keep, n_off, False)

    def finish(qi, carry):
        a0 = acc_st[qi]
        a1 = acc_st[nq + qi]
        ot = jnp.concatenate([a0[:HEAD_DIM] / a0[HEAD_DIM:HEAD_DIM + 1],
                              a1[HEAD_DIM:] / a1[0:1]], axis=0)
        rows = pl.ds(pl.multiple_of(qi * tq, tq), tq)
        o_ref[0, rows, :] = (ot.T * g_ref[0, rows, :].astype(F32)).astype(BF16)
        return carry

    lax.fori_loop(0, nq, finish, 0)


def _fox_call(qaug_t, kaug, vaug_t, ga, stats):
    B, S, _ = kaug.shape
    tq, tk = TQ_FOX, TK_FOX
    assert tq == tk
    nq = S // tq
    tab, n_diag, n_off = _fox_tables(nq)
    pair = lambda b, p: (b, 0, p)
    return pl.pallas_call(
        functools.partial(_fox_kernel, n_diag=n_diag, n_off=n_off),
        grid=(B, FOX_HEADS // 2),
        in_specs=[pl.BlockSpec(memory_space=pltpu.SMEM),
                  pl.BlockSpec(memory_space=pltpu.SMEM),
                  pl.BlockSpec((1, 2 * LANES, S), lambda b, p: (b, p, 0)),
                  pl.BlockSpec((1, S, 2 * LANES), pair),
                  pl.BlockSpec((1, 2 * LANES, S), lambda b, p: (b, p, 0)),
                  pl.BlockSpec((1, S, LANES), pair)],
        out_specs=pl.BlockSpec((1, S, LANES), pair),
        out_shape=jax.ShapeDtypeStruct((B, S, FOX_W), BF16),
        scratch_shapes=[pltpu.SMEM((4, n_off + 1), jnp.int32),
                        pltpu.VMEM((2, tk, tq), F32), pltpu.VMEM((2, tk, tq), F32),
                        pltpu.VMEM((2, tk, tq), BF16), pltpu.VMEM((2, tk, tq), BF16),
                        pltpu.VMEM((2, 1, tq), F32), pltpu.VMEM((2, 1, tq), F32),
                        pltpu.VMEM((2 * nq + 1, 1, tq), F32),
                        pltpu.VMEM((2 * nq + 1, LANES, tq), F32)],
        compiler_params=pltpu.CompilerParams(
            dimension_semantics=("arbitrary", "arbitrary"), vmem_limit_bytes=VMEM_LIMIT),
        name="fox",
    )(jnp.asarray(tab), stats, qaug_t, kaug, vaug_t, ga)


def _swa_kernel(sinks_ref, q_ref, kp_ref, kc_ref, vtp_ref, vtc_ref, g_ref, o_ref,
                s_sc, p_sc, e_sc):
    i = pl.program_id(1)
    nsub = TQ_SWA // WINDOW
    group = SWA_Q_HEADS // SWA_KV_HEADS
    ncol = group * WINDOW
    kall = jnp.concatenate([kp_ref[0], kc_ref[0]], axis=0)
    vtall = jnp.concatenate([vtp_ref[0], vtc_ref[0]], axis=1)
    lane = lax.broadcasted_iota(jnp.int32, (WINDOW, LANES), 1)
    lo = lane < HEAD_DIM
    zero = jnp.zeros((WINDOW, LANES), BF16)
    kj_ = lax.broadcasted_iota(jnp.int32, (2 * WINDOW, ncol), 0)
    qi_ = lax.broadcasted_iota(jnp.int32, (2 * WINDOW, ncol), 1) & (WINDOW - 1)
    rel = qi_ + WINDOW - kj_
    band = (rel >= 0) & (rel < WINDOW)
    nt = (((1,), (1,)), ((), ()))
    probs = [(r, g) for r in range(nsub) for g in range(SWA_KV_HEADS)]

    def stage_qk(n):
        r, g = probs[n]
        rows = slice(r * WINDOW, (r + 1) * WINDOW)
        kg = kall[r * WINDOW:(r + 2) * WINDOW, g * LANES:(g + 1) * LANES]
        c0 = g * group * HEAD_DIM
        qp0 = q_ref[0, rows, c0:c0 + LANES]
        qp1 = q_ref[0, rows, c0 + LANES:c0 + 2 * LANES]
        qs = jnp.concatenate([jnp.where(lo, qp0, zero), jnp.where(lo, qp1, zero),
                              jnp.where(lo, zero, qp0), jnp.where(lo, zero, qp1)], axis=0)
        s_sc[n] = lax.dot_general(kg, qs, nt, preferred_element_type=F32)

    def stage_softmax(n):
        r, g = probs[n]
        valid = band & (kj_ >= jnp.where(i == 0, WINDOW, 0)) if r == 0 else band
        st = jnp.where(valid, s_sc[n], NEG_BIG)
        heads = (group * g, group * g + 2, group * g + 1, group * g + 3)
        sink = jnp.concatenate(
            [jnp.full((1, WINDOW), sinks_ref[hd] * LOG2E, F32) for hd in heads], axis=1)
        m = jnp.maximum(jnp.max(st, axis=0, keepdims=True), sink)
        p_sc[n] = jnp.exp2(st - m).astype(BF16)
        e_sc[n] = jnp.exp2(sink - m)

    def stage_pv(n):
        r, g = probs[n]
        rows = slice(r * WINDOW, (r + 1) * WINDOW)
        c0 = g * group * HEAD_DIM
        vtg = vtall[g * LANES:(g + 1) * LANES, r * WINDOW:(r + 2) * WINDOW]
        acc = jnp.dot(vtg, p_sc[n], preferred_element_type=F32)
        l = acc[HEAD_DIM:HEAD_DIM + 1] + e_sc[n]
        on = acc[:HEAD_DIM] * (1.0 / l)
        pair0 = jnp.concatenate([on[:, 0:WINDOW], on[:, 2 * WINDOW:3 * WINDOW]], axis=0).T
        pair1 = jnp.concatenate([on[:, WINDOW:2 * WINDOW], on[:, 3 * WINDOW:]], axis=0).T
        o_ref[0, rows, c0:c0 + LANES] = (
            pair0 * g_ref[0, rows, c0:c0 + LANES].astype(F32)).astype(BF16)
        o_ref[0, rows, c0 + LANES:c0 + 2 * LANES] = (
            pair1 * g_ref[0, rows, c0 + LANES:c0 + 2 * LANES].astype(F32)).astype(BF16)

    for t in range(len(probs) + 2):
        if t < len(probs):
            stage_qk(t)
        if 1 <= t <= len(probs):
            stage_softmax(t - 1)
        if t >= 2:
            stage_pv(t - 2)


def _swa_call(sinks, qb, kd, vdt, gb):
    B, S, _ = qb.shape
    tq = TQ_SWA
    nsub = tq // WINDOW
    cur = lambda b, i: (b, i, 0)
    prev = lambda b, i: (b, jnp.maximum(i * nsub - 1, 0), 0)
    cur_t = lambda b, i: (b, 0, i)
    prev_t = lambda b, i: (b, 0, jnp.maximum(i * nsub - 1, 0))
    kvw = 2 * SWA_KV_W
    nprob = nsub * SWA_KV_HEADS
    ncol = SWA_Q_HEADS // SWA_KV_HEADS * WINDOW
    return pl.pallas_call(
        _swa_kernel,
        grid=(B, S // tq),
        in_specs=[pl.BlockSpec(memory_space=pltpu.SMEM),
                  pl.BlockSpec((1, tq, SWA_W), cur),
                  pl.BlockSpec((1, WINDOW, kvw), prev),
                  pl.BlockSpec((1, tq, kvw), cur),
                  pl.BlockSpec((1, kvw, WINDOW), prev_t),
                  pl.BlockSpec((1, kvw, tq), cur_t),
                  pl.BlockSpec((1, tq, SWA_W), cur)],
        out_specs=pl.BlockSpec((1, tq, SWA_W), cur),
        out_shape=jax.ShapeDtypeStruct((B, S, SWA_W), BF16),
        scratch_shapes=[pltpu.VMEM((nprob, 2 * WINDOW, ncol), F32),
                        pltpu.VMEM((nprob, 2 * WINDOW, ncol), BF16),
                        pltpu.VMEM((nprob, 1, ncol), F32)],
        compiler_params=pltpu.CompilerParams(
            dimension_semantics=("arbitrary", "arbitrary"), vmem_limit_bytes=VMEM_LIMIT),
        name="swa",
    )(sinks, qb, kd, kd, vdt, vdt, gb)


def _outproj_kernel(oa_ref, ob_ref, x_ref, gate_ref, gpost_ref, wa_ref, wb_ref, out_ref):
    y = (jnp.dot(oa_ref[0], wa_ref[...], preferred_element_type=F32)
         + jnp.dot(ob_ref[0], wb_ref[...], preferred_element_type=F32))
    ms = jnp.mean(y * y, axis=-1, keepdims=True)
    yn = y * lax.rsqrt(ms + RMS_EPS) * gpost_ref[...]
    out_ref[0] = x_ref[0] + gate_ref[0] * yn


def _outproj_call(oa, ob, x, gate, g_post, wa, wb):
    B, S, D = x.shape
    tm = TM_OUT
    row = lambda b, t: (b, t, 0)
    const2 = lambda b, t: (0, 0)
    return pl.pallas_call(
        _outproj_kernel,
        grid=(B, S // tm),
        in_specs=[pl.BlockSpec((1, tm, FOX_W), row),
                  pl.BlockSpec((1, tm, SWA_W), row),
                  pl.BlockSpec((1, tm, D), row),
                  pl.BlockSpec((1, 1, D), lambda b, t: (b, 0, 0)),
                  pl.BlockSpec((1, D), const2),
                  pl.BlockSpec((FOX_W, D), const2),
                  pl.BlockSpec((SWA_W, D), const2)],
        out_specs=pl.BlockSpec((1, tm, D), row),
        out_shape=jax.ShapeDtypeStruct((B, S, D), F32),
        compiler_params=pltpu.CompilerParams(
            dimension_semantics=("arbitrary", "arbitrary"), vmem_limit_bytes=VMEM_LIMIT),
        name="out_proj",
    )(oa, ob, x, gate, g_post, wa, wb)


def _perm_w_in(w):
    sc = HEAD_DIM ** -0.5
    o = 0
    qa = w[:, o:o + FOX_W]; o += FOX_W
    ka = w[:, o:o + FOX_W]; o += FOX_W
    va = w[:, o:o + FOX_W]; o += FOX_W
    fa = w[:, o:o + FOX_HEADS]; o += FOX_HEADS
    za = w[:, o:o + FOX_W]; o += FOX_W
    qb = w[:, o:o + SWA_W]; o += SWA_W
    kb = w[:, o:o + SWA_KV_W]; o += SWA_KV_W
    vb = w[:, o:o + SWA_KV_W]; o += SWA_KV_W
    zb = w[:, o:o + SWA_W]
    fpad = jnp.concatenate(
        [fa, fa, fa, jnp.zeros((w.shape[0], LANES - 3 * FOX_HEADS), w.dtype)], axis=1)
    sc2 = sc * LOG2E
    return jnp.concatenate([qa * sc2, ka, va, za, qb * sc2, zb, kb, vb, fpad],
                           axis=1).astype(BF16)


def _aug_constants():
    eq = np.zeros((LANES, AUG_W), np.float32)
    ek = np.zeros((LANES, AUG_W), np.float32)
    oq = np.zeros((1, AUG_W), np.float32)
    ok = np.zeros((1, AUG_W), np.float32)
    for hd in range(FOX_HEADS):
        base = LANES * hd + (HEAD_DIM if hd % 2 == 0 else 0)
        for part in range(3):
            eq[part * FOX_HEADS + hd, base + part] = 1.0
            ok[0, base + part] = 1.0
            ek[part * FOX_HEADS + hd, base + 3 + part] = -1.0
            oq[0, base + 3 + part] = 1.0
    return (jnp.asarray(eq, BF16), jnp.asarray(ek, BF16), jnp.asarray(oq), jnp.asarray(ok))


def kernel(x, c, positions, w_ada, b_ada, g_pre, w_in, b_fgate, sinks, w_out, g_post):
    B, S, D = x.shape
    depth = w_ada.shape[0]
    assert TM_IN == TQ_FOX == TK_FOX
    half = HEAD_DIM // 2
    inv_freq = ROPE_THETA ** (-jnp.arange(half, dtype=F32) / half)
    invf = jnp.tile(inv_freq, LANES // half)[:, None]
    pos3 = positions[:, None, :]
    ind = np.zeros((2 * FOX_W, LANES), np.float32)
    ind[np.arange(2 * FOX_W), np.arange(2 * FOX_W) // HEAD_DIM] = 1.0
    ind = jnp.asarray(ind, BF16)
    tri = jnp.asarray(np.tril(np.ones((TM_IN, TM_IN), np.float32)), BF16)
    eq, ek, oq, ok = _aug_constants()
    c_pad = jnp.zeros((8, D), F32).at[:B].set(c)
    for l in range(depth):
        mod = _mod_call(c_pad, w_ada[l], b_ada[l][None, :])[:B]
        shift = mod[:, None, 0:D]
        scale = mod[:, None, D:2 * D]
        gate = mod[:, None, 2 * D:3 * D]
        bf_pad = jnp.concatenate(
            [b_fgate[l]] * 3 + [jnp.zeros((LANES - 3 * FOX_HEADS,), F32)])[None, :]
        qaug, kaug, vaug, ga, qb, kd, vd, gb, stats = _inproj_call(
            x, shift, scale, g_pre[l][None, :], _perm_w_in(w_in[l]), pos3, invf, bf_pad,
            tri, eq, ek, oq, ok, ind)
        oa = _fox_call(qaug, kaug, vaug, ga, stats[:, :, 0:3, 0:2 * FOX_HEADS])
        ob = _swa_call(sinks[l], qb, kd, vd, gb)
        wo = w_out[l].astype(BF16)
        x = _outproj_call(oa, ob, x, gate, g_post[l][None, :], wo[:FOX_W], wo[FOX_W:])
    return x
```

```python
import functools

import jax
import jax.numpy as jnp
import numpy as np
from jax import lax
from jax.experimental import pallas as pl
from jax.experimental.pallas import tpu as pltpu

D_MODEL = 1024
HEAD_DIM = 64
FOX_HEADS = 8
SWA_Q_HEADS = 8
SWA_KV_HEADS = 2
WINDOW = 128
ROPE_THETA = 10000.0
RMS_EPS = 1e-6
FOX_W = FOX_HEADS * HEAD_DIM
SWA_W = SWA_Q_HEADS * HEAD_DIM
SWA_KV_W = SWA_KV_HEADS * HEAD_DIM

LANES = 128
AUG_W = FOX_HEADS * LANES

OFF_QA = 0
OFF_KA = OFF_QA + FOX_W
OFF_VA = OFF_KA + FOX_W
OFF_ZA = OFF_VA + FOX_W
OFF_QB = OFF_ZA + FOX_W
OFF_ZB = OFF_QB + SWA_W
OFF_KD = OFF_ZB + SWA_W
OFF_VD = OFF_KD + SWA_KV_W
OFF_F = OFF_VD + SWA_KV_W
IN_NP = OFF_F + LANES

TM_IN = 512
TQ_FOX = 512
TK_FOX = 512
TQ_SWA = 512
NEG_BIG = -1e30
LOG2E = 1.4426950408889634
PRUNE_T = 140.0
NORM_MARGIN = 1.01
VMEM_LIMIT = 56 * 1024 * 1024

F32 = jnp.float32
BF16 = jnp.bfloat16


def _split3(a):
    hi = a.astype(BF16)
    r = a - hi.astype(F32)
    mid = r.astype(BF16)
    lo = (r - mid.astype(F32)).astype(BF16)
    return hi, mid, lo


def _mod_kernel(c_ref, w_ref, b_ref, o_ref):
    c = c_ref[...]
    sc = c * (1.0 / (1.0 + jnp.exp(-c)))
    o_ref[...] = jnp.dot(sc, w_ref[...], precision=lax.Precision.HIGHEST,
                         preferred_element_type=F32) + b_ref[...]


def _mod_call(c_pad, w_ada, b_ada):
    rows = c_pad.shape[0]
    n = w_ada.shape[1]
    bn = D_MODEL
    return pl.pallas_call(
        _mod_kernel,
        grid=(n // bn,),
        in_specs=[pl.BlockSpec((rows, D_MODEL), lambda j: (0, 0)),
                  pl.BlockSpec((D_MODEL, bn), lambda j: (0, j)),
                  pl.BlockSpec((1, bn), lambda j: (0, j))],
        out_specs=pl.BlockSpec((rows, bn), lambda j: (0, j)),
        out_shape=jax.ShapeDtypeStruct((rows, n), F32),
        name="mod",
    )(c_pad, w_ada, b_ada)


def _inproj_kernel(x_ref, shift_ref, scale_ref, gpre_ref, w_ref, pos_ref, invf_ref, bf_ref,
                   tri_ref, eq_ref, ek_ref, oq_ref, ok_ref, ind_ref,
                   qaug_ref, kaug_ref, vaug_ref, ga_ref, qb_ref, kd_ref, vd_ref, gb_ref, st_ref,
                   carry_ref):
    tm = x_ref.shape[1]

    @pl.when(pl.program_id(1) == 0)
    def _():
        carry_ref[...] = jnp.zeros_like(carry_ref)

    x = x_ref[0]
    ms = jnp.mean(x * x, axis=-1, keepdims=True)
    y = x * lax.rsqrt(ms + RMS_EPS) * gpre_ref[...]
    h = (y * (1.0 + scale_ref[0]) + shift_ref[0]).astype(BF16)

    def proj(off, width):
        return jnp.dot(h, w_ref[:, off:off + width], preferred_element_type=F32)

    f = proj(OFF_F, LANES) + bf_ref[...]
    ls = jnp.minimum(f, 0.0) - jnp.log1p(jnp.exp(-jnp.abs(f)))
    lane = lax.broadcasted_iota(jnp.int32, (tm, LANES), 1)

    def by_group(a, b_, c_):
        return jnp.where(lane < FOX_HEADS, a, jnp.where(lane < 2 * FOX_HEADS, b_, c_))

    part = by_group(*(t.astype(F32) for t in _split3(ls))).astype(BF16)
    psum = jnp.dot(tri_ref[...], part, preferred_element_type=F32)
    cum = (psum + pltpu.roll(psum, LANES - FOX_HEADS, 1)
           + pltpu.roll(psum, LANES - 2 * FOX_HEADS, 1))
    cum = cum + carry_ref[...]
    carry_ref[...] = cum[tm - 1:tm, :]

    cum2 = cum * LOG2E
    cum2 = by_group(cum2, pltpu.roll(cum2, FOX_HEADS, 1), pltpu.roll(cum2, 2 * FOX_HEADS, 1))
    cs = by_group(*(t.astype(F32) for t in _split3(cum2))).astype(BF16)
    augq = jnp.dot(cs, eq_ref[...], preferred_element_type=F32) + oq_ref[...]
    augk = jnp.dot(cs, ek_ref[...], preferred_element_type=F32) + ok_ref[...]

    lane_w = lax.broadcasted_iota(jnp.int32, (tm, AUG_W), 1)
    data = (((lane_w >> 6) ^ (lane_w >> 7)) & 1) == 0

    def rep(a):
        return jnp.concatenate(
            [a[:, LANES * (hd // 2):LANES * (hd // 2 + 1)] for hd in range(FOX_HEADS)], axis=1)

    qa = proj(OFF_QA, FOX_W)
    ka = proj(OFF_KA, FOX_W)
    qaug_ref[0] = jnp.where(data, rep(qa), augq).T.astype(BF16)
    kaug_ref[0] = jnp.where(data, rep(ka), augk).astype(BF16)

    q16 = qa.astype(BF16).astype(F32)
    k16 = ka.astype(BF16).astype(F32)
    sq = jnp.concatenate([q16 * q16, k16 * k16], axis=1).astype(BF16)
    nrm2 = jnp.dot(sq, ind_ref[...], preferred_element_type=F32)
    nmax = jnp.sqrt(jnp.max(nrm2, axis=0, keepdims=True)) * NORM_MARGIN
    st_ref[0, 0] = jnp.concatenate(
        [nmax, cum2[0:1], cum2[tm - 1:tm], jnp.zeros((5, LANES), F32)], axis=0)
    vaug_ref[0] = jnp.where(data, rep(proj(OFF_VA, FOX_W)), 1.0).T.astype(BF16)

    za = proj(OFF_ZA, FOX_W)
    ga_ref[0] = (za * (1.0 / (1.0 + jnp.exp(-za)))).astype(BF16)
    zb = proj(OFF_ZB, SWA_W)
    gb_ref[0] = (zb * (1.0 / (1.0 + jnp.exp(-zb)))).astype(BF16)

    ang_t = invf_ref[...] * pos_ref[0].astype(F32)
    cosv = jnp.cos(ang_t).T
    sinv = jnp.sin(ang_t).T
    first = (lane & (HEAD_DIM // 2)) == 0
    sin_signed = jnp.where(first, -sinv, sinv)

    def rope(a):
        outs = []
        for cidx in range(a.shape[1] // LANES):
            blk = a[:, cidx * LANES:(cidx + 1) * LANES]
            other = jnp.where(first, pltpu.roll(blk, LANES - HEAD_DIM // 2, 1),
                              pltpu.roll(blk, HEAD_DIM // 2, 1))
            outs.append(blk * cosv + other * sin_signed)
        return jnp.concatenate(outs, axis=1)

    qb_ref[0] = rope(proj(OFF_QB, SWA_W)).astype(BF16)
    kr = rope(proj(OFF_KD, SWA_KV_W))
    ks = pltpu.roll(kr, HEAD_DIM, 1)
    low = lane < HEAD_DIM
    kd_ref[0] = jnp.concatenate([jnp.where(low, kr, ks), jnp.where(low, ks, kr)],
                                axis=1).astype(BF16)
    vt = proj(OFF_VD, SWA_KV_W).T
    ones = jnp.ones((HEAD_DIM, tm), F32)
    vd_ref[0] = jnp.concatenate([vt[:HEAD_DIM], ones, vt[HEAD_DIM:], ones],
                                axis=0).astype(BF16)


def _inproj_call(x, shift, scale, g_pre, w_perm, pos3, invf, bf_pad, tri, eq, ek, oq, ok, ind):
    B, S, D = x.shape
    tm = TM_IN
    row = lambda b, t: (b, t, 0)
    per_b = lambda b, t: (b, 0, 0)
    const2 = lambda b, t: (0, 0)
    out_w = (AUG_W, AUG_W, AUG_W, FOX_W, SWA_W, 2 * SWA_KV_W, 2 * SWA_KV_W, SWA_W)
    transposed = (0, 2, 6)
    return pl.pallas_call(
        _inproj_kernel,
        grid=(B, S // tm),
        in_specs=[pl.BlockSpec((1, tm, D), row),
                  pl.BlockSpec((1, 1, D), per_b),
                  pl.BlockSpec((1, 1, D), per_b),
                  pl.BlockSpec((1, D), const2),
                  pl.BlockSpec((D, IN_NP), const2),
                  pl.BlockSpec((1, 1, tm), lambda b, t: (b, 0, t)),
                  pl.BlockSpec((LANES, 1), const2),
                  pl.BlockSpec((1, LANES), const2),
                  pl.BlockSpec((tm, tm), const2),
                  pl.BlockSpec((LANES, AUG_W), const2),
                  pl.BlockSpec((LANES, AUG_W), const2),
                  pl.BlockSpec((1, AUG_W), const2),
                  pl.BlockSpec((1, AUG_W), const2),
                  pl.BlockSpec((2 * FOX_W, LANES), const2)],
        out_specs=[pl.BlockSpec((1, w, tm), lambda b, t: (b, 0, t)) if i in transposed
                   else pl.BlockSpec((1, tm, w), row) for i, w in enumerate(out_w)]
        + [pl.BlockSpec((1, 1, 8, LANES), lambda b, t: (b, t, 0, 0))],
        out_shape=[jax.ShapeDtypeStruct((B, w, S) if i in transposed else (B, S, w), BF16)
                   for i, w in enumerate(out_w)]
        + [jax.ShapeDtypeStruct((B, S // tm, 8, LANES), F32)],
        scratch_shapes=[pltpu.VMEM((1, LANES), F32)],
        compiler_params=pltpu.CompilerParams(
            dimension_semantics=("arbitrary", "arbitrary"), vmem_limit_bytes=VMEM_LIMIT),
        name="in_proj",
    )(x, shift, scale, g_pre, w_perm, pos3, invf, bf_pad, tri, eq, ek, oq, ok, ind)


def _fox_tables(nq):
    diag = [(qi, qi, hh * nq + qi, hh) for hh in range(2) for qi in range(nq)]
    tab = np.array(diag + [(1, 0, 2 * nq, 0)], np.int32).T
    return tab, len(diag), nq * (nq - 1)


def _fox_kernel(tab_ref, st_ref, q_ref, k_ref, v_ref, g_ref, o_ref,
                off_tab, s0, s1, p0, p1, al0, al1, m_st, acc_st, *, n_diag, n_off):
    tq = TQ_FOX
    tk = TK_FOX
    nq = q_ref.shape[2] // tq
    s_buf = (s0, s1)
    p_buf = (p0, p1)
    al_buf = (al0, al1)

    def run(tab, n_items, dummy, diag):
        half = (n_items + 1) // 2
        lens = (half, n_items - half)

        def item(stream, t):
            return jnp.where(t < lens[stream], stream * half + t, dummy)

        def stage_qk(t, slot):
            for sm in range(2):
                e = item(sm, t)
                qoff = pl.multiple_of(tab[0, e] * tq, tq)
                koff = pl.multiple_of(tab[1, e] * tk, tk)
                hoff = pl.multiple_of(tab[3, e] * LANES, LANES)
                qt = q_ref[0, pl.ds(hoff, LANES), pl.ds(qoff, tq)]
                k = k_ref[0, pl.ds(koff, tk), pl.ds(hoff, LANES)]
                s_buf[slot][sm] = jnp.dot(k, qt, preferred_element_type=F32)

        def stage_softmax(t, slot):
            for sm in range(2):
                st = tab[2, item(sm, t)]
                s = s_buf[slot][sm]
                if diag:
                    kr = lax.broadcasted_iota(jnp.int32, (tk, tq), 0)
                    qc = lax.broadcasted_iota(jnp.int32, (tk, tq), 1)
                    s = jnp.where(kr <= qc, s, NEG_BIG)
                    m_next = jnp.max(s, axis=0, keepdims=True)
                else:
                    m_prev = m_st[st]
                    m_next = jnp.maximum(m_prev, jnp.max(s, axis=0, keepdims=True))
                    al_buf[slot][sm] = jnp.exp2(m_prev - m_next)
                p_buf[slot][sm] = jnp.exp2(s - m_next).astype(BF16)
                m_st[st] = m_next

        def stage_pv(t, slot):
            for sm in range(2):
                e = item(sm, t)
                koff = pl.multiple_of(tab[1, e] * tk, tk)
                hoff = pl.multiple_of(tab[3, e] * LANES, LANES)
                st = tab[2, e]
                vt = v_ref[0, pl.ds(hoff, LANES), pl.ds(koff, tk)]
                pv = jnp.dot(vt, p_buf[slot][sm], preferred_element_type=F32)
                if diag:
                    acc_st[st] = pv
                else:
                    acc_st[st] = al_buf[slot][sm] * acc_st[st] + pv

        stage_qk(0, 0)
        stage_qk(1, 1)
        stage_softmax(0, 0)

        def body(u, carry):
            for d in range(2):
                t = 2 * u + 1 + d
                stage_qk(t + 1, d % 2)
                stage_softmax(t, (d + 1) % 2)
                stage_pv(t - 1, d % 2)
            return carry

        lax.fori_loop(0, (half + 1) // 2, body, 0)

    assert tab_ref.shape[1] == n_diag + 1
    run(tab_ref, n_diag, n_diag, True)

    b = pl.program_id(0)
    pr = pl.program_id(1)

    def put(idx, qrow, j, state, head):
        off_tab[0, idx] = qrow
        off_tab[1, idx] = j
        off_tab[2, idx] = state
        off_tab[3, idx] = head

    n_keep = jnp.int32(0)
    for hh in range(2):
        hd = 2 * pr + hh

        kmax = lax.fori_loop(
            0, nq, lambda j, m, hd=hd: jnp.maximum(m, st_ref[b, j, 0, FOX_HEADS + hd]),
            jnp.float32(0.0))

        def list_row(qi, cnt, hh=hh, hd=hd, kmax=kmax):
            base = (st_ref[b, qi, 0, hd] * (st_ref[b, qi, 0, FOX_HEADS + hd] + kmax)
                    + st_ref[b, qi, 1, hd])

            def contributes(carry):
                j, _ = carry
                bound = base - st_ref[b, jnp.maximum(j, 0), 2, hd]
                return jnp.logical_and(j >= 0, bound > -PRUNE_T)

            def take(carry):
                j, cnt = carry
                put(cnt, qi, j, hh * nq + qi, hh)
                return j - 1, cnt + 1

            return lax.while_loop(contributes, take, (qi - 1, cnt))[1]

        n_keep = lax.fori_loop(1, nq, list_row, n_keep)
    put(n_off, 1, 0, 2 * nq, 0)
    run(off_tab, n_keep, n_off, False)

    def finish(qi, carry):
        a0 = acc_st[qi]
        a1 = acc_st[nq + qi]
        ot = jnp.concatenate([a0[:HEAD_DIM] / a0[HEAD_DIM:HEAD_DIM + 1],
                              a1[HEAD_DIM:] / a1[0:1]], axis=0)
        rows = pl.ds(pl.multiple_of(qi * tq, tq), tq)
        o_ref[0, rows, :] = (ot.T * g_ref[0, rows, :].astype(F32)).astype(BF16)
        return carry

    lax.fori_loop(0, nq, finish, 0)


def _fox_call(qaug_t, kaug, vaug_t, ga, stats):
    B, S, _ = kaug.shape
    tq, tk = TQ_FOX, TK_FOX
    assert tq == tk
    nq = S // tq
    tab, n_diag, n_off = _fox_tables(nq)
    pair = lambda b, p: (b, 0, p)
    return pl.pallas_call(
        functools.partial(_fox_kernel, n_diag=n_diag, n_off=n_off),
        grid=(B, FOX_HEADS // 2),
        in_specs=[pl.BlockSpec(memory_space=pltpu.SMEM),
                  pl.BlockSpec(memory_space=pltpu.SMEM),
                  pl.BlockSpec((1, 2 * LANES, S), lambda b, p: (b, p, 0)),
                  pl.BlockSpec((1, S, 2 * LANES), pair),
                  pl.BlockSpec((1, 2 * LANES, S), lambda b, p: (b, p, 0)),
                  pl.BlockSpec((1, S, LANES), pair)],
        out_specs=pl.BlockSpec((1, S, LANES), pair),
        out_shape=jax.ShapeDtypeStruct((B, S, FOX_W), BF16),
        scratch_shapes=[pltpu.SMEM((4, n_off + 1), jnp.int32),
                        pltpu.VMEM((2, tk, tq), F32), pltpu.VMEM((2, tk, tq), F32),
                        pltpu.VMEM((2, tk, tq), BF16), pltpu.VMEM((2, tk, tq), BF16),
                        pltpu.VMEM((2, 1, tq), F32), pltpu.VMEM((2, 1, tq), F32),
                        pltpu.VMEM((2 * nq + 1, 1, tq), F32),
                        pltpu.VMEM((2 * nq + 1, LANES, tq), F32)],
        compiler_params=pltpu.CompilerParams(
            dimension_semantics=("arbitrary", "arbitrary"), vmem_limit_bytes=VMEM_LIMIT),
        name="fox",
    )(jnp.asarray(tab), stats, qaug_t, kaug, vaug_t, ga)


def _swa_out_kernel(sinks_ref, q_ref, kp_ref, kc_ref, vtp_ref, vtc_ref, g_ref,
                    oa_ref, x_ref, gate_ref, gpost_ref, wa_ref, wb_ref, out_ref,
                    s_sc, p_sc, e_sc, ob_sc):
    i = pl.program_id(1)
    nsub = TQ_SWA // WINDOW
    group = SWA_Q_HEADS // SWA_KV_HEADS
    ncol = group * WINDOW
    kall = jnp.concatenate([kp_ref[0], kc_ref[0]], axis=0)
    vtall = jnp.concatenate([vtp_ref[0], vtc_ref[0]], axis=1)
    lane = lax.broadcasted_iota(jnp.int32, (WINDOW, LANES), 1)
    lo = lane < HEAD_DIM
    zero = jnp.zeros((WINDOW, LANES), BF16)
    kj_ = lax.broadcasted_iota(jnp.int32, (2 * WINDOW, ncol), 0)
    qi_ = lax.broadcasted_iota(jnp.int32, (2 * WINDOW, ncol), 1) & (WINDOW - 1)
    rel = qi_ + WINDOW - kj_
    band = (rel >= 0) & (rel < WINDOW)
    nt = (((1,), (1,)), ((), ()))
    probs = [(r, g) for r in range(nsub) for g in range(SWA_KV_HEADS)]

    def stage_qk(n):
        r, g = probs[n]
        rows = slice(r * WINDOW, (r + 1) * WINDOW)
        kg = kall[r * WINDOW:(r + 2) * WINDOW, g * LANES:(g + 1) * LANES]
        c0 = g * group * HEAD_DIM
        qp0 = q_ref[0, rows, c0:c0 + LANES]
        qp1 = q_ref[0, rows, c0 + LANES:c0 + 2 * LANES]
        qs = jnp.concatenate([jnp.where(lo, qp0, zero), jnp.where(lo, qp1, zero),
                              jnp.where(lo, zero, qp0), jnp.where(lo, zero, qp1)], axis=0)
        s_sc[n] = lax.dot_general(kg, qs, nt, preferred_element_type=F32)

    def stage_softmax(n):
        r, g = probs[n]
        valid = band & (kj_ >= jnp.where(i == 0, WINDOW, 0)) if r == 0 else band
        st = jnp.where(valid, s_sc[n], NEG_BIG)
        heads = (group * g, group * g + 2, group * g + 1, group * g + 3)
        sink = jnp.concatenate(
            [jnp.full((1, WINDOW), sinks_ref[hd] * LOG2E, F32) for hd in heads], axis=1)
        m = jnp.maximum(jnp.max(st, axis=0, keepdims=True), sink)
        p_sc[n] = jnp.exp2(st - m).astype(BF16)
        e_sc[n] = jnp.exp2(sink - m)

    def stage_pv(n):
        r, g = probs[n]
        rows = slice(r * WINDOW, (r + 1) * WINDOW)
        c0 = g * group * HEAD_DIM
        vtg = vtall[g * LANES:(g + 1) * LANES, r * WINDOW:(r + 2) * WINDOW]
        acc = jnp.dot(vtg, p_sc[n], preferred_element_type=F32)
        l = acc[HEAD_DIM:HEAD_DIM + 1] + e_sc[n]
        on = acc[:HEAD_DIM] * (1.0 / l)
        pair0 = jnp.concatenate([on[:, 0:WINDOW], on[:, 2 * WINDOW:3 * WINDOW]], axis=0).T
        pair1 = jnp.concatenate([on[:, WINDOW:2 * WINDOW], on[:, 3 * WINDOW:]], axis=0).T
        ob_sc[rows, c0:c0 + LANES] = (
            pair0 * g_ref[0, rows, c0:c0 + LANES].astype(F32)).astype(BF16)
        ob_sc[rows, c0 + LANES:c0 + 2 * LANES] = (
            pair1 * g_ref[0, rows, c0 + LANES:c0 + 2 * LANES].astype(F32)).astype(BF16)

    ya = jnp.dot(oa_ref[0], wa_ref[...], preferred_element_type=F32)

    for t in range(len(probs) + 2):
        if t < len(probs):
            stage_qk(t)
        if 1 <= t <= len(probs):
            stage_softmax(t - 1)
        if t >= 2:
            stage_pv(t - 2)

    y = ya + jnp.dot(ob_sc[...], wb_ref[...], preferred_element_type=F32)
    ms = jnp.mean(y * y, axis=-1, keepdims=True)
    yn = y * lax.rsqrt(ms + RMS_EPS) * gpost_ref[...]
    out_ref[0] = x_ref[0] + gate_ref[0] * yn


def _swa_out_call(sinks, qb, kd, vdt, gb, oa, x, gate, g_post, wa, wb):
    B, S, D = x.shape
    tq = TQ_SWA
    nsub = tq // WINDOW
    cur = lambda b, i: (b, i, 0)
    prev = lambda b, i: (b, jnp.maximum(i * nsub - 1, 0), 0)
    cur_t = lambda b, i: (b, 0, i)
    prev_t = lambda b, i: (b, 0, jnp.maximum(i * nsub - 1, 0))
    const2 = lambda b, i: (0, 0)
    kvw = 2 * SWA_KV_W
    nprob = nsub * SWA_KV_HEADS
    ncol = SWA_Q_HEADS // SWA_KV_HEADS * WINDOW
    return pl.pallas_call(
        _swa_out_kernel,
        grid=(B, S // tq),
        in_specs=[pl.BlockSpec(memory_space=pltpu.SMEM),
                  pl.BlockSpec((1, tq, SWA_W), cur),
                  pl.BlockSpec((1, WINDOW, kvw), prev),
                  pl.BlockSpec((1, tq, kvw), cur),
                  pl.BlockSpec((1, kvw, WINDOW), prev_t),
                  pl.BlockSpec((1, kvw, tq), cur_t),
                  pl.BlockSpec((1, tq, SWA_W), cur),
                  pl.BlockSpec((1, tq, FOX_W), cur),
                  pl.BlockSpec((1, tq, D), cur),
                  pl.BlockSpec((1, 1, D), lambda b, i: (b, 0, 0)),
                  pl.BlockSpec((1, D), const2),
                  pl.BlockSpec((FOX_W, D), const2),
                  pl.BlockSpec((SWA_W, D), const2)],
        out_specs=pl.BlockSpec((1, tq, D), cur),
        out_shape=jax.ShapeDtypeStruct((B, S, D), F32),
        scratch_shapes=[pltpu.VMEM((nprob, 2 * WINDOW, ncol), F32),
                        pltpu.VMEM((nprob, 2 * WINDOW, ncol), BF16),
                        pltpu.VMEM((nprob, 1, ncol), F32),
                        pltpu.VMEM((tq, SWA_W), BF16)],
        compiler_params=pltpu.CompilerParams(
            dimension_semantics=("arbitrary", "arbitrary"), vmem_limit_bytes=VMEM_LIMIT),
        name="swa_out",
    )(sinks, qb, kd, kd, vdt, vdt, gb, oa, x, gate, g_post, wa, wb)


def _perm_w_in(w):
    sc = HEAD_DIM ** -0.5
    o = 0
    qa = w[:, o:o + FOX_W]; o += FOX_W
    ka = w[:, o:o + FOX_W]; o += FOX_W
    va = w[:, o:o + FOX_W]; o += FOX_W
    fa = w[:, o:o + FOX_HEADS]; o += FOX_HEADS
    za = w[:, o:o + FOX_W]; o += FOX_W
    qb = w[:, o:o + SWA_W]; o += SWA_W
    kb = w[:, o:o + SWA_KV_W]; o += SWA_KV_W
    vb = w[:, o:o + SWA_KV_W]; o += SWA_KV_W
    zb = w[:, o:o + SWA_W]
    fpad = jnp.concatenate(
        [fa, fa, fa, jnp.zeros((w.shape[0], LANES - 3 * FOX_HEADS), w.dtype)], axis=1)
    sc2 = sc * LOG2E
    return jnp.concatenate([qa * sc2, ka, va, za, qb * sc2, zb, kb, vb, fpad],
                           axis=1).astype(BF16)


def _aug_constants():
    eq = np.zeros((LANES, AUG_W), np.float32)
    ek = np.zeros((LANES, AUG_W), np.float32)
    oq = np.zeros((1, AUG_W), np.float32)
    ok = np.zeros((1, AUG_W), np.float32)
    for hd in range(FOX_HEADS):
        base = LANES * hd + (HEAD_DIM if hd % 2 == 0 else 0)
        for part in range(3):
            eq[part * FOX_HEADS + hd, base + part] = 1.0
            ok[0, base + part] = 1.0
            ek[part * FOX_HEADS + hd, base + 3 + part] = -1.0
            oq[0, base + 3 + part] = 1.0
    return (jnp.asarray(eq, BF16), jnp.asarray(ek, BF16), jnp.asarray(oq), jnp.asarray(ok))


def kernel(x, c, positions, w_ada, b_ada, g_pre, w_in, b_fgate, sinks, w_out, g_post):
    B, S, D = x.shape
    depth = w_ada.shape[0]
    assert TM_IN == TQ_FOX == TK_FOX
    half = HEAD_DIM // 2
    inv_freq = ROPE_THETA ** (-jnp.arange(half, dtype=F32) / half)
    invf = jnp.tile(inv_freq, LANES // half)[:, None]
    pos3 = positions[:, None, :]
    ind = np.zeros((2 * FOX_W, LANES), np.float32)
    ind[np.arange(2 * FOX_W), np.arange(2 * FOX_W) // HEAD_DIM] = 1.0
    ind = jnp.asarray(ind, BF16)
    tri = jnp.asarray(np.tril(np.ones((TM_IN, TM_IN), np.float32)), BF16)
    eq, ek, oq, ok = _aug_constants()
    c_pad = jnp.zeros((8, D), F32).at[:B].set(c)
    for l in range(depth):
        mod = _mod_call(c_pad, w_ada[l], b_ada[l][None, :])[:B]
        shift = mod[:, None, 0:D]
        scale = mod[:, None, D:2 * D]
        gate = mod[:, None, 2 * D:3 * D]
        bf_pad = jnp.concatenate(
            [b_fgate[l]] * 3 + [jnp.zeros((LANES - 3 * FOX_HEADS,), F32)])[None, :]
        qaug, kaug, vaug, ga, qb, kd, vd, gb, stats = _inproj_call(
            x, shift, scale, g_pre[l][None, :], _perm_w_in(w_in[l]), pos3, invf, bf_pad,
            tri, eq, ek, oq, ok, ind)
        oa = _fox_call(qaug, kaug, vaug, ga, stats[:, :, 0:3, 0:2 * FOX_HEADS])
        wo = w_out[l].astype(BF16)
        x = _swa_out_call(sinks[l], qb, kd, vd, gb, oa, x, gate, g_post[l][None, :],
                          wo[:FOX_W], wo[FOX_W:])
    return x
```

```python
import functools

import jax
import jax.numpy as jnp
import numpy as np
from jax import lax
from jax.experimental import pallas as pl
from jax.experimental.pallas import tpu as pltpu

D_MODEL = 1024
HEAD_DIM = 64
FOX_HEADS = 8
SWA_Q_HEADS = 8
SWA_KV_HEADS = 2
WINDOW = 128
ROPE_THETA = 10000.0
RMS_EPS = 1e-6
FOX_W = FOX_HEADS * HEAD_DIM
SWA_W = SWA_Q_HEADS * HEAD_DIM
SWA_KV_W = SWA_KV_HEADS * HEAD_DIM

LANES = 128
AUG_W = FOX_HEADS * LANES

OFF_QA = 0
OFF_KA = OFF_QA + FOX_W
OFF_VA = OFF_KA + FOX_W
OFF_ZA = OFF_VA + FOX_W
OFF_QB = OFF_ZA + FOX_W
OFF_ZB = OFF_QB + SWA_W
OFF_KD = OFF_ZB + SWA_W
OFF_VD = OFF_KD + SWA_KV_W
OFF_F = OFF_VD + SWA_KV_W
IN_NP = OFF_F + LANES

TM_IN = 512
TQ_FOX = 512
TK_FOX = 512
TQ_SWA = 512
NEG_BIG = -1e30
LOG2E = 1.4426950408889634
PRUNE_T = 140.0
NORM_MARGIN = 1.01
FIXED_REF_T = 100.0
VMEM_LIMIT = 56 * 1024 * 1024

F32 = jnp.float32
BF16 = jnp.bfloat16


def _split3(a):
    hi = a.astype(BF16)
    r = a - hi.astype(F32)
    mid = r.astype(BF16)
    lo = (r - mid.astype(F32)).astype(BF16)
    return hi, mid, lo


def _mod_kernel(c_ref, w_ref, b_ref, o_ref):
    c = c_ref[...]
    sc = c * (1.0 / (1.0 + jnp.exp(-c)))
    o_ref[...] = jnp.dot(sc, w_ref[...], precision=lax.Precision.HIGHEST,
                         preferred_element_type=F32) + b_ref[...]


def _mod_call(c_pad, w_ada, b_ada):
    rows = c_pad.shape[0]
    n = w_ada.shape[1]
    bn = D_MODEL
    return pl.pallas_call(
        _mod_kernel,
        grid=(n // bn,),
        in_specs=[pl.BlockSpec((rows, D_MODEL), lambda j: (0, 0)),
                  pl.BlockSpec((D_MODEL, bn), lambda j: (0, j)),
                  pl.BlockSpec((1, bn), lambda j: (0, j))],
        out_specs=pl.BlockSpec((rows, bn), lambda j: (0, j)),
        out_shape=jax.ShapeDtypeStruct((rows, n), F32),
        name="mod",
    )(c_pad, w_ada, b_ada)


def _inproj_kernel(x_ref, shift_ref, scale_ref, gpre_ref, w_ref, pos_ref, invf_ref, bf_ref,
                   tri_ref, eq_ref, ek_ref, oq_ref, ok_ref, ind_ref,
                   qaug_ref, kaug_ref, vaug_ref, ga_ref, qb_ref, kd_ref, vd_ref, gb_ref, st_ref,
                   carry_ref):
    tm = x_ref.shape[1]

    @pl.when(pl.program_id(1) == 0)
    def _():
        carry_ref[...] = jnp.zeros_like(carry_ref)

    x = x_ref[0]
    ms = jnp.mean(x * x, axis=-1, keepdims=True)
    y = x * lax.rsqrt(ms + RMS_EPS) * gpre_ref[...]
    h = (y * (1.0 + scale_ref[0]) + shift_ref[0]).astype(BF16)

    def proj(off, width):
        return jnp.dot(h, w_ref[:, off:off + width], preferred_element_type=F32)

    f = proj(OFF_F, LANES) + bf_ref[...]
    ls = jnp.minimum(f, 0.0) - jnp.log1p(jnp.exp(-jnp.abs(f)))
    lane = lax.broadcasted_iota(jnp.int32, (tm, LANES), 1)

    def by_group(a, b_, c_):
        return jnp.where(lane < FOX_HEADS, a, jnp.where(lane < 2 * FOX_HEADS, b_, c_))

    part = by_group(*(t.astype(F32) for t in _split3(ls))).astype(BF16)
    psum = jnp.dot(tri_ref[...], part, preferred_element_type=F32)
    cum = (psum + pltpu.roll(psum, LANES - FOX_HEADS, 1)
           + pltpu.roll(psum, LANES - 2 * FOX_HEADS, 1))
    cum = cum + carry_ref[...]
    carry_ref[...] = cum[tm - 1:tm, :]

    cum2 = cum * LOG2E
    cum2 = by_group(cum2, pltpu.roll(cum2, FOX_HEADS, 1), pltpu.roll(cum2, 2 * FOX_HEADS, 1))
    cs = by_group(*(t.astype(F32) for t in _split3(cum2))).astype(BF16)
    augq = jnp.dot(cs, eq_ref[...], preferred_element_type=F32) + oq_ref[...]
    augk = jnp.dot(cs, ek_ref[...], preferred_element_type=F32) + ok_ref[...]

    lane_w = lax.broadcasted_iota(jnp.int32, (tm, AUG_W), 1)
    data = (((lane_w >> 6) ^ (lane_w >> 7)) & 1) == 0

    def rep(a):
        return jnp.concatenate(
            [a[:, LANES * (hd // 2):LANES * (hd // 2 + 1)] for hd in range(FOX_HEADS)], axis=1)

    qa = proj(OFF_QA, FOX_W)
    ka = proj(OFF_KA, FOX_W)
    qaug_ref[0] = jnp.where(data, rep(qa), augq).T.astype(BF16)
    kaug_ref[0] = jnp.where(data, rep(ka), augk).astype(BF16)

    q16 = qa.astype(BF16).astype(F32)
    k16 = ka.astype(BF16).astype(F32)
    sq = jnp.concatenate([q16 * q16, k16 * k16], axis=1).astype(BF16)
    nrm2 = jnp.dot(sq, ind_ref[...], preferred_element_type=F32)
    nmax = jnp.sqrt(jnp.max(nrm2, axis=0, keepdims=True)) * NORM_MARGIN
    st_ref[0, 0] = jnp.concatenate(
        [nmax, cum2[0:1], cum2[tm - 1:tm], jnp.zeros((5, LANES), F32)], axis=0)
    vaug_ref[0] = jnp.where(data, rep(proj(OFF_VA, FOX_W)), 1.0).T.astype(BF16)

    za = proj(OFF_ZA, FOX_W)
    ga_ref[0] = (za * (1.0 / (1.0 + jnp.exp(-za)))).astype(BF16)
    zb = proj(OFF_ZB, SWA_W)
    gb_ref[0] = (zb * (1.0 / (1.0 + jnp.exp(-zb)))).astype(BF16)

    ang_t = invf_ref[...] * pos_ref[0].astype(F32)
    cosv = jnp.cos(ang_t).T
    sinv = jnp.sin(ang_t).T
    first = (lane & (HEAD_DIM // 2)) == 0
    sin_signed = jnp.where(first, -sinv, sinv)

    def rope(a):
        outs = []
        for cidx in range(a.shape[1] // LANES):
            blk = a[:, cidx * LANES:(cidx + 1) * LANES]
            other = jnp.where(first, pltpu.roll(blk, LANES - HEAD_DIM // 2, 1),
                              pltpu.roll(blk, HEAD_DIM // 2, 1))
            outs.append(blk * cosv + other * sin_signed)
        return jnp.concatenate(outs, axis=1)

    qb_ref[0] = rope(proj(OFF_QB, SWA_W)).astype(BF16)
    kr = rope(proj(OFF_KD, SWA_KV_W))
    ks = pltpu.roll(kr, HEAD_DIM, 1)
    low = lane < HEAD_DIM
    kd_ref[0] = jnp.concatenate([jnp.where(low, kr, ks), jnp.where(low, ks, kr)],
                                axis=1).astype(BF16)
    vt = proj(OFF_VD, SWA_KV_W).T
    ones = jnp.ones((HEAD_DIM, tm), F32)
    vd_ref[0] = jnp.concatenate([vt[:HEAD_DIM], ones, vt[HEAD_DIM:], ones],
                                axis=0).astype(BF16)


def _inproj_call(x, shift, scale, g_pre, w_perm, pos3, invf, bf_pad, tri, eq, ek, oq, ok, ind):
    B, S, D = x.shape
    tm = TM_IN
    row = lambda b, t: (b, t, 0)
    per_b = lambda b, t: (b, 0, 0)
    const2 = lambda b, t: (0, 0)
    out_w = (AUG_W, AUG_W, AUG_W, FOX_W, SWA_W, 2 * SWA_KV_W, 2 * SWA_KV_W, SWA_W)
    transposed = (0, 2, 6)
    return pl.pallas_call(
        _inproj_kernel,
        grid=(B, S // tm),
        in_specs=[pl.BlockSpec((1, tm, D), row),
                  pl.BlockSpec((1, 1, D), per_b),
                  pl.BlockSpec((1, 1, D), per_b),
                  pl.BlockSpec((1, D), const2),
                  pl.BlockSpec((D, IN_NP), const2),
                  pl.BlockSpec((1, 1, tm), lambda b, t: (b, 0, t)),
                  pl.BlockSpec((LANES, 1), const2),
                  pl.BlockSpec((1, LANES), const2),
                  pl.BlockSpec((tm, tm), const2),
                  pl.BlockSpec((LANES, AUG_W), const2),
                  pl.BlockSpec((LANES, AUG_W), const2),
                  pl.BlockSpec((1, AUG_W), const2),
                  pl.BlockSpec((1, AUG_W), const2),
                  pl.BlockSpec((2 * FOX_W, LANES), const2)],
        out_specs=[pl.BlockSpec((1, w, tm), lambda b, t: (b, 0, t)) if i in transposed
                   else pl.BlockSpec((1, tm, w), row) for i, w in enumerate(out_w)]
        + [pl.BlockSpec((1, 1, 8, LANES), lambda b, t: (b, t, 0, 0))],
        out_shape=[jax.ShapeDtypeStruct((B, w, S) if i in transposed else (B, S, w), BF16)
                   for i, w in enumerate(out_w)]
        + [jax.ShapeDtypeStruct((B, S // tm, 8, LANES), F32)],
        scratch_shapes=[pltpu.VMEM((1, LANES), F32)],
        compiler_params=pltpu.CompilerParams(
            dimension_semantics=("arbitrary", "arbitrary"), vmem_limit_bytes=VMEM_LIMIT),
        name="in_proj",
    )(x, shift, scale, g_pre, w_perm, pos3, invf, bf_pad, tri, eq, ek, oq, ok, ind)


def _fox_tables(nq):
    diag = [(qi, qi, hh * nq + qi, hh) for hh in range(2) for qi in range(nq)]
    tab = np.array(diag + [(1, 0, 2 * nq, 0)], np.int32).T
    return tab, len(diag), nq * (nq - 1)


def _fox_kernel(tab_ref, st_ref, q_ref, k_ref, v_ref, g_ref, o_ref,
                off_tab, gen_tab, s0, s1, p0, p1, al0, al1, m_st, acc_st, *, n_diag, n_off):
    tq = TQ_FOX
    tk = TK_FOX
    nq = q_ref.shape[2] // tq
    s_buf = (s0, s1)
    p_buf = (p0, p1)
    al_buf = (al0, al1)

    def run(tab, n_items, dummy, diag):
        half = (n_items + 1) // 2
        lens = (half, n_items - half)

        def item(stream, t):
            return jnp.where(t < lens[stream], stream * half + t, dummy)

        def stage_qk(t, slot):
            for sm in range(2):
                e = item(sm, t)
                qoff = pl.multiple_of(tab[0, e] * tq, tq)
                koff = pl.multiple_of(tab[1, e] * tk, tk)
                hoff = pl.multiple_of(tab[3, e] * LANES, LANES)
                qt = q_ref[0, pl.ds(hoff, LANES), pl.ds(qoff, tq)]
                k = k_ref[0, pl.ds(koff, tk), pl.ds(hoff, LANES)]
                s_buf[slot][sm] = jnp.dot(k, qt, preferred_element_type=F32)

        def stage_softmax(t, slot):
            for sm in range(2):
                st = tab[2, item(sm, t)]
                s = s_buf[slot][sm]
                if diag:
                    kr = lax.broadcasted_iota(jnp.int32, (tk, tq), 0)
                    qc = lax.broadcasted_iota(jnp.int32, (tk, tq), 1)
                    s = jnp.where(kr <= qc, s, NEG_BIG)
                    m_next = jnp.max(s, axis=0, keepdims=True)
                else:
                    m_prev = m_st[st]
                    m_next = jnp.maximum(m_prev, jnp.max(s, axis=0, keepdims=True))
                    al_buf[slot][sm] = jnp.exp2(m_prev - m_next)
                p_buf[slot][sm] = jnp.exp2(s - m_next).astype(BF16)
                m_st[st] = m_next

        def stage_pv(t, slot):
            for sm in range(2):
                e = item(sm, t)
                koff = pl.multiple_of(tab[1, e] * tk, tk)
                hoff = pl.multiple_of(tab[3, e] * LANES, LANES)
                st = tab[2, e]
                vt = v_ref[0, pl.ds(hoff, LANES), pl.ds(koff, tk)]
                pv = jnp.dot(vt, p_buf[slot][sm], preferred_element_type=F32)
                if diag:
                    acc_st[st] = pv
                else:
                    acc_st[st] = al_buf[slot][sm] * acc_st[st] + pv

        stage_qk(0, 0)
        stage_qk(1, 1)
        stage_softmax(0, 0)

        def body(u, carry):
            for d in range(2):
                t = 2 * u + 1 + d
                stage_qk(t + 1, d % 2)
                stage_softmax(t, (d + 1) % 2)
                stage_pv(t - 1, d % 2)
            return carry

        lax.fori_loop(0, (half + 1) // 2, body, 0)

    def run_fixed(tab, n_items, dummy):
        half = (n_items + 1) // 2
        lens = (half, n_items - half)

        def item(stream, t):
            return jnp.where(t < lens[stream], stream * half + t, dummy)

        def stage_probs(t, slot):
            for sm in range(2):
                e = item(sm, t)
                qoff = pl.multiple_of(tab[0, e] * tq, tq)
                koff = pl.multiple_of(tab[1, e] * tk, tk)
                hoff = pl.multiple_of(tab[3, e] * LANES, LANES)
                qt = q_ref[0, pl.ds(hoff, LANES), pl.ds(qoff, tq)]
                k = k_ref[0, pl.ds(koff, tk), pl.ds(hoff, LANES)]
                s = jnp.dot(k, qt, preferred_element_type=F32)
                p_buf[slot][sm] = jnp.exp2(s - m_st[tab[2, e]]).astype(BF16)

        def stage_pv(t, slot):
            for sm in range(2):
                e = item(sm, t)
                koff = pl.multiple_of(tab[1, e] * tk, tk)
                hoff = pl.multiple_of(tab[3, e] * LANES, LANES)
                st = tab[2, e]
                vt = v_ref[0, pl.ds(hoff, LANES), pl.ds(koff, tk)]
                acc_st[st] = acc_st[st] + jnp.dot(vt, p_buf[slot][sm],
                                                  preferred_element_type=F32)

        stage_probs(0, 0)

        def body(u, carry):
            for d in range(2):
                t = 2 * u + 1 + d
                stage_probs(t, (d + 1) % 2)
                stage_pv(t - 1, d % 2)
            return carry

        lax.fori_loop(0, (half + 1) // 2, body, 0)

    assert tab_ref.shape[1] == n_diag + 1
    run(tab_ref, n_diag, n_diag, True)

    b = pl.program_id(0)
    pr = pl.program_id(1)

    def put(tab, idx, qrow, j, state, head):
        tab[0, idx] = qrow
        tab[1, idx] = j
        tab[2, idx] = state
        tab[3, idx] = head

    n_fix = jnp.int32(0)
    n_gen = jnp.int32(0)
    for hh in range(2):
        hd = 2 * pr + hh

        kmax = lax.fori_loop(
            0, nq, lambda j, m, hd=hd: jnp.maximum(m, st_ref[b, j, 0, FOX_HEADS + hd]),
            jnp.float32(0.0))

        def list_row(qi, counts, hh=hh, hd=hd, kmax=kmax):
            base = (st_ref[b, qi, 0, hd] * (st_ref[b, qi, 0, FOX_HEADS + hd] + kmax)
                    + st_ref[b, qi, 1, hd])
            fixed_ok = base - st_ref[b, qi - 1, 2, hd] <= FIXED_REF_T

            def scan(tab, enabled, cnt):
                def contributes(carry):
                    j, _ = carry
                    bound = base - st_ref[b, jnp.maximum(j, 0), 2, hd]
                    return jnp.logical_and(jnp.logical_and(enabled, j >= 0), bound > -PRUNE_T)

                def take(carry):
                    j, cnt = carry
                    put(tab, cnt, qi, j, hh * nq + qi, hh)
                    return j - 1, cnt + 1

                return lax.while_loop(contributes, take, (qi - 1, cnt))[1]

            return (scan(off_tab, fixed_ok, counts[0]),
                    scan(gen_tab, jnp.logical_not(fixed_ok), counts[1]))

        n_fix, n_gen = lax.fori_loop(1, nq, list_row, (n_fix, n_gen))
    put(off_tab, n_off, 1, 0, 2 * nq, 0)
    put(gen_tab, n_off, 1, 0, 2 * nq, 0)
    run_fixed(off_tab, n_fix, n_off)

    @pl.when(n_gen > 0)
    def _():
        run(gen_tab, n_gen, n_off, False)

    def finish(qi, carry):
        a0 = acc_st[qi]
        a1 = acc_st[nq + qi]
        ot = jnp.concatenate([a0[:HEAD_DIM] / a0[HEAD_DIM:HEAD_DIM + 1],
                              a1[HEAD_DIM:] / a1[0:1]], axis=0)
        rows = pl.ds(pl.multiple_of(qi * tq, tq), tq)
        o_ref[0, rows, :] = (ot.T * g_ref[0, rows, :].astype(F32)).astype(BF16)
        return carry

    lax.fori_loop(0, nq, finish, 0)


def _fox_call(qaug_t, kaug, vaug_t, ga, stats):
    B, S, _ = kaug.shape
    tq, tk = TQ_FOX, TK_FOX
    assert tq == tk
    nq = S // tq
    tab, n_diag, n_off = _fox_tables(nq)
    pair = lambda b, p: (b, 0, p)
    return pl.pallas_call(
        functools.partial(_fox_kernel, n_diag=n_diag, n_off=n_off),
        grid=(B, FOX_HEADS // 2),
        in_specs=[pl.BlockSpec(memory_space=pltpu.SMEM),
                  pl.BlockSpec(memory_space=pltpu.SMEM),
                  pl.BlockSpec((1, 2 * LANES, S), lambda b, p: (b, p, 0)),
                  pl.BlockSpec((1, S, 2 * LANES), pair),
                  pl.BlockSpec((1, 2 * LANES, S), lambda b, p: (b, p, 0)),
                  pl.BlockSpec((1, S, LANES), pair)],
        out_specs=pl.BlockSpec((1, S, LANES), pair),
        out_shape=jax.ShapeDtypeStruct((B, S, FOX_W), BF16),
        scratch_shapes=[pltpu.SMEM((4, n_off + 1), jnp.int32),
                        pltpu.SMEM((4, n_off + 1), jnp.int32),
                        pltpu.VMEM((2, tk, tq), F32), pltpu.VMEM((2, tk, tq), F32),
                        pltpu.VMEM((2, tk, tq), BF16), pltpu.VMEM((2, tk, tq), BF16),
                        pltpu.VMEM((2, 1, tq), F32), pltpu.VMEM((2, 1, tq), F32),
                        pltpu.VMEM((2 * nq + 1, 1, tq), F32),
                        pltpu.VMEM((2 * nq + 1, LANES, tq), F32)],
        compiler_params=pltpu.CompilerParams(
            dimension_semantics=("arbitrary", "arbitrary"), vmem_limit_bytes=VMEM_LIMIT),
        name="fox",
    )(jnp.asarray(tab), stats, qaug_t, kaug, vaug_t, ga)


def _swa_out_kernel(sinks_ref, q_ref, kp_ref, kc_ref, vtp_ref, vtc_ref, g_ref,
                    oa_ref, x_ref, gate_ref, gpost_ref, wa_ref, wb_ref, out_ref,
                    s_sc, p_sc, e_sc, ob_sc):
    i = pl.program_id(1)
    nsub = TQ_SWA // WINDOW
    group = SWA_Q_HEADS // SWA_KV_HEADS
    ncol = group * WINDOW
    kall = jnp.concatenate([kp_ref[0], kc_ref[0]], axis=0)
    vtall = jnp.concatenate([vtp_ref[0], vtc_ref[0]], axis=1)
    lane = lax.broadcasted_iota(jnp.int32, (WINDOW, LANES), 1)
    lo = lane < HEAD_DIM
    zero = jnp.zeros((WINDOW, LANES), BF16)
    kj_ = lax.broadcasted_iota(jnp.int32, (2 * WINDOW, ncol), 0)
    qi_ = lax.broadcasted_iota(jnp.int32, (2 * WINDOW, ncol), 1) & (WINDOW - 1)
    rel = qi_ + WINDOW - kj_
    band = (rel >= 0) & (rel < WINDOW)
    nt = (((1,), (1,)), ((), ()))
    probs = [(r, g) for r in range(nsub) for g in range(SWA_KV_HEADS)]

    def stage_qk(n):
        r, g = probs[n]
        rows = slice(r * WINDOW, (r + 1) * WINDOW)
        kg = kall[r * WINDOW:(r + 2) * WINDOW, g * LANES:(g + 1) * LANES]
        c0 = g * group * HEAD_DIM
        qp0 = q_ref[0, rows, c0:c0 + LANES]
        qp1 = q_ref[0, rows, c0 + LANES:c0 + 2 * LANES]
        qs = jnp.concatenate([jnp.where(lo, qp0, zero), jnp.where(lo, qp1, zero),
                              jnp.where(lo, zero, qp0), jnp.where(lo, zero, qp1)], axis=0)
        s_sc[n] = lax.dot_general(kg, qs, nt, preferred_element_type=F32)

    def stage_softmax(n):
        r, g = probs[n]
        valid = band & (kj_ >= jnp.where(i == 0, WINDOW, 0)) if r == 0 else band
        st = jnp.where(valid, s_sc[n], NEG_BIG)
        heads = (group * g, group * g + 2, group * g + 1, group * g + 3)
        sink = jnp.concatenate(
            [jnp.full((1, WINDOW), sinks_ref[hd] * LOG2E, F32) for hd in heads], axis=1)
        m = jnp.maximum(jnp.max(st, axis=0, keepdims=True), sink)
        p_sc[n] = jnp.exp2(st - m).astype(BF16)
        e_sc[n] = jnp.exp2(sink - m)

    def stage_pv(n):
        r, g = probs[n]
        rows = slice(r * WINDOW, (r + 1) * WINDOW)
        c0 = g * group * HEAD_DIM
        vtg = vtall[g * LANES:(g + 1) * LANES, r * WINDOW:(r + 2) * WINDOW]
        acc = jnp.dot(vtg, p_sc[n], preferred_element_type=F32)
        l = acc[HEAD_DIM:HEAD_DIM + 1] + e_sc[n]
        on = acc[:HEAD_DIM] * (1.0 / l)
        pair0 = jnp.concatenate([on[:, 0:WINDOW], on[:, 2 * WINDOW:3 * WINDOW]], axis=0).T
        pair1 = jnp.concatenate([on[:, WINDOW:2 * WINDOW], on[:, 3 * WINDOW:]], axis=0).T
        ob_sc[rows, c0:c0 + LANES] = (
            pair0 * g_ref[0, rows, c0:c0 + LANES].astype(F32)).astype(BF16)
        ob_sc[rows, c0 + LANES:c0 + 2 * LANES] = (
            pair1 * g_ref[0, rows, c0 + LANES:c0 + 2 * LANES].astype(F32)).astype(BF16)

    ya = jnp.dot(oa_ref[0], wa_ref[...], preferred_element_type=F32)

    for t in range(len(probs) + 2):
        if t < len(probs):
            stage_qk(t)
        if 1 <= t <= len(probs):
            stage_softmax(t - 1)
        if t >= 2:
            stage_pv(t - 2)

    y = ya + jnp.dot(ob_sc[...], wb_ref[...], preferred_element_type=F32)
    ms = jnp.mean(y * y, axis=-1, keepdims=True)
    yn = y * lax.rsqrt(ms + RMS_EPS) * gpost_ref[...]
    out_ref[0] = x_ref[0] + gate_ref[0] * yn


def _swa_out_call(sinks, qb, kd, vdt, gb, oa, x, gate, g_post, wa, wb):
    B, S, D = x.shape
    tq = TQ_SWA
    nsub = tq // WINDOW
    cur = lambda b, i: (b, i, 0)
    prev = lambda b, i: (b, jnp.maximum(i * nsub - 1, 0), 0)
    cur_t = lambda b, i: (b, 0, i)
    prev_t = lambda b, i: (b, 0, jnp.maximum(i * nsub - 1, 0))
    const2 = lambda b, i: (0, 0)
    kvw = 2 * SWA_KV_W
    nprob = nsub * SWA_KV_HEADS
    ncol = SWA_Q_HEADS // SWA_KV_HEADS * WINDOW
    return pl.pallas_call(
        _swa_out_kernel,
        grid=(B, S // tq),
        in_specs=[pl.BlockSpec(memory_space=pltpu.SMEM),
                  pl.BlockSpec((1, tq, SWA_W), cur),
                  pl.BlockSpec((1, WINDOW, kvw), prev),
                  pl.BlockSpec((1, tq, kvw), cur),
                  pl.BlockSpec((1, kvw, WINDOW), prev_t),
                  pl.BlockSpec((1, kvw, tq), cur_t),
                  pl.BlockSpec((1, tq, SWA_W), cur),
                  pl.BlockSpec((1, tq, FOX_W), cur),
                  pl.BlockSpec((1, tq, D), cur),
                  pl.BlockSpec((1, 1, D), lambda b, i: (b, 0, 0)),
                  pl.BlockSpec((1, D), const2),
                  pl.BlockSpec((FOX_W, D), const2),
                  pl.BlockSpec((SWA_W, D), const2)],
        out_specs=pl.BlockSpec((1, tq, D), cur),
        out_shape=jax.ShapeDtypeStruct((B, S, D), F32),
        scratch_shapes=[pltpu.VMEM((nprob, 2 * WINDOW, ncol), F32),
                        pltpu.VMEM((nprob, 2 * WINDOW, ncol), BF16),
                        pltpu.VMEM((nprob, 1, ncol), F32),
                        pltpu.VMEM((tq, SWA_W), BF16)],
        compiler_params=pltpu.CompilerParams(
            dimension_semantics=("arbitrary", "arbitrary"), vmem_limit_bytes=VMEM_LIMIT),
        name="swa_out",
    )(sinks, qb, kd, kd, vdt, vdt, gb, oa, x, gate, g_post, wa, wb)


def _perm_w_in(w):
    sc = HEAD_DIM ** -0.5
    o = 0
    qa = w[:, o:o + FOX_W]; o += FOX_W
    ka = w[:, o:o + FOX_W]; o += FOX_W
    va = w[:, o:o + FOX_W]; o += FOX_W
    fa = w[:, o:o + FOX_HEADS]; o += FOX_HEADS
    za = w[:, o:o + FOX_W]; o += FOX_W
    qb = w[:, o:o + SWA_W]; o += SWA_W
    kb = w[:, o:o + SWA_KV_W]; o += SWA_KV_W
    vb = w[:, o:o + SWA_KV_W]; o += SWA_KV_W
    zb = w[:, o:o + SWA_W]
    fpad = jnp.concatenate(
        [fa, fa, fa, jnp.zeros((w.shape[0], LANES - 3 * FOX_HEADS), w.dtype)], axis=1)
    sc2 = sc * LOG2E
    return jnp.concatenate([qa * sc2, ka, va, za, qb * sc2, zb, kb, vb, fpad],
                           axis=1).astype(BF16)


def _aug_constants():
    eq = np.zeros((LANES, AUG_W), np.float32)
    ek = np.zeros((LANES, AUG_W), np.float32)
    oq = np.zeros((1, AUG_W), np.float32)
    ok = np.zeros((1, AUG_W), np.float32)
    for hd in range(FOX_HEADS):
        base = LANES * hd + (HEAD_DIM if hd % 2 == 0 else 0)
        for part in range(3):
            eq[part * FOX_HEADS + hd, base + part] = 1.0
            ok[0, base + part] = 1.0
            ek[part * FOX_HEADS + hd, base + 3 + part] = -1.0
            oq[0, base + 3 + part] = 1.0
    return (jnp.asarray(eq, BF16), jnp.asarray(ek, BF16), jnp.asarray(oq), jnp.asarray(ok))


def kernel(x, c, positions, w_ada, b_ada, g_pre, w_in, b_fgate, sinks, w_out, g_post):
    B, S, D = x.shape
    depth = w_ada.shape[0]
    assert TM_IN == TQ_FOX == TK_FOX
    half = HEAD_DIM // 2
    inv_freq = ROPE_THETA ** (-jnp.arange(half, dtype=F32) / half)
    invf = jnp.tile(inv_freq, LANES // half)[:, None]
    pos3 = positions[:, None, :]
    ind = np.zeros((2 * FOX_W, LANES), np.float32)
    ind[np.arange(2 * FOX_W), np.arange(2 * FOX_W) // HEAD_DIM] = 1.0
    ind = jnp.asarray(ind, BF16)
    tri = jnp.asarray(np.tril(np.ones((TM_IN, TM_IN), np.float32)), BF16)
    eq, ek, oq, ok = _aug_constants()
    c_pad = jnp.zeros((8, D), F32).at[:B].set(c)
    for l in range(depth):
        mod = _mod_call(c_pad, w_ada[l], b_ada[l][None, :])[:B]
        shift = mod[:, None, 0:D]
        scale = mod[:, None, D:2 * D]
        gate = mod[:, None, 2 * D:3 * D]
        bf_pad = jnp.concatenate(
            [b_fgate[l]] * 3 + [jnp.zeros((LANES - 3 * FOX_HEADS,), F32)])[None, :]
        qaug, kaug, vaug, ga, qb, kd, vd, gb, stats = _inproj_call(
            x, shift, scale, g_pre[l][None, :], _perm_w_in(w_in[l]), pos3, invf, bf_pad,
            tri, eq, ek, oq, ok, ind)
        oa = _fox_call(qaug, kaug, vaug, ga, stats[:, :, 0:3, 0:2 * FOX_HEADS])
        wo = w_out[l].astype(BF16)
        x = _swa_out_call(sinks[l], qb, kd, vd, gb, oa, x, gate, g_post[l][None, :],
                          wo[:FOX_W], wo[FOX_W:])
    return x
```

```python
import functools

import jax
import jax.numpy as jnp
import numpy as np
from jax import lax
from jax.experimental import pallas as pl
from jax.experimental.pallas import tpu as pltpu

D_MODEL = 1024
HEAD_DIM = 64
FOX_HEADS = 8
SWA_Q_HEADS = 8
SWA_KV_HEADS = 2
WINDOW = 128
ROPE_THETA = 10000.0
RMS_EPS = 1e-6
FOX_W = FOX_HEADS * HEAD_DIM
SWA_W = SWA_Q_HEADS * HEAD_DIM
SWA_KV_W = SWA_KV_HEADS * HEAD_DIM

LANES = 128
AUG_W = FOX_HEADS * LANES

OFF_QA = 0
OFF_KA = OFF_QA + FOX_W
OFF_VA = OFF_KA + FOX_W
OFF_ZA = OFF_VA + FOX_W
OFF_QB = OFF_ZA + FOX_W
OFF_ZB = OFF_QB + SWA_W
OFF_KD = OFF_ZB + SWA_W
OFF_VD = OFF_KD + SWA_KV_W
OFF_F = OFF_VD + SWA_KV_W
IN_NP = OFF_F + LANES

TM_IN = 1024
TM_SUB = 512
TQ_FOX = 512
TK_FOX = 512
TQ_SWA = 512
NEG_BIG = -1e30
LOG2E = 1.4426950408889634
PRUNE_T = 140.0
NORM_MARGIN = 1.01
FIXED_REF_T = 100.0
VMEM_LIMIT = 56 * 1024 * 1024

F32 = jnp.float32
BF16 = jnp.bfloat16


def _split3(a):
    hi = a.astype(BF16)
    r = a - hi.astype(F32)
    mid = r.astype(BF16)
    lo = (r - mid.astype(F32)).astype(BF16)
    return hi, mid, lo


def _mod_kernel(c_ref, w_ref, b_ref, o_ref):
    c = c_ref[...]
    sc = c * (1.0 / (1.0 + jnp.exp(-c)))
    o_ref[...] = jnp.dot(sc, w_ref[...], precision=lax.Precision.HIGHEST,
                         preferred_element_type=F32) + b_ref[...]


def _mod_call(c_pad, w_ada, b_ada):
    rows = c_pad.shape[0]
    n = w_ada.shape[1]
    bn = D_MODEL
    return pl.pallas_call(
        _mod_kernel,
        grid=(n // bn,),
        in_specs=[pl.BlockSpec((rows, D_MODEL), lambda j: (0, 0)),
                  pl.BlockSpec((D_MODEL, bn), lambda j: (0, j)),
                  pl.BlockSpec((1, bn), lambda j: (0, j))],
        out_specs=pl.BlockSpec((rows, bn), lambda j: (0, j)),
        out_shape=jax.ShapeDtypeStruct((rows, n), F32),
        name="mod",
    )(c_pad, w_ada, b_ada)


def _inproj_kernel(x_ref, shift_ref, scale_ref, gpre_ref, w_ref, pos_ref, invf_ref, bf_ref,
                   tri_ref, eqk_ref, oq_ref, ok_ref, ind_ref,
                   qaug_ref, kaug_ref, vaug_ref, ga_ref, qb_ref, kd_ref, vd_ref, gb_ref, st_ref,
                   carry_ref):
    tm = TM_SUB

    @pl.when(pl.program_id(1) == 0)
    def _():
        carry_ref[...] = jnp.zeros_like(carry_ref)

    for sub in range(x_ref.shape[1] // tm):
        _inproj_subtile(sub, slice(sub * tm, (sub + 1) * tm),
                        x_ref, shift_ref, scale_ref, gpre_ref, w_ref, pos_ref, invf_ref, bf_ref,
                        tri_ref, eqk_ref, oq_ref, ok_ref, ind_ref,
                        qaug_ref, kaug_ref, vaug_ref, ga_ref, qb_ref, kd_ref, vd_ref, gb_ref,
                        st_ref, carry_ref)


def _inproj_subtile(sub, rows, x_ref, shift_ref, scale_ref, gpre_ref, w_ref, pos_ref, invf_ref,
                    bf_ref, tri_ref, eqk_ref, oq_ref, ok_ref, ind_ref,
                    qaug_ref, kaug_ref, vaug_ref, ga_ref, qb_ref, kd_ref, vd_ref, gb_ref,
                    st_ref, carry_ref):
    tm = TM_SUB
    x = x_ref[0, rows, :]
    ms = jnp.mean(x * x, axis=-1, keepdims=True)
    y = x * lax.rsqrt(ms + RMS_EPS) * gpre_ref[...]
    h = (y * (1.0 + scale_ref[0]) + shift_ref[0]).astype(BF16)

    def proj(off, width):
        return jnp.dot(h, w_ref[:, off:off + width], preferred_element_type=F32)

    tail = proj(OFF_KD, 3 * LANES)
    f = tail[:, OFF_F - OFF_KD:] + bf_ref[...]
    ls = jnp.minimum(f, 0.0) - jnp.log1p(jnp.exp(-jnp.abs(f)))
    lane = lax.broadcasted_iota(jnp.int32, (tm, LANES), 1)

    def by_group(a, b_, c_):
        return jnp.where(lane < FOX_HEADS, a, jnp.where(lane < 2 * FOX_HEADS, b_, c_))

    part = by_group(*(t.astype(F32) for t in _split3(ls))).astype(BF16)
    psum = jnp.dot(tri_ref[...], part, preferred_element_type=F32)
    cum = (psum + pltpu.roll(psum, LANES - FOX_HEADS, 1)
           + pltpu.roll(psum, LANES - 2 * FOX_HEADS, 1))
    cum = cum + carry_ref[...]
    carry_ref[...] = cum[tm - 1:tm, :]

    cum2 = cum * LOG2E
    cum2 = by_group(cum2, pltpu.roll(cum2, FOX_HEADS, 1), pltpu.roll(cum2, 2 * FOX_HEADS, 1))
    cs = by_group(*(t.astype(F32) for t in _split3(cum2))).astype(BF16)
    placed = jnp.dot(cs, eqk_ref[...], preferred_element_type=F32)
    augq = placed * ok_ref[...] + oq_ref[...]
    augk = placed * oq_ref[...] + ok_ref[...]

    lane_w = lax.broadcasted_iota(jnp.int32, (tm, AUG_W), 1)
    data = (((lane_w >> 6) ^ (lane_w >> 7)) & 1) == 0

    def rep(a):
        return jnp.concatenate(
            [a[:, LANES * (hd // 2):LANES * (hd // 2 + 1)] for hd in range(FOX_HEADS)], axis=1)

    qa = proj(OFF_QA, FOX_W)
    ka = proj(OFF_KA, FOX_W)
    qaug_ref[0, :, rows] = jnp.where(data, rep(qa), augq).T.astype(BF16)
    kaug_ref[0, rows, :] = jnp.where(data, rep(ka), augk).astype(BF16)

    q16 = qa.astype(BF16).astype(F32)
    k16 = ka.astype(BF16).astype(F32)
    sq = jnp.concatenate([q16 * q16, k16 * k16], axis=1).astype(BF16)
    nrm2 = jnp.dot(sq, ind_ref[...], preferred_element_type=F32)
    nmax = jnp.sqrt(jnp.max(nrm2, axis=0, keepdims=True)) * NORM_MARGIN
    st_ref[0, sub] = jnp.concatenate(
        [nmax, cum2[0:1], cum2[tm - 1:tm], jnp.zeros((5, LANES), F32)], axis=0)
    vaug_ref[0, :, rows] = jnp.where(data, rep(proj(OFF_VA, FOX_W)), 1.0).T.astype(BF16)

    za = proj(OFF_ZA, FOX_W)
    ga_ref[0, rows, :] = (za * (1.0 / (1.0 + jnp.exp(-za)))).astype(BF16)
    zb = proj(OFF_ZB, SWA_W)
    gb_ref[0, rows, :] = (zb * (1.0 / (1.0 + jnp.exp(-zb)))).astype(BF16)

    ang_t = invf_ref[...] * pos_ref[0, :, rows].astype(F32)
    reps = LANES // (HEAD_DIM // 2)
    cosv = jnp.concatenate([jnp.cos(ang_t)] * reps, axis=0).T
    sinv = jnp.concatenate([jnp.sin(ang_t)] * reps, axis=0).T
    first = (lane & (HEAD_DIM // 2)) == 0
    sin_signed = jnp.where(first, -sinv, sinv)

    def rope(a):
        outs = []
        for cidx in range(a.shape[1] // LANES):
            blk = a[:, cidx * LANES:(cidx + 1) * LANES]
            other = jnp.where(first, pltpu.roll(blk, LANES - HEAD_DIM // 2, 1),
                              pltpu.roll(blk, HEAD_DIM // 2, 1))
            outs.append(blk * cosv + other * sin_signed)
        return jnp.concatenate(outs, axis=1)

    qb_ref[0, rows, :] = rope(proj(OFF_QB, SWA_W)).astype(BF16)
    kr = rope(tail[:, :LANES])
    ks = pltpu.roll(kr, HEAD_DIM, 1)
    low = lane < HEAD_DIM
    kd_ref[0, rows, :] = jnp.concatenate([jnp.where(low, kr, ks), jnp.where(low, ks, kr)],
                                axis=1).astype(BF16)
    vt = tail[:, OFF_VD - OFF_KD:OFF_F - OFF_KD].T
    ones = jnp.ones((HEAD_DIM, tm), F32)
    vd_ref[0, :, rows] = jnp.concatenate([vt[:HEAD_DIM], ones, vt[HEAD_DIM:], ones],
                                axis=0).astype(BF16)


def _inproj_call(x, shift, scale, g_pre, w_perm, pos3, invf, bf_pad, tri, eqk, oq, ok, ind):
    B, S, D = x.shape
    tm = TM_IN
    row = lambda b, t: (b, t, 0)
    per_b = lambda b, t: (b, 0, 0)
    const2 = lambda b, t: (0, 0)
    out_w = (AUG_W, AUG_W, AUG_W, FOX_W, SWA_W, 2 * SWA_KV_W, 2 * SWA_KV_W, SWA_W)
    transposed = (0, 2, 6)
    return pl.pallas_call(
        _inproj_kernel,
        grid=(B, S // tm),
        in_specs=[pl.BlockSpec((1, tm, D), row),
                  pl.BlockSpec((1, 1, D), per_b),
                  pl.BlockSpec((1, 1, D), per_b),
                  pl.BlockSpec((1, D), const2),
                  pl.BlockSpec((D, IN_NP), const2),
                  pl.BlockSpec((1, 1, tm), lambda b, t: (b, 0, t)),
                  pl.BlockSpec((HEAD_DIM // 2, 1), const2),
                  pl.BlockSpec((1, LANES), const2),
                  pl.BlockSpec((TM_SUB, TM_SUB), const2),
                  pl.BlockSpec((LANES, AUG_W), const2),
                  pl.BlockSpec((1, AUG_W), const2),
                  pl.BlockSpec((1, AUG_W), const2),
                  pl.BlockSpec((2 * FOX_W, LANES), const2)],
        out_specs=[pl.BlockSpec((1, w, tm), lambda b, t: (b, 0, t)) if i in transposed
                   else pl.BlockSpec((1, tm, w), row) for i, w in enumerate(out_w)]
        + [pl.BlockSpec((1, tm // TM_SUB, 8, LANES), lambda b, t: (b, t, 0, 0))],
        out_shape=[jax.ShapeDtypeStruct((B, w, S) if i in transposed else (B, S, w), BF16)
                   for i, w in enumerate(out_w)]
        + [jax.ShapeDtypeStruct((B, S // TM_SUB, 8, LANES), F32)],
        scratch_shapes=[pltpu.VMEM((1, LANES), F32)],
        compiler_params=pltpu.CompilerParams(
            dimension_semantics=("arbitrary", "arbitrary"), vmem_limit_bytes=VMEM_LIMIT),
        name="in_proj",
    )(x, shift, scale, g_pre, w_perm, pos3, invf, bf_pad, tri, eqk, oq, ok, ind)


def _fox_tables(nq):
    diag = [(qi, qi, hh * nq + qi, hh) for hh in range(2) for qi in range(nq)]
    tab = np.array(diag + [(1, 0, 2 * nq, 0)], np.int32).T
    return tab, len(diag), nq * (nq - 1)


def _fox_kernel(tab_ref, st_ref, q_ref, k_ref, v_ref, g_ref, o_ref,
                off_tab, gen_tab, s0, s1, p0, p1, al0, al1, m_st, acc_st, *, n_diag, n_off):
    tq = TQ_FOX
    tk = TK_FOX
    nq = q_ref.shape[2] // tq
    s_buf = (s0, s1)
    p_buf = (p0, p1)
    al_buf = (al0, al1)

    def run(tab, n_items, dummy, diag):
        half = (n_items + 1) // 2
        lens = (half, n_items - half)

        def item(stream, t):
            return jnp.where(t < lens[stream], stream * half + t, dummy)

        def stage_qk(t, slot):
            for sm in range(2):
                e = item(sm, t)
                qoff = pl.multiple_of(tab[0, e] * tq, tq)
                koff = pl.multiple_of(tab[1, e] * tk, tk)
                hoff = pl.multiple_of(tab[3, e] * LANES, LANES)
                qt = q_ref[0, pl.ds(hoff, LANES), pl.ds(qoff, tq)]
                k = k_ref[0, pl.ds(koff, tk), pl.ds(hoff, LANES)]
                s_buf[slot][sm] = jnp.dot(k, qt, preferred_element_type=F32)

        def stage_softmax(t, slot):
            for sm in range(2):
                st = tab[2, item(sm, t)]
                s = s_buf[slot][sm]
                if diag:
                    kr = lax.broadcasted_iota(jnp.int32, (tk, tq), 0)
                    qc = lax.broadcasted_iota(jnp.int32, (tk, tq), 1)
                    s = jnp.where(kr <= qc, s, NEG_BIG)
                    m_next = jnp.max(s, axis=0, keepdims=True)
                else:
                    m_prev = m_st[st]
                    m_next = jnp.maximum(m_prev, jnp.max(s, axis=0, keepdims=True))
                    al_buf[slot][sm] = jnp.exp2(m_prev - m_next)
                p_buf[slot][sm] = jnp.exp2(s - m_next).astype(BF16)
                m_st[st] = m_next

        def stage_pv(t, slot):
            for sm in range(2):
                e = item(sm, t)
                koff = pl.multiple_of(tab[1, e] * tk, tk)
                hoff = pl.multiple_of(tab[3, e] * LANES, LANES)
                st = tab[2, e]
                vt = v_ref[0, pl.ds(hoff, LANES), pl.ds(koff, tk)]
                pv = jnp.dot(vt, p_buf[slot][sm], preferred_element_type=F32)
                if diag:
                    acc_st[st] = pv
                else:
                    acc_st[st] = al_buf[slot][sm] * acc_st[st] + pv

        stage_qk(0, 0)
        stage_qk(1, 1)
        stage_softmax(0, 0)

        def body(u, carry):
            for d in range(2):
                t = 2 * u + 1 + d
                stage_qk(t + 1, d % 2)
                stage_softmax(t, (d + 1) % 2)
                stage_pv(t - 1, d % 2)
            return carry

        lax.fori_loop(0, (half + 1) // 2, body, 0)

    def run_fixed(tab, n_items, dummy):
        half = (n_items + 1) // 2
        lens = (half, n_items - half)

        def item(stream, t):
            return jnp.where(t < lens[stream], stream * half + t, dummy)

        def stage_probs(t, slot):
            for sm in range(2):
                e = item(sm, t)
                qoff = pl.multiple_of(tab[0, e] * tq, tq)
                koff = pl.multiple_of(tab[1, e] * tk, tk)
                hoff = pl.multiple_of(tab[3, e] * LANES, LANES)
                qt = q_ref[0, pl.ds(hoff, LANES), pl.ds(qoff, tq)]
                k = k_ref[0, pl.ds(koff, tk), pl.ds(hoff, LANES)]
                s = jnp.dot(k, qt, preferred_element_type=F32)
                p_buf[slot][sm] = jnp.exp2(s - m_st[tab[2, e]]).astype(BF16)

        def stage_pv(t, slot):
            for sm in range(2):
                e = item(sm, t)
                koff = pl.multiple_of(tab[1, e] * tk, tk)
                hoff = pl.multiple_of(tab[3, e] * LANES, LANES)
                st = tab[2, e]
                vt = v_ref[0, pl.ds(hoff, LANES), pl.ds(koff, tk)]
                acc_st[st] = acc_st[st] + jnp.dot(vt, p_buf[slot][sm],
                                                  preferred_element_type=F32)

        stage_probs(0, 0)

        def body(u, carry):
            for d in range(2):
                t = 2 * u + 1 + d
                stage_probs(t, (d + 1) % 2)
                stage_pv(t - 1, d % 2)
            return carry

        lax.fori_loop(0, (half + 1) // 2, body, 0)

    assert tab_ref.shape[1] == n_diag + 1
    run(tab_ref, n_diag, n_diag, True)

    b = pl.program_id(0)
    pr = pl.program_id(1)

    def put(tab, idx, qrow, j, state, head):
        tab[0, idx] = qrow
        tab[1, idx] = j
        tab[2, idx] = state
        tab[3, idx] = head

    n_fix = jnp.int32(0)
    n_gen = jnp.int32(0)
    for hh in range(2):
        hd = 2 * pr + hh

        kmax = lax.fori_loop(
            0, nq, lambda j, m, hd=hd: jnp.maximum(m, st_ref[b, j, 0, FOX_HEADS + hd]),
            jnp.float32(0.0))

        def list_row(qi, counts, hh=hh, hd=hd, kmax=kmax):
            base = (st_ref[b, qi, 0, hd] * (st_ref[b, qi, 0, FOX_HEADS + hd] + kmax)
                    + st_ref[b, qi, 1, hd])
            fixed_ok = base - st_ref[b, qi - 1, 2, hd] <= FIXED_REF_T

            def scan(tab, enabled, cnt):
                def contributes(carry):
                    j, _ = carry
                    bound = base - st_ref[b, jnp.maximum(j, 0), 2, hd]
                    return jnp.logical_and(jnp.logical_and(enabled, j >= 0), bound > -PRUNE_T)

                def take(carry):
                    j, cnt = carry
                    put(tab, cnt, qi, j, hh * nq + qi, hh)
                    return j - 1, cnt + 1

                return lax.while_loop(contributes, take, (qi - 1, cnt))[1]

            return (scan(off_tab, fixed_ok, counts[0]),
                    scan(gen_tab, jnp.logical_not(fixed_ok), counts[1]))

        n_fix, n_gen = lax.fori_loop(1, nq, list_row, (n_fix, n_gen))
    put(off_tab, n_off, 1, 0, 2 * nq, 0)
    put(gen_tab, n_off, 1, 0, 2 * nq, 0)
    run_fixed(off_tab, n_fix, n_off)

    @pl.when(n_gen > 0)
    def _():
        run(gen_tab, n_gen, n_off, False)

    def finish(qi, carry):
        a0 = acc_st[qi]
        a1 = acc_st[nq + qi]
        ot = jnp.concatenate([a0[:HEAD_DIM] / a0[HEAD_DIM:HEAD_DIM + 1],
                              a1[HEAD_DIM:] / a1[0:1]], axis=0)
        rows = pl.ds(pl.multiple_of(qi * tq, tq), tq)
        o_ref[0, rows, :] = (ot.T * g_ref[0, rows, :].astype(F32)).astype(BF16)
        return carry

    lax.fori_loop(0, nq, finish, 0)


def _fox_call(qaug_t, kaug, vaug_t, ga, stats):
    B, S, _ = kaug.shape
    tq, tk = TQ_FOX, TK_FOX
    assert tq == tk
    nq = S // tq
    tab, n_diag, n_off = _fox_tables(nq)
    pair = lambda b, p: (b, 0, p)
    return pl.pallas_call(
        functools.partial(_fox_kernel, n_diag=n_diag, n_off=n_off),
        grid=(B, FOX_HEADS // 2),
        in_specs=[pl.BlockSpec(memory_space=pltpu.SMEM),
                  pl.BlockSpec(memory_space=pltpu.SMEM),
                  pl.BlockSpec((1, 2 * LANES, S), lambda b, p: (b, p, 0)),
                  pl.BlockSpec((1, S, 2 * LANES), pair),
                  pl.BlockSpec((1, 2 * LANES, S), lambda b, p: (b, p, 0)),
                  pl.BlockSpec((1, S, LANES), pair)],
        out_specs=pl.BlockSpec((1, S, LANES), pair),
        out_shape=jax.ShapeDtypeStruct((B, S, FOX_W), BF16),
        scratch_shapes=[pltpu.SMEM((4, n_off + 1), jnp.int32),
                        pltpu.SMEM((4, n_off + 1), jnp.int32),
                        pltpu.VMEM((2, tk, tq), F32), pltpu.VMEM((2, tk, tq), F32),
                        pltpu.VMEM((2, tk, tq), BF16), pltpu.VMEM((2, tk, tq), BF16),
                        pltpu.VMEM((2, 1, tq), F32), pltpu.VMEM((2, 1, tq), F32),
                        pltpu.VMEM((2 * nq + 1, 1, tq), F32),
                        pltpu.VMEM((2 * nq + 1, LANES, tq), F32)],
        compiler_params=pltpu.CompilerParams(
            dimension_semantics=("arbitrary", "arbitrary"), vmem_limit_bytes=VMEM_LIMIT),
        name="fox",
    )(jnp.asarray(tab), stats, qaug_t, kaug, vaug_t, ga)


def _swa_out_kernel(sinks_ref, q_ref, kp_ref, kc_ref, vtp_ref, vtc_ref, g_ref,
                    oa_ref, x_ref, gate_ref, gpost_ref, wa_ref, wb_ref, out_ref,
                    s_sc, p_sc, e_sc, ob_sc):
    i = pl.program_id(1)
    nsub = TQ_SWA // WINDOW
    group = SWA_Q_HEADS // SWA_KV_HEADS
    ncol = group * WINDOW
    kall = jnp.concatenate([kp_ref[0], kc_ref[0]], axis=0)
    vtall = jnp.concatenate([vtp_ref[0], vtc_ref[0]], axis=1)
    lane = lax.broadcasted_iota(jnp.int32, (WINDOW, LANES), 1)
    lo = lane < HEAD_DIM
    zero = jnp.zeros((WINDOW, LANES), BF16)
    kj_ = lax.broadcasted_iota(jnp.int32, (2 * WINDOW, ncol), 0)
    qi_ = lax.broadcasted_iota(jnp.int32, (2 * WINDOW, ncol), 1) & (WINDOW - 1)
    rel = qi_ + WINDOW - kj_
    band = (rel >= 0) & (rel < WINDOW)
    nt = (((1,), (1,)), ((), ()))
    probs = [(r, g) for r in range(nsub) for g in range(SWA_KV_HEADS)]

    def stage_qk(n):
        r, g = probs[n]
        rows = slice(r * WINDOW, (r + 1) * WINDOW)
        kg = kall[r * WINDOW:(r + 2) * WINDOW, g * LANES:(g + 1) * LANES]
        c0 = g * group * HEAD_DIM
        qp0 = q_ref[0, rows, c0:c0 + LANES]
        qp1 = q_ref[0, rows, c0 + LANES:c0 + 2 * LANES]
        qs = jnp.concatenate([jnp.where(lo, qp0, zero), jnp.where(lo, qp1, zero),
                              jnp.where(lo, zero, qp0), jnp.where(lo, zero, qp1)], axis=0)
        s_sc[n] = lax.dot_general(kg, qs, nt, preferred_element_type=F32)

    def stage_softmax(n):
        r, g = probs[n]
        valid = band & (kj_ >= jnp.where(i == 0, WINDOW, 0)) if r == 0 else band
        st = jnp.where(valid, s_sc[n], NEG_BIG)
        heads = (group * g, group * g + 2, group * g + 1, group * g + 3)
        sink = jnp.concatenate(
            [jnp.full((1, WINDOW), sinks_ref[hd] * LOG2E, F32) for hd in heads], axis=1)
        m = jnp.maximum(jnp.max(st, axis=0, keepdims=True), sink)
        p_sc[n] = jnp.exp2(st - m).astype(BF16)
        e_sc[n] = jnp.exp2(sink - m)

    def stage_pv(n):
        r, g = probs[n]
        rows = slice(r * WINDOW, (r + 1) * WINDOW)
        c0 = g * group * HEAD_DIM
        vtg = vtall[g * LANES:(g + 1) * LANES, r * WINDOW:(r + 2) * WINDOW]
        acc = jnp.dot(vtg, p_sc[n], preferred_element_type=F32)
        l = acc[HEAD_DIM:HEAD_DIM + 1] + e_sc[n]
        on = acc[:HEAD_DIM] * (1.0 / l)
        pair0 = jnp.concatenate([on[:, 0:WINDOW], on[:, 2 * WINDOW:3 * WINDOW]], axis=0).T
        pair1 = jnp.concatenate([on[:, WINDOW:2 * WINDOW], on[:, 3 * WINDOW:]], axis=0).T
        ob_sc[rows, c0:c0 + LANES] = (
            pair0 * g_ref[0, rows, c0:c0 + LANES].astype(F32)).astype(BF16)
        ob_sc[rows, c0 + LANES:c0 + 2 * LANES] = (
            pair1 * g_ref[0, rows, c0 + LANES:c0 + 2 * LANES].astype(F32)).astype(BF16)

    ya = jnp.dot(oa_ref[0], wa_ref[...], preferred_element_type=F32)

    for t in range(len(probs) + 2):
        if t < len(probs):
            stage_qk(t)
        if 1 <= t <= len(probs):
            stage_softmax(t - 1)
        if t >= 2:
            stage_pv(t - 2)

    y = ya + jnp.dot(ob_sc[...], wb_ref[...], preferred_element_type=F32)
    ms = jnp.mean(y * y, axis=-1, keepdims=True)
    yn = y * lax.rsqrt(ms + RMS_EPS) * gpost_ref[...]
    out_ref[0] = x_ref[0] + gate_ref[0] * yn


def _swa_out_call(sinks, qb, kd, vdt, gb, oa, x, gate, g_post, wa, wb):
    B, S, D = x.shape
    tq = TQ_SWA
    nsub = tq // WINDOW
    cur = lambda b, i: (b, i, 0)
    prev = lambda b, i: (b, jnp.maximum(i * nsub - 1, 0), 0)
    cur_t = lambda b, i: (b, 0, i)
    prev_t = lambda b, i: (b, 0, jnp.maximum(i * nsub - 1, 0))
    const2 = lambda b, i: (0, 0)
    kvw = 2 * SWA_KV_W
    nprob = nsub * SWA_KV_HEADS
    ncol = SWA_Q_HEADS // SWA_KV_HEADS * WINDOW
    return pl.pallas_call(
        _swa_out_kernel,
        grid=(B, S // tq),
        in_specs=[pl.BlockSpec(memory_space=pltpu.SMEM),
                  pl.BlockSpec((1, tq, SWA_W), cur),
                  pl.BlockSpec((1, WINDOW, kvw), prev),
                  pl.BlockSpec((1, tq, kvw), cur),
                  pl.BlockSpec((1, kvw, WINDOW), prev_t),
                  pl.BlockSpec((1, kvw, tq), cur_t),
                  pl.BlockSpec((1, tq, SWA_W), cur),
                  pl.BlockSpec((1, tq, FOX_W), cur),
                  pl.BlockSpec((1, tq, D), cur),
                  pl.BlockSpec((1, 1, D), lambda b, i: (b, 0, 0)),
                  pl.BlockSpec((1, D), const2),
                  pl.BlockSpec((FOX_W, D), const2),
                  pl.BlockSpec((SWA_W, D), const2)],
        out_specs=pl.BlockSpec((1, tq, D), cur),
        out_shape=jax.ShapeDtypeStruct((B, S, D), F32),
        scratch_shapes=[pltpu.VMEM((nprob, 2 * WINDOW, ncol), F32),
                        pltpu.VMEM((nprob, 2 * WINDOW, ncol), BF16),
                        pltpu.VMEM((nprob, 1, ncol), F32),
                        pltpu.VMEM((tq, SWA_W), BF16)],
        compiler_params=pltpu.CompilerParams(
            dimension_semantics=("arbitrary", "arbitrary"), vmem_limit_bytes=VMEM_LIMIT),
        name="swa_out",
    )(sinks, qb, kd, kd, vdt, vdt, gb, oa, x, gate, g_post, wa, wb)


def _perm_w_in(w):
    sc = HEAD_DIM ** -0.5
    o = 0
    qa = w[:, o:o + FOX_W]; o += FOX_W
    ka = w[:, o:o + FOX_W]; o += FOX_W
    va = w[:, o:o + FOX_W]; o += FOX_W
    fa = w[:, o:o + FOX_HEADS]; o += FOX_HEADS
    za = w[:, o:o + FOX_W]; o += FOX_W
    qb = w[:, o:o + SWA_W]; o += SWA_W
    kb = w[:, o:o + SWA_KV_W]; o += SWA_KV_W
    vb = w[:, o:o + SWA_KV_W]; o += SWA_KV_W
    zb = w[:, o:o + SWA_W]
    fpad = jnp.concatenate(
        [fa, fa, fa, jnp.zeros((w.shape[0], LANES - 3 * FOX_HEADS), w.dtype)], axis=1)
    sc2 = sc * LOG2E
    return jnp.concatenate([qa * sc2, ka, va, za, qb * sc2, zb, kb, vb, fpad],
                           axis=1).astype(BF16)


def _aug_constants():
    eq = np.zeros((LANES, AUG_W), np.float32)
    ek = np.zeros((LANES, AUG_W), np.float32)
    oq = np.zeros((1, AUG_W), np.float32)
    ok = np.zeros((1, AUG_W), np.float32)
    for hd in range(FOX_HEADS):
        base = LANES * hd + (HEAD_DIM if hd % 2 == 0 else 0)
        for part in range(3):
            eq[part * FOX_HEADS + hd, base + part] = 1.0
            ok[0, base + part] = 1.0
            ek[part * FOX_HEADS + hd, base + 3 + part] = -1.0
            oq[0, base + 3 + part] = 1.0
    return jnp.asarray(eq + ek, BF16), jnp.asarray(oq), jnp.asarray(ok)


def kernel(x, c, positions, w_ada, b_ada, g_pre, w_in, b_fgate, sinks, w_out, g_post):
    B, S, D = x.shape
    depth = w_ada.shape[0]
    assert TM_SUB == TQ_FOX == TK_FOX
    half = HEAD_DIM // 2
    inv_freq = ROPE_THETA ** (-jnp.arange(half, dtype=F32) / half)
    invf = inv_freq[:, None]
    pos3 = positions[:, None, :]
    ind = np.zeros((2 * FOX_W, LANES), np.float32)
    ind[np.arange(2 * FOX_W), np.arange(2 * FOX_W) // HEAD_DIM] = 1.0
    ind = jnp.asarray(ind, BF16)
    tri = jnp.asarray(np.tril(np.ones((TM_SUB, TM_SUB), np.float32)), BF16)
    eqk, oq, ok = _aug_constants()
    c_pad = jnp.zeros((8, D), F32).at[:B].set(c)
    for l in range(depth):
        mod = _mod_call(c_pad, w_ada[l], b_ada[l][None, :])[:B]
        shift = mod[:, None, 0:D]
        scale = mod[:, None, D:2 * D]
        gate = mod[:, None, 2 * D:3 * D]
        bf_pad = jnp.concatenate(
            [b_fgate[l]] * 3 + [jnp.zeros((LANES - 3 * FOX_HEADS,), F32)])[None, :]
        qaug, kaug, vaug, ga, qb, kd, vd, gb, stats = _inproj_call(
            x, shift, scale, g_pre[l][None, :], _perm_w_in(w_in[l]), pos3, invf, bf_pad,
            tri, eqk, oq, ok, ind)
        oa = _fox_call(qaug, kaug, vaug, ga, stats[:, :, 0:3, 0:2 * FOX_HEADS])
        wo = w_out[l].astype(BF16)
        x = _swa_out_call(sinks[l], qb, kd, vd, gb, oa, x, gate, g_post[l][None, :],
                          wo[:FOX_W], wo[FOX_W:])
    return x
```

```python
import functools

import jax
import jax.numpy as jnp
import numpy as np
from jax import lax
from jax.experimental import pallas as pl
from jax.experimental.pallas import tpu as pltpu

D_MODEL = 1024
HEAD_DIM = 64
FOX_HEADS = 8
SWA_Q_HEADS = 8
SWA_KV_HEADS = 2
WINDOW = 128
ROPE_THETA = 10000.0
RMS_EPS = 1e-6
FOX_W = FOX_HEADS * HEAD_DIM
SWA_W = SWA_Q_HEADS * HEAD_DIM
SWA_KV_W = SWA_KV_HEADS * HEAD_DIM

LANES = 128
AUG_W = FOX_HEADS * LANES

OFF_QA = 0
OFF_KA = OFF_QA + FOX_W
OFF_VA = OFF_KA + FOX_W
OFF_ZA = OFF_VA + FOX_W
OFF_QB = OFF_ZA + FOX_W
OFF_ZB = OFF_QB + SWA_W
OFF_KD = OFF_ZB + SWA_W
OFF_VD = OFF_KD + SWA_KV_W
OFF_F = OFF_VD + SWA_KV_W
IN_NP = OFF_F + LANES

TM_IN = 1024
TM_SUB = 512
TQ_FOX = 512
TK_FOX = 512
TQ_SWA = 512
NEG_BIG = -1e30
LOG2E = 1.4426950408889634
PRUNE_T = 140.0
NORM_MARGIN = 1.01
FIXED_REF_T = 100.0
VMEM_LIMIT = 56 * 1024 * 1024

F32 = jnp.float32
BF16 = jnp.bfloat16


def _split3(a):
    hi = a.astype(BF16)
    r = a - hi.astype(F32)
    mid = r.astype(BF16)
    lo = (r - mid.astype(F32)).astype(BF16)
    return hi, mid, lo


def _mod_kernel(c_ref, w_ref, b_ref, o_ref):
    c = c_ref[...]
    sc = c * (1.0 / (1.0 + jnp.exp(-c)))
    o_ref[...] = jnp.dot(sc, w_ref[...], precision=lax.Precision.HIGHEST,
                         preferred_element_type=F32) + b_ref[...]


def _mod_call(c_pad, w_ada, b_ada):
    rows = c_pad.shape[0]
    n = w_ada.shape[1]
    bn = D_MODEL
    return pl.pallas_call(
        _mod_kernel,
        grid=(n // bn,),
        in_specs=[pl.BlockSpec((rows, D_MODEL), lambda j: (0, 0)),
                  pl.BlockSpec((D_MODEL, bn), lambda j: (0, j)),
                  pl.BlockSpec((1, bn), lambda j: (0, j))],
        out_specs=pl.BlockSpec((rows, bn), lambda j: (0, j)),
        out_shape=jax.ShapeDtypeStruct((rows, n), F32),
        name="mod",
    )(c_pad, w_ada, b_ada)


def _inproj_kernel(x_ref, shift_ref, scale_ref, gpre_ref, w_ref, pos_ref, invf_ref, bf_ref,
                   tri_ref, eqk_ref, oq_ref, ok_ref, ind_ref,
                   qaug_ref, kaug_ref, vaug_ref, ga_ref, qb_ref, kd_ref, vd_ref, gb_ref, st_ref,
                   carry_ref):
    tm = TM_SUB

    @pl.when(pl.program_id(1) == 0)
    def _():
        carry_ref[...] = jnp.zeros_like(carry_ref)

    for sub in range(x_ref.shape[1] // tm):
        _inproj_subtile(sub, slice(sub * tm, (sub + 1) * tm),
                        x_ref, shift_ref, scale_ref, gpre_ref, w_ref, pos_ref, invf_ref, bf_ref,
                        tri_ref, eqk_ref, oq_ref, ok_ref, ind_ref,
                        qaug_ref, kaug_ref, vaug_ref, ga_ref, qb_ref, kd_ref, vd_ref, gb_ref,
                        st_ref, carry_ref)


def _inproj_subtile(sub, rows, x_ref, shift_ref, scale_ref, gpre_ref, w_ref, pos_ref, invf_ref,
                    bf_ref, tri_ref, eqk_ref, oq_ref, ok_ref, ind_ref,
                    qaug_ref, kaug_ref, vaug_ref, ga_ref, qb_ref, kd_ref, vd_ref, gb_ref,
                    st_ref, carry_ref):
    tm = TM_SUB
    x = x_ref[0, rows, :]
    ms = jnp.mean(x * x, axis=-1, keepdims=True)
    y = x * lax.rsqrt(ms + RMS_EPS) * gpre_ref[...]
    h = (y * (1.0 + scale_ref[0]) + shift_ref[0]).astype(BF16)

    def proj(off, width):
        return jnp.dot(h, w_ref[:, off:off + width], preferred_element_type=F32)

    tail = proj(OFF_KD, 3 * LANES)
    f = tail[:, OFF_F - OFF_KD:] + bf_ref[...]
    ls = jnp.minimum(f, 0.0) - jnp.log1p(jnp.exp(-jnp.abs(f)))
    lane = lax.broadcasted_iota(jnp.int32, (tm, LANES), 1)

    def by_group(a, b_, c_):
        return jnp.where(lane < FOX_HEADS, a, jnp.where(lane < 2 * FOX_HEADS, b_, c_))

    part = by_group(*(t.astype(F32) for t in _split3(ls))).astype(BF16)
    psum = jnp.dot(tri_ref[...], part, preferred_element_type=F32)
    cum = (psum + pltpu.roll(psum, LANES - FOX_HEADS, 1)
           + pltpu.roll(psum, LANES - 2 * FOX_HEADS, 1))
    cum = cum + carry_ref[...]
    carry_ref[...] = cum[tm - 1:tm, :]

    cum2 = cum * LOG2E
    cum2 = by_group(cum2, pltpu.roll(cum2, FOX_HEADS, 1), pltpu.roll(cum2, 2 * FOX_HEADS, 1))
    cs = by_group(*(t.astype(F32) for t in _split3(cum2))).astype(BF16)
    placed = jnp.dot(cs, eqk_ref[...], preferred_element_type=F32)
    augq = placed * ok_ref[...] + oq_ref[...]
    augk = placed * oq_ref[...] + ok_ref[...]

    lane_w = lax.broadcasted_iota(jnp.int32, (tm, AUG_W), 1)
    data = (((lane_w >> 6) ^ (lane_w >> 7)) & 1) == 0

    def rep(a):
        return jnp.concatenate(
            [a[:, LANES * (hd // 2):LANES * (hd // 2 + 1)] for hd in range(FOX_HEADS)], axis=1)

    qa = proj(OFF_QA, FOX_W)
    ka = proj(OFF_KA, FOX_W)
    qaug_ref[0, :, rows] = jnp.where(data, rep(qa), augq).T.astype(BF16)
    kaug_ref[0, rows, :] = jnp.where(data, rep(ka), augk).astype(BF16)

    q16 = qa.astype(BF16).astype(F32)
    k16 = ka.astype(BF16).astype(F32)
    sq = jnp.concatenate([q16 * q16, k16 * k16], axis=1).astype(BF16)
    nrm2 = jnp.dot(sq, ind_ref[...], preferred_element_type=F32)
    nmax = jnp.sqrt(jnp.max(nrm2, axis=0, keepdims=True)) * NORM_MARGIN
    st_ref[0, sub] = jnp.concatenate(
        [nmax, cum2[0:1], cum2[tm - 1:tm], jnp.zeros((5, LANES), F32)], axis=0)
    vaug_ref[0, :, rows] = jnp.where(data, rep(proj(OFF_VA, FOX_W)), 1.0).T.astype(BF16)

    za = proj(OFF_ZA, FOX_W)
    ga_ref[0, rows, :] = (za * (1.0 / (1.0 + jnp.exp(-za)))).astype(BF16)
    zb = proj(OFF_ZB, SWA_W)
    gb_ref[0, rows, :] = (zb * (1.0 / (1.0 + jnp.exp(-zb)))).astype(BF16)

    ang_t = invf_ref[...] * pos_ref[0, :, rows].astype(F32)
    reps = LANES // (HEAD_DIM // 2)
    cosv = jnp.concatenate([jnp.cos(ang_t)] * reps, axis=0).T
    sinv = jnp.concatenate([jnp.sin(ang_t)] * reps, axis=0).T
    first = (lane & (HEAD_DIM // 2)) == 0
    sin_signed = jnp.where(first, -sinv, sinv)

    def rope(a):
        outs = []
        for cidx in range(a.shape[1] // LANES):
            blk = a[:, cidx * LANES:(cidx + 1) * LANES]
            other = jnp.where(first, pltpu.roll(blk, LANES - HEAD_DIM // 2, 1),
                              pltpu.roll(blk, HEAD_DIM // 2, 1))
            outs.append(blk * cosv + other * sin_signed)
        return jnp.concatenate(outs, axis=1)

    qb_ref[0, rows, :] = rope(proj(OFF_QB, SWA_W)).astype(BF16)
    kr = rope(tail[:, :LANES])
    ks = pltpu.roll(kr, HEAD_DIM, 1)
    low = lane < HEAD_DIM
    kd_ref[0, rows, :] = jnp.concatenate([jnp.where(low, kr, ks), jnp.where(low, ks, kr)],
                                axis=1).astype(BF16)
    vt = tail[:, OFF_VD - OFF_KD:OFF_F - OFF_KD].T
    ones = jnp.ones((HEAD_DIM, tm), F32)
    vd_ref[0, :, rows] = jnp.concatenate([vt[:HEAD_DIM], ones, vt[HEAD_DIM:], ones],
                                axis=0).astype(BF16)


def _inproj_call(x, shift, scale, g_pre, w_perm, pos3, invf, bf_pad, tri, eqk, oq, ok, ind):
    B, S, D = x.shape
    tm = TM_IN
    row = lambda b, t: (b, t, 0)
    per_b = lambda b, t: (b, 0, 0)
    const2 = lambda b, t: (0, 0)
    out_w = (AUG_W, AUG_W, AUG_W, FOX_W, SWA_W, 2 * SWA_KV_W, 2 * SWA_KV_W, SWA_W)
    transposed = (0, 2, 6)
    return pl.pallas_call(
        _inproj_kernel,
        grid=(B, S // tm),
        in_specs=[pl.BlockSpec((1, tm, D), row),
                  pl.BlockSpec((1, 1, D), per_b),
                  pl.BlockSpec((1, 1, D), per_b),
                  pl.BlockSpec((1, D), const2),
                  pl.BlockSpec((D, IN_NP), const2),
                  pl.BlockSpec((1, 1, tm), lambda b, t: (b, 0, t)),
                  pl.BlockSpec((HEAD_DIM // 2, 1), const2),
                  pl.BlockSpec((1, LANES), const2),
                  pl.BlockSpec((TM_SUB, TM_SUB), const2),
                  pl.BlockSpec((LANES, AUG_W), const2),
                  pl.BlockSpec((1, AUG_W), const2),
                  pl.BlockSpec((1, AUG_W), const2),
                  pl.BlockSpec((2 * FOX_W, LANES), const2)],
        out_specs=[pl.BlockSpec((1, w, tm), lambda b, t: (b, 0, t)) if i in transposed
                   else pl.BlockSpec((1, tm, w), row) for i, w in enumerate(out_w)]
        + [pl.BlockSpec((1, tm // TM_SUB, 8, LANES), lambda b, t: (b, t, 0, 0))],
        out_shape=[jax.ShapeDtypeStruct((B, w, S) if i in transposed else (B, S, w), BF16)
                   for i, w in enumerate(out_w)]
        + [jax.ShapeDtypeStruct((B, S // TM_SUB, 8, LANES), F32)],
        scratch_shapes=[pltpu.VMEM((1, LANES), F32)],
        compiler_params=pltpu.CompilerParams(
            dimension_semantics=("arbitrary", "arbitrary"), vmem_limit_bytes=VMEM_LIMIT),
        name="in_proj",
    )(x, shift, scale, g_pre, w_perm, pos3, invf, bf_pad, tri, eqk, oq, ok, ind)


def _fox_tables(nq):
    diag = [(qi, qi, hh * nq + qi, hh) for hh in range(2) for qi in range(nq)]
    tab = np.array(diag + [(1, 0, 2 * nq, 0)], np.int32).T
    return tab, len(diag), nq * (nq - 1)


def _fox_kernel(tab_ref, st_ref, q_ref, k_ref, v_ref, g_ref, o_ref,
                off_tab, gen_tab, s0, s1, p0, p1, al0, al1, m_st, acc_st, *, n_diag, n_off):
    tq = TQ_FOX
    tk = TK_FOX
    nq = q_ref.shape[2] // tq
    s_buf = (s0, s1)
    p_buf = (p0, p1)
    al_buf = (al0, al1)

    def run(tab, n_items, dummy, diag):
        half = (n_items + 1) // 2
        lens = (half, n_items - half)
        hk, hq = tk // 2, tq // 2

        def item(stream, t):
            return jnp.where(t < lens[stream], stream * half + t, dummy)

        def stage_qk(t, slot):
            for sm in range(2):
                e = item(sm, t)
                qoff = pl.multiple_of(tab[0, e] * tq, tq)
                koff = pl.multiple_of(tab[1, e] * tk, tk)
                hoff = pl.multiple_of(tab[3, e] * LANES, LANES)
                qt = q_ref[0, pl.ds(hoff, LANES), pl.ds(qoff, tq)]
                k = k_ref[0, pl.ds(koff, tk), pl.ds(hoff, LANES)]
                if diag:
                    s_buf[slot][sm, :hk, :] = jnp.dot(k[:hk], qt, preferred_element_type=F32)
                    s_buf[slot][sm, hk:, hq:] = jnp.dot(k[hk:], qt[:, hq:],
                                                        preferred_element_type=F32)
                else:
                    s_buf[slot][sm] = jnp.dot(k, qt, preferred_element_type=F32)

        def stage_softmax(t, slot):
            for sm in range(2):
                st = tab[2, item(sm, t)]
                if diag:
                    kr = lax.broadcasted_iota(jnp.int32, (hk, tq), 0)
                    qc = lax.broadcasted_iota(jnp.int32, (hk, tq), 1)
                    top = jnp.where(kr <= qc, s_buf[slot][sm, :hk, :], NEG_BIG)
                    kr2 = lax.broadcasted_iota(jnp.int32, (tk - hk, tq - hq), 0)
                    qc2 = lax.broadcasted_iota(jnp.int32, (tk - hk, tq - hq), 1)
                    low = jnp.where(kr2 <= qc2, s_buf[slot][sm, hk:, hq:], NEG_BIG)
                    top_a, top_b = top[:, :hq], top[:, hq:]
                    m_a = jnp.max(top_a, axis=0, keepdims=True)
                    m_b = jnp.maximum(jnp.max(top_b, axis=0, keepdims=True),
                                      jnp.max(low, axis=0, keepdims=True))
                    m_next = jnp.concatenate([m_a, m_b], axis=1)
                    p_buf[slot][sm, :hk, :hq] = jnp.exp2(top_a - m_a).astype(BF16)
                    p_buf[slot][sm, :hk, hq:] = jnp.exp2(top_b - m_b).astype(BF16)
                    p_buf[slot][sm, hk:, hq:] = jnp.exp2(low - m_b).astype(BF16)
                else:
                    s = s_buf[slot][sm]
                    m_prev = m_st[st]
                    m_next = jnp.maximum(m_prev, jnp.max(s, axis=0, keepdims=True))
                    al_buf[slot][sm] = jnp.exp2(m_prev - m_next)
                    p_buf[slot][sm] = jnp.exp2(s - m_next).astype(BF16)
                m_st[st] = m_next

        def stage_pv(t, slot):
            for sm in range(2):
                e = item(sm, t)
                koff = pl.multiple_of(tab[1, e] * tk, tk)
                hoff = pl.multiple_of(tab[3, e] * LANES, LANES)
                st = tab[2, e]
                vt = v_ref[0, pl.ds(hoff, LANES), pl.ds(koff, tk)]
                if diag:
                    acc_st[st] = jnp.concatenate(
                        [jnp.dot(vt[:, :hk], p_buf[slot][sm, :hk, :hq],
                                 preferred_element_type=F32),
                         jnp.dot(vt, p_buf[slot][sm, :, hq:], preferred_element_type=F32)],
                        axis=1)
                else:
                    pv = jnp.dot(vt, p_buf[slot][sm], preferred_element_type=F32)
                    acc_st[st] = al_buf[slot][sm] * acc_st[st] + pv

        stage_qk(0, 0)
        stage_qk(1, 1)
        stage_softmax(0, 0)

        def body(u, carry):
            for d in range(2):
                t = 2 * u + 1 + d
                stage_qk(t + 1, d % 2)
                stage_softmax(t, (d + 1) % 2)
                stage_pv(t - 1, d % 2)
            return carry

        lax.fori_loop(0, (half + 1) // 2, body, 0)

    def run_fixed(tab, n_items, dummy):
        half = (n_items + 1) // 2
        lens = (half, n_items - half)

        def item(stream, t):
            return jnp.where(t < lens[stream], stream * half + t, dummy)

        def stage_probs(t, slot):
            for sm in range(2):
                e = item(sm, t)
                qoff = pl.multiple_of(tab[0, e] * tq, tq)
                koff = pl.multiple_of(tab[1, e] * tk, tk)
                hoff = pl.multiple_of(tab[3, e] * LANES, LANES)
                qt = q_ref[0, pl.ds(hoff, LANES), pl.ds(qoff, tq)]
                k = k_ref[0, pl.ds(koff, tk), pl.ds(hoff, LANES)]
                s = jnp.dot(k, qt, preferred_element_type=F32)
                p_buf[slot][sm] = jnp.exp2(s - m_st[tab[2, e]]).astype(BF16)

        def stage_pv(t, slot):
            for sm in range(2):
                e = item(sm, t)
                koff = pl.multiple_of(tab[1, e] * tk, tk)
                hoff = pl.multiple_of(tab[3, e] * LANES, LANES)
                st = tab[2, e]
                vt = v_ref[0, pl.ds(hoff, LANES), pl.ds(koff, tk)]
                acc_st[st] = acc_st[st] + jnp.dot(vt, p_buf[slot][sm],
                                                  preferred_element_type=F32)

        stage_probs(0, 0)

        def body(u, carry):
            for d in range(2):
                t = 2 * u + 1 + d
                stage_probs(t, (d + 1) % 2)
                stage_pv(t - 1, d % 2)
            return carry

        lax.fori_loop(0, (half + 1) // 2, body, 0)

    assert tab_ref.shape[1] == n_diag + 1
    run(tab_ref, n_diag, n_diag, True)

    b = pl.program_id(0)
    pr = pl.program_id(1)

    def put(tab, idx, qrow, j, state, head):
        tab[0, idx] = qrow
        tab[1, idx] = j
        tab[2, idx] = state
        tab[3, idx] = head

    n_fix = jnp.int32(0)
    n_gen = jnp.int32(0)
    for hh in range(2):
        hd = 2 * pr + hh

        kmax = lax.fori_loop(
            0, nq, lambda j, m, hd=hd: jnp.maximum(m, st_ref[b, j, 0, FOX_HEADS + hd]),
            jnp.float32(0.0))

        def list_row(qi, counts, hh=hh, hd=hd, kmax=kmax):
            base = (st_ref[b, qi, 0, hd] * (st_ref[b, qi, 0, FOX_HEADS + hd] + kmax)
                    + st_ref[b, qi, 1, hd])
            fixed_ok = base - st_ref[b, qi - 1, 2, hd] <= FIXED_REF_T

            def scan(tab, enabled, cnt):
                def contributes(carry):
                    j, _ = carry
                    bound = base - st_ref[b, jnp.maximum(j, 0), 2, hd]
                    return jnp.logical_and(jnp.logical_and(enabled, j >= 0), bound > -PRUNE_T)

                def take(carry):
                    j, cnt = carry
                    put(tab, cnt, qi, j, hh * nq + qi, hh)
                    return j - 1, cnt + 1

                return lax.while_loop(contributes, take, (qi - 1, cnt))[1]

            return (scan(off_tab, fixed_ok, counts[0]),
                    scan(gen_tab, jnp.logical_not(fixed_ok), counts[1]))

        n_fix, n_gen = lax.fori_loop(1, nq, list_row, (n_fix, n_gen))
    put(off_tab, n_off, 1, 0, 2 * nq, 0)
    put(gen_tab, n_off, 1, 0, 2 * nq, 0)
    run_fixed(off_tab, n_fix, n_off)

    @pl.when(n_gen > 0)
    def _():
        run(gen_tab, n_gen, n_off, False)

    def finish(qi, carry):
        a0 = acc_st[qi]
        a1 = acc_st[nq + qi]
        ot = jnp.concatenate([a0[:HEAD_DIM] / a0[HEAD_DIM:HEAD_DIM + 1],
                              a1[HEAD_DIM:] / a1[0:1]], axis=0)
        rows = pl.ds(pl.multiple_of(qi * tq, tq), tq)
        o_ref[0, rows, :] = (ot.T * g_ref[0, rows, :].astype(F32)).astype(BF16)
        return carry

    lax.fori_loop(0, nq, finish, 0)


def _fox_call(qaug_t, kaug, vaug_t, ga, stats):
    B, S, _ = kaug.shape
    tq, tk = TQ_FOX, TK_FOX
    assert tq == tk
    nq = S // tq
    tab, n_diag, n_off = _fox_tables(nq)
    pair = lambda b, p: (b, 0, p)
    return pl.pallas_call(
        functools.partial(_fox_kernel, n_diag=n_diag, n_off=n_off),
        grid=(B, FOX_HEADS // 2),
        in_specs=[pl.BlockSpec(memory_space=pltpu.SMEM),
                  pl.BlockSpec(memory_space=pltpu.SMEM),
                  pl.BlockSpec((1, 2 * LANES, S), lambda b, p: (b, p, 0)),
                  pl.BlockSpec((1, S, 2 * LANES), pair),
                  pl.BlockSpec((1, 2 * LANES, S), lambda b, p: (b, p, 0)),
                  pl.BlockSpec((1, S, LANES), pair)],
        out_specs=pl.BlockSpec((1, S, LANES), pair),
        out_shape=jax.ShapeDtypeStruct((B, S, FOX_W), BF16),
        scratch_shapes=[pltpu.SMEM((4, n_off + 1), jnp.int32),
                        pltpu.SMEM((4, n_off + 1), jnp.int32),
                        pltpu.VMEM((2, tk, tq), F32), pltpu.VMEM((2, tk, tq), F32),
                        pltpu.VMEM((2, tk, tq), BF16), pltpu.VMEM((2, tk, tq), BF16),
                        pltpu.VMEM((2, 1, tq), F32), pltpu.VMEM((2, 1, tq), F32),
                        pltpu.VMEM((2 * nq + 1, 1, tq), F32),
                        pltpu.VMEM((2 * nq + 1, LANES, tq), F32)],
        compiler_params=pltpu.CompilerParams(
            dimension_semantics=("arbitrary", "arbitrary"), vmem_limit_bytes=VMEM_LIMIT),
        name="fox",
    )(jnp.asarray(tab), stats, qaug_t, kaug, vaug_t, ga)


def _swa_out_kernel(sinks_ref, q_ref, kp_ref, kc_ref, vtp_ref, vtc_ref, g_ref,
                    oa_ref, x_ref, gate_ref, gpost_ref, wa_ref, wb_ref, out_ref,
                    s_sc, p_sc, e_sc, ob_sc):
    i = pl.program_id(1)
    nsub = TQ_SWA // WINDOW
    group = SWA_Q_HEADS // SWA_KV_HEADS
    ncol = group * WINDOW
    kall = jnp.concatenate([kp_ref[0], kc_ref[0]], axis=0)
    vtall = jnp.concatenate([vtp_ref[0], vtc_ref[0]], axis=1)
    lane = lax.broadcasted_iota(jnp.int32, (WINDOW, LANES), 1)
    lo = lane < HEAD_DIM
    zero = jnp.zeros((WINDOW, LANES), BF16)
    kj_ = lax.broadcasted_iota(jnp.int32, (2 * WINDOW, ncol), 0)
    qi_ = lax.broadcasted_iota(jnp.int32, (2 * WINDOW, ncol), 1) & (WINDOW - 1)
    rel = qi_ + WINDOW - kj_
    band = (rel >= 0) & (rel < WINDOW)
    nt = (((1,), (1,)), ((), ()))
    probs = [(r, g) for r in range(nsub) for g in range(SWA_KV_HEADS)]

    def stage_qk(n):
        r, g = probs[n]
        rows = slice(r * WINDOW, (r + 1) * WINDOW)
        kg = kall[r * WINDOW:(r + 2) * WINDOW, g * LANES:(g + 1) * LANES]
        c0 = g * group * HEAD_DIM
        qp0 = q_ref[0, rows, c0:c0 + LANES]
        qp1 = q_ref[0, rows, c0 + LANES:c0 + 2 * LANES]
        qs = jnp.concatenate([jnp.where(lo, qp0, zero), jnp.where(lo, qp1, zero),
                              jnp.where(lo, zero, qp0), jnp.where(lo, zero, qp1)], axis=0)
        s_sc[n] = lax.dot_general(kg, qs, nt, preferred_element_type=F32)

    def stage_softmax(n):
        r, g = probs[n]
        valid = band & (kj_ >= jnp.where(i == 0, WINDOW, 0)) if r == 0 else band
        st = jnp.where(valid, s_sc[n], NEG_BIG)
        heads = (group * g, group * g + 2, group * g + 1, group * g + 3)
        sink = jnp.concatenate(
            [jnp.full((1, WINDOW), sinks_ref[hd] * LOG2E, F32) for hd in heads], axis=1)
        m = jnp.maximum(jnp.max(st, axis=0, keepdims=True), sink)
        p_sc[n] = jnp.exp2(st - m).astype(BF16)
        e_sc[n] = jnp.exp2(sink - m)

    def stage_pv(n):
        r, g = probs[n]
        rows = slice(r * WINDOW, (r + 1) * WINDOW)
        c0 = g * group * HEAD_DIM
        vtg = vtall[g * LANES:(g + 1) * LANES, r * WINDOW:(r + 2) * WINDOW]
        acc = jnp.dot(vtg, p_sc[n], preferred_element_type=F32)
        l = acc[HEAD_DIM:HEAD_DIM + 1] + e_sc[n]
        on = acc[:HEAD_DIM] * (1.0 / l)
        pair0 = jnp.concatenate([on[:, 0:WINDOW], on[:, 2 * WINDOW:3 * WINDOW]], axis=0).T
        pair1 = jnp.concatenate([on[:, WINDOW:2 * WINDOW], on[:, 3 * WINDOW:]], axis=0).T
        ob_sc[rows, c0:c0 + LANES] = (
            pair0 * g_ref[0, rows, c0:c0 + LANES].astype(F32)).astype(BF16)
        ob_sc[rows, c0 + LANES:c0 + 2 * LANES] = (
            pair1 * g_ref[0, rows, c0 + LANES:c0 + 2 * LANES].astype(F32)).astype(BF16)

    for t in range(len(probs) + 2):
        if t < len(probs):
            stage_qk(t)
        if 1 <= t <= len(probs):
            stage_softmax(t - 1)
        if t >= 2:
            stage_pv(t - 2)

    y = (jnp.dot(oa_ref[0], wa_ref[...], preferred_element_type=F32)
         + jnp.dot(ob_sc[...], wb_ref[...], preferred_element_type=F32))
    ms = jnp.mean(y * y, axis=-1, keepdims=True)
    yn = y * lax.rsqrt(ms + RMS_EPS) * gpost_ref[...]
    out_ref[0] = x_ref[0] + gate_ref[0] * yn


def _swa_out_call(sinks, qb, kd, vdt, gb, oa, x, gate, g_post, wa, wb):
    B, S, D = x.shape
    tq = TQ_SWA
    nsub = tq // WINDOW
    cur = lambda b, i: (b, i, 0)
    prev = lambda b, i: (b, jnp.maximum(i * nsub - 1, 0), 0)
    cur_t = lambda b, i: (b, 0, i)
    prev_t = lambda b, i: (b, 0, jnp.maximum(i * nsub - 1, 0))
    const2 = lambda b, i: (0, 0)
    kvw = 2 * SWA_KV_W
    nprob = nsub * SWA_KV_HEADS
    ncol = SWA_Q_HEADS // SWA_KV_HEADS * WINDOW
    return pl.pallas_call(
        _swa_out_kernel,
        grid=(B, S // tq),
        in_specs=[pl.BlockSpec(memory_space=pltpu.SMEM),
                  pl.BlockSpec((1, tq, SWA_W), cur),
                  pl.BlockSpec((1, WINDOW, kvw), prev),
                  pl.BlockSpec((1, tq, kvw), cur),
                  pl.BlockSpec((1, kvw, WINDOW), prev_t),
                  pl.BlockSpec((1, kvw, tq), cur_t),
                  pl.BlockSpec((1, tq, SWA_W), cur),
                  pl.BlockSpec((1, tq, FOX_W), cur),
                  pl.BlockSpec((1, tq, D), cur),
                  pl.BlockSpec((1, 1, D), lambda b, i: (b, 0, 0)),
                  pl.BlockSpec((1, D), const2),
                  pl.BlockSpec((FOX_W, D), const2),
                  pl.BlockSpec((SWA_W, D), const2)],
        out_specs=pl.BlockSpec((1, tq, D), cur),
        out_shape=jax.ShapeDtypeStruct((B, S, D), F32),
        scratch_shapes=[pltpu.VMEM((nprob, 2 * WINDOW, ncol), F32),
                        pltpu.VMEM((nprob, 2 * WINDOW, ncol), BF16),
                        pltpu.VMEM((nprob, 1, ncol), F32),
                        pltpu.VMEM((tq, SWA_W), BF16)],
        compiler_params=pltpu.CompilerParams(
            dimension_semantics=("arbitrary", "arbitrary"), vmem_limit_bytes=VMEM_LIMIT),
        name="swa_out",
    )(sinks, qb, kd, kd, vdt, vdt, gb, oa, x, gate, g_post, wa, wb)


def _perm_w_in(w):
    sc = HEAD_DIM ** -0.5
    o = 0
    qa = w[:, o:o + FOX_W]; o += FOX_W
    ka = w[:, o:o + FOX_W]; o += FOX_W
    va = w[:, o:o + FOX_W]; o += FOX_W
    fa = w[:, o:o + FOX_HEADS]; o += FOX_HEADS
    za = w[:, o:o + FOX_W]; o += FOX_W
    qb = w[:, o:o + SWA_W]; o += SWA_W
    kb = w[:, o:o + SWA_KV_W]; o += SWA_KV_W
    vb = w[:, o:o + SWA_KV_W]; o += SWA_KV_W
    zb = w[:, o:o + SWA_W]
    fpad = jnp.concatenate(
        [fa, fa, fa, jnp.zeros((w.shape[0], LANES - 3 * FOX_HEADS), w.dtype)], axis=1)
    sc2 = sc * LOG2E
    return jnp.concatenate([qa * sc2, ka, va, za, qb * sc2, zb, kb, vb, fpad],
                           axis=1).astype(BF16)


def _aug_constants():
    eq = np.zeros((LANES, AUG_W), np.float32)
    ek = np.zeros((LANES, AUG_W), np.float32)
    oq = np.zeros((1, AUG_W), np.float32)
    ok = np.zeros((1, AUG_W), np.float32)
    for hd in range(FOX_HEADS):
        base = LANES * hd + (HEAD_DIM if hd % 2 == 0 else 0)
        for part in range(3):
            eq[part * FOX_HEADS + hd, base + part] = 1.0
            ok[0, base + part] = 1.0
            ek[part * FOX_HEADS + hd, base + 3 + part] = -1.0
            oq[0, base + 3 + part] = 1.0
    return jnp.asarray(eq + ek, BF16), jnp.asarray(oq), jnp.asarray(ok)


def kernel(x, c, positions, w_ada, b_ada, g_pre, w_in, b_fgate, sinks, w_out, g_post):
    B, S, D = x.shape
    depth = w_ada.shape[0]
    assert TM_SUB == TQ_FOX == TK_FOX
    half = HEAD_DIM // 2
    inv_freq = ROPE_THETA ** (-jnp.arange(half, dtype=F32) / half)
    invf = inv_freq[:, None]
    pos3 = positions[:, None, :]
    ind = np.zeros((2 * FOX_W, LANES), np.float32)
    ind[np.arange(2 * FOX_W), np.arange(2 * FOX_W) // HEAD_DIM] = 1.0
    ind = jnp.asarray(ind, BF16)
    tri = jnp.asarray(np.tril(np.ones((TM_SUB, TM_SUB), np.float32)), BF16)
    eqk, oq, ok = _aug_constants()
    c_pad = jnp.zeros((8, D), F32).at[:B].set(c)
    for l in range(depth):
        mod = _mod_call(c_pad, w_ada[l], b_ada[l][None, :])[:B]
        shift = mod[:, None, 0:D]
        scale = mod[:, None, D:2 * D]
        gate = mod[:, None, 2 * D:3 * D]
        bf_pad = jnp.concatenate(
            [b_fgate[l]] * 3 + [jnp.zeros((LANES - 3 * FOX_HEADS,), F32)])[None, :]
        qaug, kaug, vaug, ga, qb, kd, vd, gb, stats = _inproj_call(
            x, shift, scale, g_pre[l][None, :], _perm_w_in(w_in[l]), pos3, invf, bf_pad,
            tri, eqk, oq, ok, ind)
        oa = _fox_call(qaug, kaug, vaug, ga, stats[:, :, 0:3, 0:2 * FOX_HEADS])
        wo = w_out[l].astype(BF16)
        x = _swa_out_call(sinks[l], qb, kd, vd, gb, oa, x, gate, g_post[l][None, :],
                          wo[:FOX_W], wo[FOX_W:])
    return x
```

```python
import functools

import jax
import jax.numpy as jnp
import numpy as np
from jax import lax
from jax.experimental import pallas as pl
from jax.experimental.pallas import tpu as pltpu

D_MODEL = 1024
HEAD_DIM = 64
FOX_HEADS = 8
SWA_Q_HEADS = 8
SWA_KV_HEADS = 2
WINDOW = 128
ROPE_THETA = 10000.0
RMS_EPS = 1e-6
FOX_W = FOX_HEADS * HEAD_DIM
SWA_W = SWA_Q_HEADS * HEAD_DIM
SWA_KV_W = SWA_KV_HEADS * HEAD_DIM

LANES = 128
AUG_W = FOX_HEADS * LANES

OFF_QA = 0
OFF_KA = OFF_QA + FOX_W
OFF_VA = OFF_KA + FOX_W
OFF_ZA = OFF_VA + FOX_W
OFF_QB = OFF_ZA + FOX_W
OFF_ZB = OFF_QB + SWA_W
OFF_KD = OFF_ZB + SWA_W
OFF_VD = OFF_KD + SWA_KV_W
OFF_F = OFF_VD + SWA_KV_W
IN_NP = OFF_F + LANES

TM_IN = 1024
TM_SUB = 512
TQ_FOX = 512
TK_FOX = 512
TQ_SWA = 512
NEG_BIG = -1e30
LOG2E = 1.4426950408889634
PRUNE_T = 140.0
NORM_MARGIN = 1.01
FIXED_REF_T = 100.0
VMEM_LIMIT = 56 * 1024 * 1024

F32 = jnp.float32
BF16 = jnp.bfloat16


def _split3(a):
    hi = a.astype(BF16)
    r = a - hi.astype(F32)
    mid = r.astype(BF16)
    lo = (r - mid.astype(F32)).astype(BF16)
    return hi, mid, lo


def _mod_kernel(c_ref, w_ref, b_ref, o_ref):
    c = c_ref[...]
    sc = c * (1.0 / (1.0 + jnp.exp(-c)))
    o_ref[...] = jnp.dot(sc, w_ref[...], precision=lax.Precision.HIGHEST,
                         preferred_element_type=F32) + b_ref[...]


def _mod_call(c_pad, w_ada, b_ada):
    rows = c_pad.shape[0]
    n = w_ada.shape[1]
    bn = D_MODEL
    return pl.pallas_call(
        _mod_kernel,
        grid=(n // bn,),
        in_specs=[pl.BlockSpec((rows, D_MODEL), lambda j: (0, 0)),
                  pl.BlockSpec((D_MODEL, bn), lambda j: (0, j)),
                  pl.BlockSpec((1, bn), lambda j: (0, j))],
        out_specs=pl.BlockSpec((rows, bn), lambda j: (0, j)),
        out_shape=jax.ShapeDtypeStruct((rows, n), F32),
        name="mod",
    )(c_pad, w_ada, b_ada)


def _inproj_kernel(x_ref, shift_ref, scale_ref, gpre_ref, w_ref, pos_ref, invf_ref, bf_ref,
                   tri_ref, eqk_ref, oq_ref, ok_ref, ind_ref,
                   qaug_ref, kaug_ref, vaug_ref, ga_ref, qb_ref, kd_ref, vd_ref, gb_ref, st_ref,
                   carry_ref):
    tm = TM_SUB

    @pl.when(pl.program_id(1) == 0)
    def _():
        carry_ref[...] = jnp.zeros_like(carry_ref)

    for sub in range(x_ref.shape[1] // tm):
        _inproj_subtile(sub, slice(sub * tm, (sub + 1) * tm),
                        x_ref, shift_ref, scale_ref, gpre_ref, w_ref, pos_ref, invf_ref, bf_ref,
                        tri_ref, eqk_ref, oq_ref, ok_ref, ind_ref,
                        qaug_ref, kaug_ref, vaug_ref, ga_ref, qb_ref, kd_ref, vd_ref, gb_ref,
                        st_ref, carry_ref)


def _inproj_subtile(sub, rows, x_ref, shift_ref, scale_ref, gpre_ref, w_ref, pos_ref, invf_ref,
                    bf_ref, tri_ref, eqk_ref, oq_ref, ok_ref, ind_ref,
                    qaug_ref, kaug_ref, vaug_ref, ga_ref, qb_ref, kd_ref, vd_ref, gb_ref,
                    st_ref, carry_ref):
    tm = TM_SUB
    x = x_ref[0, rows, :]
    ms = jnp.mean(x * x, axis=-1, keepdims=True)
    y = x * lax.rsqrt(ms + RMS_EPS) * gpre_ref[...]
    h = (y * (1.0 + scale_ref[0]) + shift_ref[0]).astype(BF16)

    def proj(off, width):
        return lax.dot_general(h, w_ref[off:off + width, :], (((1,), (1,)), ((), ())),
                               preferred_element_type=F32)

    tail = proj(OFF_KD, 3 * LANES)
    f = tail[:, OFF_F - OFF_KD:] + bf_ref[...]
    ls = jnp.minimum(f, 0.0) - jnp.log1p(jnp.exp(-jnp.abs(f)))
    lane = lax.broadcasted_iota(jnp.int32, (tm, LANES), 1)

    def by_group(a, b_, c_):
        return jnp.where(lane < FOX_HEADS, a, jnp.where(lane < 2 * FOX_HEADS, b_, c_))

    part = by_group(*(t.astype(F32) for t in _split3(ls))).astype(BF16)
    psum = jnp.dot(tri_ref[...], part, preferred_element_type=F32)
    cum = (psum + pltpu.roll(psum, LANES - FOX_HEADS, 1)
           + pltpu.roll(psum, LANES - 2 * FOX_HEADS, 1))
    cum = cum + carry_ref[...]
    carry_ref[...] = cum[tm - 1:tm, :]

    cum2 = cum * LOG2E
    cum2 = by_group(cum2, pltpu.roll(cum2, FOX_HEADS, 1), pltpu.roll(cum2, 2 * FOX_HEADS, 1))
    cs = by_group(*(t.astype(F32) for t in _split3(cum2))).astype(BF16)
    placed = jnp.dot(cs, eqk_ref[...], preferred_element_type=F32)
    augq = placed * ok_ref[...] + oq_ref[...]
    augk = placed * oq_ref[...] + ok_ref[...]

    lane_w = lax.broadcasted_iota(jnp.int32, (tm, AUG_W), 1)
    data = (((lane_w >> 6) ^ (lane_w >> 7)) & 1) == 0

    def rep(a):
        return jnp.concatenate(
            [a[:, LANES * (hd // 2):LANES * (hd // 2 + 1)] for hd in range(FOX_HEADS)], axis=1)

    qa = proj(OFF_QA, FOX_W)
    ka = proj(OFF_KA, FOX_W)
    qaug_ref[0, :, rows] = jnp.where(data, rep(qa), augq).T.astype(BF16)
    kaug_ref[0, rows, :] = jnp.where(data, rep(ka), augk).astype(BF16)

    q16 = qa.astype(BF16).astype(F32)
    k16 = ka.astype(BF16).astype(F32)
    sq = jnp.concatenate([q16 * q16, k16 * k16], axis=1).astype(BF16)
    nrm2 = jnp.dot(sq, ind_ref[...], preferred_element_type=F32)
    nmax = jnp.sqrt(jnp.max(nrm2, axis=0, keepdims=True)) * NORM_MARGIN
    st_ref[0, sub] = jnp.concatenate(
        [nmax, cum2[0:1], cum2[tm - 1:tm], jnp.zeros((5, LANES), F32)], axis=0)
    vaug_ref[0, :, rows] = jnp.where(data, rep(proj(OFF_VA, FOX_W)), 1.0).T.astype(BF16)

    za = proj(OFF_ZA, FOX_W)
    ga_ref[0, rows, :] = (za * (1.0 / (1.0 + jnp.exp(-za)))).astype(BF16)
    zb = proj(OFF_ZB, SWA_W)
    gb_ref[0, rows, :] = (zb * (1.0 / (1.0 + jnp.exp(-zb)))).astype(BF16)

    ang_t = invf_ref[...] * pos_ref[0, :, rows].astype(F32)
    reps = LANES // (HEAD_DIM // 2)
    cosv = jnp.concatenate([jnp.cos(ang_t)] * reps, axis=0).T
    sinv = jnp.concatenate([jnp.sin(ang_t)] * reps, axis=0).T
    first = (lane & (HEAD_DIM // 2)) == 0
    sin_signed = jnp.where(first, -sinv, sinv)

    def rope(a):
        outs = []
        for cidx in range(a.shape[1] // LANES):
            blk = a[:, cidx * LANES:(cidx + 1) * LANES]
            other = jnp.where(first, pltpu.roll(blk, LANES - HEAD_DIM // 2, 1),
                              pltpu.roll(blk, HEAD_DIM // 2, 1))
            outs.append(blk * cosv + other * sin_signed)
        return jnp.concatenate(outs, axis=1)

    qb_ref[0, rows, :] = rope(proj(OFF_QB, SWA_W)).astype(BF16)
    kr = rope(tail[:, :LANES])
    ks = pltpu.roll(kr, HEAD_DIM, 1)
    low = lane < HEAD_DIM
    kd_ref[0, rows, :] = jnp.concatenate([jnp.where(low, kr, ks), jnp.where(low, ks, kr)],
                                axis=1).astype(BF16)
    vt = tail[:, OFF_VD - OFF_KD:OFF_F - OFF_KD].T
    ones = jnp.ones((HEAD_DIM, tm), F32)
    vd_ref[0, :, rows] = jnp.concatenate([vt[:HEAD_DIM], ones, vt[HEAD_DIM:], ones],
                                axis=0).astype(BF16)


def _inproj_call(x, shift, scale, g_pre, w_perm, pos3, invf, bf_pad, tri, eqk, oq, ok, ind):
    B, S, D = x.shape
    tm = TM_IN
    row = lambda b, t: (b, t, 0)
    per_b = lambda b, t: (b, 0, 0)
    const2 = lambda b, t: (0, 0)
    out_w = (AUG_W, AUG_W, AUG_W, FOX_W, SWA_W, 2 * SWA_KV_W, 2 * SWA_KV_W, SWA_W)
    transposed = (0, 2, 6)
    return pl.pallas_call(
        _inproj_kernel,
        grid=(B, S // tm),
        in_specs=[pl.BlockSpec((1, tm, D), row),
                  pl.BlockSpec((1, 1, D), per_b),
                  pl.BlockSpec((1, 1, D), per_b),
                  pl.BlockSpec((1, D), const2),
                  pl.BlockSpec((IN_NP, D), const2),
                  pl.BlockSpec((1, 1, tm), lambda b, t: (b, 0, t)),
                  pl.BlockSpec((HEAD_DIM // 2, 1), const2),
                  pl.BlockSpec((1, LANES), const2),
                  pl.BlockSpec((TM_SUB, TM_SUB), const2),
                  pl.BlockSpec((LANES, AUG_W), const2),
                  pl.BlockSpec((1, AUG_W), const2),
                  pl.BlockSpec((1, AUG_W), const2),
                  pl.BlockSpec((2 * FOX_W, LANES), const2)],
        out_specs=[pl.BlockSpec((1, w, tm), lambda b, t: (b, 0, t)) if i in transposed
                   else pl.BlockSpec((1, tm, w), row) for i, w in enumerate(out_w)]
        + [pl.BlockSpec((1, tm // TM_SUB, 8, LANES), lambda b, t: (b, t, 0, 0))],
        out_shape=[jax.ShapeDtypeStruct((B, w, S) if i in transposed else (B, S, w), BF16)
                   for i, w in enumerate(out_w)]
        + [jax.ShapeDtypeStruct((B, S // TM_SUB, 8, LANES), F32)],
        scratch_shapes=[pltpu.VMEM((1, LANES), F32)],
        compiler_params=pltpu.CompilerParams(
            dimension_semantics=("arbitrary", "arbitrary"), vmem_limit_bytes=VMEM_LIMIT),
        name="in_proj",
    )(x, shift, scale, g_pre, w_perm, pos3, invf, bf_pad, tri, eqk, oq, ok, ind)


def _fox_tables(nq):
    diag = [(qi, qi, hh * nq + qi, hh) for hh in range(2) for qi in range(nq)]
    tab = np.array(diag + [(1, 0, 2 * nq, 0)], np.int32).T
    return tab, len(diag), nq * (nq - 1)


def _fox_kernel(tab_ref, st_ref, q_ref, k_ref, v_ref, g_ref, o_ref,
                off_tab, gen_tab, s0, s1, p0, p1, al0, al1, m_st, acc_st, *, n_diag, n_off):
    tq = TQ_FOX
    tk = TK_FOX
    nq = q_ref.shape[2] // tq
    s_buf = (s0, s1)
    p_buf = (p0, p1)
    al_buf = (al0, al1)

    def run(tab, n_items, dummy, diag):
        half = (n_items + 1) // 2
        lens = (half, n_items - half)
        hk, hq = tk // 2, tq // 2

        def item(stream, t):
            return jnp.where(t < lens[stream], stream * half + t, dummy)

        def stage_qk(t, slot):
            for sm in range(2):
                e = item(sm, t)
                qoff = pl.multiple_of(tab[0, e] * tq, tq)
                koff = pl.multiple_of(tab[1, e] * tk, tk)
                hoff = pl.multiple_of(tab[3, e] * LANES, LANES)
                qt = q_ref[0, pl.ds(hoff, LANES), pl.ds(qoff, tq)]
                k = k_ref[0, pl.ds(koff, tk), pl.ds(hoff, LANES)]
                if diag:
                    s_buf[slot][sm, :hk, :] = jnp.dot(k[:hk], qt, preferred_element_type=F32)
                    s_buf[slot][sm, hk:, hq:] = jnp.dot(k[hk:], qt[:, hq:],
                                                        preferred_element_type=F32)
                else:
                    s_buf[slot][sm] = jnp.dot(k, qt, preferred_element_type=F32)

        def stage_softmax(t, slot):
            for sm in range(2):
                st = tab[2, item(sm, t)]
                if diag:
                    kr = lax.broadcasted_iota(jnp.int32, (hk, tq), 0)
                    qc = lax.broadcasted_iota(jnp.int32, (hk, tq), 1)
                    top = jnp.where(kr <= qc, s_buf[slot][sm, :hk, :], NEG_BIG)
                    kr2 = lax.broadcasted_iota(jnp.int32, (tk - hk, tq - hq), 0)
                    qc2 = lax.broadcasted_iota(jnp.int32, (tk - hk, tq - hq), 1)
                    low = jnp.where(kr2 <= qc2, s_buf[slot][sm, hk:, hq:], NEG_BIG)
                    top_a, top_b = top[:, :hq], top[:, hq:]
                    m_a = jnp.max(top_a, axis=0, keepdims=True)
                    m_b = jnp.maximum(jnp.max(top_b, axis=0, keepdims=True),
                                      jnp.max(low, axis=0, keepdims=True))
                    m_next = jnp.concatenate([m_a, m_b], axis=1)
                    p_buf[slot][sm, :hk, :hq] = jnp.exp2(top_a - m_a).astype(BF16)
                    p_buf[slot][sm, :hk, hq:] = jnp.exp2(top_b - m_b).astype(BF16)
                    p_buf[slot][sm, hk:, hq:] = jnp.exp2(low - m_b).astype(BF16)
                else:
                    s = s_buf[slot][sm]
                    m_prev = m_st[st]
                    m_next = jnp.maximum(m_prev, jnp.max(s, axis=0, keepdims=True))
                    al_buf[slot][sm] = jnp.exp2(m_prev - m_next)
                    p_buf[slot][sm] = jnp.exp2(s - m_next).astype(BF16)
                m_st[st] = m_next

        def stage_pv(t, slot):
            for sm in range(2):
                e = item(sm, t)
                koff = pl.multiple_of(tab[1, e] * tk, tk)
                hoff = pl.multiple_of(tab[3, e] * LANES, LANES)
                st = tab[2, e]
                vt = v_ref[0, pl.ds(hoff, LANES), pl.ds(koff, tk)]
                if diag:
                    acc_st[st] = jnp.concatenate(
                        [jnp.dot(vt[:, :hk], p_buf[slot][sm, :hk, :hq],
                                 preferred_element_type=F32),
                         jnp.dot(vt, p_buf[slot][sm, :, hq:], preferred_element_type=F32)],
                        axis=1)
                else:
                    pv = jnp.dot(vt, p_buf[slot][sm], preferred_element_type=F32)
                    acc_st[st] = al_buf[slot][sm] * acc_st[st] + pv

        stage_qk(0, 0)
        stage_qk(1, 1)
        stage_softmax(0, 0)

        def body(u, carry):
            for d in range(2):
                t = 2 * u + 1 + d
                stage_qk(t + 1, d % 2)
                stage_softmax(t, (d + 1) % 2)
                stage_pv(t - 1, d % 2)
            return carry

        lax.fori_loop(0, (half + 1) // 2, body, 0)

    def run_fixed(tab, n_items, dummy):
        half = (n_items + 1) // 2
        lens = (half, n_items - half)

        def item(stream, t):
            return jnp.where(t < lens[stream], stream * half + t, dummy)

        def stage_probs(t, slot):
            for sm in range(2):
                e = item(sm, t)
                qoff = pl.multiple_of(tab[0, e] * tq, tq)
                koff = pl.multiple_of(tab[1, e] * tk, tk)
                hoff = pl.multiple_of(tab[3, e] * LANES, LANES)
                qt = q_ref[0, pl.ds(hoff, LANES), pl.ds(qoff, tq)]
                k = k_ref[0, pl.ds(koff, tk), pl.ds(hoff, LANES)]
                s = jnp.dot(k, qt, preferred_element_type=F32)
                p_buf[slot][sm] = jnp.exp2(s - m_st[tab[2, e]]).astype(BF16)

        def stage_pv(t, slot):
            for sm in range(2):
                e = item(sm, t)
                koff = pl.multiple_of(tab[1, e] * tk, tk)
                hoff = pl.multiple_of(tab[3, e] * LANES, LANES)
                st = tab[2, e]
                vt = v_ref[0, pl.ds(hoff, LANES), pl.ds(koff, tk)]
                acc_st[st] = acc_st[st] + jnp.dot(vt, p_buf[slot][sm],
                                                  preferred_element_type=F32)

        stage_probs(0, 0)

        def body(u, carry):
            for d in range(2):
                t = 2 * u + 1 + d
                stage_probs(t, (d + 1) % 2)
                stage_pv(t - 1, d % 2)
            return carry

        lax.fori_loop(0, (half + 1) // 2, body, 0)

    assert tab_ref.shape[1] == n_diag + 1
    run(tab_ref, n_diag, n_diag, True)

    b = pl.program_id(0)
    pr = pl.program_id(1)

    def put(tab, idx, qrow, j, state, head):
        tab[0, idx] = qrow
        tab[1, idx] = j
        tab[2, idx] = state
        tab[3, idx] = head

    n_fix = jnp.int32(0)
    n_gen = jnp.int32(0)
    for hh in range(2):
        hd = 2 * pr + hh

        kmax = lax.fori_loop(
            0, nq, lambda j, m, hd=hd: jnp.maximum(m, st_ref[b, j, 0, FOX_HEADS + hd]),
            jnp.float32(0.0))

        def list_row(qi, counts, hh=hh, hd=hd, kmax=kmax):
            base = (st_ref[b, qi, 0, hd] * (st_ref[b, qi, 0, FOX_HEADS + hd] + kmax)
                    + st_ref[b, qi, 1, hd])
            fixed_ok = base - st_ref[b, qi - 1, 2, hd] <= FIXED_REF_T

            def scan(tab, enabled, cnt):
                def contributes(carry):
                    j, _ = carry
                    bound = base - st_ref[b, jnp.maximum(j, 0), 2, hd]
                    return jnp.logical_and(jnp.logical_and(enabled, j >= 0), bound > -PRUNE_T)

                def take(carry):
                    j, cnt = carry
                    put(tab, cnt, qi, j, hh * nq + qi, hh)
                    return j - 1, cnt + 1

                return lax.while_loop(contributes, take, (qi - 1, cnt))[1]

            return (scan(off_tab, fixed_ok, counts[0]),
                    scan(gen_tab, jnp.logical_not(fixed_ok), counts[1]))

        n_fix, n_gen = lax.fori_loop(1, nq, list_row, (n_fix, n_gen))
    put(off_tab, n_off, 1, 0, 2 * nq, 0)
    put(gen_tab, n_off, 1, 0, 2 * nq, 0)
    run_fixed(off_tab, n_fix, n_off)

    @pl.when(n_gen > 0)
    def _():
        run(gen_tab, n_gen, n_off, False)

    def finish(qi, carry):
        a0 = acc_st[qi]
        a1 = acc_st[nq + qi]
        ot = jnp.concatenate([a0[:HEAD_DIM] / a0[HEAD_DIM:HEAD_DIM + 1],
                              a1[HEAD_DIM:] / a1[0:1]], axis=0)
        rows = pl.ds(pl.multiple_of(qi * tq, tq), tq)
        o_ref[0, rows, :] = (ot.T * g_ref[0, rows, :].astype(F32)).astype(BF16)
        return carry

    lax.fori_loop(0, nq, finish, 0)


def _fox_call(qaug_t, kaug, vaug_t, ga, stats):
    B, S, _ = kaug.shape
    tq, tk = TQ_FOX, TK_FOX
    assert tq == tk
    nq = S // tq
    tab, n_diag, n_off = _fox_tables(nq)
    pair = lambda b, p: (b, 0, p)
    return pl.pallas_call(
        functools.partial(_fox_kernel, n_diag=n_diag, n_off=n_off),
        grid=(B, FOX_HEADS // 2),
        in_specs=[pl.BlockSpec(memory_space=pltpu.SMEM),
                  pl.BlockSpec(memory_space=pltpu.SMEM),
                  pl.BlockSpec((1, 2 * LANES, S), lambda b, p: (b, p, 0)),
                  pl.BlockSpec((1, S, 2 * LANES), pair),
                  pl.BlockSpec((1, 2 * LANES, S), lambda b, p: (b, p, 0)),
                  pl.BlockSpec((1, S, LANES), pair)],
        out_specs=pl.BlockSpec((1, S, LANES), pair),
        out_shape=jax.ShapeDtypeStruct((B, S, FOX_W), BF16),
        scratch_shapes=[pltpu.SMEM((4, n_off + 1), jnp.int32),
                        pltpu.SMEM((4, n_off + 1), jnp.int32),
                        pltpu.VMEM((2, tk, tq), F32), pltpu.VMEM((2, tk, tq), F32),
                        pltpu.VMEM((2, tk, tq), BF16), pltpu.VMEM((2, tk, tq), BF16),
                        pltpu.VMEM((2, 1, tq), F32), pltpu.VMEM((2, 1, tq), F32),
                        pltpu.VMEM((2 * nq + 1, 1, tq), F32),
                        pltpu.VMEM((2 * nq + 1, LANES, tq), F32)],
        compiler_params=pltpu.CompilerParams(
            dimension_semantics=("arbitrary", "arbitrary"), vmem_limit_bytes=VMEM_LIMIT),
        name="fox",
    )(jnp.asarray(tab), stats, qaug_t, kaug, vaug_t, ga)


def _swa_out_kernel(sinks_ref, q_ref, kp_ref, kc_ref, vtp_ref, vtc_ref, g_ref,
                    oa_ref, x_ref, gate_ref, gpost_ref, wa_ref, wb_ref, out_ref,
                    s_sc, p_sc, e_sc, ob_sc):
    i = pl.program_id(1)
    nsub = TQ_SWA // WINDOW
    group = SWA_Q_HEADS // SWA_KV_HEADS
    ncol = group * WINDOW
    kall = jnp.concatenate([kp_ref[0], kc_ref[0]], axis=0)
    vtall = jnp.concatenate([vtp_ref[0], vtc_ref[0]], axis=1)
    lane = lax.broadcasted_iota(jnp.int32, (WINDOW, LANES), 1)
    lo = lane < HEAD_DIM
    zero = jnp.zeros((WINDOW, LANES), BF16)
    kj_ = lax.broadcasted_iota(jnp.int32, (2 * WINDOW, ncol), 0)
    qi_ = lax.broadcasted_iota(jnp.int32, (2 * WINDOW, ncol), 1) & (WINDOW - 1)
    rel = qi_ + WINDOW - kj_
    band = (rel >= 0) & (rel < WINDOW)
    nt = (((1,), (1,)), ((), ()))
    probs = [(r, g) for r in range(nsub) for g in range(SWA_KV_HEADS)]

    def stage_qk(n):
        r, g = probs[n]
        rows = slice(r * WINDOW, (r + 1) * WINDOW)
        kg = kall[r * WINDOW:(r + 2) * WINDOW, g * LANES:(g + 1) * LANES]
        c0 = g * group * HEAD_DIM
        qp0 = q_ref[0, rows, c0:c0 + LANES]
        qp1 = q_ref[0, rows, c0 + LANES:c0 + 2 * LANES]
        qs = jnp.concatenate([jnp.where(lo, qp0, zero), jnp.where(lo, qp1, zero),
                              jnp.where(lo, zero, qp0), jnp.where(lo, zero, qp1)], axis=0)
        s_sc[n] = lax.dot_general(kg, qs, nt, preferred_element_type=F32)

    def stage_softmax(n):
        r, g = probs[n]
        valid = band & (kj_ >= jnp.where(i == 0, WINDOW, 0)) if r == 0 else band
        st = jnp.where(valid, s_sc[n], NEG_BIG)
        heads = (group * g, group * g + 2, group * g + 1, group * g + 3)
        sink = jnp.concatenate(
            [jnp.full((1, WINDOW), sinks_ref[hd] * LOG2E, F32) for hd in heads], axis=1)
        m = jnp.maximum(jnp.max(st, axis=0, keepdims=True), sink)
        p_sc[n] = jnp.exp2(st - m).astype(BF16)
        e_sc[n] = jnp.exp2(sink - m)

    def stage_pv(n):
        r, g = probs[n]
        rows = slice(r * WINDOW, (r + 1) * WINDOW)
        c0 = g * group * HEAD_DIM
        vtg = vtall[g * LANES:(g + 1) * LANES, r * WINDOW:(r + 2) * WINDOW]
        acc = jnp.dot(vtg, p_sc[n], preferred_element_type=F32)
        l = acc[HEAD_DIM:HEAD_DIM + 1] + e_sc[n]
        on = acc[:HEAD_DIM] * (1.0 / l)
        pair0 = jnp.concatenate([on[:, 0:WINDOW], on[:, 2 * WINDOW:3 * WINDOW]], axis=0).T
        pair1 = jnp.concatenate([on[:, WINDOW:2 * WINDOW], on[:, 3 * WINDOW:]], axis=0).T
        ob_sc[rows, c0:c0 + LANES] = (
            pair0 * g_ref[0, rows, c0:c0 + LANES].astype(F32)).astype(BF16)
        ob_sc[rows, c0 + LANES:c0 + 2 * LANES] = (
            pair1 * g_ref[0, rows, c0 + LANES:c0 + 2 * LANES].astype(F32)).astype(BF16)

    for t in range(len(probs) + 2):
        if t < len(probs):
            stage_qk(t)
        if 1 <= t <= len(probs):
            stage_softmax(t - 1)
        if t >= 2:
            stage_pv(t - 2)

    y = (jnp.dot(oa_ref[0], wa_ref[...], preferred_element_type=F32)
         + jnp.dot(ob_sc[...], wb_ref[...], preferred_element_type=F32))
    ms = jnp.mean(y * y, axis=-1, keepdims=True)
    yn = y * lax.rsqrt(ms + RMS_EPS) * gpost_ref[...]
    out_ref[0] = x_ref[0] + gate_ref[0] * yn


def _swa_out_call(sinks, qb, kd, vdt, gb, oa, x, gate, g_post, wa, wb):
    B, S, D = x.shape
    tq = TQ_SWA
    nsub = tq // WINDOW
    cur = lambda b, i: (b, i, 0)
    prev = lambda b, i: (b, jnp.maximum(i * nsub - 1, 0), 0)
    cur_t = lambda b, i: (b, 0, i)
    prev_t = lambda b, i: (b, 0, jnp.maximum(i * nsub - 1, 0))
    const2 = lambda b, i: (0, 0)
    kvw = 2 * SWA_KV_W
    nprob = nsub * SWA_KV_HEADS
    ncol = SWA_Q_HEADS // SWA_KV_HEADS * WINDOW
    return pl.pallas_call(
        _swa_out_kernel,
        grid=(B, S // tq),
        in_specs=[pl.BlockSpec(memory_space=pltpu.SMEM),
                  pl.BlockSpec((1, tq, SWA_W), cur),
                  pl.BlockSpec((1, WINDOW, kvw), prev),
                  pl.BlockSpec((1, tq, kvw), cur),
                  pl.BlockSpec((1, kvw, WINDOW), prev_t),
                  pl.BlockSpec((1, kvw, tq), cur_t),
                  pl.BlockSpec((1, tq, SWA_W), cur),
                  pl.BlockSpec((1, tq, FOX_W), cur),
                  pl.BlockSpec((1, tq, D), cur),
                  pl.BlockSpec((1, 1, D), lambda b, i: (b, 0, 0)),
                  pl.BlockSpec((1, D), const2),
                  pl.BlockSpec((FOX_W, D), const2),
                  pl.BlockSpec((SWA_W, D), const2)],
        out_specs=pl.BlockSpec((1, tq, D), cur),
        out_shape=jax.ShapeDtypeStruct((B, S, D), F32),
        scratch_shapes=[pltpu.VMEM((nprob, 2 * WINDOW, ncol), F32),
                        pltpu.VMEM((nprob, 2 * WINDOW, ncol), BF16),
                        pltpu.VMEM((nprob, 1, ncol), F32),
                        pltpu.VMEM((tq, SWA_W), BF16)],
        compiler_params=pltpu.CompilerParams(
            dimension_semantics=("arbitrary", "arbitrary"), vmem_limit_bytes=VMEM_LIMIT),
        name="swa_out",
    )(sinks, qb, kd, kd, vdt, vdt, gb, oa, x, gate, g_post, wa, wb)


def _perm_w_in(w):
    wt = w.T
    sc = HEAD_DIM ** -0.5
    o = 0
    qa = wt[o:o + FOX_W]; o += FOX_W
    ka = wt[o:o + FOX_W]; o += FOX_W
    va = wt[o:o + FOX_W]; o += FOX_W
    fa = wt[o:o + FOX_HEADS]; o += FOX_HEADS
    za = wt[o:o + FOX_W]; o += FOX_W
    qb = wt[o:o + SWA_W]; o += SWA_W
    kb = wt[o:o + SWA_KV_W]; o += SWA_KV_W
    vb = wt[o:o + SWA_KV_W]; o += SWA_KV_W
    zb = wt[o:o + SWA_W]
    fpad = jnp.concatenate(
        [fa, fa, fa, jnp.zeros((LANES - 3 * FOX_HEADS, wt.shape[1]), w.dtype)], axis=0)
    sc2 = sc * LOG2E
    return jnp.concatenate([qa * sc2, ka, va, za, qb * sc2, zb, kb, vb, fpad],
                           axis=0).astype(BF16)


def _aug_constants():
    eq = np.zeros((LANES, AUG_W), np.float32)
    ek = np.zeros((LANES, AUG_W), np.float32)
    oq = np.zeros((1, AUG_W), np.float32)
    ok = np.zeros((1, AUG_W), np.float32)
    for hd in range(FOX_HEADS):
        base = LANES * hd + (HEAD_DIM if hd % 2 == 0 else 0)
        for part in range(3):
            eq[part * FOX_HEADS + hd, base + part] = 1.0
            ok[0, base + part] = 1.0
            ek[part * FOX_HEADS + hd, base + 3 + part] = -1.0
            oq[0, base + 3 + part] = 1.0
    return jnp.asarray(eq + ek, BF16), jnp.asarray(oq), jnp.asarray(ok)


def kernel(x, c, positions, w_ada, b_ada, g_pre, w_in, b_fgate, sinks, w_out, g_post):
    B, S, D = x.shape
    depth = w_ada.shape[0]
    assert TM_SUB == TQ_FOX == TK_FOX
    half = HEAD_DIM // 2
    inv_freq = ROPE_THETA ** (-jnp.arange(half, dtype=F32) / half)
    invf = inv_freq[:, None]
    pos3 = positions[:, None, :]
    ind = np.zeros((2 * FOX_W, LANES), np.float32)
    ind[np.arange(2 * FOX_W), np.arange(2 * FOX_W) // HEAD_DIM] = 1.0
    ind = jnp.asarray(ind, BF16)
    tri = jnp.asarray(np.tril(np.ones((TM_SUB, TM_SUB), np.float32)), BF16)
    eqk, oq, ok = _aug_constants()
    c_pad = jnp.zeros((8, D), F32).at[:B].set(c)
    for l in range(depth):
        mod = _mod_call(c_pad, w_ada[l], b_ada[l][None, :])[:B]
        shift = mod[:, None, 0:D]
        scale = mod[:, None, D:2 * D]
        gate = mod[:, None, 2 * D:3 * D]
        bf_pad = jnp.concatenate(
            [b_fgate[l]] * 3 + [jnp.zeros((LANES - 3 * FOX_HEADS,), F32)])[None, :]
        qaug, kaug, vaug, ga, qb, kd, vd, gb, stats = _inproj_call(
            x, shift, scale, g_pre[l][None, :], _perm_w_in(w_in[l]), pos3, invf, bf_pad,
            tri, eqk, oq, ok, ind)
        oa = _fox_call(qaug, kaug, vaug, ga, stats[:, :, 0:3, 0:2 * FOX_HEADS])
        wo = w_out[l].astype(BF16)
        x = _swa_out_call(sinks[l], qb, kd, vd, gb, oa, x, gate, g_post[l][None, :],
                          wo[:FOX_W], wo[FOX_W:])
    return x
```

```python
import functools

import jax
import jax.numpy as jnp
import numpy as np
from jax import lax
from jax.experimental import pallas as pl
from jax.experimental.pallas import tpu as pltpu

D_MODEL = 1024
HEAD_DIM = 64
FOX_HEADS = 8
SWA_Q_HEADS = 8
SWA_KV_HEADS = 2
WINDOW = 128
ROPE_THETA = 10000.0
RMS_EPS = 1e-6
FOX_W = FOX_HEADS * HEAD_DIM
SWA_W = SWA_Q_HEADS * HEAD_DIM
SWA_KV_W = SWA_KV_HEADS * HEAD_DIM

LANES = 128
AUG_W = FOX_HEADS * LANES
V_ROWS = HEAD_DIM + 16
V_W = FOX_HEADS * V_ROWS

OFF_QA = 0
OFF_KA = OFF_QA + FOX_W
OFF_VA = OFF_KA + FOX_W
OFF_ZA = OFF_VA + FOX_W
OFF_QB = OFF_ZA + FOX_W
OFF_ZB = OFF_QB + SWA_W
OFF_KD = OFF_ZB + SWA_W
OFF_VD = OFF_KD + SWA_KV_W
OFF_F = OFF_VD + SWA_KV_W
IN_NP = OFF_F + LANES

TM_IN = 1024
TM_SUB = 512
TQ_FOX = 512
TK_FOX = 512
TQ_SWA = 512
NEG_BIG = -1e30
LOG2E = 1.4426950408889634
PRUNE_T = 140.0
NORM_MARGIN = 1.01
FIXED_REF_T = 100.0
VMEM_LIMIT = 56 * 1024 * 1024

F32 = jnp.float32
BF16 = jnp.bfloat16


def _split3(a):
    hi = a.astype(BF16)
    r = a - hi.astype(F32)
    mid = r.astype(BF16)
    lo = (r - mid.astype(F32)).astype(BF16)
    return hi, mid, lo


def _mod_kernel(c_ref, w_ref, b_ref, o_ref):
    c = c_ref[...]
    sc = c * (1.0 / (1.0 + jnp.exp(-c)))
    o_ref[...] = jnp.dot(sc, w_ref[...], precision=lax.Precision.HIGHEST,
                         preferred_element_type=F32) + b_ref[...]


def _mod_call(c_pad, w_ada, b_ada):
    rows = c_pad.shape[0]
    n = w_ada.shape[1]
    bn = D_MODEL
    return pl.pallas_call(
        _mod_kernel,
        grid=(n // bn,),
        in_specs=[pl.BlockSpec((rows, D_MODEL), lambda j: (0, 0)),
                  pl.BlockSpec((D_MODEL, bn), lambda j: (0, j)),
                  pl.BlockSpec((1, bn), lambda j: (0, j))],
        out_specs=pl.BlockSpec((rows, bn), lambda j: (0, j)),
        out_shape=jax.ShapeDtypeStruct((rows, n), F32),
        name="mod",
    )(c_pad, w_ada, b_ada)


def _inproj_kernel(x_ref, shift_ref, scale_ref, gpre_ref, w_ref, pos_ref, invf_ref, bf_ref,
                   tri_ref, eqk_ref, oq_ref, ok_ref, ind_ref,
                   qaug_ref, kaug_ref, vaug_ref, ga_ref, qb_ref, kd_ref, vd_ref, gb_ref, st_ref,
                   carry_ref):
    tm = TM_SUB

    @pl.when(pl.program_id(1) == 0)
    def _():
        carry_ref[...] = jnp.zeros_like(carry_ref)

    for sub in range(x_ref.shape[1] // tm):
        _inproj_subtile(sub, slice(sub * tm, (sub + 1) * tm),
                        x_ref, shift_ref, scale_ref, gpre_ref, w_ref, pos_ref, invf_ref, bf_ref,
                        tri_ref, eqk_ref, oq_ref, ok_ref, ind_ref,
                        qaug_ref, kaug_ref, vaug_ref, ga_ref, qb_ref, kd_ref, vd_ref, gb_ref,
                        st_ref, carry_ref)


def _inproj_subtile(sub, rows, x_ref, shift_ref, scale_ref, gpre_ref, w_ref, pos_ref, invf_ref,
                    bf_ref, tri_ref, eqk_ref, oq_ref, ok_ref, ind_ref,
                    qaug_ref, kaug_ref, vaug_ref, ga_ref, qb_ref, kd_ref, vd_ref, gb_ref,
                    st_ref, carry_ref):
    tm = TM_SUB
    x = x_ref[0, rows, :]
    ms = jnp.mean(x * x, axis=-1, keepdims=True)
    y = x * lax.rsqrt(ms + RMS_EPS) * gpre_ref[...]
    h = (y * (1.0 + scale_ref[0]) + shift_ref[0]).astype(BF16)

    def proj(off, width):
        return lax.dot_general(h, w_ref[off:off + width, :], (((1,), (1,)), ((), ())),
                               preferred_element_type=F32)

    tail = proj(OFF_KD, 3 * LANES)
    f = tail[:, OFF_F - OFF_KD:] + bf_ref[...]
    ls = jnp.minimum(f, 0.0) - jnp.log1p(jnp.exp(-jnp.abs(f)))
    lane = lax.broadcasted_iota(jnp.int32, (tm, LANES), 1)

    def by_group(a, b_, c_):
        return jnp.where(lane < FOX_HEADS, a, jnp.where(lane < 2 * FOX_HEADS, b_, c_))

    part = by_group(*(t.astype(F32) for t in _split3(ls))).astype(BF16)
    psum = jnp.dot(tri_ref[...], part, preferred_element_type=F32)
    cum = (psum + pltpu.roll(psum, LANES - FOX_HEADS, 1)
           + pltpu.roll(psum, LANES - 2 * FOX_HEADS, 1))
    cum = cum + carry_ref[...]
    carry_ref[...] = cum[tm - 1:tm, :]

    cum2 = cum * LOG2E
    cum2 = by_group(cum2, pltpu.roll(cum2, FOX_HEADS, 1), pltpu.roll(cum2, 2 * FOX_HEADS, 1))
    cs = by_group(*(t.astype(F32) for t in _split3(cum2))).astype(BF16)
    placed = jnp.dot(cs, eqk_ref[...], preferred_element_type=F32)
    augq = placed * ok_ref[...] + oq_ref[...]
    augk = placed * oq_ref[...] + ok_ref[...]

    lane_w = lax.broadcasted_iota(jnp.int32, (tm, AUG_W), 1)
    data = (((lane_w >> 6) ^ (lane_w >> 7)) & 1) == 0

    def rep(a):
        return jnp.concatenate(
            [a[:, LANES * (hd // 2):LANES * (hd // 2 + 1)] for hd in range(FOX_HEADS)], axis=1)

    qa = proj(OFF_QA, FOX_W)
    ka = proj(OFF_KA, FOX_W)
    qaug_ref[0, :, rows] = jnp.where(data, rep(qa), augq).T.astype(BF16)
    kaug_ref[0, rows, :] = jnp.where(data, rep(ka), augk).astype(BF16)

    q16 = qa.astype(BF16).astype(F32)
    k16 = ka.astype(BF16).astype(F32)
    sq = jnp.concatenate([q16 * q16, k16 * k16], axis=1).astype(BF16)
    nrm2 = jnp.dot(sq, ind_ref[...], preferred_element_type=F32)
    nmax = jnp.sqrt(jnp.max(nrm2, axis=0, keepdims=True)) * NORM_MARGIN
    st_ref[0, sub] = jnp.concatenate(
        [nmax, cum2[0:1], cum2[tm - 1:tm], jnp.zeros((5, LANES), F32)], axis=0)
    vat = proj(OFF_VA, FOX_W).T
    ones_v = jnp.ones((V_ROWS - HEAD_DIM, tm), F32)
    vaug_ref[0, :, rows] = jnp.concatenate(
        [piece for hd in range(FOX_HEADS)
         for piece in (vat[hd * HEAD_DIM:(hd + 1) * HEAD_DIM], ones_v)], axis=0).astype(BF16)

    za = proj(OFF_ZA, FOX_W)
    ga_ref[0, rows, :] = (za * (1.0 / (1.0 + jnp.exp(-za)))).astype(BF16)
    zb = proj(OFF_ZB, SWA_W)
    gb_ref[0, rows, :] = (zb * (1.0 / (1.0 + jnp.exp(-zb)))).astype(BF16)

    ang_t = invf_ref[...] * pos_ref[0, :, rows].astype(F32)
    reps = LANES // (HEAD_DIM // 2)
    cosv = jnp.concatenate([jnp.cos(ang_t)] * reps, axis=0).T
    sinv = jnp.concatenate([jnp.sin(ang_t)] * reps, axis=0).T
    first = (lane & (HEAD_DIM // 2)) == 0
    sin_signed = jnp.where(first, -sinv, sinv)

    def rope(a):
        outs = []
        for cidx in range(a.shape[1] // LANES):
            blk = a[:, cidx * LANES:(cidx + 1) * LANES]
            other = jnp.where(first, pltpu.roll(blk, LANES - HEAD_DIM // 2, 1),
                              pltpu.roll(blk, HEAD_DIM // 2, 1))
            outs.append(blk * cosv + other * sin_signed)
        return jnp.concatenate(outs, axis=1)

    qb_ref[0, rows, :] = rope(proj(OFF_QB, SWA_W)).astype(BF16)
    kr = rope(tail[:, :LANES])
    ks = pltpu.roll(kr, HEAD_DIM, 1)
    low = lane < HEAD_DIM
    kd_ref[0, rows, :] = jnp.concatenate([jnp.where(low, kr, ks), jnp.where(low, ks, kr)],
                                axis=1).astype(BF16)
    vt = tail[:, OFF_VD - OFF_KD:OFF_F - OFF_KD].T
    ones = jnp.ones((HEAD_DIM, tm), F32)
    vd_ref[0, :, rows] = jnp.concatenate([vt[:HEAD_DIM], ones, vt[HEAD_DIM:], ones],
                                axis=0).astype(BF16)


def _inproj_call(x, shift, scale, g_pre, w_perm, pos3, invf, bf_pad, tri, eqk, oq, ok, ind):
    B, S, D = x.shape
    tm = TM_IN
    row = lambda b, t: (b, t, 0)
    per_b = lambda b, t: (b, 0, 0)
    const2 = lambda b, t: (0, 0)
    out_w = (AUG_W, AUG_W, V_W, FOX_W, SWA_W, 2 * SWA_KV_W, 2 * SWA_KV_W, SWA_W)
    transposed = (0, 2, 6)
    return pl.pallas_call(
        _inproj_kernel,
        grid=(B, S // tm),
        in_specs=[pl.BlockSpec((1, tm, D), row),
                  pl.BlockSpec((1, 1, D), per_b),
                  pl.BlockSpec((1, 1, D), per_b),
                  pl.BlockSpec((1, D), const2),
                  pl.BlockSpec((IN_NP, D), const2),
                  pl.BlockSpec((1, 1, tm), lambda b, t: (b, 0, t)),
                  pl.BlockSpec((HEAD_DIM // 2, 1), const2),
                  pl.BlockSpec((1, LANES), const2),
                  pl.BlockSpec((TM_SUB, TM_SUB), const2),
                  pl.BlockSpec((LANES, AUG_W), const2),
                  pl.BlockSpec((1, AUG_W), const2),
                  pl.BlockSpec((1, AUG_W), const2),
                  pl.BlockSpec((2 * FOX_W, LANES), const2)],
        out_specs=[pl.BlockSpec((1, w, tm), lambda b, t: (b, 0, t)) if i in transposed
                   else pl.BlockSpec((1, tm, w), row) for i, w in enumerate(out_w)]
        + [pl.BlockSpec((1, tm // TM_SUB, 8, LANES), lambda b, t: (b, t, 0, 0))],
        out_shape=[jax.ShapeDtypeStruct((B, w, S) if i in transposed else (B, S, w), BF16)
                   for i, w in enumerate(out_w)]
        + [jax.ShapeDtypeStruct((B, S // TM_SUB, 8, LANES), F32)],
        scratch_shapes=[pltpu.VMEM((1, LANES), F32)],
        compiler_params=pltpu.CompilerParams(
            dimension_semantics=("arbitrary", "arbitrary"), vmem_limit_bytes=VMEM_LIMIT),
        name="in_proj",
    )(x, shift, scale, g_pre, w_perm, pos3, invf, bf_pad, tri, eqk, oq, ok, ind)


def _fox_tables(nq):
    diag = [(qi, qi, hh * nq + qi, hh) for hh in range(2) for qi in range(nq)]
    tab = np.array(diag + [(1, 0, 2 * nq, 0)], np.int32).T
    return tab, len(diag), nq * (nq - 1)


def _fox_kernel(tab_ref, st_ref, q_ref, k_ref, v_ref, g_ref, o_ref,
                off_tab, gen_tab, s0, s1, p0, p1, al0, al1, m_st, acc_st, *, n_diag, n_off):
    tq = TQ_FOX
    tk = TK_FOX
    nq = q_ref.shape[2] // tq
    s_buf = (s0, s1)
    p_buf = (p0, p1)
    al_buf = (al0, al1)

    def run(tab, n_items, dummy, diag):
        half = (n_items + 1) // 2
        lens = (half, n_items - half)
        hk, hq = tk // 2, tq // 2

        def item(stream, t):
            return jnp.where(t < lens[stream], stream * half + t, dummy)

        def stage_qk(t, slot):
            for sm in range(2):
                e = item(sm, t)
                qoff = pl.multiple_of(tab[0, e] * tq, tq)
                koff = pl.multiple_of(tab[1, e] * tk, tk)
                hoff = pl.multiple_of(tab[3, e] * LANES, LANES)
                qt = q_ref[0, pl.ds(hoff, LANES), pl.ds(qoff, tq)]
                k = k_ref[0, pl.ds(koff, tk), pl.ds(hoff, LANES)]
                if diag:
                    s_buf[slot][sm, :hk, :] = jnp.dot(k[:hk], qt, preferred_element_type=F32)
                    s_buf[slot][sm, hk:, hq:] = jnp.dot(k[hk:], qt[:, hq:],
                                                        preferred_element_type=F32)
                else:
                    s_buf[slot][sm] = jnp.dot(k, qt, preferred_element_type=F32)

        def stage_softmax(t, slot):
            for sm in range(2):
                st = tab[2, item(sm, t)]
                if diag:
                    kr = lax.broadcasted_iota(jnp.int32, (hk, tq), 0)
                    qc = lax.broadcasted_iota(jnp.int32, (hk, tq), 1)
                    top = jnp.where(kr <= qc, s_buf[slot][sm, :hk, :], NEG_BIG)
                    kr2 = lax.broadcasted_iota(jnp.int32, (tk - hk, tq - hq), 0)
                    qc2 = lax.broadcasted_iota(jnp.int32, (tk - hk, tq - hq), 1)
                    low = jnp.where(kr2 <= qc2, s_buf[slot][sm, hk:, hq:], NEG_BIG)
                    top_a, top_b = top[:, :hq], top[:, hq:]
                    m_a = jnp.max(top_a, axis=0, keepdims=True)
                    m_b = jnp.maximum(jnp.max(top_b, axis=0, keepdims=True),
                                      jnp.max(low, axis=0, keepdims=True))
                    m_next = jnp.concatenate([m_a, m_b], axis=1)
                    p_buf[slot][sm, :hk, :hq] = jnp.exp2(top_a - m_a).astype(BF16)
                    p_buf[slot][sm, :hk, hq:] = jnp.exp2(top_b - m_b).astype(BF16)
                    p_buf[slot][sm, hk:, hq:] = jnp.exp2(low - m_b).astype(BF16)
                else:
                    s = s_buf[slot][sm]
                    m_prev = m_st[st]
                    m_next = jnp.maximum(m_prev, jnp.max(s, axis=0, keepdims=True))
                    al_buf[slot][sm] = jnp.exp2(m_prev - m_next)
                    p_buf[slot][sm] = jnp.exp2(s - m_next).astype(BF16)
                m_st[st] = m_next

        def stage_pv(t, slot):
            for sm in range(2):
                e = item(sm, t)
                koff = pl.multiple_of(tab[1, e] * tk, tk)
                voff = pl.multiple_of(tab[3, e] * V_ROWS, 16)
                st = tab[2, e]
                vt = v_ref[0, pl.ds(voff, V_ROWS), pl.ds(koff, tk)]
                if diag:
                    acc_st[st] = jnp.concatenate(
                        [jnp.dot(vt[:, :hk], p_buf[slot][sm, :hk, :hq],
                                 preferred_element_type=F32),
                         jnp.dot(vt, p_buf[slot][sm, :, hq:], preferred_element_type=F32)],
                        axis=1)
                else:
                    pv = jnp.dot(vt, p_buf[slot][sm], preferred_element_type=F32)
                    acc_st[st] = al_buf[slot][sm] * acc_st[st] + pv

        stage_qk(0, 0)
        stage_qk(1, 1)
        stage_softmax(0, 0)

        def body(u, carry):
            for d in range(2):
                t = 2 * u + 1 + d
                stage_qk(t + 1, d % 2)
                stage_softmax(t, (d + 1) % 2)
                stage_pv(t - 1, d % 2)
            return carry

        lax.fori_loop(0, (half + 1) // 2, body, 0)

    def run_fixed(tab, n_items, dummy):
        half = (n_items + 1) // 2
        lens = (half, n_items - half)

        def item(stream, t):
            return jnp.where(t < lens[stream], stream * half + t, dummy)

        def stage_probs(t, slot):
            for sm in range(2):
                e = item(sm, t)
                qoff = pl.multiple_of(tab[0, e] * tq, tq)
                koff = pl.multiple_of(tab[1, e] * tk, tk)
                hoff = pl.multiple_of(tab[3, e] * LANES, LANES)
                qt = q_ref[0, pl.ds(hoff, LANES), pl.ds(qoff, tq)]
                k = k_ref[0, pl.ds(koff, tk), pl.ds(hoff, LANES)]
                s = jnp.dot(k, qt, preferred_element_type=F32)
                p_buf[slot][sm] = jnp.exp2(s - m_st[tab[2, e]]).astype(BF16)

        def stage_pv(t, slot):
            for sm in range(2):
                e = item(sm, t)
                koff = pl.multiple_of(tab[1, e] * tk, tk)
                voff = pl.multiple_of(tab[3, e] * V_ROWS, 16)
                st = tab[2, e]
                vt = v_ref[0, pl.ds(voff, V_ROWS), pl.ds(koff, tk)]
                acc_st[st] = acc_st[st] + jnp.dot(vt, p_buf[slot][sm],
                                                  preferred_element_type=F32)

        stage_probs(0, 0)

        def body(u, carry):
            for d in range(2):
                t = 2 * u + 1 + d
                stage_probs(t, (d + 1) % 2)
                stage_pv(t - 1, d % 2)
            return carry

        lax.fori_loop(0, (half + 1) // 2, body, 0)

    assert tab_ref.shape[1] == n_diag + 1
    run(tab_ref, n_diag, n_diag, True)

    b = pl.program_id(0)
    pr = pl.program_id(1)

    def put(tab, idx, qrow, j, state, head):
        tab[0, idx] = qrow
        tab[1, idx] = j
        tab[2, idx] = state
        tab[3, idx] = head

    n_fix = jnp.int32(0)
    n_gen = jnp.int32(0)
    for hh in range(2):
        hd = 2 * pr + hh

        kmax = lax.fori_loop(
            0, nq, lambda j, m, hd=hd: jnp.maximum(m, st_ref[b, j, 0, FOX_HEADS + hd]),
            jnp.float32(0.0))

        def list_row(qi, counts, hh=hh, hd=hd, kmax=kmax):
            base = (st_ref[b, qi, 0, hd] * (st_ref[b, qi, 0, FOX_HEADS + hd] + kmax)
                    + st_ref[b, qi, 1, hd])
            fixed_ok = base - st_ref[b, qi - 1, 2, hd] <= FIXED_REF_T

            def scan(tab, enabled, cnt):
                def contributes(carry):
                    j, _ = carry
                    bound = base - st_ref[b, jnp.maximum(j, 0), 2, hd]
                    return jnp.logical_and(jnp.logical_and(enabled, j >= 0), bound > -PRUNE_T)

                def take(carry):
                    j, cnt = carry
                    put(tab, cnt, qi, j, hh * nq + qi, hh)
                    return j - 1, cnt + 1

                return lax.while_loop(contributes, take, (qi - 1, cnt))[1]

            return (scan(off_tab, fixed_ok, counts[0]),
                    scan(gen_tab, jnp.logical_not(fixed_ok), counts[1]))

        n_fix, n_gen = lax.fori_loop(1, nq, list_row, (n_fix, n_gen))
    put(off_tab, n_off, 1, 0, 2 * nq, 0)
    put(gen_tab, n_off, 1, 0, 2 * nq, 0)
    run_fixed(off_tab, n_fix, n_off)

    @pl.when(n_gen > 0)
    def _():
        run(gen_tab, n_gen, n_off, False)

    def finish(qi, carry):
        a0 = acc_st[qi]
        a1 = acc_st[nq + qi]
        ot = jnp.concatenate([a0[:HEAD_DIM] / a0[HEAD_DIM:HEAD_DIM + 1],
                              a1[:HEAD_DIM] / a1[HEAD_DIM:HEAD_DIM + 1]], axis=0)
        rows = pl.ds(pl.multiple_of(qi * tq, tq), tq)
        o_ref[0, rows, :] = (ot.T * g_ref[0, rows, :].astype(F32)).astype(BF16)
        return carry

    lax.fori_loop(0, nq, finish, 0)


def _fox_call(qaug_t, kaug, vaug_t, ga, stats):
    B, S, _ = kaug.shape
    tq, tk = TQ_FOX, TK_FOX
    assert tq == tk
    nq = S // tq
    tab, n_diag, n_off = _fox_tables(nq)
    pair = lambda b, p: (b, 0, p)
    return pl.pallas_call(
        functools.partial(_fox_kernel, n_diag=n_diag, n_off=n_off),
        grid=(B, FOX_HEADS // 2),
        in_specs=[pl.BlockSpec(memory_space=pltpu.SMEM),
                  pl.BlockSpec(memory_space=pltpu.SMEM),
                  pl.BlockSpec((1, 2 * LANES, S), lambda b, p: (b, p, 0)),
                  pl.BlockSpec((1, S, 2 * LANES), pair),
                  pl.BlockSpec((1, 2 * V_ROWS, S), lambda b, p: (b, p, 0)),
                  pl.BlockSpec((1, S, LANES), pair)],
        out_specs=pl.BlockSpec((1, S, LANES), pair),
        out_shape=jax.ShapeDtypeStruct((B, S, FOX_W), BF16),
        scratch_shapes=[pltpu.SMEM((4, n_off + 1), jnp.int32),
                        pltpu.SMEM((4, n_off + 1), jnp.int32),
                        pltpu.VMEM((2, tk, tq), F32), pltpu.VMEM((2, tk, tq), F32),
                        pltpu.VMEM((2, tk, tq), BF16), pltpu.VMEM((2, tk, tq), BF16),
                        pltpu.VMEM((2, 1, tq), F32), pltpu.VMEM((2, 1, tq), F32),
                        pltpu.VMEM((2 * nq + 1, 1, tq), F32),
                        pltpu.VMEM((2 * nq + 1, V_ROWS, tq), F32)],
        compiler_params=pltpu.CompilerParams(
            dimension_semantics=("arbitrary", "arbitrary"), vmem_limit_bytes=VMEM_LIMIT),
        name="fox",
    )(jnp.asarray(tab), stats, qaug_t, kaug, vaug_t, ga)


def _swa_out_kernel(sinks_ref, q_ref, kp_ref, kc_ref, vtp_ref, vtc_ref, g_ref,
                    oa_ref, x_ref, gate_ref, gpost_ref, wa_ref, wb_ref, out_ref,
                    s_sc, p_sc, e_sc, ob_sc):
    i = pl.program_id(1)
    nsub = TQ_SWA // WINDOW
    group = SWA_Q_HEADS // SWA_KV_HEADS
    ncol = group * WINDOW
    kall = jnp.concatenate([kp_ref[0], kc_ref[0]], axis=0)
    vtall = jnp.concatenate([vtp_ref[0], vtc_ref[0]], axis=1)
    lane = lax.broadcasted_iota(jnp.int32, (WINDOW, LANES), 1)
    lo = lane < HEAD_DIM
    zero = jnp.zeros((WINDOW, LANES), BF16)
    kj_ = lax.broadcasted_iota(jnp.int32, (2 * WINDOW, ncol), 0)
    qi_ = lax.broadcasted_iota(jnp.int32, (2 * WINDOW, ncol), 1) & (WINDOW - 1)
    rel = qi_ + WINDOW - kj_
    band = (rel >= 0) & (rel < WINDOW)
    nt = (((1,), (1,)), ((), ()))
    probs = [(r, g) for r in range(nsub) for g in range(SWA_KV_HEADS)]

    def stage_qk(n):
        r, g = probs[n]
        rows = slice(r * WINDOW, (r + 1) * WINDOW)
        kg = kall[r * WINDOW:(r + 2) * WINDOW, g * LANES:(g + 1) * LANES]
        c0 = g * group * HEAD_DIM
        qp0 = q_ref[0, rows, c0:c0 + LANES]
        qp1 = q_ref[0, rows, c0 + LANES:c0 + 2 * LANES]
        qs = jnp.concatenate([jnp.where(lo, qp0, zero), jnp.where(lo, qp1, zero),
                              jnp.where(lo, zero, qp0), jnp.where(lo, zero, qp1)], axis=0)
        s_sc[n] = lax.dot_general(kg, qs, nt, preferred_element_type=F32)

    def stage_softmax(n):
        r, g = probs[n]
        valid = band & (kj_ >= jnp.where(i == 0, WINDOW, 0)) if r == 0 else band
        st = jnp.where(valid, s_sc[n], NEG_BIG)
        heads = (group * g, group * g + 2, group * g + 1, group * g + 3)
        sink = jnp.concatenate(
            [jnp.full((1, WINDOW), sinks_ref[hd] * LOG2E, F32) for hd in heads], axis=1)
        m = jnp.maximum(jnp.max(st, axis=0, keepdims=True), sink)
        p_sc[n] = jnp.exp2(st - m).astype(BF16)
        e_sc[n] = jnp.exp2(sink - m)

    def stage_pv(n):
        r, g = probs[n]
        rows = slice(r * WINDOW, (r + 1) * WINDOW)
        c0 = g * group * HEAD_DIM
        vtg = vtall[g * LANES:(g + 1) * LANES, r * WINDOW:(r + 2) * WINDOW]
        acc = jnp.dot(vtg, p_sc[n], preferred_element_type=F32)
        l = acc[HEAD_DIM:HEAD_DIM + 1] + e_sc[n]
        on = acc[:HEAD_DIM] * (1.0 / l)
        pair0 = jnp.concatenate([on[:, 0:WINDOW], on[:, 2 * WINDOW:3 * WINDOW]], axis=0).T
        pair1 = jnp.concatenate([on[:, WINDOW:2 * WINDOW], on[:, 3 * WINDOW:]], axis=0).T
        ob_sc[rows, c0:c0 + LANES] = (
            pair0 * g_ref[0, rows, c0:c0 + LANES].astype(F32)).astype(BF16)
        ob_sc[rows, c0 + LANES:c0 + 2 * LANES] = (
            pair1 * g_ref[0, rows, c0 + LANES:c0 + 2 * LANES].astype(F32)).astype(BF16)

    for t in range(len(probs) + 2):
        if t < len(probs):
            stage_qk(t)
        if 1 <= t <= len(probs):
            stage_softmax(t - 1)
        if t >= 2:
            stage_pv(t - 2)

    y = (jnp.dot(oa_ref[0], wa_ref[...], preferred_element_type=F32)
         + jnp.dot(ob_sc[...], wb_ref[...], preferred_element_type=F32))
    ms = jnp.mean(y * y, axis=-1, keepdims=True)
    yn = y * lax.rsqrt(ms + RMS_EPS) * gpost_ref[...]
    out_ref[0] = x_ref[0] + gate_ref[0] * yn


def _swa_out_call(sinks, qb, kd, vdt, gb, oa, x, gate, g_post, wa, wb):
    B, S, D = x.shape
    tq = TQ_SWA
    nsub = tq // WINDOW
    cur = lambda b, i: (b, i, 0)
    prev = lambda b, i: (b, jnp.maximum(i * nsub - 1, 0), 0)
    cur_t = lambda b, i: (b, 0, i)
    prev_t = lambda b, i: (b, 0, jnp.maximum(i * nsub - 1, 0))
    const2 = lambda b, i: (0, 0)
    kvw = 2 * SWA_KV_W
    nprob = nsub * SWA_KV_HEADS
    ncol = SWA_Q_HEADS // SWA_KV_HEADS * WINDOW
    return pl.pallas_call(
        _swa_out_kernel,
        grid=(B, S // tq),
        in_specs=[pl.BlockSpec(memory_space=pltpu.SMEM),
                  pl.BlockSpec((1, tq, SWA_W), cur),
                  pl.BlockSpec((1, WINDOW, kvw), prev),
                  pl.BlockSpec((1, tq, kvw), cur),
                  pl.BlockSpec((1, kvw, WINDOW), prev_t),
                  pl.BlockSpec((1, kvw, tq), cur_t),
                  pl.BlockSpec((1, tq, SWA_W), cur),
                  pl.BlockSpec((1, tq, FOX_W), cur),
                  pl.BlockSpec((1, tq, D), cur),
                  pl.BlockSpec((1, 1, D), lambda b, i: (b, 0, 0)),
                  pl.BlockSpec((1, D), const2),
                  pl.BlockSpec((FOX_W, D), const2),
                  pl.BlockSpec((SWA_W, D), const2)],
        out_specs=pl.BlockSpec((1, tq, D), cur),
        out_shape=jax.ShapeDtypeStruct((B, S, D), F32),
        scratch_shapes=[pltpu.VMEM((nprob, 2 * WINDOW, ncol), F32),
                        pltpu.VMEM((nprob, 2 * WINDOW, ncol), BF16),
                        pltpu.VMEM((nprob, 1, ncol), F32),
                        pltpu.VMEM((tq, SWA_W), BF16)],
        compiler_params=pltpu.CompilerParams(
            dimension_semantics=("arbitrary", "arbitrary"), vmem_limit_bytes=VMEM_LIMIT),
        name="swa_out",
    )(sinks, qb, kd, kd, vdt, vdt, gb, oa, x, gate, g_post, wa, wb)


def _perm_w_in(w):
    wt = w.T
    sc = HEAD_DIM ** -0.5
    o = 0
    qa = wt[o:o + FOX_W]; o += FOX_W
    ka = wt[o:o + FOX_W]; o += FOX_W
    va = wt[o:o + FOX_W]; o += FOX_W
    fa = wt[o:o + FOX_HEADS]; o += FOX_HEADS
    za = wt[o:o + FOX_W]; o += FOX_W
    qb = wt[o:o + SWA_W]; o += SWA_W
    kb = wt[o:o + SWA_KV_W]; o += SWA_KV_W
    vb = wt[o:o + SWA_KV_W]; o += SWA_KV_W
    zb = wt[o:o + SWA_W]
    fpad = jnp.concatenate(
        [fa, fa, fa, jnp.zeros((LANES - 3 * FOX_HEADS, wt.shape[1]), w.dtype)], axis=0)
    sc2 = sc * LOG2E
    return jnp.concatenate([qa * sc2, ka, va, za, qb * sc2, zb, kb, vb, fpad],
                           axis=0).astype(BF16)


def _aug_constants():
    eq = np.zeros((LANES, AUG_W), np.float32)
    ek = np.zeros((LANES, AUG_W), np.float32)
    oq = np.zeros((1, AUG_W), np.float32)
    ok = np.zeros((1, AUG_W), np.float32)
    for hd in range(FOX_HEADS):
        base = LANES * hd + (HEAD_DIM if hd % 2 == 0 else 0)
        for part in range(3):
            eq[part * FOX_HEADS + hd, base + part] = 1.0
            ok[0, base + part] = 1.0
            ek[part * FOX_HEADS + hd, base + 3 + part] = -1.0
            oq[0, base + 3 + part] = 1.0
    return jnp.asarray(eq + ek, BF16), jnp.asarray(oq), jnp.asarray(ok)


def kernel(x, c, positions, w_ada, b_ada, g_pre, w_in, b_fgate, sinks, w_out, g_post):
    B, S, D = x.shape
    depth = w_ada.shape[0]
    assert TM_SUB == TQ_FOX == TK_FOX
    half = HEAD_DIM // 2
    inv_freq = ROPE_THETA ** (-jnp.arange(half, dtype=F32) / half)
    invf = inv_freq[:, None]
    pos3 = positions[:, None, :]
    ind = np.zeros((2 * FOX_W, LANES), np.float32)
    ind[np.arange(2 * FOX_W), np.arange(2 * FOX_W) // HEAD_DIM] = 1.0
    ind = jnp.asarray(ind, BF16)
    tri = jnp.asarray(np.tril(np.ones((TM_SUB, TM_SUB), np.float32)), BF16)
    eqk, oq, ok = _aug_constants()
    c_pad = jnp.zeros((8, D), F32).at[:B].set(c)
    for l in range(depth):
        mod = _mod_call(c_pad, w_ada[l], b_ada[l][None, :])[:B]
        shift = mod[:, None, 0:D]
        scale = mod[:, None, D:2 * D]
        gate = mod[:, None, 2 * D:3 * D]
        bf_pad = jnp.concatenate(
            [b_fgate[l]] * 3 + [jnp.zeros((LANES - 3 * FOX_HEADS,), F32)])[None, :]
        qaug, kaug, vaug, ga, qb, kd, vd, gb, stats = _inproj_call(
            x, shift, scale, g_pre[l][None, :], _perm_w_in(w_in[l]), pos3, invf, bf_pad,
            tri, eqk, oq, ok, ind)
        oa = _fox_call(qaug, kaug, vaug, ga, stats[:, :, 0:3, 0:2 * FOX_HEADS])
        wo = w_out[l].astype(BF16)
        x = _swa_out_call(sinks[l], qb, kd, vd, gb, oa, x, gate, g_post[l][None, :],
                          wo[:FOX_W], wo[FOX_W:])
    return x
```

```python
import functools

import jax
import jax.numpy as jnp
import numpy as np
from jax import lax
from jax.experimental import pallas as pl
from jax.experimental.pallas import tpu as pltpu

D_MODEL = 1024
HEAD_DIM = 64
FOX_HEADS = 8
SWA_Q_HEADS = 8
SWA_KV_HEADS = 2
WINDOW = 128
ROPE_THETA = 10000.0
RMS_EPS = 1e-6
FOX_W = FOX_HEADS * HEAD_DIM
SWA_W = SWA_Q_HEADS * HEAD_DIM
SWA_KV_W = SWA_KV_HEADS * HEAD_DIM

LANES = 128
AUG_W = FOX_HEADS * LANES

OFF_QA = 0
OFF_KA = OFF_QA + FOX_W
OFF_VA = OFF_KA + FOX_W
OFF_ZA = OFF_VA + FOX_W
OFF_QB = OFF_ZA + FOX_W
OFF_ZB = OFF_QB + SWA_W
OFF_KD = OFF_ZB + SWA_W
OFF_VD = OFF_KD + SWA_KV_W
OFF_F = OFF_VD + SWA_KV_W
IN_NP = OFF_F + LANES

TM_IN = 1024
TM_SUB = 512
TQ_FOX = 512
TK_FOX = 512
TQ_SWA = 1024
NEG_BIG = -1e30
LOG2E = 1.4426950408889634
PRUNE_T = 140.0
NORM_MARGIN = 1.01
FIXED_REF_T = 100.0
VMEM_LIMIT = 56 * 1024 * 1024

F32 = jnp.float32
BF16 = jnp.bfloat16


def _split3(a):
    hi = a.astype(BF16)
    r = a - hi.astype(F32)
    mid = r.astype(BF16)
    lo = (r - mid.astype(F32)).astype(BF16)
    return hi, mid, lo


def _silu(z):
    hz = 0.5 * z
    return hz + hz * jnp.tanh(hz)


def _mod_kernel(c_ref, w_ref, b_ref, o_ref):
    c = c_ref[...]
    sc = c * (1.0 / (1.0 + jnp.exp(-c)))
    o_ref[...] = jnp.dot(sc, w_ref[...], precision=lax.Precision.HIGHEST,
                         preferred_element_type=F32) + b_ref[...]


def _mod_call(c_pad, w_ada, b_ada):
    rows = c_pad.shape[0]
    n = w_ada.shape[1]
    bn = D_MODEL
    return pl.pallas_call(
        _mod_kernel,
        grid=(n // bn,),
        in_specs=[pl.BlockSpec((rows, D_MODEL), lambda j: (0, 0)),
                  pl.BlockSpec((D_MODEL, bn), lambda j: (0, j)),
                  pl.BlockSpec((1, bn), lambda j: (0, j))],
        out_specs=pl.BlockSpec((rows, bn), lambda j: (0, j)),
        out_shape=jax.ShapeDtypeStruct((rows, n), F32),
        name="mod",
    )(c_pad, w_ada, b_ada)


def _inproj_kernel(x_ref, shift_ref, scale_ref, gpre_ref, w_ref, pos_ref, invf_ref, bf_ref,
                   tri_ref, eqk_ref, oq_ref, ok_ref, ind_ref,
                   qaug_ref, kaug_ref, vaug_ref, ga_ref, qb_ref, kd_ref, vd_ref, gb_ref, st_ref,
                   carry_ref):
    tm = TM_SUB

    @pl.when(pl.program_id(1) == 0)
    def _():
        carry_ref[...] = jnp.zeros_like(carry_ref)

    for sub in range(x_ref.shape[1] // tm):
        _inproj_subtile(sub, slice(sub * tm, (sub + 1) * tm),
                        x_ref, shift_ref, scale_ref, gpre_ref, w_ref, pos_ref, invf_ref, bf_ref,
                        tri_ref, eqk_ref, oq_ref, ok_ref, ind_ref,
                        qaug_ref, kaug_ref, vaug_ref, ga_ref, qb_ref, kd_ref, vd_ref, gb_ref,
                        st_ref, carry_ref)


def _inproj_subtile(sub, rows, x_ref, shift_ref, scale_ref, gpre_ref, w_ref, pos_ref, invf_ref,
                    bf_ref, tri_ref, eqk_ref, oq_ref, ok_ref, ind_ref,
                    qaug_ref, kaug_ref, vaug_ref, ga_ref, qb_ref, kd_ref, vd_ref, gb_ref,
                    st_ref, carry_ref):
    tm = TM_SUB
    x = x_ref[0, rows, :]
    ms = jnp.mean(x * x, axis=-1, keepdims=True)
    gain = gpre_ref[...] * (1.0 + scale_ref[0])
    h = (x * lax.rsqrt(ms + RMS_EPS) * gain + shift_ref[0]).astype(BF16)

    def proj(off, width):
        return lax.dot_general(h, w_ref[off:off + width, :], (((1,), (1,)), ((), ())),
                               preferred_element_type=F32)

    tail = proj(OFF_KD, 3 * LANES)
    f = tail[:, OFF_F - OFF_KD:] + bf_ref[...]
    ls = jnp.minimum(f, 0.0) - jnp.log1p(jnp.exp(-jnp.abs(f)))
    lane = lax.broadcasted_iota(jnp.int32, (tm, LANES), 1)

    def by_group(a, b_, c_):
        return jnp.where(lane < FOX_HEADS, a, jnp.where(lane < 2 * FOX_HEADS, b_, c_))

    part = by_group(*(t.astype(F32) for t in _split3(ls))).astype(BF16)
    psum = jnp.dot(tri_ref[...], part, preferred_element_type=F32)
    cum = (psum + pltpu.roll(psum, LANES - FOX_HEADS, 1)
           + pltpu.roll(psum, LANES - 2 * FOX_HEADS, 1))
    cum = cum + carry_ref[...]
    carry_ref[...] = cum[tm - 1:tm, :]

    cum2 = cum * LOG2E
    cum2 = by_group(cum2, pltpu.roll(cum2, FOX_HEADS, 1), pltpu.roll(cum2, 2 * FOX_HEADS, 1))
    cs = by_group(*(t.astype(F32) for t in _split3(cum2))).astype(BF16)
    placed = jnp.dot(cs, eqk_ref[...], preferred_element_type=F32)
    augq = placed * ok_ref[...] + oq_ref[...]
    augk = placed * oq_ref[...] + ok_ref[...]

    lane_w = lax.broadcasted_iota(jnp.int32, (tm, AUG_W), 1)
    data = (((lane_w >> 6) ^ (lane_w >> 7)) & 1) == 0

    def rep(a):
        return jnp.concatenate(
            [a[:, LANES * (hd // 2):LANES * (hd // 2 + 1)] for hd in range(FOX_HEADS)], axis=1)

    qa = proj(OFF_QA, FOX_W)
    ka = proj(OFF_KA, FOX_W)
    qaug_ref[0, :, rows] = jnp.where(data, rep(qa), augq).T.astype(BF16)
    kaug_ref[0, rows, :] = jnp.where(data, rep(ka), augk).astype(BF16)

    sq = jnp.concatenate([qa * qa, ka * ka], axis=1).astype(BF16)
    nrm2 = jnp.dot(sq, ind_ref[...], preferred_element_type=F32)
    nmax = jnp.sqrt(jnp.max(nrm2, axis=0, keepdims=True)) * NORM_MARGIN
    st_ref[0, sub] = jnp.concatenate(
        [nmax, cum2[0:1], cum2[tm - 1:tm], jnp.zeros((5, LANES), F32)], axis=0)
    vaug_ref[0, :, rows] = jnp.where(data, rep(proj(OFF_VA, FOX_W)), 1.0).T.astype(BF16)

    za = proj(OFF_ZA, FOX_W)
    ga_ref[0, rows, :] = _silu(za).astype(BF16)
    zb = proj(OFF_ZB, SWA_W)
    gb_ref[0, rows, :] = _silu(zb).astype(BF16)

    ang_t = invf_ref[...] * pos_ref[0, :, rows].astype(F32)
    reps = LANES // (HEAD_DIM // 2)
    cosv = jnp.concatenate([jnp.cos(ang_t)] * reps, axis=0).T
    sinv = jnp.concatenate([jnp.sin(ang_t)] * reps, axis=0).T
    first = (lane & (HEAD_DIM // 2)) == 0
    sin_signed = jnp.where(first, -sinv, sinv)

    def rope(a):
        outs = []
        for cidx in range(a.shape[1] // LANES):
            blk = a[:, cidx * LANES:(cidx + 1) * LANES]
            other = jnp.where(first, pltpu.roll(blk, LANES - HEAD_DIM // 2, 1),
                              pltpu.roll(blk, HEAD_DIM // 2, 1))
            outs.append(blk * cosv + other * sin_signed)
        return jnp.concatenate(outs, axis=1)

    qb_ref[0, rows, :] = rope(proj(OFF_QB, SWA_W)).astype(BF16)
    kr = rope(tail[:, :LANES])
    ks = pltpu.roll(kr, HEAD_DIM, 1)
    low = lane < HEAD_DIM
    kd_ref[0, rows, :] = jnp.concatenate([jnp.where(low, kr, ks), jnp.where(low, ks, kr)],
                                axis=1).astype(BF16)
    vt = tail[:, OFF_VD - OFF_KD:OFF_F - OFF_KD].T
    ones = jnp.ones((HEAD_DIM, tm), F32)
    vd_ref[0, :, rows] = jnp.concatenate([vt[:HEAD_DIM], ones, vt[HEAD_DIM:], ones],
                                axis=0).astype(BF16)


def _inproj_call(x, shift, scale, g_pre, w_perm, pos3, invf, bf_pad, tri, eqk, oq, ok, ind):
    B, S, D = x.shape
    tm = TM_IN
    row = lambda b, t: (b, t, 0)
    per_b = lambda b, t: (b, 0, 0)
    const2 = lambda b, t: (0, 0)
    out_w = (AUG_W, AUG_W, AUG_W, FOX_W, SWA_W, 2 * SWA_KV_W, 2 * SWA_KV_W, SWA_W)
    transposed = (0, 2, 6)
    return pl.pallas_call(
        _inproj_kernel,
        grid=(B, S // tm),
        in_specs=[pl.BlockSpec((1, tm, D), row),
                  pl.BlockSpec((1, 1, D), per_b),
                  pl.BlockSpec((1, 1, D), per_b),
                  pl.BlockSpec((1, D), const2),
                  pl.BlockSpec((IN_NP, D), const2),
                  pl.BlockSpec((1, 1, tm), lambda b, t: (b, 0, t)),
                  pl.BlockSpec((HEAD_DIM // 2, 1), const2),
                  pl.BlockSpec((1, LANES), const2),
                  pl.BlockSpec((TM_SUB, TM_SUB), const2),
                  pl.BlockSpec((LANES, AUG_W), const2),
                  pl.BlockSpec((1, AUG_W), const2),
                  pl.BlockSpec((1, AUG_W), const2),
                  pl.BlockSpec((2 * FOX_W, LANES), const2)],
        out_specs=[pl.BlockSpec((1, w, tm), lambda b, t: (b, 0, t)) if i in transposed
                   else pl.BlockSpec((1, tm, w), row) for i, w in enumerate(out_w)]
        + [pl.BlockSpec((1, tm // TM_SUB, 8, LANES), lambda b, t: (b, t, 0, 0))],
        out_shape=[jax.ShapeDtypeStruct((B, w, S) if i in transposed else (B, S, w), BF16)
                   for i, w in enumerate(out_w)]
        + [jax.ShapeDtypeStruct((B, S // TM_SUB, 8, LANES), F32)],
        scratch_shapes=[pltpu.VMEM((1, LANES), F32)],
        compiler_params=pltpu.CompilerParams(
            dimension_semantics=("arbitrary", "arbitrary"), vmem_limit_bytes=VMEM_LIMIT),
        name="in_proj",
    )(x, shift, scale, g_pre, w_perm, pos3, invf, bf_pad, tri, eqk, oq, ok, ind)


def _fox_tables(nq):
    diag = [(qi, qi, hh * nq + qi, hh) for hh in range(2) for qi in range(nq)]
    tab = np.array(diag + [(1, 0, 2 * nq, 0)], np.int32).T
    return tab, len(diag), nq * (nq - 1)


def _fox_kernel(tab_ref, st_ref, q_ref, k_ref, v_ref, g_ref, o_ref,
                off_tab, gen_tab, s0, s1, p0, p1, al0, al1, m_st, acc_st, *, n_diag, n_off):
    tq = TQ_FOX
    tk = TK_FOX
    nq = q_ref.shape[2] // tq
    s_buf = (s0, s1)
    p_buf = (p0, p1)
    al_buf = (al0, al1)

    def run(tab, n_items, dummy, diag):
        half = (n_items + 1) // 2
        lens = (half, n_items - half)
        hk, hq = tk // 2, tq // 2

        def item(stream, t):
            return jnp.where(t < lens[stream], stream * half + t, dummy)

        def stage_qk(t, slot):
            for sm in range(2):
                e = item(sm, t)
                qoff = pl.multiple_of(tab[0, e] * tq, tq)
                koff = pl.multiple_of(tab[1, e] * tk, tk)
                hoff = pl.multiple_of(tab[3, e] * LANES, LANES)
                qt = q_ref[0, pl.ds(hoff, LANES), pl.ds(qoff, tq)]
                k = k_ref[0, pl.ds(koff, tk), pl.ds(hoff, LANES)]
                if diag:
                    s_buf[slot][sm, :hk, :] = jnp.dot(k[:hk], qt, preferred_element_type=F32)
                    s_buf[slot][sm, hk:, hq:] = jnp.dot(k[hk:], qt[:, hq:],
                                                        preferred_element_type=F32)
                else:
                    s_buf[slot][sm] = jnp.dot(k, qt, preferred_element_type=F32)

        def stage_softmax(t, slot):
            for sm in range(2):
                st = tab[2, item(sm, t)]
                if diag:
                    kr = lax.broadcasted_iota(jnp.int32, (hk, tq), 0)
                    qc = lax.broadcasted_iota(jnp.int32, (hk, tq), 1)
                    top = jnp.where(kr <= qc, s_buf[slot][sm, :hk, :], NEG_BIG)
                    kr2 = lax.broadcasted_iota(jnp.int32, (tk - hk, tq - hq), 0)
                    qc2 = lax.broadcasted_iota(jnp.int32, (tk - hk, tq - hq), 1)
                    low = jnp.where(kr2 <= qc2, s_buf[slot][sm, hk:, hq:], NEG_BIG)
                    top_a, top_b = top[:, :hq], top[:, hq:]
                    m_a = jnp.max(top_a, axis=0, keepdims=True)
                    m_b = jnp.maximum(jnp.max(top_b, axis=0, keepdims=True),
                                      jnp.max(low, axis=0, keepdims=True))
                    m_next = jnp.concatenate([m_a, m_b], axis=1)
                    p_buf[slot][sm, :hk, :hq] = jnp.exp2(top_a - m_a).astype(BF16)
                    p_buf[slot][sm, :hk, hq:] = jnp.exp2(top_b - m_b).astype(BF16)
                    p_buf[slot][sm, hk:, hq:] = jnp.exp2(low - m_b).astype(BF16)
                else:
                    s = s_buf[slot][sm]
                    m_prev = m_st[st]
                    m_next = jnp.maximum(m_prev, jnp.max(s, axis=0, keepdims=True))
                    al_buf[slot][sm] = jnp.exp2(m_prev - m_next)
                    p_buf[slot][sm] = jnp.exp2(s - m_next).astype(BF16)
                m_st[st] = m_next

        def stage_pv(t, slot):
            for sm in range(2):
                e = item(sm, t)
                koff = pl.multiple_of(tab[1, e] * tk, tk)
                hoff = pl.multiple_of(tab[3, e] * LANES, LANES)
                st = tab[2, e]
                vt = v_ref[0, pl.ds(hoff, LANES), pl.ds(koff, tk)]
                if diag:
                    acc_st[st] = jnp.concatenate(
                        [jnp.dot(vt[:, :hk], p_buf[slot][sm, :hk, :hq],
                                 preferred_element_type=F32),
                         jnp.dot(vt, p_buf[slot][sm, :, hq:], preferred_element_type=F32)],
                        axis=1)
                else:
                    pv = jnp.dot(vt, p_buf[slot][sm], preferred_element_type=F32)
                    acc_st[st] = al_buf[slot][sm] * acc_st[st] + pv

        stage_qk(0, 0)
        stage_qk(1, 1)
        stage_softmax(0, 0)

        def body(u, carry):
            for d in range(2):
                t = 2 * u + 1 + d
                stage_qk(t + 1, d % 2)
                stage_softmax(t, (d + 1) % 2)
                stage_pv(t - 1, d % 2)
            return carry

        lax.fori_loop(0, (half + 1) // 2, body, 0)

    def run_fixed(tab, n_items, dummy):
        half = (n_items + 1) // 2
        lens = (half, n_items - half)

        def item(stream, t):
            return jnp.where(t < lens[stream], stream * half + t, dummy)

        def stage_probs(t, slot):
            for sm in range(2):
                e = item(sm, t)
                qoff = pl.multiple_of(tab[0, e] * tq, tq)
                koff = pl.multiple_of(tab[1, e] * tk, tk)
                hoff = pl.multiple_of(tab[3, e] * LANES, LANES)
                qt = q_ref[0, pl.ds(hoff, LANES), pl.ds(qoff, tq)]
                k = k_ref[0, pl.ds(koff, tk), pl.ds(hoff, LANES)]
                s = jnp.dot(k, qt, preferred_element_type=F32)
                p_buf[slot][sm] = jnp.exp2(s - m_st[tab[2, e]]).astype(BF16)

        def stage_pv(t, slot):
            for sm in range(2):
                e = item(sm, t)
                koff = pl.multiple_of(tab[1, e] * tk, tk)
                hoff = pl.multiple_of(tab[3, e] * LANES, LANES)
                st = tab[2, e]
                vt = v_ref[0, pl.ds(hoff, LANES), pl.ds(koff, tk)]
                acc_st[st] = acc_st[st] + jnp.dot(vt, p_buf[slot][sm],
                                                  preferred_element_type=F32)

        stage_probs(0, 0)

        def body(u, carry):
            for d in range(2):
                t = 2 * u + 1 + d
                stage_probs(t, (d + 1) % 2)
                stage_pv(t - 1, d % 2)
            return carry

        lax.fori_loop(0, (half + 1) // 2, body, 0)

    assert tab_ref.shape[1] == n_diag + 1
    run(tab_ref, n_diag, n_diag, True)

    b = pl.program_id(0)
    pr = pl.program_id(1)

    def put(tab, idx, qrow, j, state, head):
        tab[0, idx] = qrow
        tab[1, idx] = j
        tab[2, idx] = state
        tab[3, idx] = head

    n_fix = jnp.int32(0)
    n_gen = jnp.int32(0)
    for hh in range(2):
        hd = 2 * pr + hh

        kmax = lax.fori_loop(
            0, nq, lambda j, m, hd=hd: jnp.maximum(m, st_ref[b, j, 0, FOX_HEADS + hd]),
            jnp.float32(0.0))

        def list_row(qi, counts, hh=hh, hd=hd, kmax=kmax):
            base = (st_ref[b, qi, 0, hd] * (st_ref[b, qi, 0, FOX_HEADS + hd] + kmax)
                    + st_ref[b, qi, 1, hd])
            fixed_ok = base - st_ref[b, qi - 1, 2, hd] <= FIXED_REF_T

            def scan(tab, enabled, cnt):
                def contributes(carry):
                    j, _ = carry
                    bound = base - st_ref[b, jnp.maximum(j, 0), 2, hd]
                    return jnp.logical_and(jnp.logical_and(enabled, j >= 0), bound > -PRUNE_T)

                def take(carry):
                    j, cnt = carry
                    put(tab, cnt, qi, j, hh * nq + qi, hh)
                    return j - 1, cnt + 1

                return lax.while_loop(contributes, take, (qi - 1, cnt))[1]

            return (scan(off_tab, fixed_ok, counts[0]),
                    scan(gen_tab, jnp.logical_not(fixed_ok), counts[1]))

        n_fix, n_gen = lax.fori_loop(1, nq, list_row, (n_fix, n_gen))
    put(off_tab, n_off, 1, 0, 2 * nq, 0)
    put(gen_tab, n_off, 1, 0, 2 * nq, 0)
    run_fixed(off_tab, n_fix, n_off)

    @pl.when(n_gen > 0)
    def _():
        run(gen_tab, n_gen, n_off, False)

    def finish(qi, carry):
        a0 = acc_st[qi]
        a1 = acc_st[nq + qi]
        ot = jnp.concatenate([a0[:HEAD_DIM] / a0[HEAD_DIM:HEAD_DIM + 1],
                              a1[HEAD_DIM:] / a1[0:1]], axis=0)
        rows = pl.ds(pl.multiple_of(qi * tq, tq), tq)
        o_ref[0, rows, :] = (ot.T * g_ref[0, rows, :].astype(F32)).astype(BF16)
        return carry

    lax.fori_loop(0, nq, finish, 0)


def _fox_call(qaug_t, kaug, vaug_t, ga, stats):
    B, S, _ = kaug.shape
    tq, tk = TQ_FOX, TK_FOX
    assert tq == tk
    nq = S // tq
    tab, n_diag, n_off = _fox_tables(nq)
    pair = lambda b, p: (b, 0, p)
    return pl.pallas_call(
        functools.partial(_fox_kernel, n_diag=n_diag, n_off=n_off),
        grid=(B, FOX_HEADS // 2),
        in_specs=[pl.BlockSpec(memory_space=pltpu.SMEM),
                  pl.BlockSpec(memory_space=pltpu.SMEM),
                  pl.BlockSpec((1, 2 * LANES, S), lambda b, p: (b, p, 0)),
                  pl.BlockSpec((1, S, 2 * LANES), pair),
                  pl.BlockSpec((1, 2 * LANES, S), lambda b, p: (b, p, 0)),
                  pl.BlockSpec((1, S, LANES), pair)],
        out_specs=pl.BlockSpec((1, S, LANES), pair),
        out_shape=jax.ShapeDtypeStruct((B, S, FOX_W), BF16),
        scratch_shapes=[pltpu.SMEM((4, n_off + 1), jnp.int32),
                        pltpu.SMEM((4, n_off + 1), jnp.int32),
                        pltpu.VMEM((2, tk, tq), F32), pltpu.VMEM((2, tk, tq), F32),
                        pltpu.VMEM((2, tk, tq), BF16), pltpu.VMEM((2, tk, tq), BF16),
                        pltpu.VMEM((2, 1, tq), F32), pltpu.VMEM((2, 1, tq), F32),
                        pltpu.VMEM((2 * nq + 1, 1, tq), F32),
                        pltpu.VMEM((2 * nq + 1, LANES, tq), F32)],
        compiler_params=pltpu.CompilerParams(
            dimension_semantics=("arbitrary", "arbitrary"), vmem_limit_bytes=VMEM_LIMIT),
        name="fox",
    )(jnp.asarray(tab), stats, qaug_t, kaug, vaug_t, ga)


def _swa_out_kernel(sinks_ref, q_ref, kp_ref, kc_ref, vtp_ref, vtc_ref, g_ref,
                    oa_ref, x_ref, gate_ref, gpost_ref, wa_ref, wb_ref, out_ref,
                    s_sc, p_sc, e_sc, ob_sc):
    i = pl.program_id(1)
    nsub = TQ_SWA // WINDOW
    group = SWA_Q_HEADS // SWA_KV_HEADS
    ncol = group * WINDOW
    kall = jnp.concatenate([kp_ref[0], kc_ref[0]], axis=0)
    vtall = jnp.concatenate([vtp_ref[0], vtc_ref[0]], axis=1)
    lane = lax.broadcasted_iota(jnp.int32, (WINDOW, LANES), 1)
    lo = lane < HEAD_DIM
    zero = jnp.zeros((WINDOW, LANES), BF16)
    kj_ = lax.broadcasted_iota(jnp.int32, (2 * WINDOW, ncol), 0)
    qi_ = lax.broadcasted_iota(jnp.int32, (2 * WINDOW, ncol), 1) & (WINDOW - 1)
    rel = qi_ + WINDOW - kj_
    band = (rel >= 0) & (rel < WINDOW)
    nt = (((1,), (1,)), ((), ()))
    probs = [(r, g) for r in range(nsub) for g in range(SWA_KV_HEADS)]

    def stage_qk(n):
        r, g = probs[n]
        rows = slice(r * WINDOW, (r + 1) * WINDOW)
        kg = kall[r * WINDOW:(r + 2) * WINDOW, g * LANES:(g + 1) * LANES]
        c0 = g * group * HEAD_DIM
        qp0 = q_ref[0, rows, c0:c0 + LANES]
        qp1 = q_ref[0, rows, c0 + LANES:c0 + 2 * LANES]
        qs = jnp.concatenate([jnp.where(lo, qp0, zero), jnp.where(lo, qp1, zero),
                              jnp.where(lo, zero, qp0), jnp.where(lo, zero, qp1)], axis=0)
        s_sc[n] = lax.dot_general(kg, qs, nt, preferred_element_type=F32)

    def stage_softmax(n):
        r, g = probs[n]
        valid = band & (kj_ >= jnp.where(i == 0, WINDOW, 0)) if r == 0 else band
        st = jnp.where(valid, s_sc[n], NEG_BIG)
        heads = (group * g, group * g + 2, group * g + 1, group * g + 3)
        sink = jnp.concatenate(
            [jnp.full((1, WINDOW), sinks_ref[hd] * LOG2E, F32) for hd in heads], axis=1)
        m = jnp.maximum(jnp.max(st, axis=0, keepdims=True), sink)
        p_sc[n] = jnp.exp2(st - m).astype(BF16)
        e_sc[n] = jnp.exp2(sink - m)

    def stage_pv(n):
        r, g = probs[n]
        rows = slice(r * WINDOW, (r + 1) * WINDOW)
        c0 = g * group * HEAD_DIM
        vtg = vtall[g * LANES:(g + 1) * LANES, r * WINDOW:(r + 2) * WINDOW]
        acc = jnp.dot(vtg, p_sc[n], preferred_element_type=F32)
        l = acc[HEAD_DIM:HEAD_DIM + 1] + e_sc[n]
        on = acc[:HEAD_DIM] * (1.0 / l)
        pair0 = jnp.concatenate([on[:, 0:WINDOW], on[:, 2 * WINDOW:3 * WINDOW]], axis=0).T
        pair1 = jnp.concatenate([on[:, WINDOW:2 * WINDOW], on[:, 3 * WINDOW:]], axis=0).T
        ob_sc[rows, c0:c0 + LANES] = (
            pair0 * g_ref[0, rows, c0:c0 + LANES].astype(F32)).astype(BF16)
        ob_sc[rows, c0 + LANES:c0 + 2 * LANES] = (
            pair1 * g_ref[0, rows, c0 + LANES:c0 + 2 * LANES].astype(F32)).astype(BF16)

    for t in range(len(probs) + 2):
        if t < len(probs):
            stage_qk(t)
        if 1 <= t <= len(probs):
            stage_softmax(t - 1)
        if t >= 2:
            stage_pv(t - 2)

    y = (jnp.dot(oa_ref[0], wa_ref[...], preferred_element_type=F32)
         + jnp.dot(ob_sc[...], wb_ref[...], preferred_element_type=F32))
    ms = jnp.mean(y * y, axis=-1, keepdims=True)
    yn = y * lax.rsqrt(ms + RMS_EPS) * gpost_ref[...]
    out_ref[0] = x_ref[0] + gate_ref[0] * yn


def _swa_out_call(sinks, qb, kd, vdt, gb, oa, x, gate, g_post, wa, wb):
    B, S, D = x.shape
    tq = TQ_SWA
    nsub = tq // WINDOW
    cur = lambda b, i: (b, i, 0)
    prev = lambda b, i: (b, jnp.maximum(i * nsub - 1, 0), 0)
    cur_t = lambda b, i: (b, 0, i)
    prev_t = lambda b, i: (b, 0, jnp.maximum(i * nsub - 1, 0))
    const2 = lambda b, i: (0, 0)
    kvw = 2 * SWA_KV_W
    nprob = nsub * SWA_KV_HEADS
    ncol = SWA_Q_HEADS // SWA_KV_HEADS * WINDOW
    return pl.pallas_call(
        _swa_out_kernel,
        grid=(B, S // tq),
        in_specs=[pl.BlockSpec(memory_space=pltpu.SMEM),
                  pl.BlockSpec((1, tq, SWA_W), cur),
                  pl.BlockSpec((1, WINDOW, kvw), prev),
                  pl.BlockSpec((1, tq, kvw), cur),
                  pl.BlockSpec((1, kvw, WINDOW), prev_t),
                  pl.BlockSpec((1, kvw, tq), cur_t),
                  pl.BlockSpec((1, tq, SWA_W), cur),
                  pl.BlockSpec((1, tq, FOX_W), cur),
                  pl.BlockSpec((1, tq, D), cur),
                  pl.BlockSpec((1, 1, D), lambda b, i: (b, 0, 0)),
                  pl.BlockSpec((1, D), const2),
                  pl.BlockSpec((FOX_W, D), const2),
                  pl.BlockSpec((SWA_W, D), const2)],
        out_specs=pl.BlockSpec((1, tq, D), cur),
        out_shape=jax.ShapeDtypeStruct((B, S, D), F32),
        scratch_shapes=[pltpu.VMEM((nprob, 2 * WINDOW, ncol), F32),
                        pltpu.VMEM((nprob, 2 * WINDOW, ncol), BF16),
                        pltpu.VMEM((nprob, 1, ncol), F32),
                        pltpu.VMEM((tq, SWA_W), BF16)],
        compiler_params=pltpu.CompilerParams(
            dimension_semantics=("arbitrary", "arbitrary"), vmem_limit_bytes=VMEM_LIMIT),
        name="swa_out",
    )(sinks, qb, kd, kd, vdt, vdt, gb, oa, x, gate, g_post, wa, wb)


def _perm_w_in(w):
    wt = w.T
    sc = HEAD_DIM ** -0.5
    o = 0
    qa = wt[o:o + FOX_W]; o += FOX_W
    ka = wt[o:o + FOX_W]; o += FOX_W
    va = wt[o:o + FOX_W]; o += FOX_W
    fa = wt[o:o + FOX_HEADS]; o += FOX_HEADS
    za = wt[o:o + FOX_W]; o += FOX_W
    qb = wt[o:o + SWA_W]; o += SWA_W
    kb = wt[o:o + SWA_KV_W]; o += SWA_KV_W
    vb = wt[o:o + SWA_KV_W]; o += SWA_KV_W
    zb = wt[o:o + SWA_W]
    fpad = jnp.concatenate(
        [fa, fa, fa, jnp.zeros((LANES - 3 * FOX_HEADS, wt.shape[1]), w.dtype)], axis=0)
    sc2 = sc * LOG2E
    return jnp.concatenate([qa * sc2, ka, va, za, qb * sc2, zb, kb, vb, fpad],
                           axis=0).astype(BF16)


def _aug_constants():
    eq = np.zeros((LANES, AUG_W), np.float32)
    ek = np.zeros((LANES, AUG_W), np.float32)
    oq = np.zeros((1, AUG_W), np.float32)
    ok = np.zeros((1, AUG_W), np.float32)
    for hd in range(FOX_HEADS):
        base = LANES * hd + (HEAD_DIM if hd % 2 == 0 else 0)
        for part in range(3):
            eq[part * FOX_HEADS + hd, base + part] = 1.0
            ok[0, base + part] = 1.0
            ek[part * FOX_HEADS + hd, base + 3 + part] = -1.0
            oq[0, base + 3 + part] = 1.0
    return jnp.asarray(eq + ek, BF16), jnp.asarray(oq), jnp.asarray(ok)


def kernel(x, c, positions, w_ada, b_ada, g_pre, w_in, b_fgate, sinks, w_out, g_post):
    B, S, D = x.shape
    depth = w_ada.shape[0]
    assert TM_SUB == TQ_FOX == TK_FOX
    half = HEAD_DIM // 2
    inv_freq = ROPE_THETA ** (-jnp.arange(half, dtype=F32) / half)
    invf = inv_freq[:, None]
    pos3 = positions[:, None, :]
    ind = np.zeros((2 * FOX_W, LANES), np.float32)
    ind[np.arange(2 * FOX_W), np.arange(2 * FOX_W) // HEAD_DIM] = 1.0
    ind = jnp.asarray(ind, BF16)
    tri = jnp.asarray(np.tril(np.ones((TM_SUB, TM_SUB), np.float32)), BF16)
    eqk, oq, ok = _aug_constants()
    c_pad = jnp.zeros((8, D), F32).at[:B].set(c)
    for l in range(depth):
        mod = _mod_call(c_pad, w_ada[l], b_ada[l][None, :])[:B]
        shift = mod[:, None, 0:D]
        scale = mod[:, None, D:2 * D]
        gate = mod[:, None, 2 * D:3 * D]
        bf_pad = jnp.concatenate(
            [b_fgate[l]] * 3 + [jnp.zeros((LANES - 3 * FOX_HEADS,), F32)])[None, :]
        qaug, kaug, vaug, ga, qb, kd, vd, gb, stats = _inproj_call(
            x, shift, scale, g_pre[l][None, :], _perm_w_in(w_in[l]), pos3, invf, bf_pad,
            tri, eqk, oq, ok, ind)
        oa = _fox_call(qaug, kaug, vaug, ga, stats[:, :, 0:3, 0:2 * FOX_HEADS])
        wo = w_out[l].astype(BF16)
        x = _swa_out_call(sinks[l], qb, kd, vd, gb, oa, x, gate, g_post[l][None, :],
                          wo[:FOX_W], wo[FOX_W:])
    return x
```

```python
import functools

import jax
import jax.numpy as jnp
import numpy as np
from jax import lax
from jax.experimental import pallas as pl
from jax.experimental.pallas import tpu as pltpu

D_MODEL = 1024
HEAD_DIM = 64
FOX_HEADS = 8
SWA_Q_HEADS = 8
SWA_KV_HEADS = 2
WINDOW = 128
ROPE_THETA = 10000.0
RMS_EPS = 1e-6
FOX_W = FOX_HEADS * HEAD_DIM
SWA_W = SWA_Q_HEADS * HEAD_DIM
SWA_KV_W = SWA_KV_HEADS * HEAD_DIM

LANES = 128
AUG_W = FOX_HEADS * LANES

OFF_QA = 0
OFF_KA = OFF_QA + FOX_W
OFF_VA = OFF_KA + FOX_W
OFF_ZA = OFF_VA + FOX_W
OFF_QB = OFF_ZA + FOX_W
OFF_ZB = OFF_QB + SWA_W
OFF_KD = OFF_ZB + SWA_W
OFF_VD = OFF_KD + SWA_KV_W
OFF_F = OFF_VD + SWA_KV_W
IN_NP = OFF_F + LANES

TM_IN = 1024
TM_SUB = 512
TQ_FOX = 512
TK_FOX = 512
FIXED_UNROLL = 4
TQ_SWA = 1024
NEG_BIG = -1e30
LOG2E = 1.4426950408889634
PRUNE_T = 140.0
NORM_MARGIN = 1.01
FIXED_REF_T = 100.0
VMEM_LIMIT = 56 * 1024 * 1024

F32 = jnp.float32
BF16 = jnp.bfloat16


def _split3(a):
    hi = a.astype(BF16)
    r = a - hi.astype(F32)
    mid = r.astype(BF16)
    lo = (r - mid.astype(F32)).astype(BF16)
    return hi, mid, lo


def _silu(z):
    hz = 0.5 * z
    return hz + hz * jnp.tanh(hz)


def _mod_kernel(c_ref, w_ref, b_ref, o_ref):
    c = c_ref[...]
    sc = c * (1.0 / (1.0 + jnp.exp(-c)))
    o_ref[...] = jnp.dot(sc, w_ref[...], precision=lax.Precision.HIGHEST,
                         preferred_element_type=F32) + b_ref[...]


def _mod_call(c_pad, w_ada, b_ada):
    rows = c_pad.shape[0]
    n = w_ada.shape[1]
    bn = D_MODEL
    return pl.pallas_call(
        _mod_kernel,
        grid=(n // bn,),
        in_specs=[pl.BlockSpec((rows, D_MODEL), lambda j: (0, 0)),
                  pl.BlockSpec((D_MODEL, bn), lambda j: (0, j)),
                  pl.BlockSpec((1, bn), lambda j: (0, j))],
        out_specs=pl.BlockSpec((rows, bn), lambda j: (0, j)),
        out_shape=jax.ShapeDtypeStruct((rows, n), F32),
        name="mod",
    )(c_pad, w_ada, b_ada)


def _inproj_kernel(x_ref, shift_ref, scale_ref, gpre_ref, w_ref, pos_ref, invf_ref, bf_ref,
                   tri_ref, eqk_ref, oq_ref, ok_ref, ind_ref,
                   qaug_ref, kaug_ref, vaug_ref, ga_ref, qb_ref, kd_ref, vd_ref, gb_ref, st_ref,
                   carry_ref):
    tm = TM_SUB

    @pl.when(pl.program_id(1) == 0)
    def _():
        carry_ref[...] = jnp.zeros_like(carry_ref)

    for sub in range(x_ref.shape[1] // tm):
        _inproj_subtile(sub, slice(sub * tm, (sub + 1) * tm),
                        x_ref, shift_ref, scale_ref, gpre_ref, w_ref, pos_ref, invf_ref, bf_ref,
                        tri_ref, eqk_ref, oq_ref, ok_ref, ind_ref,
                        qaug_ref, kaug_ref, vaug_ref, ga_ref, qb_ref, kd_ref, vd_ref, gb_ref,
                        st_ref, carry_ref)


def _inproj_subtile(sub, rows, x_ref, shift_ref, scale_ref, gpre_ref, w_ref, pos_ref, invf_ref,
                    bf_ref, tri_ref, eqk_ref, oq_ref, ok_ref, ind_ref,
                    qaug_ref, kaug_ref, vaug_ref, ga_ref, qb_ref, kd_ref, vd_ref, gb_ref,
                    st_ref, carry_ref):
    tm = TM_SUB
    x = x_ref[0, rows, :]
    ms = jnp.mean(x * x, axis=-1, keepdims=True)
    gain = gpre_ref[...] * (1.0 + scale_ref[0])
    h = (x * lax.rsqrt(ms + RMS_EPS) * gain + shift_ref[0]).astype(BF16)

    def proj(off, width):
        return lax.dot_general(h, w_ref[off:off + width, :], (((1,), (1,)), ((), ())),
                               preferred_element_type=F32)

    tail = proj(OFF_KD, 3 * LANES)
    f = tail[:, OFF_F - OFF_KD:] + bf_ref[...]
    ls = jnp.minimum(f, 0.0) - jnp.log1p(jnp.exp(-jnp.abs(f)))
    lane = lax.broadcasted_iota(jnp.int32, (tm, LANES), 1)

    def by_group(a, b_, c_):
        return jnp.where(lane < FOX_HEADS, a, jnp.where(lane < 2 * FOX_HEADS, b_, c_))

    part = by_group(*(t.astype(F32) for t in _split3(ls))).astype(BF16)
    psum = jnp.dot(tri_ref[...], part, preferred_element_type=F32)
    cum = (psum + pltpu.roll(psum, LANES - FOX_HEADS, 1)
           + pltpu.roll(psum, LANES - 2 * FOX_HEADS, 1))
    cum = cum + carry_ref[...]
    carry_ref[...] = cum[tm - 1:tm, :]

    cum2 = cum * LOG2E
    cum2 = by_group(cum2, pltpu.roll(cum2, FOX_HEADS, 1), pltpu.roll(cum2, 2 * FOX_HEADS, 1))
    cs = by_group(*(t.astype(F32) for t in _split3(cum2))).astype(BF16)
    placed = jnp.dot(cs, eqk_ref[...], preferred_element_type=F32)
    augq = placed * ok_ref[...] + oq_ref[...]
    augk = placed * oq_ref[...] + ok_ref[...]

    lane_w = lax.broadcasted_iota(jnp.int32, (tm, AUG_W), 1)
    data = (((lane_w >> 6) ^ (lane_w >> 7)) & 1) == 0

    def rep(a):
        return jnp.concatenate(
            [a[:, LANES * (hd // 2):LANES * (hd // 2 + 1)] for hd in range(FOX_HEADS)], axis=1)

    qa = proj(OFF_QA, FOX_W)
    ka = proj(OFF_KA, FOX_W)
    qaug_ref[0, :, rows] = jnp.where(data, rep(qa), augq).T.astype(BF16)
    kaug_ref[0, rows, :] = jnp.where(data, rep(ka), augk).astype(BF16)

    sq = jnp.concatenate([qa * qa, ka * ka], axis=1).astype(BF16)
    nrm2 = jnp.dot(sq, ind_ref[...], preferred_element_type=F32)
    nmax = jnp.sqrt(jnp.max(nrm2, axis=0, keepdims=True)) * NORM_MARGIN
    st_ref[0, sub] = jnp.concatenate(
        [nmax, cum2[0:1], cum2[tm - 1:tm], jnp.zeros((5, LANES), F32)], axis=0)
    vaug_ref[0, :, rows] = jnp.where(data, rep(proj(OFF_VA, FOX_W)), 1.0).T.astype(BF16)

    za = proj(OFF_ZA, FOX_W)
    ga_ref[0, rows, :] = _silu(za).astype(BF16)
    zb = proj(OFF_ZB, SWA_W)
    gb_ref[0, rows, :] = _silu(zb).astype(BF16)

    ang_t = invf_ref[...] * pos_ref[0, :, rows].astype(F32)
    reps = LANES // (HEAD_DIM // 2)
    cosv = jnp.concatenate([jnp.cos(ang_t)] * reps, axis=0).T
    sinv = jnp.concatenate([jnp.sin(ang_t)] * reps, axis=0).T
    first = (lane & (HEAD_DIM // 2)) == 0
    sin_signed = jnp.where(first, -sinv, sinv)

    def rope(a):
        outs = []
        for cidx in range(a.shape[1] // LANES):
            blk = a[:, cidx * LANES:(cidx + 1) * LANES]
            other = jnp.where(first, pltpu.roll(blk, LANES - HEAD_DIM // 2, 1),
                              pltpu.roll(blk, HEAD_DIM // 2, 1))
            outs.append(blk * cosv + other * sin_signed)
        return jnp.concatenate(outs, axis=1)

    qb_ref[0, rows, :] = rope(proj(OFF_QB, SWA_W)).astype(BF16)
    kr = rope(tail[:, :LANES])
    ks = pltpu.roll(kr, HEAD_DIM, 1)
    low = lane < HEAD_DIM
    kd_ref[0, rows, :] = jnp.concatenate([jnp.where(low, kr, ks), jnp.where(low, ks, kr)],
                                axis=1).astype(BF16)
    vt = tail[:, OFF_VD - OFF_KD:OFF_F - OFF_KD].T
    ones = jnp.ones((HEAD_DIM, tm), F32)
    vd_ref[0, :, rows] = jnp.concatenate([vt[:HEAD_DIM], ones, vt[HEAD_DIM:], ones],
                                axis=0).astype(BF16)


def _inproj_call(x, shift, scale, g_pre, w_perm, pos3, invf, bf_pad, tri, eqk, oq, ok, ind):
    B, S, D = x.shape
    tm = TM_IN
    row = lambda b, t: (b, t, 0)
    per_b = lambda b, t: (b, 0, 0)
    const2 = lambda b, t: (0, 0)
    out_w = (AUG_W, AUG_W, AUG_W, FOX_W, SWA_W, 2 * SWA_KV_W, 2 * SWA_KV_W, SWA_W)
    transposed = (0, 2, 6)
    return pl.pallas_call(
        _inproj_kernel,
        grid=(B, S // tm),
        in_specs=[pl.BlockSpec((1, tm, D), row),
                  pl.BlockSpec((1, 1, D), per_b),
                  pl.BlockSpec((1, 1, D), per_b),
                  pl.BlockSpec((1, D), const2),
                  pl.BlockSpec((IN_NP, D), const2),
                  pl.BlockSpec((1, 1, tm), lambda b, t: (b, 0, t)),
                  pl.BlockSpec((HEAD_DIM // 2, 1), const2),
                  pl.BlockSpec((1, LANES), const2),
                  pl.BlockSpec((TM_SUB, TM_SUB), const2),
                  pl.BlockSpec((LANES, AUG_W), const2),
                  pl.BlockSpec((1, AUG_W), const2),
                  pl.BlockSpec((1, AUG_W), const2),
                  pl.BlockSpec((2 * FOX_W, LANES), const2)],
        out_specs=[pl.BlockSpec((1, w, tm), lambda b, t: (b, 0, t)) if i in transposed
                   else pl.BlockSpec((1, tm, w), row) for i, w in enumerate(out_w)]
        + [pl.BlockSpec((1, tm // TM_SUB, 8, LANES), lambda b, t: (b, t, 0, 0))],
        out_shape=[jax.ShapeDtypeStruct((B, w, S) if i in transposed else (B, S, w), BF16)
                   for i, w in enumerate(out_w)]
        + [jax.ShapeDtypeStruct((B, S // TM_SUB, 8, LANES), F32)],
        scratch_shapes=[pltpu.VMEM((1, LANES), F32)],
        compiler_params=pltpu.CompilerParams(
            dimension_semantics=("arbitrary", "arbitrary"), vmem_limit_bytes=VMEM_LIMIT),
        name="in_proj",
    )(x, shift, scale, g_pre, w_perm, pos3, invf, bf_pad, tri, eqk, oq, ok, ind)


def _fox_tables(nq):
    diag = [(qi, qi, hh * nq + qi, hh) for hh in range(2) for qi in range(nq)]
    tab = np.array(diag + [(1, 0, 2 * nq, 0)], np.int32).T
    return tab, len(diag), nq * (nq - 1)


def _fox_kernel(tab_ref, st_ref, q_ref, k_ref, v_ref, g_ref, o_ref,
                off_tab, gen_tab, s0, s1, p0, p1, al0, al1, m_st, acc_st, *, n_diag, n_off):
    tq = TQ_FOX
    tk = TK_FOX
    nq = q_ref.shape[2] // tq
    s_buf = (s0, s1)
    p_buf = (p0, p1)
    al_buf = (al0, al1)

    def run(tab, n_items, dummy, diag):
        half = (n_items + 1) // 2
        lens = (half, n_items - half)
        hk, hq = tk // 2, tq // 2

        def item(stream, t):
            return jnp.where(t < lens[stream], stream * half + t, dummy)

        def stage_qk(t, slot):
            for sm in range(2):
                e = item(sm, t)
                qoff = pl.multiple_of(tab[0, e] * tq, tq)
                koff = pl.multiple_of(tab[1, e] * tk, tk)
                hoff = pl.multiple_of(tab[3, e] * LANES, LANES)
                qt = q_ref[0, pl.ds(hoff, LANES), pl.ds(qoff, tq)]
                k = k_ref[0, pl.ds(koff, tk), pl.ds(hoff, LANES)]
                if diag:
                    s_buf[slot][sm, :hk, :] = jnp.dot(k[:hk], qt, preferred_element_type=F32)
                    s_buf[slot][sm, hk:, hq:] = jnp.dot(k[hk:], qt[:, hq:],
                                                        preferred_element_type=F32)
                else:
                    s_buf[slot][sm] = jnp.dot(k, qt, preferred_element_type=F32)

        def stage_softmax(t, slot):
            for sm in range(2):
                st = tab[2, item(sm, t)]
                if diag:
                    kr = lax.broadcasted_iota(jnp.int32, (hk, tq), 0)
                    qc = lax.broadcasted_iota(jnp.int32, (hk, tq), 1)
                    top = jnp.where(kr <= qc, s_buf[slot][sm, :hk, :], NEG_BIG)
                    kr2 = lax.broadcasted_iota(jnp.int32, (tk - hk, tq - hq), 0)
                    qc2 = lax.broadcasted_iota(jnp.int32, (tk - hk, tq - hq), 1)
                    low = jnp.where(kr2 <= qc2, s_buf[slot][sm, hk:, hq:], NEG_BIG)
                    top_a, top_b = top[:, :hq], top[:, hq:]
                    m_a = jnp.max(top_a, axis=0, keepdims=True)
                    m_b = jnp.maximum(jnp.max(top_b, axis=0, keepdims=True),
                                      jnp.max(low, axis=0, keepdims=True))
                    m_next = jnp.concatenate([m_a, m_b], axis=1)
                    p_buf[slot][sm, :hk, :hq] = jnp.exp2(top_a - m_a).astype(BF16)
                    p_buf[slot][sm, :hk, hq:] = jnp.exp2(top_b - m_b).astype(BF16)
                    p_buf[slot][sm, hk:, hq:] = jnp.exp2(low - m_b).astype(BF16)
                else:
                    s = s_buf[slot][sm]
                    m_prev = m_st[st]
                    m_next = jnp.maximum(m_prev, jnp.max(s, axis=0, keepdims=True))
                    al_buf[slot][sm] = jnp.exp2(m_prev - m_next)
                    p_buf[slot][sm] = jnp.exp2(s - m_next).astype(BF16)
                m_st[st] = m_next

        def stage_pv(t, slot):
            for sm in range(2):
                e = item(sm, t)
                koff = pl.multiple_of(tab[1, e] * tk, tk)
                hoff = pl.multiple_of(tab[3, e] * LANES, LANES)
                st = tab[2, e]
                vt = v_ref[0, pl.ds(hoff, LANES), pl.ds(koff, tk)]
                if diag:
                    acc_st[st] = jnp.concatenate(
                        [jnp.dot(vt[:, :hk], p_buf[slot][sm, :hk, :hq],
                                 preferred_element_type=F32),
                         jnp.dot(vt, p_buf[slot][sm, :, hq:], preferred_element_type=F32)],
                        axis=1)
                else:
                    pv = jnp.dot(vt, p_buf[slot][sm], preferred_element_type=F32)
                    acc_st[st] = al_buf[slot][sm] * acc_st[st] + pv

        stage_qk(0, 0)
        stage_qk(1, 1)
        stage_softmax(0, 0)

        def body(u, carry):
            for d in range(2):
                t = 2 * u + 1 + d
                stage_qk(t + 1, d % 2)
                stage_softmax(t, (d + 1) % 2)
                stage_pv(t - 1, d % 2)
            return carry

        lax.fori_loop(0, (half + 1) // 2, body, 0)

    def run_fixed(tab, n_items, dummy):
        half = (n_items + 1) // 2
        lens = (half, n_items - half)

        def item(stream, t):
            return jnp.where(t < lens[stream], stream * half + t, dummy)

        def stage_probs(t, slot):
            for sm in range(2):
                e = item(sm, t)
                qoff = pl.multiple_of(tab[0, e] * tq, tq)
                koff = pl.multiple_of(tab[1, e] * tk, tk)
                hoff = pl.multiple_of(tab[3, e] * LANES, LANES)
                qt = q_ref[0, pl.ds(hoff, LANES), pl.ds(qoff, tq)]
                k = k_ref[0, pl.ds(koff, tk), pl.ds(hoff, LANES)]
                s = jnp.dot(k, qt, preferred_element_type=F32)
                p_buf[slot][sm] = jnp.exp2(s - m_st[tab[2, e]]).astype(BF16)

        def stage_pv(t, slot):
            for sm in range(2):
                e = item(sm, t)
                koff = pl.multiple_of(tab[1, e] * tk, tk)
                hoff = pl.multiple_of(tab[3, e] * LANES, LANES)
                st = tab[2, e]
                vt = v_ref[0, pl.ds(hoff, LANES), pl.ds(koff, tk)]
                acc_st[st] = acc_st[st] + jnp.dot(vt, p_buf[slot][sm],
                                                  preferred_element_type=F32)

        stage_probs(0, 0)

        def body(u, carry):
            for d in range(FIXED_UNROLL):
                t = FIXED_UNROLL * u + 1 + d
                stage_probs(t, (d + 1) % 2)
                stage_pv(t - 1, d % 2)
            return carry

        lax.fori_loop(0, (half + FIXED_UNROLL - 1) // FIXED_UNROLL, body, 0)

    assert FIXED_UNROLL % 2 == 0 and tab_ref.shape[1] == n_diag + 1
    run(tab_ref, n_diag, n_diag, True)

    b = pl.program_id(0)
    pr = pl.program_id(1)

    def put(tab, idx, qrow, j, state, head):
        tab[0, idx] = qrow
        tab[1, idx] = j
        tab[2, idx] = state
        tab[3, idx] = head

    n_fix = jnp.int32(0)
    n_gen = jnp.int32(0)
    for hh in range(2):
        hd = 2 * pr + hh

        kmax = lax.fori_loop(
            0, nq, lambda j, m, hd=hd: jnp.maximum(m, st_ref[b, j, 0, FOX_HEADS + hd]),
            jnp.float32(0.0))

        def list_row(qi, counts, hh=hh, hd=hd, kmax=kmax):
            base = (st_ref[b, qi, 0, hd] * (st_ref[b, qi, 0, FOX_HEADS + hd] + kmax)
                    + st_ref[b, qi, 1, hd])
            fixed_ok = base - st_ref[b, qi - 1, 2, hd] <= FIXED_REF_T

            def scan(tab, enabled, cnt):
                def contributes(carry):
                    j, _ = carry
                    bound = base - st_ref[b, jnp.maximum(j, 0), 2, hd]
                    return jnp.logical_and(jnp.logical_and(enabled, j >= 0), bound > -PRUNE_T)

                def take(carry):
                    j, cnt = carry
                    put(tab, cnt, qi, j, hh * nq + qi, hh)
                    return j - 1, cnt + 1

                return lax.while_loop(contributes, take, (qi - 1, cnt))[1]

            return (scan(off_tab, fixed_ok, counts[0]),
                    scan(gen_tab, jnp.logical_not(fixed_ok), counts[1]))

        n_fix, n_gen = lax.fori_loop(1, nq, list_row, (n_fix, n_gen))
    put(off_tab, n_off, 1, 0, 2 * nq, 0)
    put(gen_tab, n_off, 1, 0, 2 * nq, 0)
    run_fixed(off_tab, n_fix, n_off)

    @pl.when(n_gen > 0)
    def _():
        run(gen_tab, n_gen, n_off, False)

    def finish(qi, carry):
        a0 = acc_st[qi]
        a1 = acc_st[nq + qi]
        ot = jnp.concatenate([a0[:HEAD_DIM] / a0[HEAD_DIM:HEAD_DIM + 1],
                              a1[HEAD_DIM:] / a1[0:1]], axis=0)
        rows = pl.ds(pl.multiple_of(qi * tq, tq), tq)
        o_ref[0, rows, :] = (ot.T * g_ref[0, rows, :].astype(F32)).astype(BF16)
        return carry

    lax.fori_loop(0, nq, finish, 0)


def _fox_call(qaug_t, kaug, vaug_t, ga, stats):
    B, S, _ = kaug.shape
    tq, tk = TQ_FOX, TK_FOX
    assert tq == tk
    nq = S // tq
    tab, n_diag, n_off = _fox_tables(nq)
    pair = lambda b, p: (b, 0, p)
    return pl.pallas_call(
        functools.partial(_fox_kernel, n_diag=n_diag, n_off=n_off),
        grid=(B, FOX_HEADS // 2),
        in_specs=[pl.BlockSpec(memory_space=pltpu.SMEM),
                  pl.BlockSpec(memory_space=pltpu.SMEM),
                  pl.BlockSpec((1, 2 * LANES, S), lambda b, p: (b, p, 0)),
                  pl.BlockSpec((1, S, 2 * LANES), pair),
                  pl.BlockSpec((1, 2 * LANES, S), lambda b, p: (b, p, 0)),
                  pl.BlockSpec((1, S, LANES), pair)],
        out_specs=pl.BlockSpec((1, S, LANES), pair),
        out_shape=jax.ShapeDtypeStruct((B, S, FOX_W), BF16),
        scratch_shapes=[pltpu.SMEM((4, n_off + 1), jnp.int32),
                        pltpu.SMEM((4, n_off + 1), jnp.int32),
                        pltpu.VMEM((2, tk, tq), F32), pltpu.VMEM((2, tk, tq), F32),
                        pltpu.VMEM((2, tk, tq), BF16), pltpu.VMEM((2, tk, tq), BF16),
                        pltpu.VMEM((2, 1, tq), F32), pltpu.VMEM((2, 1, tq), F32),
                        pltpu.VMEM((2 * nq + 1, 1, tq), F32),
                        pltpu.VMEM((2 * nq + 1, LANES, tq), F32)],
        compiler_params=pltpu.CompilerParams(
            dimension_semantics=("arbitrary", "arbitrary"), vmem_limit_bytes=VMEM_LIMIT),
        name="fox",
    )(jnp.asarray(tab), stats, qaug_t, kaug, vaug_t, ga)


def _swa_out_kernel(sinks_ref, q_ref, kp_ref, kc_ref, vtp_ref, vtc_ref, g_ref,
                    oa_ref, x_ref, gate_ref, gpost_ref, wa_ref, wb_ref, out_ref,
                    s_sc, p_sc, e_sc, ob_sc):
    i = pl.program_id(1)
    nsub = TQ_SWA // WINDOW
    group = SWA_Q_HEADS // SWA_KV_HEADS
    ncol = group * WINDOW
    kall = jnp.concatenate([kp_ref[0], kc_ref[0]], axis=0)
    vtall = jnp.concatenate([vtp_ref[0], vtc_ref[0]], axis=1)
    lane = lax.broadcasted_iota(jnp.int32, (WINDOW, LANES), 1)
    lo = lane < HEAD_DIM
    zero = jnp.zeros((WINDOW, LANES), BF16)
    kj_ = lax.broadcasted_iota(jnp.int32, (2 * WINDOW, ncol), 0)
    qi_ = lax.broadcasted_iota(jnp.int32, (2 * WINDOW, ncol), 1) & (WINDOW - 1)
    rel = qi_ + WINDOW - kj_
    band = (rel >= 0) & (rel < WINDOW)
    nt = (((1,), (1,)), ((), ()))
    probs = [(r, g) for r in range(nsub) for g in range(SWA_KV_HEADS)]

    def stage_qk(n):
        r, g = probs[n]
        rows = slice(r * WINDOW, (r + 1) * WINDOW)
        kg = kall[r * WINDOW:(r + 2) * WINDOW, g * LANES:(g + 1) * LANES]
        c0 = g * group * HEAD_DIM
        qp0 = q_ref[0, rows, c0:c0 + LANES]
        qp1 = q_ref[0, rows, c0 + LANES:c0 + 2 * LANES]
        qs = jnp.concatenate([jnp.where(lo, qp0, zero), jnp.where(lo, qp1, zero),
                              jnp.where(lo, zero, qp0), jnp.where(lo, zero, qp1)], axis=0)
        s_sc[n] = lax.dot_general(kg, qs, nt, preferred_element_type=F32)

    def stage_softmax(n):
        r, g = probs[n]
        valid = band & (kj_ >= jnp.where(i == 0, WINDOW, 0)) if r == 0 else band
        st = jnp.where(valid, s_sc[n], NEG_BIG)
        heads = (group * g, group * g + 2, group * g + 1, group * g + 3)
        sink = jnp.concatenate(
            [jnp.full((1, WINDOW), sinks_ref[hd] * LOG2E, F32) for hd in heads], axis=1)
        m = jnp.maximum(jnp.max(st, axis=0, keepdims=True), sink)
        p_sc[n] = jnp.exp2(st - m).astype(BF16)
        e_sc[n] = jnp.exp2(sink - m)

    def stage_pv(n):
        r, g = probs[n]
        rows = slice(r * WINDOW, (r + 1) * WINDOW)
        c0 = g * group * HEAD_DIM
        vtg = vtall[g * LANES:(g + 1) * LANES, r * WINDOW:(r + 2) * WINDOW]
        acc = jnp.dot(vtg, p_sc[n], preferred_element_type=F32)
        l = acc[HEAD_DIM:HEAD_DIM + 1] + e_sc[n]
        on = acc[:HEAD_DIM] * (1.0 / l)
        pair0 = jnp.concatenate([on[:, 0:WINDOW], on[:, 2 * WINDOW:3 * WINDOW]], axis=0).T
        pair1 = jnp.concatenate([on[:, WINDOW:2 * WINDOW], on[:, 3 * WINDOW:]], axis=0).T
        ob_sc[rows, c0:c0 + LANES] = (
            pair0 * g_ref[0, rows, c0:c0 + LANES].astype(F32)).astype(BF16)
        ob_sc[rows, c0 + LANES:c0 + 2 * LANES] = (
            pair1 * g_ref[0, rows, c0 + LANES:c0 + 2 * LANES].astype(F32)).astype(BF16)

    for t in range(len(probs) + 2):
        if t < len(probs):
            stage_qk(t)
        if 1 <= t <= len(probs):
            stage_softmax(t - 1)
        if t >= 2:
            stage_pv(t - 2)

    y = (jnp.dot(oa_ref[0], wa_ref[...], preferred_element_type=F32)
         + jnp.dot(ob_sc[...], wb_ref[...], preferred_element_type=F32))
    ms = jnp.mean(y * y, axis=-1, keepdims=True)
    yn = y * lax.rsqrt(ms + RMS_EPS) * gpost_ref[...]
    out_ref[0] = x_ref[0] + gate_ref[0] * yn


def _swa_out_call(sinks, qb, kd, vdt, gb, oa, x, gate, g_post, wa, wb):
    B, S, D = x.shape
    tq = TQ_SWA
    nsub = tq // WINDOW
    cur = lambda b, i: (b, i, 0)
    prev = lambda b, i: (b, jnp.maximum(i * nsub - 1, 0), 0)
    cur_t = lambda b, i: (b, 0, i)
    prev_t = lambda b, i: (b, 0, jnp.maximum(i * nsub - 1, 0))
    const2 = lambda b, i: (0, 0)
    kvw = 2 * SWA_KV_W
    nprob = nsub * SWA_KV_HEADS
    ncol = SWA_Q_HEADS // SWA_KV_HEADS * WINDOW
    return pl.pallas_call(
        _swa_out_kernel,
        grid=(B, S // tq),
        in_specs=[pl.BlockSpec(memory_space=pltpu.SMEM),
                  pl.BlockSpec((1, tq, SWA_W), cur),
                  pl.BlockSpec((1, WINDOW, kvw), prev),
                  pl.BlockSpec((1, tq, kvw), cur),
                  pl.BlockSpec((1, kvw, WINDOW), prev_t),
                  pl.BlockSpec((1, kvw, tq), cur_t),
                  pl.BlockSpec((1, tq, SWA_W), cur),
                  pl.BlockSpec((1, tq, FOX_W), cur),
                  pl.BlockSpec((1, tq, D), cur),
                  pl.BlockSpec((1, 1, D), lambda b, i: (b, 0, 0)),
                  pl.BlockSpec((1, D), const2),
                  pl.BlockSpec((FOX_W, D), const2),
                  pl.BlockSpec((SWA_W, D), const2)],
        out_specs=pl.BlockSpec((1, tq, D), cur),
        out_shape=jax.ShapeDtypeStruct((B, S, D), F32),
        scratch_shapes=[pltpu.VMEM((nprob, 2 * WINDOW, ncol), F32),
                        pltpu.VMEM((nprob, 2 * WINDOW, ncol), BF16),
                        pltpu.VMEM((nprob, 1, ncol), F32),
                        pltpu.VMEM((tq, SWA_W), BF16)],
        compiler_params=pltpu.CompilerParams(
            dimension_semantics=("arbitrary", "arbitrary"), vmem_limit_bytes=VMEM_LIMIT),
        name="swa_out",
    )(sinks, qb, kd, kd, vdt, vdt, gb, oa, x, gate, g_post, wa, wb)


def _perm_w_in(w):
    wt = w.T
    sc = HEAD_DIM ** -0.5
    o = 0
    qa = wt[o:o + FOX_W]; o += FOX_W
    ka = wt[o:o + FOX_W]; o += FOX_W
    va = wt[o:o + FOX_W]; o += FOX_W
    fa = wt[o:o + FOX_HEADS]; o += FOX_HEADS
    za = wt[o:o + FOX_W]; o += FOX_W
    qb = wt[o:o + SWA_W]; o += SWA_W
    kb = wt[o:o + SWA_KV_W]; o += SWA_KV_W
    vb = wt[o:o + SWA_KV_W]; o += SWA_KV_W
    zb = wt[o:o + SWA_W]
    fpad = jnp.concatenate(
        [fa, fa, fa, jnp.zeros((LANES - 3 * FOX_HEADS, wt.shape[1]), w.dtype)], axis=0)
    sc2 = sc * LOG2E
    return jnp.concatenate([qa * sc2, ka, va, za, qb * sc2, zb, kb, vb, fpad],
                           axis=0).astype(BF16)


def _aug_constants():
    eq = np.zeros((LANES, AUG_W), np.float32)
    ek = np.zeros((LANES, AUG_W), np.float32)
    oq = np.zeros((1, AUG_W), np.float32)
    ok = np.zeros((1, AUG_W), np.float32)
    for hd in range(FOX_HEADS):
        base = LANES * hd + (HEAD_DIM if hd % 2 == 0 else 0)
        for part in range(3):
            eq[part * FOX_HEADS + hd, base + part] = 1.0
            ok[0, base + part] = 1.0
            ek[part * FOX_HEADS + hd, base + 3 + part] = -1.0
            oq[0, base + 3 + part] = 1.0
    return jnp.asarray(eq + ek, BF16), jnp.asarray(oq), jnp.asarray(ok)


def kernel(x, c, positions, w_ada, b_ada, g_pre, w_in, b_fgate, sinks, w_out, g_post):
    B, S, D = x.shape
    depth = w_ada.shape[0]
    assert TM_SUB == TQ_FOX == TK_FOX
    half = HEAD_DIM // 2
    inv_freq = ROPE_THETA ** (-jnp.arange(half, dtype=F32) / half)
    invf = inv_freq[:, None]
    pos3 = positions[:, None, :]
    ind = np.zeros((2 * FOX_W, LANES), np.float32)
    ind[np.arange(2 * FOX_W), np.arange(2 * FOX_W) // HEAD_DIM] = 1.0
    ind = jnp.asarray(ind, BF16)
    tri = jnp.asarray(np.tril(np.ones((TM_SUB, TM_SUB), np.float32)), BF16)
    eqk, oq, ok = _aug_constants()
    c_pad = jnp.zeros((8, D), F32).at[:B].set(c)
    for l in range(depth):
        mod = _mod_call(c_pad, w_ada[l], b_ada[l][None, :])[:B]
        shift = mod[:, None, 0:D]
        scale = mod[:, None, D:2 * D]
        gate = mod[:, None, 2 * D:3 * D]
        bf_pad = jnp.concatenate(
            [b_fgate[l]] * 3 + [jnp.zeros((LANES - 3 * FOX_HEADS,), F32)])[None, :]
        qaug, kaug, vaug, ga, qb, kd, vd, gb, stats = _inproj_call(
            x, shift, scale, g_pre[l][None, :], _perm_w_in(w_in[l]), pos3, invf, bf_pad,
            tri, eqk, oq, ok, ind)
        oa = _fox_call(qaug, kaug, vaug, ga, stats[:, :, 0:3, 0:2 * FOX_HEADS])
        wo = w_out[l].astype(BF16)
        x = _swa_out_call(sinks[l], qb, kd, vd, gb, oa, x, gate, g_post[l][None, :],
                          wo[:FOX_W], wo[FOX_W:])
    return x
```

```python
import functools

import jax
import jax.numpy as jnp
import numpy as np
from jax import lax
from jax.experimental import pallas as pl
from jax.experimental.pallas import tpu as pltpu

D_MODEL = 1024
HEAD_DIM = 64
FOX_HEADS = 8
SWA_Q_HEADS = 8
SWA_KV_HEADS = 2
WINDOW = 128
ROPE_THETA = 10000.0
RMS_EPS = 1e-6
FOX_W = FOX_HEADS * HEAD_DIM
SWA_W = SWA_Q_HEADS * HEAD_DIM
SWA_KV_W = SWA_KV_HEADS * HEAD_DIM

LANES = 128
AUG_W = FOX_HEADS * LANES

OFF_QA = 0
OFF_KA = OFF_QA + FOX_W
OFF_VA = OFF_KA + FOX_W
OFF_ZA = OFF_VA + FOX_W
OFF_QB = OFF_ZA + FOX_W
OFF_ZB = OFF_QB + SWA_W
OFF_KD = OFF_ZB + SWA_W
OFF_VD = OFF_KD + SWA_KV_W
OFF_F = OFF_VD + SWA_KV_W
IN_NP = OFF_F + LANES

TM_IN = 1024
TM_SUB = 512
TQ_FOX = 512
TK_FOX = 512
FIXED_UNROLL = 8
DIAG_UNROLL = 4
TQ_SWA = 1024
NEG_BIG = -1e30
LOG2E = 1.4426950408889634
PRUNE_T = 140.0
NORM_MARGIN = 1.01
FIXED_REF_T = 100.0
VMEM_LIMIT = 56 * 1024 * 1024

F32 = jnp.float32
BF16 = jnp.bfloat16


def _split3(a):
    hi = a.astype(BF16)
    r = a - hi.astype(F32)
    mid = r.astype(BF16)
    lo = (r - mid.astype(F32)).astype(BF16)
    return hi, mid, lo


def _silu(z):
    hz = 0.5 * z
    return hz + hz * jnp.tanh(hz)


def _mod_kernel(c_ref, w_ref, b_ref, o_ref):
    c = c_ref[...]
    sc = c * (1.0 / (1.0 + jnp.exp(-c)))
    o_ref[...] = jnp.dot(sc, w_ref[...], precision=lax.Precision.HIGHEST,
                         preferred_element_type=F32) + b_ref[...]


def _mod_call(c_pad, w_ada, b_ada):
    rows = c_pad.shape[0]
    n = w_ada.shape[1]
    bn = D_MODEL
    return pl.pallas_call(
        _mod_kernel,
        grid=(n // bn,),
        in_specs=[pl.BlockSpec((rows, D_MODEL), lambda j: (0, 0)),
                  pl.BlockSpec((D_MODEL, bn), lambda j: (0, j)),
                  pl.BlockSpec((1, bn), lambda j: (0, j))],
        out_specs=pl.BlockSpec((rows, bn), lambda j: (0, j)),
        out_shape=jax.ShapeDtypeStruct((rows, n), F32),
        name="mod",
    )(c_pad, w_ada, b_ada)


def _inproj_kernel(x_ref, shift_ref, scale_ref, gpre_ref, w_ref, pos_ref, invf_ref, bf_ref,
                   tri_ref, eqk_ref, oq_ref, ok_ref, ind_ref,
                   qaug_ref, kaug_ref, vaug_ref, ga_ref, qb_ref, kd_ref, vd_ref, gb_ref, st_ref,
                   carry_ref):
    tm = TM_SUB

    @pl.when(pl.program_id(1) == 0)
    def _():
        carry_ref[...] = jnp.zeros_like(carry_ref)

    for sub in range(x_ref.shape[1] // tm):
        _inproj_subtile(sub, slice(sub * tm, (sub + 1) * tm),
                        x_ref, shift_ref, scale_ref, gpre_ref, w_ref, pos_ref, invf_ref, bf_ref,
                        tri_ref, eqk_ref, oq_ref, ok_ref, ind_ref,
                        qaug_ref, kaug_ref, vaug_ref, ga_ref, qb_ref, kd_ref, vd_ref, gb_ref,
                        st_ref, carry_ref)


def _inproj_subtile(sub, rows, x_ref, shift_ref, scale_ref, gpre_ref, w_ref, pos_ref, invf_ref,
                    bf_ref, tri_ref, eqk_ref, oq_ref, ok_ref, ind_ref,
                    qaug_ref, kaug_ref, vaug_ref, ga_ref, qb_ref, kd_ref, vd_ref, gb_ref,
                    st_ref, carry_ref):
    tm = TM_SUB
    x = x_ref[0, rows, :]
    ms = jnp.mean(x * x, axis=-1, keepdims=True)
    gain = gpre_ref[...] * (1.0 + scale_ref[0])
    h = (x * lax.rsqrt(ms + RMS_EPS) * gain + shift_ref[0]).astype(BF16)

    def proj(off, width):
        return lax.dot_general(h, w_ref[off:off + width, :], (((1,), (1,)), ((), ())),
                               preferred_element_type=F32)

    tail = proj(OFF_KD, 3 * LANES)
    f = tail[:, OFF_F - OFF_KD:] + bf_ref[...]
    ls = jnp.minimum(f, 0.0) - jnp.log1p(jnp.exp(-jnp.abs(f)))
    lane = lax.broadcasted_iota(jnp.int32, (tm, LANES), 1)

    def by_group(a, b_, c_):
        return jnp.where(lane < FOX_HEADS, a, jnp.where(lane < 2 * FOX_HEADS, b_, c_))

    part = by_group(*(t.astype(F32) for t in _split3(ls))).astype(BF16)
    psum = jnp.dot(tri_ref[...], part, preferred_element_type=F32)
    cum = (psum + pltpu.roll(psum, LANES - FOX_HEADS, 1)
           + pltpu.roll(psum, LANES - 2 * FOX_HEADS, 1))
    cum = cum + carry_ref[...]
    carry_ref[...] = cum[tm - 1:tm, :]

    cum2 = cum * LOG2E
    cum2 = by_group(cum2, pltpu.roll(cum2, FOX_HEADS, 1), pltpu.roll(cum2, 2 * FOX_HEADS, 1))
    cs = by_group(*(t.astype(F32) for t in _split3(cum2))).astype(BF16)
    placed = jnp.dot(cs, eqk_ref[...], preferred_element_type=F32)
    augq = placed * ok_ref[...] + oq_ref[...]
    augk = placed * oq_ref[...] + ok_ref[...]

    lane_w = lax.broadcasted_iota(jnp.int32, (tm, AUG_W), 1)
    data = (((lane_w >> 6) ^ (lane_w >> 7)) & 1) == 0

    def rep(a):
        return jnp.concatenate(
            [a[:, LANES * (hd // 2):LANES * (hd // 2 + 1)] for hd in range(FOX_HEADS)], axis=1)

    qa = proj(OFF_QA, FOX_W)
    ka = proj(OFF_KA, FOX_W)
    qaug_ref[0, :, rows] = jnp.where(data, rep(qa), augq).T.astype(BF16)
    kaug_ref[0, rows, :] = jnp.where(data, rep(ka), augk).astype(BF16)

    sq = jnp.concatenate([qa * qa, ka * ka], axis=1).astype(BF16)
    nrm2 = jnp.dot(sq, ind_ref[...], preferred_element_type=F32)
    nmax = jnp.sqrt(jnp.max(nrm2, axis=0, keepdims=True)) * NORM_MARGIN
    st_ref[0, sub] = jnp.concatenate(
        [nmax, cum2[0:1], cum2[tm - 1:tm], jnp.zeros((5, LANES), F32)], axis=0)
    vaug_ref[0, :, rows] = jnp.where(data, rep(proj(OFF_VA, FOX_W)), 1.0).T.astype(BF16)

    za = proj(OFF_ZA, FOX_W)
    ga_ref[0, rows, :] = _silu(za).astype(BF16)
    zb = proj(OFF_ZB, SWA_W)
    gb_ref[0, rows, :] = _silu(zb).astype(BF16)

    ang_t = invf_ref[...] * pos_ref[0, :, rows].astype(F32)
    reps = LANES // (HEAD_DIM // 2)
    cosv = jnp.concatenate([jnp.cos(ang_t)] * reps, axis=0).T
    sinv = jnp.concatenate([jnp.sin(ang_t)] * reps, axis=0).T
    first = (lane & (HEAD_DIM // 2)) == 0
    sin_signed = jnp.where(first, -sinv, sinv)

    def rope(a):
        outs = []
        for cidx in range(a.shape[1] // LANES):
            blk = a[:, cidx * LANES:(cidx + 1) * LANES]
            other = jnp.where(first, pltpu.roll(blk, LANES - HEAD_DIM // 2, 1),
                              pltpu.roll(blk, HEAD_DIM // 2, 1))
            outs.append(blk * cosv + other * sin_signed)
        return jnp.concatenate(outs, axis=1)

    qb_ref[0, rows, :] = rope(proj(OFF_QB, SWA_W)).astype(BF16)
    kr = rope(tail[:, :LANES])
    ks = pltpu.roll(kr, HEAD_DIM, 1)
    low = lane < HEAD_DIM
    kd_ref[0, rows, :] = jnp.concatenate([jnp.where(low, kr, ks), jnp.where(low, ks, kr)],
                                axis=1).astype(BF16)
    vt = tail[:, OFF_VD - OFF_KD:OFF_F - OFF_KD].T
    ones = jnp.ones((HEAD_DIM, tm), F32)
    vd_ref[0, :, rows] = jnp.concatenate([vt[:HEAD_DIM], ones, vt[HEAD_DIM:], ones],
                                axis=0).astype(BF16)


def _inproj_call(x, shift, scale, g_pre, w_perm, pos3, invf, bf_pad, tri, eqk, oq, ok, ind):
    B, S, D = x.shape
    tm = TM_IN
    row = lambda b, t: (b, t, 0)
    per_b = lambda b, t: (b, 0, 0)
    const2 = lambda b, t: (0, 0)
    out_w = (AUG_W, AUG_W, AUG_W, FOX_W, SWA_W, 2 * SWA_KV_W, 2 * SWA_KV_W, SWA_W)
    transposed = (0, 2, 6)
    return pl.pallas_call(
        _inproj_kernel,
        grid=(B, S // tm),
        in_specs=[pl.BlockSpec((1, tm, D), row),
                  pl.BlockSpec((1, 1, D), per_b),
                  pl.BlockSpec((1, 1, D), per_b),
                  pl.BlockSpec((1, D), const2),
                  pl.BlockSpec((IN_NP, D), const2),
                  pl.BlockSpec((1, 1, tm), lambda b, t: (b, 0, t)),
                  pl.BlockSpec((HEAD_DIM // 2, 1), const2),
                  pl.BlockSpec((1, LANES), const2),
                  pl.BlockSpec((TM_SUB, TM_SUB), const2),
                  pl.BlockSpec((LANES, AUG_W), const2),
                  pl.BlockSpec((1, AUG_W), const2),
                  pl.BlockSpec((1, AUG_W), const2),
                  pl.BlockSpec((2 * FOX_W, LANES), const2)],
        out_specs=[pl.BlockSpec((1, w, tm), lambda b, t: (b, 0, t)) if i in transposed
                   else pl.BlockSpec((1, tm, w), row) for i, w in enumerate(out_w)]
        + [pl.BlockSpec((1, tm // TM_SUB, 8, LANES), lambda b, t: (b, t, 0, 0))],
        out_shape=[jax.ShapeDtypeStruct((B, w, S) if i in transposed else (B, S, w), BF16)
                   for i, w in enumerate(out_w)]
        + [jax.ShapeDtypeStruct((B, S // TM_SUB, 8, LANES), F32)],
        scratch_shapes=[pltpu.VMEM((1, LANES), F32)],
        compiler_params=pltpu.CompilerParams(
            dimension_semantics=("arbitrary", "arbitrary"), vmem_limit_bytes=VMEM_LIMIT),
        name="in_proj",
    )(x, shift, scale, g_pre, w_perm, pos3, invf, bf_pad, tri, eqk, oq, ok, ind)


def _fox_tables(nq):
    diag = [(qi, qi, hh * nq + qi, hh) for hh in range(2) for qi in range(nq)]
    tab = np.array(diag + [(1, 0, 2 * nq, 0)], np.int32).T
    return tab, len(diag), nq * (nq - 1)


def _fox_kernel(tab_ref, st_ref, q_ref, k_ref, v_ref, g_ref, o_ref,
                off_tab, gen_tab, s0, s1, p0, p1, al0, al1, m_st, acc_st, *, n_diag, n_off):
    tq = TQ_FOX
    tk = TK_FOX
    nq = q_ref.shape[2] // tq
    s_buf = (s0, s1)
    p_buf = (p0, p1)
    al_buf = (al0, al1)

    def run(tab, n_items, dummy, diag):
        half = (n_items + 1) // 2
        lens = (half, n_items - half)
        hk, hq = tk // 2, tq // 2

        def item(stream, t):
            return jnp.where(t < lens[stream], stream * half + t, dummy)

        def stage_qk(t, slot):
            for sm in range(2):
                e = item(sm, t)
                qoff = pl.multiple_of(tab[0, e] * tq, tq)
                koff = pl.multiple_of(tab[1, e] * tk, tk)
                hoff = pl.multiple_of(tab[3, e] * LANES, LANES)
                qt = q_ref[0, pl.ds(hoff, LANES), pl.ds(qoff, tq)]
                k = k_ref[0, pl.ds(koff, tk), pl.ds(hoff, LANES)]
                if diag:
                    s_buf[slot][sm, :hk, :] = jnp.dot(k[:hk], qt, preferred_element_type=F32)
                    s_buf[slot][sm, hk:, hq:] = jnp.dot(k[hk:], qt[:, hq:],
                                                        preferred_element_type=F32)
                else:
                    s_buf[slot][sm] = jnp.dot(k, qt, preferred_element_type=F32)

        def stage_softmax(t, slot):
            for sm in range(2):
                st = tab[2, item(sm, t)]
                if diag:
                    kr = lax.broadcasted_iota(jnp.int32, (hk, tq), 0)
                    qc = lax.broadcasted_iota(jnp.int32, (hk, tq), 1)
                    top = jnp.where(kr <= qc, s_buf[slot][sm, :hk, :], NEG_BIG)
                    kr2 = lax.broadcasted_iota(jnp.int32, (tk - hk, tq - hq), 0)
                    qc2 = lax.broadcasted_iota(jnp.int32, (tk - hk, tq - hq), 1)
                    low = jnp.where(kr2 <= qc2, s_buf[slot][sm, hk:, hq:], NEG_BIG)
                    top_a, top_b = top[:, :hq], top[:, hq:]
                    m_a = jnp.max(top_a, axis=0, keepdims=True)
                    m_b = jnp.maximum(jnp.max(top_b, axis=0, keepdims=True),
                                      jnp.max(low, axis=0, keepdims=True))
                    m_next = jnp.concatenate([m_a, m_b], axis=1)
                    p_buf[slot][sm, :hk, :hq] = jnp.exp2(top_a - m_a).astype(BF16)
                    p_buf[slot][sm, :hk, hq:] = jnp.exp2(top_b - m_b).astype(BF16)
                    p_buf[slot][sm, hk:, hq:] = jnp.exp2(low - m_b).astype(BF16)
                else:
                    s = s_buf[slot][sm]
                    m_prev = m_st[st]
                    m_next = jnp.maximum(m_prev, jnp.max(s, axis=0, keepdims=True))
                    al_buf[slot][sm] = jnp.exp2(m_prev - m_next)
                    p_buf[slot][sm] = jnp.exp2(s - m_next).astype(BF16)
                m_st[st] = m_next

        def stage_pv(t, slot):
            for sm in range(2):
                e = item(sm, t)
                koff = pl.multiple_of(tab[1, e] * tk, tk)
                hoff = pl.multiple_of(tab[3, e] * LANES, LANES)
                st = tab[2, e]
                vt = v_ref[0, pl.ds(hoff, LANES), pl.ds(koff, tk)]
                if diag:
                    acc_st[st] = jnp.concatenate(
                        [jnp.dot(vt[:, :hk], p_buf[slot][sm, :hk, :hq],
                                 preferred_element_type=F32),
                         jnp.dot(vt, p_buf[slot][sm, :, hq:], preferred_element_type=F32)],
                        axis=1)
                else:
                    pv = jnp.dot(vt, p_buf[slot][sm], preferred_element_type=F32)
                    acc_st[st] = al_buf[slot][sm] * acc_st[st] + pv

        stage_qk(0, 0)
        stage_qk(1, 1)
        stage_softmax(0, 0)

        unroll = DIAG_UNROLL if diag else 2

        def body(u, carry):
            for d in range(unroll):
                t = unroll * u + 1 + d
                stage_qk(t + 1, d % 2)
                stage_softmax(t, (d + 1) % 2)
                stage_pv(t - 1, d % 2)
            return carry

        lax.fori_loop(0, (half + unroll - 1) // unroll, body, 0)

    def run_fixed(tab, n_items, dummy):
        half = (n_items + 1) // 2
        lens = (half, n_items - half)

        def item(stream, t):
            return jnp.where(t < lens[stream], stream * half + t, dummy)

        def stage_probs(t, slot):
            for sm in range(2):
                e = item(sm, t)
                qoff = pl.multiple_of(tab[0, e] * tq, tq)
                koff = pl.multiple_of(tab[1, e] * tk, tk)
                hoff = pl.multiple_of(tab[3, e] * LANES, LANES)
                qt = q_ref[0, pl.ds(hoff, LANES), pl.ds(qoff, tq)]
                k = k_ref[0, pl.ds(koff, tk), pl.ds(hoff, LANES)]
                s = jnp.dot(k, qt, preferred_element_type=F32)
                p_buf[slot][sm] = jnp.exp2(s - m_st[tab[2, e]]).astype(BF16)

        def stage_pv(t, slot):
            for sm in range(2):
                e = item(sm, t)
                koff = pl.multiple_of(tab[1, e] * tk, tk)
                hoff = pl.multiple_of(tab[3, e] * LANES, LANES)
                st = tab[2, e]
                vt = v_ref[0, pl.ds(hoff, LANES), pl.ds(koff, tk)]
                acc_st[st] = acc_st[st] + jnp.dot(vt, p_buf[slot][sm],
                                                  preferred_element_type=F32)

        stage_probs(0, 0)

        def trips(first, unroll):
            def body(u, carry):
                for d in range(unroll):
                    t = first + unroll * u + 1 + d
                    stage_probs(t, (d + 1) % 2)
                    stage_pv(t - 1, d % 2)
                return carry
            return body

        n_long = half // FIXED_UNROLL
        done = n_long * FIXED_UNROLL
        lax.fori_loop(0, n_long, trips(0, FIXED_UNROLL), 0)
        lax.fori_loop(0, (half - done + 1) // 2, trips(done, 2), 0)

    assert FIXED_UNROLL % 2 == 0 and tab_ref.shape[1] == n_diag + 1
    run(tab_ref, n_diag, n_diag, True)

    b = pl.program_id(0)
    pr = pl.program_id(1)

    def put(tab, idx, qrow, j, state, head):
        tab[0, idx] = qrow
        tab[1, idx] = j
        tab[2, idx] = state
        tab[3, idx] = head

    n_fix = jnp.int32(0)
    n_gen = jnp.int32(0)
    for hh in range(2):
        hd = 2 * pr + hh

        kmax = lax.fori_loop(
            0, nq, lambda j, m, hd=hd: jnp.maximum(m, st_ref[b, j, 0, FOX_HEADS + hd]),
            jnp.float32(0.0))

        def list_row(qi, counts, hh=hh, hd=hd, kmax=kmax):
            base = (st_ref[b, qi, 0, hd] * (st_ref[b, qi, 0, FOX_HEADS + hd] + kmax)
                    + st_ref[b, qi, 1, hd])
            fixed_ok = base - st_ref[b, qi - 1, 2, hd] <= FIXED_REF_T

            def scan(tab, enabled, cnt):
                def contributes(carry):
                    j, _ = carry
                    bound = base - st_ref[b, jnp.maximum(j, 0), 2, hd]
                    return jnp.logical_and(jnp.logical_and(enabled, j >= 0), bound > -PRUNE_T)

                def take(carry):
                    j, cnt = carry
                    put(tab, cnt, qi, j, hh * nq + qi, hh)
                    return j - 1, cnt + 1

                return lax.while_loop(contributes, take, (qi - 1, cnt))[1]

            return (scan(off_tab, fixed_ok, counts[0]),
                    scan(gen_tab, jnp.logical_not(fixed_ok), counts[1]))

        n_fix, n_gen = lax.fori_loop(1, nq, list_row, (n_fix, n_gen))
    put(off_tab, n_off, 1, 0, 2 * nq, 0)
    put(gen_tab, n_off, 1, 0, 2 * nq, 0)
    run_fixed(off_tab, n_fix, n_off)

    @pl.when(n_gen > 0)
    def _():
        run(gen_tab, n_gen, n_off, False)

    def finish(qi, carry):
        a0 = acc_st[qi]
        a1 = acc_st[nq + qi]
        ot = jnp.concatenate([a0[:HEAD_DIM] / a0[HEAD_DIM:HEAD_DIM + 1],
                              a1[HEAD_DIM:] / a1[0:1]], axis=0)
        rows = pl.ds(pl.multiple_of(qi * tq, tq), tq)
        o_ref[0, rows, :] = (ot.T * g_ref[0, rows, :].astype(F32)).astype(BF16)
        return carry

    lax.fori_loop(0, nq, finish, 0)


def _fox_call(qaug_t, kaug, vaug_t, ga, stats):
    B, S, _ = kaug.shape
    tq, tk = TQ_FOX, TK_FOX
    assert tq == tk
    nq = S // tq
    tab, n_diag, n_off = _fox_tables(nq)
    pair = lambda b, p: (b, 0, p)
    return pl.pallas_call(
        functools.partial(_fox_kernel, n_diag=n_diag, n_off=n_off),
        grid=(B, FOX_HEADS // 2),
        in_specs=[pl.BlockSpec(memory_space=pltpu.SMEM),
                  pl.BlockSpec(memory_space=pltpu.SMEM),
                  pl.BlockSpec((1, 2 * LANES, S), lambda b, p: (b, p, 0)),
                  pl.BlockSpec((1, S, 2 * LANES), pair),
                  pl.BlockSpec((1, 2 * LANES, S), lambda b, p: (b, p, 0)),
                  pl.BlockSpec((1, S, LANES), pair)],
        out_specs=pl.BlockSpec((1, S, LANES), pair),
        out_shape=jax.ShapeDtypeStruct((B, S, FOX_W), BF16),
        scratch_shapes=[pltpu.SMEM((4, n_off + 1), jnp.int32),
                        pltpu.SMEM((4, n_off + 1), jnp.int32),
                        pltpu.VMEM((2, tk, tq), F32), pltpu.VMEM((2, tk, tq), F32),
                        pltpu.VMEM((2, tk, tq), BF16), pltpu.VMEM((2, tk, tq), BF16),
                        pltpu.VMEM((2, 1, tq), F32), pltpu.VMEM((2, 1, tq), F32),
                        pltpu.VMEM((2 * nq + 1, 1, tq), F32),
                        pltpu.VMEM((2 * nq + 1, LANES, tq), F32)],
        compiler_params=pltpu.CompilerParams(
            dimension_semantics=("arbitrary", "arbitrary"), vmem_limit_bytes=VMEM_LIMIT),
        name="fox",
    )(jnp.asarray(tab), stats, qaug_t, kaug, vaug_t, ga)


def _swa_out_kernel(sinks_ref, q_ref, kp_ref, kc_ref, vtp_ref, vtc_ref, g_ref,
                    oa_ref, x_ref, gate_ref, gpost_ref, wa_ref, wb_ref, out_ref,
                    s_sc, p_sc, e_sc, ob_sc):
    i = pl.program_id(1)
    nsub = TQ_SWA // WINDOW
    group = SWA_Q_HEADS // SWA_KV_HEADS
    ncol = group * WINDOW
    kall = jnp.concatenate([kp_ref[0], kc_ref[0]], axis=0)
    vtall = jnp.concatenate([vtp_ref[0], vtc_ref[0]], axis=1)
    lane = lax.broadcasted_iota(jnp.int32, (WINDOW, LANES), 1)
    lo = lane < HEAD_DIM
    zero = jnp.zeros((WINDOW, LANES), BF16)
    kj_ = lax.broadcasted_iota(jnp.int32, (2 * WINDOW, ncol), 0)
    qi_ = lax.broadcasted_iota(jnp.int32, (2 * WINDOW, ncol), 1) & (WINDOW - 1)
    rel = qi_ + WINDOW - kj_
    band = (rel >= 0) & (rel < WINDOW)
    nt = (((1,), (1,)), ((), ()))
    probs = [(r, g) for r in range(nsub) for g in range(SWA_KV_HEADS)]

    def stage_qk(n):
        r, g = probs[n]
        rows = slice(r * WINDOW, (r + 1) * WINDOW)
        kg = kall[r * WINDOW:(r + 2) * WINDOW, g * LANES:(g + 1) * LANES]
        c0 = g * group * HEAD_DIM
        qp0 = q_ref[0, rows, c0:c0 + LANES]
        qp1 = q_ref[0, rows, c0 + LANES:c0 + 2 * LANES]
        qs = jnp.concatenate([jnp.where(lo, qp0, zero), jnp.where(lo, qp1, zero),
                              jnp.where(lo, zero, qp0), jnp.where(lo, zero, qp1)], axis=0)
        s_sc[n] = lax.dot_general(kg, qs, nt, preferred_element_type=F32)

    def stage_softmax(n):
        r, g = probs[n]
        valid = band & (kj_ >= jnp.where(i == 0, WINDOW, 0)) if r == 0 else band
        st = jnp.where(valid, s_sc[n], NEG_BIG)
        heads = (group * g, group * g + 2, group * g + 1, group * g + 3)
        sink = jnp.concatenate(
            [jnp.full((1, WINDOW), sinks_ref[hd] * LOG2E, F32) for hd in heads], axis=1)
        m = jnp.maximum(jnp.max(st, axis=0, keepdims=True), sink)
        p_sc[n] = jnp.exp2(st - m).astype(BF16)
        e_sc[n] = jnp.exp2(sink - m)

    def stage_pv(n):
        r, g = probs[n]
        rows = slice(r * WINDOW, (r + 1) * WINDOW)
        c0 = g * group * HEAD_DIM
        vtg = vtall[g * LANES:(g + 1) * LANES, r * WINDOW:(r + 2) * WINDOW]
        acc = jnp.dot(vtg, p_sc[n], preferred_element_type=F32)
        l = acc[HEAD_DIM:HEAD_DIM + 1] + e_sc[n]
        on = acc[:HEAD_DIM] * (1.0 / l)
        pair0 = jnp.concatenate([on[:, 0:WINDOW], on[:, 2 * WINDOW:3 * WINDOW]], axis=0).T
        pair1 = jnp.concatenate([on[:, WINDOW:2 * WINDOW], on[:, 3 * WINDOW:]], axis=0).T
        ob_sc[rows, c0:c0 + LANES] = (
            pair0 * g_ref[0, rows, c0:c0 + LANES].astype(F32)).astype(BF16)
        ob_sc[rows, c0 + LANES:c0 + 2 * LANES] = (
            pair1 * g_ref[0, rows, c0 + LANES:c0 + 2 * LANES].astype(F32)).astype(BF16)

    for t in range(len(probs) + 2):
        if t < len(probs):
            stage_qk(t)
        if 1 <= t <= len(probs):
            stage_softmax(t - 1)
        if t >= 2:
            stage_pv(t - 2)

    y = (jnp.dot(oa_ref[0], wa_ref[...], preferred_element_type=F32)
         + jnp.dot(ob_sc[...], wb_ref[...], preferred_element_type=F32))
    ms = jnp.mean(y * y, axis=-1, keepdims=True)
    yn = y * lax.rsqrt(ms + RMS_EPS) * gpost_ref[...]
    out_ref[0] = x_ref[0] + gate_ref[0] * yn


def _swa_out_call(sinks, qb, kd, vdt, gb, oa, x, gate, g_post, wa, wb):
    B, S, D = x.shape
    tq = TQ_SWA
    nsub = tq // WINDOW
    cur = lambda b, i: (b, i, 0)
    prev = lambda b, i: (b, jnp.maximum(i * nsub - 1, 0), 0)
    cur_t = lambda b, i: (b, 0, i)
    prev_t = lambda b, i: (b, 0, jnp.maximum(i * nsub - 1, 0))
    const2 = lambda b, i: (0, 0)
    kvw = 2 * SWA_KV_W
    nprob = nsub * SWA_KV_HEADS
    ncol = SWA_Q_HEADS // SWA_KV_HEADS * WINDOW
    return pl.pallas_call(
        _swa_out_kernel,
        grid=(B, S // tq),
        in_specs=[pl.BlockSpec(memory_space=pltpu.SMEM),
                  pl.BlockSpec((1, tq, SWA_W), cur),
                  pl.BlockSpec((1, WINDOW, kvw), prev),
                  pl.BlockSpec((1, tq, kvw), cur),
                  pl.BlockSpec((1, kvw, WINDOW), prev_t),
                  pl.BlockSpec((1, kvw, tq), cur_t),
                  pl.BlockSpec((1, tq, SWA_W), cur),
                  pl.BlockSpec((1, tq, FOX_W), cur),
                  pl.BlockSpec((1, tq, D), cur),
                  pl.BlockSpec((1, 1, D), lambda b, i: (b, 0, 0)),
                  pl.BlockSpec((1, D), const2),
                  pl.BlockSpec((FOX_W, D), const2),
                  pl.BlockSpec((SWA_W, D), const2)],
        out_specs=pl.BlockSpec((1, tq, D), cur),
        out_shape=jax.ShapeDtypeStruct((B, S, D), F32),
        scratch_shapes=[pltpu.VMEM((nprob, 2 * WINDOW, ncol), F32),
                        pltpu.VMEM((nprob, 2 * WINDOW, ncol), BF16),
                        pltpu.VMEM((nprob, 1, ncol), F32),
                        pltpu.VMEM((tq, SWA_W), BF16)],
        compiler_params=pltpu.CompilerParams(
            dimension_semantics=("arbitrary", "arbitrary"), vmem_limit_bytes=VMEM_LIMIT),
        name="swa_out",
    )(sinks, qb, kd, kd, vdt, vdt, gb, oa, x, gate, g_post, wa, wb)


def _perm_w_in(w):
    wt = w.T
    sc = HEAD_DIM ** -0.5
    o = 0
    qa = wt[o:o + FOX_W]; o += FOX_W
    ka = wt[o:o + FOX_W]; o += FOX_W
    va = wt[o:o + FOX_W]; o += FOX_W
    fa = wt[o:o + FOX_HEADS]; o += FOX_HEADS
    za = wt[o:o + FOX_W]; o += FOX_W
    qb = wt[o:o + SWA_W]; o += SWA_W
    kb = wt[o:o + SWA_KV_W]; o += SWA_KV_W
    vb = wt[o:o + SWA_KV_W]; o += SWA_KV_W
    zb = wt[o:o + SWA_W]
    fpad = jnp.concatenate(
        [fa, fa, fa, jnp.zeros((LANES - 3 * FOX_HEADS, wt.shape[1]), w.dtype)], axis=0)
    sc2 = sc * LOG2E
    return jnp.concatenate([qa * sc2, ka, va, za, qb * sc2, zb, kb, vb, fpad],
                           axis=0).astype(BF16)


def _aug_constants():
    eq = np.zeros((LANES, AUG_W), np.float32)
    ek = np.zeros((LANES, AUG_W), np.float32)
    oq = np.zeros((1, AUG_W), np.float32)
    ok = np.zeros((1, AUG_W), np.float32)
    for hd in range(FOX_HEADS):
        base = LANES * hd + (HEAD_DIM if hd % 2 == 0 else 0)
        for part in range(3):
            eq[part * FOX_HEADS + hd, base + part] = 1.0
            ok[0, base + part] = 1.0
            ek[part * FOX_HEADS + hd, base + 3 + part] = -1.0
            oq[0, base + 3 + part] = 1.0
    return jnp.asarray(eq + ek, BF16), jnp.asarray(oq), jnp.asarray(ok)


def kernel(x, c, positions, w_ada, b_ada, g_pre, w_in, b_fgate, sinks, w_out, g_post):
    B, S, D = x.shape
    depth = w_ada.shape[0]
    assert TM_SUB == TQ_FOX == TK_FOX
    half = HEAD_DIM // 2
    inv_freq = ROPE_THETA ** (-jnp.arange(half, dtype=F32) / half)
    invf = inv_freq[:, None]
    pos3 = positions[:, None, :]
    ind = np.zeros((2 * FOX_W, LANES), np.float32)
    ind[np.arange(2 * FOX_W), np.arange(2 * FOX_W) // HEAD_DIM] = 1.0
    ind = jnp.asarray(ind, BF16)
    tri = jnp.asarray(np.tril(np.ones((TM_SUB, TM_SUB), np.float32)), BF16)
    eqk, oq, ok = _aug_constants()
    c_pad = jnp.zeros((8, D), F32).at[:B].set(c)
    for l in range(depth):
        mod = _mod_call(c_pad, w_ada[l], b_ada[l][None, :])[:B]
        shift = mod[:, None, 0:D]
        scale = mod[:, None, D:2 * D]
        gate = mod[:, None, 2 * D:3 * D]
        bf_pad = jnp.concatenate(
            [b_fgate[l]] * 3 + [jnp.zeros((LANES - 3 * FOX_HEADS,), F32)])[None, :]
        qaug, kaug, vaug, ga, qb, kd, vd, gb, stats = _inproj_call(
            x, shift, scale, g_pre[l][None, :], _perm_w_in(w_in[l]), pos3, invf, bf_pad,
            tri, eqk, oq, ok, ind)
        oa = _fox_call(qaug, kaug, vaug, ga, stats[:, :, 0:3, 0:2 * FOX_HEADS])
        wo = w_out[l].astype(BF16)
        x = _swa_out_call(sinks[l], qb, kd, vd, gb, oa, x, gate, g_post[l][None, :],
                          wo[:FOX_W], wo[FOX_W:])
    return x
```

```python
import functools

import jax
import jax.numpy as jnp
import numpy as np
from jax import lax
from jax.experimental import pallas as pl
from jax.experimental.pallas import tpu as pltpu

D_MODEL = 1024
HEAD_DIM = 64
FOX_HEADS = 8
SWA_Q_HEADS = 8
SWA_KV_HEADS = 2
WINDOW = 128
ROPE_THETA = 10000.0
RMS_EPS = 1e-6
FOX_W = FOX_HEADS * HEAD_DIM
SWA_W = SWA_Q_HEADS * HEAD_DIM
SWA_KV_W = SWA_KV_HEADS * HEAD_DIM

LANES = 128
AUG_W = FOX_HEADS * LANES

OFF_QA = 0
OFF_KA = OFF_QA + FOX_W
OFF_VA = OFF_KA + FOX_W
OFF_ZA = OFF_VA + FOX_W
OFF_QB = OFF_ZA + FOX_W
OFF_ZB = OFF_QB + SWA_W
OFF_KD = OFF_ZB + SWA_W
OFF_VD = OFF_KD + SWA_KV_W
OFF_F = OFF_VD + SWA_KV_W
IN_NP = OFF_F + LANES

TM_IN = 1024
TM_SUB = 512
TQ_FOX = 512
TK_FOX = 512
FIXED_UNROLL = 8
DIAG_UNROLL = 4
TQ_SWA = 1024
NEG_BIG = -1e30
LOG2E = 1.4426950408889634
PRUNE_T = 140.0
NORM_MARGIN = 1.01
FIXED_REF_T = 100.0
VMEM_LIMIT = 56 * 1024 * 1024

F32 = jnp.float32
BF16 = jnp.bfloat16


def _split3(a):
    hi = a.astype(BF16)
    r = a - hi.astype(F32)
    mid = r.astype(BF16)
    lo = (r - mid.astype(F32)).astype(BF16)
    return hi, mid, lo


def _silu(z):
    hz = 0.5 * z
    return hz + hz * jnp.tanh(hz)


def _mod_kernel(c_ref, w_ref, b_ref, o_ref):
    c = c_ref[...]
    sc = c * (1.0 / (1.0 + jnp.exp(-c)))
    o_ref[...] = jnp.dot(sc, w_ref[...], precision=lax.Precision.HIGHEST,
                         preferred_element_type=F32) + b_ref[...]


def _mod_call(c_pad, w_ada, b_ada):
    rows = c_pad.shape[0]
    n = w_ada.shape[1]
    bn = D_MODEL
    return pl.pallas_call(
        _mod_kernel,
        grid=(n // bn,),
        in_specs=[pl.BlockSpec((rows, D_MODEL), lambda j: (0, 0)),
                  pl.BlockSpec((D_MODEL, bn), lambda j: (0, j)),
                  pl.BlockSpec((1, bn), lambda j: (0, j))],
        out_specs=pl.BlockSpec((rows, bn), lambda j: (0, j)),
        out_shape=jax.ShapeDtypeStruct((rows, n), F32),
        name="mod",
    )(c_pad, w_ada, b_ada)


def _inproj_kernel(x_ref, shift_ref, scale_ref, gpre_ref, w_ref, pos_ref, invf_ref, bf_ref,
                   tri_ref, eqk_ref, oq_ref, ok_ref, ind_ref,
                   qaug_ref, kaug_ref, vaug_ref, ga_ref, qb_ref, kd_ref, vd_ref, gb_ref, st_ref,
                   carry_ref):
    tm = TM_SUB

    @pl.when(pl.program_id(1) == 0)
    def _():
        carry_ref[...] = jnp.zeros_like(carry_ref)

    for sub in range(x_ref.shape[1] // tm):
        _inproj_subtile(sub, slice(sub * tm, (sub + 1) * tm),
                        x_ref, shift_ref, scale_ref, gpre_ref, w_ref, pos_ref, invf_ref, bf_ref,
                        tri_ref, eqk_ref, oq_ref, ok_ref, ind_ref,
                        qaug_ref, kaug_ref, vaug_ref, ga_ref, qb_ref, kd_ref, vd_ref, gb_ref,
                        st_ref, carry_ref)


def _inproj_subtile(sub, rows, x_ref, shift_ref, scale_ref, gpre_ref, w_ref, pos_ref, invf_ref,
                    bf_ref, tri_ref, eqk_ref, oq_ref, ok_ref, ind_ref,
                    qaug_ref, kaug_ref, vaug_ref, ga_ref, qb_ref, kd_ref, vd_ref, gb_ref,
                    st_ref, carry_ref):
    tm = TM_SUB
    x = x_ref[0, rows, :]
    ms = jnp.mean(x * x, axis=-1, keepdims=True)
    gain = gpre_ref[...] * (1.0 + scale_ref[0])
    h = (x * lax.rsqrt(ms + RMS_EPS) * gain + shift_ref[0]).astype(BF16)

    def proj(off, width):
        return lax.dot_general(h, w_ref[off:off + width, :], (((1,), (1,)), ((), ())),
                               preferred_element_type=F32)

    tail = proj(OFF_KD, 3 * LANES)
    f = tail[:, OFF_F - OFF_KD:] + bf_ref[...]
    ls = jnp.minimum(f, 0.0) - jnp.log1p(jnp.exp(-jnp.abs(f)))
    lane = lax.broadcasted_iota(jnp.int32, (tm, LANES), 1)

    def by_group(a, b_, c_):
        return jnp.where(lane < FOX_HEADS, a, jnp.where(lane < 2 * FOX_HEADS, b_, c_))

    part = by_group(*(t.astype(F32) for t in _split3(ls))).astype(BF16)
    psum = jnp.dot(tri_ref[...], part, preferred_element_type=F32)
    cum = (psum + pltpu.roll(psum, LANES - FOX_HEADS, 1)
           + pltpu.roll(psum, LANES - 2 * FOX_HEADS, 1))
    cum = cum + carry_ref[...]
    carry_ref[...] = cum[tm - 1:tm, :]

    cum2 = cum * LOG2E
    cum2 = by_group(cum2, pltpu.roll(cum2, FOX_HEADS, 1), pltpu.roll(cum2, 2 * FOX_HEADS, 1))
    cs = by_group(*(t.astype(F32) for t in _split3(cum2))).astype(BF16)
    placed = jnp.dot(cs, eqk_ref[...], preferred_element_type=F32)
    augq = placed * ok_ref[...] + oq_ref[...]
    augk = placed * oq_ref[...] + ok_ref[...]

    lane_w = lax.broadcasted_iota(jnp.int32, (tm, AUG_W), 1)
    data = (((lane_w >> 6) ^ (lane_w >> 7)) & 1) == 0

    def rep(a):
        return jnp.concatenate(
            [a[:, LANES * (hd // 2):LANES * (hd // 2 + 1)] for hd in range(FOX_HEADS)], axis=1)

    qa = proj(OFF_QA, FOX_W)
    ka = proj(OFF_KA, FOX_W)
    qaug_ref[0, :, rows] = jnp.where(data, rep(qa), augq).T.astype(BF16)
    kaug_ref[0, rows, :] = jnp.where(data, rep(ka), augk).astype(BF16)

    sq = jnp.concatenate([qa * qa, ka * ka], axis=1).astype(BF16)
    nrm2 = jnp.dot(sq, ind_ref[...], preferred_element_type=F32)
    nmax = jnp.sqrt(jnp.max(nrm2, axis=0, keepdims=True)) * NORM_MARGIN
    st_ref[0, sub] = jnp.concatenate(
        [nmax, cum2[0:1], cum2[tm - 1:tm], jnp.zeros((5, LANES), F32)], axis=0)
    vaug_ref[0, :, rows] = jnp.where(data, rep(proj(OFF_VA, FOX_W)), 1.0).T.astype(BF16)

    za = proj(OFF_ZA, FOX_W)
    ga_ref[0, rows, :] = _silu(za).astype(BF16)
    zb = proj(OFF_ZB, SWA_W)
    gb_ref[0, rows, :] = _silu(zb).astype(BF16)

    ang_t = invf_ref[...] * pos_ref[0, :, rows].astype(F32)
    reps = LANES // (HEAD_DIM // 2)
    cosv = jnp.concatenate([jnp.cos(ang_t)] * reps, axis=0).T
    sinv = jnp.concatenate([jnp.sin(ang_t)] * reps, axis=0).T
    first = (lane & (HEAD_DIM // 2)) == 0
    sin_signed = jnp.where(first, -sinv, sinv)

    def rope(a):
        outs = []
        for cidx in range(a.shape[1] // LANES):
            blk = a[:, cidx * LANES:(cidx + 1) * LANES]
            other = jnp.where(first, pltpu.roll(blk, LANES - HEAD_DIM // 2, 1),
                              pltpu.roll(blk, HEAD_DIM // 2, 1))
            outs.append(blk * cosv + other * sin_signed)
        return jnp.concatenate(outs, axis=1)

    qb_ref[0, rows, :] = rope(proj(OFF_QB, SWA_W)).astype(BF16)
    kr = rope(tail[:, :LANES])
    ks = pltpu.roll(kr, HEAD_DIM, 1)
    low = lane < HEAD_DIM
    kd_ref[0, rows, :] = jnp.concatenate([jnp.where(low, kr, ks), jnp.where(low, ks, kr)],
                                axis=1).astype(BF16)
    vt = tail[:, OFF_VD - OFF_KD:OFF_F - OFF_KD].T
    ones = jnp.ones((HEAD_DIM, tm), F32)
    vd_ref[0, :, rows] = jnp.concatenate([vt[:HEAD_DIM], ones, vt[HEAD_DIM:], ones],
                                axis=0).astype(BF16)


def _inproj_call(x, shift, scale, g_pre, w_perm, pos3, invf, bf_pad, tri, eqk, oq, ok, ind):
    B, S, D = x.shape
    tm = TM_IN
    row = lambda b, t: (b, t, 0)
    per_b = lambda b, t: (b, 0, 0)
    const2 = lambda b, t: (0, 0)
    out_w = (AUG_W, AUG_W, AUG_W, FOX_W, SWA_W, 2 * SWA_KV_W, 2 * SWA_KV_W, SWA_W)
    transposed = (0, 2, 6)
    return pl.pallas_call(
        _inproj_kernel,
        grid=(B, S // tm),
        in_specs=[pl.BlockSpec((1, tm, D), row),
                  pl.BlockSpec((1, 1, D), per_b),
                  pl.BlockSpec((1, 1, D), per_b),
                  pl.BlockSpec((1, D), const2),
                  pl.BlockSpec((IN_NP, D), const2),
                  pl.BlockSpec((1, 1, tm), lambda b, t: (b, 0, t)),
                  pl.BlockSpec((HEAD_DIM // 2, 1), const2),
                  pl.BlockSpec((1, LANES), const2),
                  pl.BlockSpec((TM_SUB, TM_SUB), const2),
                  pl.BlockSpec((LANES, AUG_W), const2),
                  pl.BlockSpec((1, AUG_W), const2),
                  pl.BlockSpec((1, AUG_W), const2),
                  pl.BlockSpec((2 * FOX_W, LANES), const2)],
        out_specs=[pl.BlockSpec((1, w, tm), lambda b, t: (b, 0, t)) if i in transposed
                   else pl.BlockSpec((1, tm, w), row) for i, w in enumerate(out_w)]
        + [pl.BlockSpec((1, tm // TM_SUB, 8, LANES), lambda b, t: (b, t, 0, 0))],
        out_shape=[jax.ShapeDtypeStruct((B, w, S) if i in transposed else (B, S, w), BF16)
                   for i, w in enumerate(out_w)]
        + [jax.ShapeDtypeStruct((B, S // TM_SUB, 8, LANES), F32)],
        scratch_shapes=[pltpu.VMEM((1, LANES), F32)],
        compiler_params=pltpu.CompilerParams(
            dimension_semantics=("arbitrary", "arbitrary"), vmem_limit_bytes=VMEM_LIMIT),
        name="in_proj",
    )(x, shift, scale, g_pre, w_perm, pos3, invf, bf_pad, tri, eqk, oq, ok, ind)


def _fox_tables(nq):
    diag = [(qi, qi, hh * nq + qi, hh) for hh in range(2) for qi in range(nq)]
    tab = np.array(diag + [(1, 0, 2 * nq, 0)], np.int32).T
    return tab, len(diag), nq * (nq - 1)


def _fox_kernel(tab_ref, st_ref, q_ref, k_ref, v_ref, g_ref, o_ref,
                off_tab, gen_tab, s0, s1, p0, p1, al0, al1, m_st, acc_st, *, n_diag, n_off):
    tq = TQ_FOX
    tk = TK_FOX
    nq = q_ref.shape[2] // tq
    s_buf = (s0, s1)
    p_buf = (p0, p1)
    al_buf = (al0, al1)

    def run(tab, n_items, dummy, diag):
        half = (n_items + 1) // 2
        lens = (half, n_items - half)
        hk, hq = tk // 2, tq // 2

        def item(stream, t):
            return jnp.where(t < lens[stream], stream * half + t, dummy)

        def stage_qk(t, slot):
            for sm in range(2):
                e = item(sm, t)
                qoff = pl.multiple_of(tab[0, e] * tq, tq)
                koff = pl.multiple_of(tab[1, e] * tk, tk)
                hoff = pl.multiple_of(tab[3, e] * LANES, LANES)
                qt = q_ref[0, pl.ds(hoff, LANES), pl.ds(qoff, tq)]
                k = k_ref[0, pl.ds(koff, tk), pl.ds(hoff, LANES)]
                if diag:
                    s_buf[slot][sm, :hk, :] = jnp.dot(k[:hk], qt, preferred_element_type=F32)
                    s_buf[slot][sm, hk:, hq:] = jnp.dot(k[hk:], qt[:, hq:],
                                                        preferred_element_type=F32)
                else:
                    s_buf[slot][sm] = jnp.dot(k, qt, preferred_element_type=F32)

        def stage_softmax(t, slot):
            for sm in range(2):
                st = tab[2, item(sm, t)]
                if diag:
                    kr = lax.broadcasted_iota(jnp.int32, (hk, tq), 0)
                    qc = lax.broadcasted_iota(jnp.int32, (hk, tq), 1)
                    top = jnp.where(kr <= qc, s_buf[slot][sm, :hk, :], NEG_BIG)
                    kr2 = lax.broadcasted_iota(jnp.int32, (tk - hk, tq - hq), 0)
                    qc2 = lax.broadcasted_iota(jnp.int32, (tk - hk, tq - hq), 1)
                    low = jnp.where(kr2 <= qc2, s_buf[slot][sm, hk:, hq:], NEG_BIG)
                    top_a, top_b = top[:, :hq], top[:, hq:]
                    m_a = jnp.max(top_a, axis=0, keepdims=True)
                    m_b = jnp.maximum(jnp.max(top_b, axis=0, keepdims=True),
                                      jnp.max(low, axis=0, keepdims=True))
                    m_next = jnp.concatenate([m_a, m_b], axis=1)
                    p_buf[slot][sm, :hk, :hq] = jnp.exp2(top_a - m_a).astype(BF16)
                    p_buf[slot][sm, :hk, hq:] = jnp.exp2(top_b - m_b).astype(BF16)
                    p_buf[slot][sm, hk:, hq:] = jnp.exp2(low - m_b).astype(BF16)
                else:
                    s = s_buf[slot][sm]
                    m_prev = m_st[st]
                    m_next = jnp.maximum(m_prev, jnp.max(s, axis=0, keepdims=True))
                    al_buf[slot][sm] = jnp.exp2(m_prev - m_next)
                    p_buf[slot][sm] = jnp.exp2(s - m_next).astype(BF16)
                m_st[st] = m_next

        def stage_pv(t, slot):
            for sm in range(2):
                e = item(sm, t)
                koff = pl.multiple_of(tab[1, e] * tk, tk)
                hoff = pl.multiple_of(tab[3, e] * LANES, LANES)
                st = tab[2, e]
                vt = v_ref[0, pl.ds(hoff, LANES), pl.ds(koff, tk)]
                if diag:
                    acc_st[st] = jnp.concatenate(
                        [jnp.dot(vt[:, :hk], p_buf[slot][sm, :hk, :hq],
                                 preferred_element_type=F32),
                         jnp.dot(vt, p_buf[slot][sm, :, hq:], preferred_element_type=F32)],
                        axis=1)
                else:
                    pv = jnp.dot(vt, p_buf[slot][sm], preferred_element_type=F32)
                    acc_st[st] = al_buf[slot][sm] * acc_st[st] + pv

        stage_qk(0, 0)
        stage_qk(1, 1)
        stage_softmax(0, 0)

        unroll = DIAG_UNROLL if diag else 2

        def body(u, carry):
            for d in range(unroll):
                t = unroll * u + 1 + d
                stage_qk(t + 1, d % 2)
                stage_softmax(t, (d + 1) % 2)
                stage_pv(t - 1, d % 2)
            return carry

        lax.fori_loop(0, (half + unroll - 1) // unroll, body, 0)

    def run_fixed(tab, n_items, dummy):
        half = (n_items + 1) // 2
        lens = (half, n_items - half)

        def item(stream, t):
            return jnp.where(t < lens[stream], stream * half + t, dummy)

        def stage_probs(t, slot):
            for sm in range(2):
                e = item(sm, t)
                qoff = pl.multiple_of(tab[0, e] * tq, tq)
                koff = pl.multiple_of(tab[1, e] * tk, tk)
                hoff = pl.multiple_of(tab[3, e] * LANES, LANES)
                qt = q_ref[0, pl.ds(hoff, LANES), pl.ds(qoff, tq)]
                k = k_ref[0, pl.ds(koff, tk), pl.ds(hoff, LANES)]
                s = jnp.dot(k, qt, preferred_element_type=F32)
                p_buf[slot][sm] = jnp.exp2(s - m_st[tab[2, e]]).astype(BF16)

        def stage_pv(t, slot):
            for sm in range(2):
                e = item(sm, t)
                koff = pl.multiple_of(tab[1, e] * tk, tk)
                hoff = pl.multiple_of(tab[3, e] * LANES, LANES)
                st = tab[2, e]
                vt = v_ref[0, pl.ds(hoff, LANES), pl.ds(koff, tk)]
                acc_st[st] = acc_st[st] + jnp.dot(vt, p_buf[slot][sm],
                                                  preferred_element_type=F32)

        stage_probs(0, 0)

        def trips(first, unroll):
            def body(u, carry):
                for d in range(unroll):
                    t = first + unroll * u + 1 + d
                    stage_probs(t, (d + 1) % 2)
                    stage_pv(t - 1, d % 2)
                return carry
            return body

        n_long = half // FIXED_UNROLL
        done = n_long * FIXED_UNROLL
        lax.fori_loop(0, n_long, trips(0, FIXED_UNROLL), 0)
        lax.fori_loop(0, (half - done + 1) // 2, trips(done, 2), 0)

    assert FIXED_UNROLL % 2 == 0 and tab_ref.shape[1] == n_diag + 1
    run(tab_ref, n_diag, n_diag, True)

    b = pl.program_id(0)
    pr = pl.program_id(1)

    def put(tab, idx, qrow, j, state, head):
        tab[0, idx] = qrow
        tab[1, idx] = j
        tab[2, idx] = state
        tab[3, idx] = head

    n_fix = jnp.int32(0)
    n_gen = jnp.int32(0)
    for hh in range(2):
        hd = 2 * pr + hh

        kmax = lax.fori_loop(
            0, nq, lambda j, m, hd=hd: jnp.maximum(m, st_ref[b, j, 0, FOX_HEADS + hd]),
            jnp.float32(0.0))

        def list_row(qi, counts, hh=hh, hd=hd, kmax=kmax):
            base = (st_ref[b, qi, 0, hd] * (st_ref[b, qi, 0, FOX_HEADS + hd] + kmax)
                    + st_ref[b, qi, 1, hd])
            fixed_ok = base - st_ref[b, qi - 1, 2, hd] <= FIXED_REF_T

            def scan(tab, enabled, cnt):
                def contributes(carry):
                    j, _ = carry
                    bound = base - st_ref[b, jnp.maximum(j, 0), 2, hd]
                    return jnp.logical_and(jnp.logical_and(enabled, j >= 0), bound > -PRUNE_T)

                def take(carry):
                    j, cnt = carry
                    put(tab, cnt, qi, j, hh * nq + qi, hh)
                    return j - 1, cnt + 1

                return lax.while_loop(contributes, take, (qi - 1, cnt))[1]

            return (scan(off_tab, fixed_ok, counts[0]),
                    scan(gen_tab, jnp.logical_not(fixed_ok), counts[1]))

        n_fix, n_gen = lax.fori_loop(1, nq, list_row, (n_fix, n_gen))
    put(off_tab, n_off, 1, 0, 2 * nq, 0)
    put(gen_tab, n_off, 1, 0, 2 * nq, 0)
    run_fixed(off_tab, n_fix, n_off)

    @pl.when(n_gen > 0)
    def _():
        run(gen_tab, n_gen, n_off, False)

    def finish(qi, carry):
        a0 = acc_st[qi]
        a1 = acc_st[nq + qi]
        ot = jnp.concatenate([a0[:HEAD_DIM] / a0[HEAD_DIM:HEAD_DIM + 1],
                              a1[HEAD_DIM:] / a1[0:1]], axis=0)
        rows = pl.ds(pl.multiple_of(qi * tq, tq), tq)
        o_ref[0, rows, :] = (ot.T * g_ref[0, rows, :].astype(F32)).astype(BF16)
        return carry

    lax.fori_loop(0, nq, finish, 0, unroll=8)


def _fox_call(qaug_t, kaug, vaug_t, ga, stats):
    B, S, _ = kaug.shape
    tq, tk = TQ_FOX, TK_FOX
    assert tq == tk
    nq = S // tq
    tab, n_diag, n_off = _fox_tables(nq)
    pair = lambda b, p: (b, 0, p)
    return pl.pallas_call(
        functools.partial(_fox_kernel, n_diag=n_diag, n_off=n_off),
        grid=(B, FOX_HEADS // 2),
        in_specs=[pl.BlockSpec(memory_space=pltpu.SMEM),
                  pl.BlockSpec(memory_space=pltpu.SMEM),
                  pl.BlockSpec((1, 2 * LANES, S), lambda b, p: (b, p, 0)),
                  pl.BlockSpec((1, S, 2 * LANES), pair),
                  pl.BlockSpec((1, 2 * LANES, S), lambda b, p: (b, p, 0)),
                  pl.BlockSpec((1, S, LANES), pair)],
        out_specs=pl.BlockSpec((1, S, LANES), pair),
        out_shape=jax.ShapeDtypeStruct((B, S, FOX_W), BF16),
        scratch_shapes=[pltpu.SMEM((4, n_off + 1), jnp.int32),
                        pltpu.SMEM((4, n_off + 1), jnp.int32),
                        pltpu.VMEM((2, tk, tq), F32), pltpu.VMEM((2, tk, tq), F32),
                        pltpu.VMEM((2, tk, tq), BF16), pltpu.VMEM((2, tk, tq), BF16),
                        pltpu.VMEM((2, 1, tq), F32), pltpu.VMEM((2, 1, tq), F32),
                        pltpu.VMEM((2 * nq + 1, 1, tq), F32),
                        pltpu.VMEM((2 * nq + 1, LANES, tq), F32)],
        compiler_params=pltpu.CompilerParams(
            dimension_semantics=("arbitrary", "arbitrary"), vmem_limit_bytes=VMEM_LIMIT),
        name="fox",
    )(jnp.asarray(tab), stats, qaug_t, kaug, vaug_t, ga)


def _swa_out_kernel(sinks_ref, q_ref, kp_ref, kc_ref, vtp_ref, vtc_ref, g_ref,
                    oa_ref, x_ref, gate_ref, gpost_ref, wa_ref, wb_ref, out_ref,
                    s_sc, p_sc, e_sc, ob_sc):
    i = pl.program_id(1)
    nsub = TQ_SWA // WINDOW
    group = SWA_Q_HEADS // SWA_KV_HEADS
    ncol = group * WINDOW
    kall = jnp.concatenate([kp_ref[0], kc_ref[0]], axis=0)
    vtall = jnp.concatenate([vtp_ref[0], vtc_ref[0]], axis=1)
    lane = lax.broadcasted_iota(jnp.int32, (WINDOW, LANES), 1)
    lo = lane < HEAD_DIM
    zero = jnp.zeros((WINDOW, LANES), BF16)
    kj_ = lax.broadcasted_iota(jnp.int32, (2 * WINDOW, ncol), 0)
    qi_ = lax.broadcasted_iota(jnp.int32, (2 * WINDOW, ncol), 1) & (WINDOW - 1)
    rel = qi_ + WINDOW - kj_
    band = (rel >= 0) & (rel < WINDOW)
    nt = (((1,), (1,)), ((), ()))
    probs = [(r, g) for r in range(nsub) for g in range(SWA_KV_HEADS)]

    def stage_qk(n):
        r, g = probs[n]
        rows = slice(r * WINDOW, (r + 1) * WINDOW)
        kg = kall[r * WINDOW:(r + 2) * WINDOW, g * LANES:(g + 1) * LANES]
        c0 = g * group * HEAD_DIM
        qp0 = q_ref[0, rows, c0:c0 + LANES]
        qp1 = q_ref[0, rows, c0 + LANES:c0 + 2 * LANES]
        qs = jnp.concatenate([jnp.where(lo, qp0, zero), jnp.where(lo, qp1, zero),
                              jnp.where(lo, zero, qp0), jnp.where(lo, zero, qp1)], axis=0)
        s_sc[n] = lax.dot_general(kg, qs, nt, preferred_element_type=F32)

    def stage_softmax(n):
        r, g = probs[n]
        valid = band & (kj_ >= jnp.where(i == 0, WINDOW, 0)) if r == 0 else band
        st = jnp.where(valid, s_sc[n], NEG_BIG)
        heads = (group * g, group * g + 2, group * g + 1, group * g + 3)
        sink = jnp.concatenate(
            [jnp.full((1, WINDOW), sinks_ref[hd] * LOG2E, F32) for hd in heads], axis=1)
        m = jnp.maximum(jnp.max(st, axis=0, keepdims=True), sink)
        p_sc[n] = jnp.exp2(st - m).astype(BF16)
        e_sc[n] = jnp.exp2(sink - m)

    def stage_pv(n):
        r, g = probs[n]
        rows = slice(r * WINDOW, (r + 1) * WINDOW)
        c0 = g * group * HEAD_DIM
        vtg = vtall[g * LANES:(g + 1) * LANES, r * WINDOW:(r + 2) * WINDOW]
        acc = jnp.dot(vtg, p_sc[n], preferred_element_type=F32)
        l = acc[HEAD_DIM:HEAD_DIM + 1] + e_sc[n]
        on = acc[:HEAD_DIM] * (1.0 / l)
        pair0 = jnp.concatenate([on[:, 0:WINDOW], on[:, 2 * WINDOW:3 * WINDOW]], axis=0).T
        pair1 = jnp.concatenate([on[:, WINDOW:2 * WINDOW], on[:, 3 * WINDOW:]], axis=0).T
        ob_sc[rows, c0:c0 + LANES] = (
            pair0 * g_ref[0, rows, c0:c0 + LANES].astype(F32)).astype(BF16)
        ob_sc[rows, c0 + LANES:c0 + 2 * LANES] = (
            pair1 * g_ref[0, rows, c0 + LANES:c0 + 2 * LANES].astype(F32)).astype(BF16)

    for t in range(len(probs) + 2):
        if t < len(probs):
            stage_qk(t)
        if 1 <= t <= len(probs):
            stage_softmax(t - 1)
        if t >= 2:
            stage_pv(t - 2)

    y = (jnp.dot(oa_ref[0], wa_ref[...], preferred_element_type=F32)
         + jnp.dot(ob_sc[...], wb_ref[...], preferred_element_type=F32))
    ms = jnp.mean(y * y, axis=-1, keepdims=True)
    yn = y * lax.rsqrt(ms + RMS_EPS) * gpost_ref[...]
    out_ref[0] = x_ref[0] + gate_ref[0] * yn


def _swa_out_call(sinks, qb, kd, vdt, gb, oa, x, gate, g_post, wa, wb):
    B, S, D = x.shape
    tq = TQ_SWA
    nsub = tq // WINDOW
    cur = lambda b, i: (b, i, 0)
    prev = lambda b, i: (b, jnp.maximum(i * nsub - 1, 0), 0)
    cur_t = lambda b, i: (b, 0, i)
    prev_t = lambda b, i: (b, 0, jnp.maximum(i * nsub - 1, 0))
    const2 = lambda b, i: (0, 0)
    kvw = 2 * SWA_KV_W
    nprob = nsub * SWA_KV_HEADS
    ncol = SWA_Q_HEADS // SWA_KV_HEADS * WINDOW
    return pl.pallas_call(
        _swa_out_kernel,
        grid=(B, S // tq),
        in_specs=[pl.BlockSpec(memory_space=pltpu.SMEM),
                  pl.BlockSpec((1, tq, SWA_W), cur),
                  pl.BlockSpec((1, WINDOW, kvw), prev),
                  pl.BlockSpec((1, tq, kvw), cur),
                  pl.BlockSpec((1, kvw, WINDOW), prev_t),
                  pl.BlockSpec((1, kvw, tq), cur_t),
                  pl.BlockSpec((1, tq, SWA_W), cur),
                  pl.BlockSpec((1, tq, FOX_W), cur),
                  pl.BlockSpec((1, tq, D), cur),
                  pl.BlockSpec((1, 1, D), lambda b, i: (b, 0, 0)),
                  pl.BlockSpec((1, D), const2),
                  pl.BlockSpec((FOX_W, D), const2),
                  pl.BlockSpec((SWA_W, D), const2)],
        out_specs=pl.BlockSpec((1, tq, D), cur),
        out_shape=jax.ShapeDtypeStruct((B, S, D), F32),
        scratch_shapes=[pltpu.VMEM((nprob, 2 * WINDOW, ncol), F32),
                        pltpu.VMEM((nprob, 2 * WINDOW, ncol), BF16),
                        pltpu.VMEM((nprob, 1, ncol), F32),
                        pltpu.VMEM((tq, SWA_W), BF16)],
        compiler_params=pltpu.CompilerParams(
            dimension_semantics=("arbitrary", "arbitrary"), vmem_limit_bytes=VMEM_LIMIT),
        name="swa_out",
    )(sinks, qb, kd, kd, vdt, vdt, gb, oa, x, gate, g_post, wa, wb)


def _perm_w_in(w):
    wt = w.T
    sc = HEAD_DIM ** -0.5
    o = 0
    qa = wt[o:o + FOX_W]; o += FOX_W
    ka = wt[o:o + FOX_W]; o += FOX_W
    va = wt[o:o + FOX_W]; o += FOX_W
    fa = wt[o:o + FOX_HEADS]; o += FOX_HEADS
    za = wt[o:o + FOX_W]; o += FOX_W
    qb = wt[o:o + SWA_W]; o += SWA_W
    kb = wt[o:o + SWA_KV_W]; o += SWA_KV_W
    vb = wt[o:o + SWA_KV_W]; o += SWA_KV_W
    zb = wt[o:o + SWA_W]
    fpad = jnp.concatenate(
        [fa, fa, fa, jnp.zeros((LANES - 3 * FOX_HEADS, wt.shape[1]), w.dtype)], axis=0)
    sc2 = sc * LOG2E
    return jnp.concatenate([qa * sc2, ka, va, za, qb * sc2, zb, kb, vb, fpad],
                           axis=0).astype(BF16)


def _aug_constants():
    eq = np.zeros((LANES, AUG_W), np.float32)
    ek = np.zeros((LANES, AUG_W), np.float32)
    oq = np.zeros((1, AUG_W), np.float32)
    ok = np.zeros((1, AUG_W), np.float32)
    for hd in range(FOX_HEADS):
        base = LANES * hd + (HEAD_DIM if hd % 2 == 0 else 0)
        for part in range(3):
            eq[part * FOX_HEADS + hd, base + part] = 1.0
            ok[0, base + part] = 1.0
            ek[part * FOX_HEADS + hd, base + 3 + part] = -1.0
            oq[0, base + 3 + part] = 1.0
    return jnp.asarray(eq + ek, BF16), jnp.asarray(oq), jnp.asarray(ok)


def kernel(x, c, positions, w_ada, b_ada, g_pre, w_in, b_fgate, sinks, w_out, g_post):
    B, S, D = x.shape
    depth = w_ada.shape[0]
    assert TM_SUB == TQ_FOX == TK_FOX
    half = HEAD_DIM // 2
    inv_freq = ROPE_THETA ** (-jnp.arange(half, dtype=F32) / half)
    invf = inv_freq[:, None]
    pos3 = positions[:, None, :]
    ind = np.zeros((2 * FOX_W, LANES), np.float32)
    ind[np.arange(2 * FOX_W), np.arange(2 * FOX_W) // HEAD_DIM] = 1.0
    ind = jnp.asarray(ind, BF16)
    tri = jnp.asarray(np.tril(np.ones((TM_SUB, TM_SUB), np.float32)), BF16)
    eqk, oq, ok = _aug_constants()
    c_pad = jnp.zeros((8, D), F32).at[:B].set(c)
    for l in range(depth):
        mod = _mod_call(c_pad, w_ada[l], b_ada[l][None, :])[:B]
        shift = mod[:, None, 0:D]
        scale = mod[:, None, D:2 * D]
        gate = mod[:, None, 2 * D:3 * D]
        bf_pad = jnp.concatenate(
            [b_fgate[l]] * 3 + [jnp.zeros((LANES - 3 * FOX_HEADS,), F32)])[None, :]
        qaug, kaug, vaug, ga, qb, kd, vd, gb, stats = _inproj_call(
            x, shift, scale, g_pre[l][None, :], _perm_w_in(w_in[l]), pos3, invf, bf_pad,
            tri, eqk, oq, ok, ind)
        oa = _fox_call(qaug, kaug, vaug, ga, stats[:, :, 0:3, 0:2 * FOX_HEADS])
        wo = w_out[l].astype(BF16)
        x = _swa_out_call(sinks[l], qb, kd, vd, gb, oa, x, gate, g_post[l][None, :],
                          wo[:FOX_W], wo[FOX_W:])
    return x
```

```python
import functools

import jax
import jax.numpy as jnp
import numpy as np
from jax import lax
from jax.experimental import pallas as pl
from jax.experimental.pallas import tpu as pltpu

D_MODEL = 1024
HEAD_DIM = 64
FOX_HEADS = 8
SWA_Q_HEADS = 8
SWA_KV_HEADS = 2
WINDOW = 128
ROPE_THETA = 10000.0
RMS_EPS = 1e-6
FOX_W = FOX_HEADS * HEAD_DIM
SWA_W = SWA_Q_HEADS * HEAD_DIM
SWA_KV_W = SWA_KV_HEADS * HEAD_DIM

LANES = 128
AUG_W = FOX_HEADS * LANES

OFF_QA = 0
OFF_KA = OFF_QA + FOX_W
OFF_VA = OFF_KA + FOX_W
OFF_ZA = OFF_VA + FOX_W
OFF_QB = OFF_ZA + FOX_W
OFF_ZB = OFF_QB + SWA_W
OFF_KD = OFF_ZB + SWA_W
OFF_VD = OFF_KD + SWA_KV_W
OFF_F = OFF_VD + SWA_KV_W
IN_NP = OFF_F + LANES

TM_IN = 1024
TM_SUB = 512
TQ_FOX = 512
TK_FOX = 512
FIXED_UNROLL = 8
DIAG_UNROLL = 4
TQ_SWA = 1024
NEG_BIG = -1e30
LOG2E = 1.4426950408889634
PRUNE_T = 140.0
NORM_MARGIN = 1.01
FIXED_REF_T = 100.0
VMEM_LIMIT = 56 * 1024 * 1024

F32 = jnp.float32
BF16 = jnp.bfloat16


def _split3(a):
    hi = a.astype(BF16)
    r = a - hi.astype(F32)
    mid = r.astype(BF16)
    lo = (r - mid.astype(F32)).astype(BF16)
    return hi, mid, lo


def _silu(z):
    hz = 0.5 * z
    return hz + hz * jnp.tanh(hz)


def _mod_kernel(c_ref, w_ref, b_ref, o_ref):
    c = c_ref[...]
    sc = c * (1.0 / (1.0 + jnp.exp(-c)))
    o_ref[...] = jnp.dot(sc, w_ref[...], precision=lax.Precision.HIGHEST,
                         preferred_element_type=F32) + b_ref[...]


def _mod_call(c_pad, w_ada, b_ada):
    rows = c_pad.shape[0]
    n = w_ada.shape[1]
    bn = D_MODEL
    return pl.pallas_call(
        _mod_kernel,
        grid=(n // bn,),
        in_specs=[pl.BlockSpec((rows, D_MODEL), lambda j: (0, 0)),
                  pl.BlockSpec((D_MODEL, bn), lambda j: (0, j)),
                  pl.BlockSpec((1, bn), lambda j: (0, j))],
        out_specs=pl.BlockSpec((rows, bn), lambda j: (0, j)),
        out_shape=jax.ShapeDtypeStruct((rows, n), F32),
        name="mod",
    )(c_pad, w_ada, b_ada)


def _inproj_kernel(x_ref, shift_ref, scale_ref, gpre_ref, w_ref, pos_ref, invf_ref, bf_ref,
                   tri_ref, eqk_ref, oq_ref, ok_ref, ind_ref,
                   qaug_ref, kaug_ref, vaug_ref, ga_ref, qb_ref, kd_ref, vd_ref, gb_ref, st_ref,
                   carry_ref):
    tm = TM_SUB

    @pl.when(pl.program_id(1) == 0)
    def _():
        carry_ref[...] = jnp.zeros_like(carry_ref)

    for sub in range(x_ref.shape[1] // tm):
        _inproj_subtile(sub, slice(sub * tm, (sub + 1) * tm),
                        x_ref, shift_ref, scale_ref, gpre_ref, w_ref, pos_ref, invf_ref, bf_ref,
                        tri_ref, eqk_ref, oq_ref, ok_ref, ind_ref,
                        qaug_ref, kaug_ref, vaug_ref, ga_ref, qb_ref, kd_ref, vd_ref, gb_ref,
                        st_ref, carry_ref)


def _inproj_subtile(sub, rows, x_ref, shift_ref, scale_ref, gpre_ref, w_ref, pos_ref, invf_ref,
                    bf_ref, tri_ref, eqk_ref, oq_ref, ok_ref, ind_ref,
                    qaug_ref, kaug_ref, vaug_ref, ga_ref, qb_ref, kd_ref, vd_ref, gb_ref,
                    st_ref, carry_ref):
    tm = TM_SUB
    x = x_ref[0, rows, :]
    ms = jnp.mean(x * x, axis=-1, keepdims=True)
    gain = gpre_ref[...] * (1.0 + scale_ref[0])
    h = (x * lax.rsqrt(ms + RMS_EPS) * gain + shift_ref[0]).astype(BF16)

    def proj(off, width):
        return lax.dot_general(h, w_ref[off:off + width, :], (((1,), (1,)), ((), ())),
                               preferred_element_type=F32)

    tail = proj(OFF_KD, 3 * LANES)
    f = tail[:, OFF_F - OFF_KD:] + bf_ref[...]
    ls = jnp.minimum(f, 0.0) - jnp.log1p(jnp.exp(-jnp.abs(f)))
    lane = lax.broadcasted_iota(jnp.int32, (tm, LANES), 1)

    def by_group(a, b_, c_):
        return jnp.where(lane < FOX_HEADS, a, jnp.where(lane < 2 * FOX_HEADS, b_, c_))

    part = by_group(*(t.astype(F32) for t in _split3(ls))).astype(BF16)
    psum = jnp.dot(tri_ref[...], part, preferred_element_type=F32)
    cum = (psum + pltpu.roll(psum, LANES - FOX_HEADS, 1)
           + pltpu.roll(psum, LANES - 2 * FOX_HEADS, 1))
    cum = cum + carry_ref[...]
    carry_ref[...] = cum[tm - 1:tm, :]

    cum2 = cum * LOG2E
    cum2 = by_group(cum2, pltpu.roll(cum2, FOX_HEADS, 1), pltpu.roll(cum2, 2 * FOX_HEADS, 1))
    cs = by_group(*(t.astype(F32) for t in _split3(cum2))).astype(BF16)
    placed = jnp.dot(cs, eqk_ref[...], preferred_element_type=F32)
    augq = placed * ok_ref[...] + oq_ref[...]
    augk = placed * oq_ref[...] + ok_ref[...]

    lane_w = lax.broadcasted_iota(jnp.int32, (tm, AUG_W), 1)
    data = (((lane_w >> 6) ^ (lane_w >> 7)) & 1) == 0

    def rep(a):
        return jnp.concatenate(
            [a[:, LANES * (hd // 2):LANES * (hd // 2 + 1)] for hd in range(FOX_HEADS)], axis=1)

    qa = proj(OFF_QA, FOX_W)
    ka = proj(OFF_KA, FOX_W)
    qaug_ref[0, :, rows] = jnp.where(data, rep(qa), augq).T.astype(BF16)
    kaug_ref[0, rows, :] = jnp.where(data, rep(ka), augk).astype(BF16)

    sq = jnp.concatenate([qa * qa, ka * ka], axis=1).astype(BF16)
    nrm2 = jnp.dot(sq, ind_ref[...], preferred_element_type=F32)
    nmax = jnp.sqrt(jnp.max(nrm2, axis=0, keepdims=True)) * NORM_MARGIN
    st_ref[0, sub] = jnp.concatenate(
        [nmax, cum2[0:1], cum2[tm - 1:tm], jnp.zeros((5, LANES), F32)], axis=0)
    vaug_ref[0, :, rows] = jnp.where(data, rep(proj(OFF_VA, FOX_W)), 1.0).T.astype(BF16)

    za = proj(OFF_ZA, FOX_W)
    ga_ref[0, rows, :] = _silu(za).astype(BF16)
    zb = proj(OFF_ZB, SWA_W)
    gb_ref[0, rows, :] = _silu(zb).astype(BF16)

    ang_t = invf_ref[...] * pos_ref[0, :, rows].astype(F32)
    reps = LANES // (HEAD_DIM // 2)
    cosv = jnp.concatenate([jnp.cos(ang_t)] * reps, axis=0).T
    sinv = jnp.concatenate([jnp.sin(ang_t)] * reps, axis=0).T
    first = (lane & (HEAD_DIM // 2)) == 0
    sin_signed = jnp.where(first, -sinv, sinv)

    def rope(a):
        outs = []
        for cidx in range(a.shape[1] // LANES):
            blk = a[:, cidx * LANES:(cidx + 1) * LANES]
            other = jnp.where(first, pltpu.roll(blk, LANES - HEAD_DIM // 2, 1),
                              pltpu.roll(blk, HEAD_DIM // 2, 1))
            outs.append(blk * cosv + other * sin_signed)
        return jnp.concatenate(outs, axis=1)

    qb_ref[0, rows, :] = rope(proj(OFF_QB, SWA_W)).astype(BF16)
    kr = rope(tail[:, :LANES])
    ks = pltpu.roll(kr, HEAD_DIM, 1)
    low = lane < HEAD_DIM
    kd_ref[0, rows, :] = jnp.concatenate([jnp.where(low, kr, ks), jnp.where(low, ks, kr)],
                                axis=1).astype(BF16)
    vt = tail[:, OFF_VD - OFF_KD:OFF_F - OFF_KD].T
    ones = jnp.ones((HEAD_DIM, tm), F32)
    vd_ref[0, :, rows] = jnp.concatenate([vt[:HEAD_DIM], ones, vt[HEAD_DIM:], ones],
                                axis=0).astype(BF16)


def _inproj_call(x, shift, scale, g_pre, w_perm, pos3, invf, bf_pad, tri, eqk, oq, ok, ind):
    B, S, D = x.shape
    tm = TM_IN
    row = lambda b, t: (b, t, 0)
    per_b = lambda b, t: (b, 0, 0)
    const2 = lambda b, t: (0, 0)
    out_w = (AUG_W, AUG_W, AUG_W, FOX_W, SWA_W, 2 * SWA_KV_W, 2 * SWA_KV_W, SWA_W)
    transposed = (0, 2, 6)
    return pl.pallas_call(
        _inproj_kernel,
        grid=(B, S // tm),
        in_specs=[pl.BlockSpec((1, tm, D), row),
                  pl.BlockSpec((1, 1, D), per_b),
                  pl.BlockSpec((1, 1, D), per_b),
                  pl.BlockSpec((1, D), const2),
                  pl.BlockSpec((IN_NP, D), const2),
                  pl.BlockSpec((1, 1, tm), lambda b, t: (b, 0, t)),
                  pl.BlockSpec((HEAD_DIM // 2, 1), const2),
                  pl.BlockSpec((1, LANES), const2),
                  pl.BlockSpec((TM_SUB, TM_SUB), const2),
                  pl.BlockSpec((LANES, AUG_W), const2),
                  pl.BlockSpec((1, AUG_W), const2),
                  pl.BlockSpec((1, AUG_W), const2),
                  pl.BlockSpec((2 * FOX_W, LANES), const2)],
        out_specs=[pl.BlockSpec((1, w, tm), lambda b, t: (b, 0, t)) if i in transposed
                   else pl.BlockSpec((1, tm, w), row) for i, w in enumerate(out_w)]
        + [pl.BlockSpec((1, tm // TM_SUB, 8, LANES), lambda b, t: (b, t, 0, 0))],
        out_shape=[jax.ShapeDtypeStruct((B, w, S) if i in transposed else (B, S, w), BF16)
                   for i, w in enumerate(out_w)]
        + [jax.ShapeDtypeStruct((B, S // TM_SUB, 8, LANES), F32)],
        scratch_shapes=[pltpu.VMEM((1, LANES), F32)],
        compiler_params=pltpu.CompilerParams(
            dimension_semantics=("arbitrary", "arbitrary"), vmem_limit_bytes=VMEM_LIMIT),
        name="in_proj",
    )(x, shift, scale, g_pre, w_perm, pos3, invf, bf_pad, tri, eqk, oq, ok, ind)


def _pack_item(qrow, j, state, head):
    return qrow | (j << 8) | (state << 16) | (head << 24)


def _unpack_item(word):
    return word & 0xFF, (word >> 8) & 0xFF, (word >> 16) & 0xFF, word >> 24


def _fox_tables(nq):
    diag = [(qi, qi, hh * nq + qi, hh) for hh in range(2) for qi in range(nq)]
    tab = np.array([_pack_item(*it) for it in diag + [(1, 0, 2 * nq, 0)]], np.int32)
    return tab, len(diag), nq * (nq - 1)


def _fox_kernel(tab_ref, st_ref, q_ref, k_ref, v_ref, g_ref, o_ref,
                off_tab, gen_tab, s0, s1, p0, p1, al0, al1, m_st, acc_st, *, n_diag, n_off):
    tq = TQ_FOX
    tk = TK_FOX
    nq = q_ref.shape[2] // tq
    s_buf = (s0, s1)
    p_buf = (p0, p1)
    al_buf = (al0, al1)

    def run(tab, n_items, dummy, diag):
        half = (n_items + 1) // 2
        lens = (half, n_items - half)
        hk, hq = tk // 2, tq // 2

        def item(stream, t):
            return jnp.where(t < lens[stream], stream * half + t, dummy)

        def stage_qk(t, slot):
            for sm in range(2):
                qrow, j, _, head = _unpack_item(tab[item(sm, t)])
                qoff = pl.multiple_of(qrow * tq, tq)
                koff = pl.multiple_of(j * tk, tk)
                hoff = pl.multiple_of(head * LANES, LANES)
                qt = q_ref[0, pl.ds(hoff, LANES), pl.ds(qoff, tq)]
                k = k_ref[0, pl.ds(koff, tk), pl.ds(hoff, LANES)]
                if diag:
                    s_buf[slot][sm, :hk, :] = jnp.dot(k[:hk], qt, preferred_element_type=F32)
                    s_buf[slot][sm, hk:, hq:] = jnp.dot(k[hk:], qt[:, hq:],
                                                        preferred_element_type=F32)
                else:
                    s_buf[slot][sm] = jnp.dot(k, qt, preferred_element_type=F32)

        def stage_softmax(t, slot):
            for sm in range(2):
                st = _unpack_item(tab[item(sm, t)])[2]
                if diag:
                    kr = lax.broadcasted_iota(jnp.int32, (hk, tq), 0)
                    qc = lax.broadcasted_iota(jnp.int32, (hk, tq), 1)
                    top = jnp.where(kr <= qc, s_buf[slot][sm, :hk, :], NEG_BIG)
                    kr2 = lax.broadcasted_iota(jnp.int32, (tk - hk, tq - hq), 0)
                    qc2 = lax.broadcasted_iota(jnp.int32, (tk - hk, tq - hq), 1)
                    low = jnp.where(kr2 <= qc2, s_buf[slot][sm, hk:, hq:], NEG_BIG)
                    top_a, top_b = top[:, :hq], top[:, hq:]
                    m_a = jnp.max(top_a, axis=0, keepdims=True)
                    m_b = jnp.maximum(jnp.max(top_b, axis=0, keepdims=True),
                                      jnp.max(low, axis=0, keepdims=True))
                    m_next = jnp.concatenate([m_a, m_b], axis=1)
                    p_buf[slot][sm, :hk, :hq] = jnp.exp2(top_a - m_a).astype(BF16)
                    p_buf[slot][sm, :hk, hq:] = jnp.exp2(top_b - m_b).astype(BF16)
                    p_buf[slot][sm, hk:, hq:] = jnp.exp2(low - m_b).astype(BF16)
                else:
                    s = s_buf[slot][sm]
                    m_prev = m_st[st]
                    m_next = jnp.maximum(m_prev, jnp.max(s, axis=0, keepdims=True))
                    al_buf[slot][sm] = jnp.exp2(m_prev - m_next)
                    p_buf[slot][sm] = jnp.exp2(s - m_next).astype(BF16)
                m_st[st] = m_next

        def stage_pv(t, slot):
            for sm in range(2):
                _, j, st, head = _unpack_item(tab[item(sm, t)])
                koff = pl.multiple_of(j * tk, tk)
                hoff = pl.multiple_of(head * LANES, LANES)
                vt = v_ref[0, pl.ds(hoff, LANES), pl.ds(koff, tk)]
                if diag:
                    acc_st[st] = jnp.concatenate(
                        [jnp.dot(vt[:, :hk], p_buf[slot][sm, :hk, :hq],
                                 preferred_element_type=F32),
                         jnp.dot(vt, p_buf[slot][sm, :, hq:], preferred_element_type=F32)],
                        axis=1)
                else:
                    pv = jnp.dot(vt, p_buf[slot][sm], preferred_element_type=F32)
                    acc_st[st] = al_buf[slot][sm] * acc_st[st] + pv

        stage_qk(0, 0)
        stage_qk(1, 1)
        stage_softmax(0, 0)

        unroll = DIAG_UNROLL if diag else 2

        def body(u, carry):
            for d in range(unroll):
                t = unroll * u + 1 + d
                stage_qk(t + 1, d % 2)
                stage_softmax(t, (d + 1) % 2)
                stage_pv(t - 1, d % 2)
            return carry

        lax.fori_loop(0, (half + unroll - 1) // unroll, body, 0)

    def run_fixed(tab, n_items, dummy):
        half = (n_items + 1) // 2
        lens = (half, n_items - half)

        def item(stream, t):
            return jnp.where(t < lens[stream], stream * half + t, dummy)

        def stage_probs(t, slot):
            for sm in range(2):
                qrow, j, st, head = _unpack_item(tab[item(sm, t)])
                qoff = pl.multiple_of(qrow * tq, tq)
                koff = pl.multiple_of(j * tk, tk)
                hoff = pl.multiple_of(head * LANES, LANES)
                qt = q_ref[0, pl.ds(hoff, LANES), pl.ds(qoff, tq)]
                k = k_ref[0, pl.ds(koff, tk), pl.ds(hoff, LANES)]
                s = jnp.dot(k, qt, preferred_element_type=F32)
                p_buf[slot][sm] = jnp.exp2(s - m_st[st]).astype(BF16)

        def stage_pv(t, slot):
            for sm in range(2):
                _, j, st, head = _unpack_item(tab[item(sm, t)])
                koff = pl.multiple_of(j * tk, tk)
                hoff = pl.multiple_of(head * LANES, LANES)
                vt = v_ref[0, pl.ds(hoff, LANES), pl.ds(koff, tk)]
                acc_st[st] = acc_st[st] + jnp.dot(vt, p_buf[slot][sm],
                                                  preferred_element_type=F32)

        stage_probs(0, 0)

        def trips(first, unroll):
            def body(u, carry):
                for d in range(unroll):
                    t = first + unroll * u + 1 + d
                    stage_probs(t, (d + 1) % 2)
                    stage_pv(t - 1, d % 2)
                return carry
            return body

        n_long = half // FIXED_UNROLL
        done = n_long * FIXED_UNROLL
        lax.fori_loop(0, n_long, trips(0, FIXED_UNROLL), 0)
        lax.fori_loop(0, (half - done + 1) // 2, trips(done, 2), 0)

    assert FIXED_UNROLL % 2 == 0 and tab_ref.shape[0] == n_diag + 1
    run(tab_ref, n_diag, n_diag, True)

    b = pl.program_id(0)
    pr = pl.program_id(1)

    n_fix = jnp.int32(0)
    n_gen = jnp.int32(0)
    for hh in range(2):
        hd = 2 * pr + hh

        kmax = lax.fori_loop(
            0, nq, lambda j, m, hd=hd: jnp.maximum(m, st_ref[b, j, 0, FOX_HEADS + hd]),
            jnp.float32(0.0))

        def list_row(qi, counts, hh=hh, hd=hd, kmax=kmax):
            base = (st_ref[b, qi, 0, hd] * (st_ref[b, qi, 0, FOX_HEADS + hd] + kmax)
                    + st_ref[b, qi, 1, hd])
            to_fixed = (base - st_ref[b, qi - 1, 2, hd] <= FIXED_REF_T).astype(jnp.int32)
            row_word = _pack_item(qi, 0, hh * nq + qi, hh)

            def contributes(carry):
                j = carry[0]
                bound = base - st_ref[b, jnp.maximum(j, 0), 2, hd]
                return jnp.logical_and(j >= 0, bound > -PRUNE_T)

            def take(carry):
                j, n_f, n_g = carry
                word = row_word | (j << 8)
                off_tab[n_f] = word
                gen_tab[n_g] = word
                return j - 1, n_f + to_fixed, n_g + 1 - to_fixed

            return lax.while_loop(contributes, take, (qi - 1,) + tuple(counts))[1:]

        n_fix, n_gen = lax.fori_loop(1, nq, list_row, (n_fix, n_gen))
    off_tab[n_off] = _pack_item(1, 0, 2 * nq, 0)
    gen_tab[n_off] = _pack_item(1, 0, 2 * nq, 0)
    run_fixed(off_tab, n_fix, n_off)

    @pl.when(n_gen > 0)
    def _():
        run(gen_tab, n_gen, n_off, False)

    def finish(qi, carry):
        a0 = acc_st[qi]
        a1 = acc_st[nq + qi]
        ot = jnp.concatenate([a0[:HEAD_DIM] / a0[HEAD_DIM:HEAD_DIM + 1],
                              a1[HEAD_DIM:] / a1[0:1]], axis=0)
        rows = pl.ds(pl.multiple_of(qi * tq, tq), tq)
        o_ref[0, rows, :] = (ot.T * g_ref[0, rows, :].astype(F32)).astype(BF16)
        return carry

    lax.fori_loop(0, nq, finish, 0, unroll=8)


def _fox_call(qaug_t, kaug, vaug_t, ga, stats):
    B, S, _ = kaug.shape
    tq, tk = TQ_FOX, TK_FOX
    assert tq == tk
    nq = S // tq
    tab, n_diag, n_off = _fox_tables(nq)
    pair = lambda b, p: (b, 0, p)
    return pl.pallas_call(
        functools.partial(_fox_kernel, n_diag=n_diag, n_off=n_off),
        grid=(B, FOX_HEADS // 2),
        in_specs=[pl.BlockSpec(memory_space=pltpu.SMEM),
                  pl.BlockSpec(memory_space=pltpu.SMEM),
                  pl.BlockSpec((1, 2 * LANES, S), lambda b, p: (b, p, 0)),
                  pl.BlockSpec((1, S, 2 * LANES), pair),
                  pl.BlockSpec((1, 2 * LANES, S), lambda b, p: (b, p, 0)),
                  pl.BlockSpec((1, S, LANES), pair)],
        out_specs=pl.BlockSpec((1, S, LANES), pair),
        out_shape=jax.ShapeDtypeStruct((B, S, FOX_W), BF16),
        scratch_shapes=[pltpu.SMEM((n_off + 1,), jnp.int32),
                        pltpu.SMEM((n_off + 1,), jnp.int32),
                        pltpu.VMEM((2, tk, tq), F32), pltpu.VMEM((2, tk, tq), F32),
                        pltpu.VMEM((2, tk, tq), BF16), pltpu.VMEM((2, tk, tq), BF16),
                        pltpu.VMEM((2, 1, tq), F32), pltpu.VMEM((2, 1, tq), F32),
                        pltpu.VMEM((2 * nq + 1, 1, tq), F32),
                        pltpu.VMEM((2 * nq + 1, LANES, tq), F32)],
        compiler_params=pltpu.CompilerParams(
            dimension_semantics=("arbitrary", "arbitrary"), vmem_limit_bytes=VMEM_LIMIT),
        name="fox",
    )(jnp.asarray(tab), stats, qaug_t, kaug, vaug_t, ga)


def _swa_out_kernel(sinks_ref, q_ref, kp_ref, kc_ref, vtp_ref, vtc_ref, g_ref,
                    oa_ref, x_ref, gate_ref, gpost_ref, wa_ref, wb_ref, out_ref,
                    s_sc, p_sc, e_sc, ob_sc):
    i = pl.program_id(1)
    nsub = TQ_SWA // WINDOW
    group = SWA_Q_HEADS // SWA_KV_HEADS
    ncol = group * WINDOW
    kall = jnp.concatenate([kp_ref[0], kc_ref[0]], axis=0)
    vtall = jnp.concatenate([vtp_ref[0], vtc_ref[0]], axis=1)
    lane = lax.broadcasted_iota(jnp.int32, (WINDOW, LANES), 1)
    lo = lane < HEAD_DIM
    zero = jnp.zeros((WINDOW, LANES), BF16)
    kj_ = lax.broadcasted_iota(jnp.int32, (2 * WINDOW, ncol), 0)
    qi_ = lax.broadcasted_iota(jnp.int32, (2 * WINDOW, ncol), 1) & (WINDOW - 1)
    rel = qi_ + WINDOW - kj_
    band = (rel >= 0) & (rel < WINDOW)
    nt = (((1,), (1,)), ((), ()))
    probs = [(r, g) for r in range(nsub) for g in range(SWA_KV_HEADS)]

    def stage_qk(n):
        r, g = probs[n]
        rows = slice(r * WINDOW, (r + 1) * WINDOW)
        kg = kall[r * WINDOW:(r + 2) * WINDOW, g * LANES:(g + 1) * LANES]
        c0 = g * group * HEAD_DIM
        qp0 = q_ref[0, rows, c0:c0 + LANES]
        qp1 = q_ref[0, rows, c0 + LANES:c0 + 2 * LANES]
        qs = jnp.concatenate([jnp.where(lo, qp0, zero), jnp.where(lo, qp1, zero),
                              jnp.where(lo, zero, qp0), jnp.where(lo, zero, qp1)], axis=0)
        s_sc[n] = lax.dot_general(kg, qs, nt, preferred_element_type=F32)

    def stage_softmax(n):
        r, g = probs[n]
        valid = band & (kj_ >= jnp.where(i == 0, WINDOW, 0)) if r == 0 else band
        st = jnp.where(valid, s_sc[n], NEG_BIG)
        heads = (group * g, group * g + 2, group * g + 1, group * g + 3)
        sink = jnp.concatenate(
            [jnp.full((1, WINDOW), sinks_ref[hd] * LOG2E, F32) for hd in heads], axis=1)
        m = jnp.maximum(jnp.max(st, axis=0, keepdims=True), sink)
        p_sc[n] = jnp.exp2(st - m).astype(BF16)
        e_sc[n] = jnp.exp2(sink - m)

    def stage_pv(n):
        r, g = probs[n]
        rows = slice(r * WINDOW, (r + 1) * WINDOW)
        c0 = g * group * HEAD_DIM
        vtg = vtall[g * LANES:(g + 1) * LANES, r * WINDOW:(r + 2) * WINDOW]
        acc = jnp.dot(vtg, p_sc[n], preferred_element_type=F32)
        l = acc[HEAD_DIM:HEAD_DIM + 1] + e_sc[n]
        on = acc[:HEAD_DIM] * (1.0 / l)
        pair0 = jnp.concatenate([on[:, 0:WINDOW], on[:, 2 * WINDOW:3 * WINDOW]], axis=0).T
        pair1 = jnp.concatenate([on[:, WINDOW:2 * WINDOW], on[:, 3 * WINDOW:]], axis=0).T
        ob_sc[rows, c0:c0 + LANES] = (
            pair0 * g_ref[0, rows, c0:c0 + LANES].astype(F32)).astype(BF16)
        ob_sc[rows, c0 + LANES:c0 + 2 * LANES] = (
            pair1 * g_ref[0, rows, c0 + LANES:c0 + 2 * LANES].astype(F32)).astype(BF16)

    for t in range(len(probs) + 2):
        if t < len(probs):
            stage_qk(t)
        if 1 <= t <= len(probs):
            stage_softmax(t - 1)
        if t >= 2:
            stage_pv(t - 2)

    y = (jnp.dot(oa_ref[0], wa_ref[...], preferred_element_type=F32)
         + jnp.dot(ob_sc[...], wb_ref[...], preferred_element_type=F32))
    ms = jnp.mean(y * y, axis=-1, keepdims=True)
    yn = y * lax.rsqrt(ms + RMS_EPS) * gpost_ref[...]
    out_ref[0] = x_ref[0] + gate_ref[0] * yn


def _swa_out_call(sinks, qb, kd, vdt, gb, oa, x, gate, g_post, wa, wb):
    B, S, D = x.shape
    tq = TQ_SWA
    nsub = tq // WINDOW
    cur = lambda b, i: (b, i, 0)
    prev = lambda b, i: (b, jnp.maximum(i * nsub - 1, 0), 0)
    cur_t = lambda b, i: (b, 0, i)
    prev_t = lambda b, i: (b, 0, jnp.maximum(i * nsub - 1, 0))
    const2 = lambda b, i: (0, 0)
    kvw = 2 * SWA_KV_W
    nprob = nsub * SWA_KV_HEADS
    ncol = SWA_Q_HEADS // SWA_KV_HEADS * WINDOW
    return pl.pallas_call(
        _swa_out_kernel,
        grid=(B, S // tq),
        in_specs=[pl.BlockSpec(memory_space=pltpu.SMEM),
                  pl.BlockSpec((1, tq, SWA_W), cur),
                  pl.BlockSpec((1, WINDOW, kvw), prev),
                  pl.BlockSpec((1, tq, kvw), cur),
                  pl.BlockSpec((1, kvw, WINDOW), prev_t),
                  pl.BlockSpec((1, kvw, tq), cur_t),
                  pl.BlockSpec((1, tq, SWA_W), cur),
                  pl.BlockSpec((1, tq, FOX_W), cur),
                  pl.BlockSpec((1, tq, D), cur),
                  pl.BlockSpec((1, 1, D), lambda b, i: (b, 0, 0)),
                  pl.BlockSpec((1, D), const2),
                  pl.BlockSpec((FOX_W, D), const2),
                  pl.BlockSpec((SWA_W, D), const2)],
        out_specs=pl.BlockSpec((1, tq, D), cur),
        out_shape=jax.ShapeDtypeStruct((B, S, D), F32),
        scratch_shapes=[pltpu.VMEM((nprob, 2 * WINDOW, ncol), F32),
                        pltpu.VMEM((nprob, 2 * WINDOW, ncol), BF16),
                        pltpu.VMEM((nprob, 1, ncol), F32),
                        pltpu.VMEM((tq, SWA_W), BF16)],
        compiler_params=pltpu.CompilerParams(
            dimension_semantics=("arbitrary", "arbitrary"), vmem_limit_bytes=VMEM_LIMIT),
        name="swa_out",
    )(sinks, qb, kd, kd, vdt, vdt, gb, oa, x, gate, g_post, wa, wb)


def _perm_w_in(w):
    wt = w.T
    sc = HEAD_DIM ** -0.5
    o = 0
    qa = wt[o:o + FOX_W]; o += FOX_W
    ka = wt[o:o + FOX_W]; o += FOX_W
    va = wt[o:o + FOX_W]; o += FOX_W
    fa = wt[o:o + FOX_HEADS]; o += FOX_HEADS
    za = wt[o:o + FOX_W]; o += FOX_W
    qb = wt[o:o + SWA_W]; o += SWA_W
    kb = wt[o:o + SWA_KV_W]; o += SWA_KV_W
    vb = wt[o:o + SWA_KV_W]; o += SWA_KV_W
    zb = wt[o:o + SWA_W]
    fpad = jnp.concatenate(
        [fa, fa, fa, jnp.zeros((LANES - 3 * FOX_HEADS, wt.shape[1]), w.dtype)], axis=0)
    sc2 = sc * LOG2E
    return jnp.concatenate([qa * sc2, ka, va, za, qb * sc2, zb, kb, vb, fpad],
                           axis=0).astype(BF16)


def _aug_constants():
    eq = np.zeros((LANES, AUG_W), np.float32)
    ek = np.zeros((LANES, AUG_W), np.float32)
    oq = np.zeros((1, AUG_W), np.float32)
    ok = np.zeros((1, AUG_W), np.float32)
    for hd in range(FOX_HEADS):
        base = LANES * hd + (HEAD_DIM if hd % 2 == 0 else 0)
        for part in range(3):
            eq[part * FOX_HEADS + hd, base + part] = 1.0
            ok[0, base + part] = 1.0
            ek[part * FOX_HEADS + hd, base + 3 + part] = -1.0
            oq[0, base + 3 + part] = 1.0
    return jnp.asarray(eq + ek, BF16), jnp.asarray(oq), jnp.asarray(ok)


def kernel(x, c, positions, w_ada, b_ada, g_pre, w_in, b_fgate, sinks, w_out, g_post):
    B, S, D = x.shape
    depth = w_ada.shape[0]
    assert TM_SUB == TQ_FOX == TK_FOX
    half = HEAD_DIM // 2
    inv_freq = ROPE_THETA ** (-jnp.arange(half, dtype=F32) / half)
    invf = inv_freq[:, None]
    pos3 = positions[:, None, :]
    ind = np.zeros((2 * FOX_W, LANES), np.float32)
    ind[np.arange(2 * FOX_W), np.arange(2 * FOX_W) // HEAD_DIM] = 1.0
    ind = jnp.asarray(ind, BF16)
    tri = jnp.asarray(np.tril(np.ones((TM_SUB, TM_SUB), np.float32)), BF16)
    eqk, oq, ok = _aug_constants()
    c_pad = jnp.zeros((8, D), F32).at[:B].set(c)
    for l in range(depth):
        mod = _mod_call(c_pad, w_ada[l], b_ada[l][None, :])[:B]
        shift = mod[:, None, 0:D]
        scale = mod[:, None, D:2 * D]
        gate = mod[:, None, 2 * D:3 * D]
        bf_pad = jnp.concatenate(
            [b_fgate[l]] * 3 + [jnp.zeros((LANES - 3 * FOX_HEADS,), F32)])[None, :]
        qaug, kaug, vaug, ga, qb, kd, vd, gb, stats = _inproj_call(
            x, shift, scale, g_pre[l][None, :], _perm_w_in(w_in[l]), pos3, invf, bf_pad,
            tri, eqk, oq, ok, ind)
        oa = _fox_call(qaug, kaug, vaug, ga, stats[:, :, 0:3, 0:2 * FOX_HEADS])
        wo = w_out[l].astype(BF16)
        x = _swa_out_call(sinks[l], qb, kd, vd, gb, oa, x, gate, g_post[l][None, :],
                          wo[:FOX_W], wo[FOX_W:])
    return x
```

```python
import functools

import jax
import jax.numpy as jnp
import numpy as np
from jax import lax
from jax.experimental import pallas as pl
from jax.experimental.pallas import tpu as pltpu

D_MODEL = 1024
HEAD_DIM = 64
FOX_HEADS = 8
SWA_Q_HEADS = 8
SWA_KV_HEADS = 2
WINDOW = 128
ROPE_THETA = 10000.0
RMS_EPS = 1e-6
FOX_W = FOX_HEADS * HEAD_DIM
SWA_W = SWA_Q_HEADS * HEAD_DIM
SWA_KV_W = SWA_KV_HEADS * HEAD_DIM

LANES = 128
AUG_W = FOX_HEADS * LANES

OFF_QA = 0
OFF_KA = OFF_QA + FOX_W
OFF_VA = OFF_KA + FOX_W
OFF_ZA = OFF_VA + FOX_W
OFF_QB = OFF_ZA + FOX_W
OFF_ZB = OFF_QB + SWA_W
OFF_KD = OFF_ZB + SWA_W
OFF_VD = OFF_KD + SWA_KV_W
OFF_F = OFF_VD + SWA_KV_W
IN_NP = OFF_F + LANES

TM_IN = 1024
TM_SUB = 512
TQ_FOX = 512
TK_FOX = 512
FIXED_UNROLL = 8
DIAG_UNROLL = 4
TQ_SWA = 1024
NEG_BIG = -1e30
LOG2E = 1.4426950408889634
PRUNE_T = 140.0
NORM_MARGIN = 1.01
FIXED_REF_T = 100.0
VMEM_LIMIT = 56 * 1024 * 1024

F32 = jnp.float32
BF16 = jnp.bfloat16


def _split3(a):
    hi = a.astype(BF16)
    r = a - hi.astype(F32)
    mid = r.astype(BF16)
    lo = (r - mid.astype(F32)).astype(BF16)
    return hi, mid, lo


def _silu(z):
    hz = 0.5 * z
    return hz + hz * jnp.tanh(hz)


def _mod_kernel(c_ref, w_ref, b_ref, o_ref):
    c = c_ref[...]
    sc = c * (1.0 / (1.0 + jnp.exp(-c)))
    o_ref[...] = jnp.dot(sc, w_ref[...], precision=lax.Precision.HIGHEST,
                         preferred_element_type=F32) + b_ref[...]


def _mod_call(c_pad, w_ada, b_ada):
    rows = c_pad.shape[0]
    n = w_ada.shape[1]
    bn = D_MODEL
    return pl.pallas_call(
        _mod_kernel,
        grid=(n // bn,),
        in_specs=[pl.BlockSpec((rows, D_MODEL), lambda j: (0, 0)),
                  pl.BlockSpec((D_MODEL, bn), lambda j: (0, j)),
                  pl.BlockSpec((1, bn), lambda j: (0, j))],
        out_specs=pl.BlockSpec((rows, bn), lambda j: (0, j)),
        out_shape=jax.ShapeDtypeStruct((rows, n), F32),
        name="mod",
    )(c_pad, w_ada, b_ada)


def _inproj_kernel(x_ref, shift_ref, scale_ref, gpre_ref, w_ref, pos_ref, invf_ref, bf_ref,
                   tri_ref, eqk_ref, oq_ref, ok_ref, ind_ref,
                   qaug_ref, kaug_ref, vaug_ref, ga_ref, qb_ref, kd_ref, vd_ref, gb_ref, st_ref,
                   carry_ref):
    tm = TM_SUB

    @pl.when(pl.program_id(1) == 0)
    def _():
        carry_ref[...] = jnp.zeros_like(carry_ref)

    for sub in range(x_ref.shape[1] // tm):
        _inproj_subtile(sub, slice(sub * tm, (sub + 1) * tm),
                        x_ref, shift_ref, scale_ref, gpre_ref, w_ref, pos_ref, invf_ref, bf_ref,
                        tri_ref, eqk_ref, oq_ref, ok_ref, ind_ref,
                        qaug_ref, kaug_ref, vaug_ref, ga_ref, qb_ref, kd_ref, vd_ref, gb_ref,
                        st_ref, carry_ref)


def _inproj_subtile(sub, rows, x_ref, shift_ref, scale_ref, gpre_ref, w_ref, pos_ref, invf_ref,
                    bf_ref, tri_ref, eqk_ref, oq_ref, ok_ref, ind_ref,
                    qaug_ref, kaug_ref, vaug_ref, ga_ref, qb_ref, kd_ref, vd_ref, gb_ref,
                    st_ref, carry_ref):
    tm = TM_SUB
    x = x_ref[0, rows, :]
    ms = jnp.mean(x * x, axis=-1, keepdims=True)
    gain = gpre_ref[...] * (1.0 + scale_ref[0])
    h = (x * lax.rsqrt(ms + RMS_EPS) * gain + shift_ref[0]).astype(BF16)

    def proj(off, width):
        return lax.dot_general(h, w_ref[off:off + width, :], (((1,), (1,)), ((), ())),
                               preferred_element_type=F32)

    tail = proj(OFF_KD, 3 * LANES)
    f = tail[:, OFF_F - OFF_KD:] + bf_ref[...]
    ls = jnp.minimum(f, 0.0) - jnp.log1p(jnp.exp(-jnp.abs(f)))
    lane = lax.broadcasted_iota(jnp.int32, (tm, LANES), 1)

    def by_group(a, b_, c_):
        return jnp.where(lane < FOX_HEADS, a, jnp.where(lane < 2 * FOX_HEADS, b_, c_))

    part = by_group(*(t.astype(F32) for t in _split3(ls))).astype(BF16)
    psum = jnp.dot(tri_ref[...], part, preferred_element_type=F32)
    cum = (psum + pltpu.roll(psum, LANES - FOX_HEADS, 1)
           + pltpu.roll(psum, LANES - 2 * FOX_HEADS, 1))
    cum = cum + carry_ref[...]
    carry_ref[...] = cum[tm - 1:tm, :]

    cum2 = cum * LOG2E
    cum2 = by_group(cum2, pltpu.roll(cum2, FOX_HEADS, 1), pltpu.roll(cum2, 2 * FOX_HEADS, 1))
    cs = by_group(*(t.astype(F32) for t in _split3(cum2))).astype(BF16)
    placed = jnp.dot(cs, eqk_ref[...], preferred_element_type=F32)
    augq = placed * ok_ref[...] + oq_ref[...]
    augk = placed * oq_ref[...] + ok_ref[...]

    lane_w = lax.broadcasted_iota(jnp.int32, (tm, AUG_W), 1)
    data = (((lane_w >> 6) ^ (lane_w >> 7)) & 1) == 0

    def rep(a):
        return jnp.concatenate(
            [a[:, LANES * (hd // 2):LANES * (hd // 2 + 1)] for hd in range(FOX_HEADS)], axis=1)

    qa = proj(OFF_QA, FOX_W)
    ka = proj(OFF_KA, FOX_W)
    qaug_ref[0, :, rows] = jnp.where(data, rep(qa), augq).T.astype(BF16)
    kaug_ref[0, rows, :] = jnp.where(data, rep(ka), augk).astype(BF16)

    sq = jnp.concatenate([qa * qa, ka * ka], axis=1).astype(BF16)
    nrm2 = jnp.dot(sq, ind_ref[...], preferred_element_type=F32)
    nmax = jnp.sqrt(jnp.max(nrm2, axis=0, keepdims=True)) * NORM_MARGIN
    st_ref[0, sub] = jnp.concatenate(
        [nmax, cum2[0:1], cum2[tm - 1:tm], jnp.zeros((5, LANES), F32)], axis=0)
    vaug_ref[0, :, rows] = jnp.where(data, rep(proj(OFF_VA, FOX_W)), 1.0).T.astype(BF16)

    za = proj(OFF_ZA, FOX_W)
    ga_ref[0, rows, :] = _silu(za).astype(BF16)
    zb = proj(OFF_ZB, SWA_W)
    gb_ref[0, rows, :] = _silu(zb).astype(BF16)

    ang_t = invf_ref[...] * pos_ref[0, :, rows].astype(F32)
    reps = LANES // (HEAD_DIM // 2)
    cosv = jnp.concatenate([jnp.cos(ang_t)] * reps, axis=0).T
    sinv = jnp.concatenate([jnp.sin(ang_t)] * reps, axis=0).T
    first = (lane & (HEAD_DIM // 2)) == 0
    sin_signed = jnp.where(first, -sinv, sinv)

    def rope(a):
        outs = []
        for cidx in range(a.shape[1] // LANES):
            blk = a[:, cidx * LANES:(cidx + 1) * LANES]
            other = jnp.where(first, pltpu.roll(blk, LANES - HEAD_DIM // 2, 1),
                              pltpu.roll(blk, HEAD_DIM // 2, 1))
            outs.append(blk * cosv + other * sin_signed)
        return jnp.concatenate(outs, axis=1)

    qb_ref[0, rows, :] = rope(proj(OFF_QB, SWA_W)).astype(BF16)
    kr = rope(tail[:, :LANES])
    ks = pltpu.roll(kr, HEAD_DIM, 1)
    low = lane < HEAD_DIM
    kd_ref[0, rows, :] = jnp.concatenate([jnp.where(low, kr, ks), jnp.where(low, ks, kr)],
                                axis=1).astype(BF16)
    vt = tail[:, OFF_VD - OFF_KD:OFF_F - OFF_KD].T
    ones = jnp.ones((HEAD_DIM, tm), F32)
    vd_ref[0, :, rows] = jnp.concatenate([vt[:HEAD_DIM], ones, vt[HEAD_DIM:], ones],
                                axis=0).astype(BF16)


def _inproj_call(x, shift, scale, g_pre, w_perm, pos3, invf, bf_pad, tri, eqk, oq, ok, ind):
    B, S, D = x.shape
    tm = TM_IN
    row = lambda b, t: (b, t, 0)
    per_b = lambda b, t: (b, 0, 0)
    const2 = lambda b, t: (0, 0)
    out_w = (AUG_W, AUG_W, AUG_W, FOX_W, SWA_W, 2 * SWA_KV_W, 2 * SWA_KV_W, SWA_W)
    transposed = (0, 2, 6)
    return pl.pallas_call(
        _inproj_kernel,
        grid=(B, S // tm),
        in_specs=[pl.BlockSpec((1, tm, D), row),
                  pl.BlockSpec((1, 1, D), per_b),
                  pl.BlockSpec((1, 1, D), per_b),
                  pl.BlockSpec((1, D), const2),
                  pl.BlockSpec((IN_NP, D), const2),
                  pl.BlockSpec((1, 1, tm), lambda b, t: (b, 0, t)),
                  pl.BlockSpec((HEAD_DIM // 2, 1), const2),
                  pl.BlockSpec((1, LANES), const2),
                  pl.BlockSpec((TM_SUB, TM_SUB), const2),
                  pl.BlockSpec((LANES, AUG_W), const2),
                  pl.BlockSpec((1, AUG_W), const2),
                  pl.BlockSpec((1, AUG_W), const2),
                  pl.BlockSpec((2 * FOX_W, LANES), const2)],
        out_specs=[pl.BlockSpec((1, w, tm), lambda b, t: (b, 0, t)) if i in transposed
                   else pl.BlockSpec((1, tm, w), row) for i, w in enumerate(out_w)]
        + [pl.BlockSpec((1, tm // TM_SUB, 8, LANES), lambda b, t: (b, t, 0, 0))],
        out_shape=[jax.ShapeDtypeStruct((B, w, S) if i in transposed else (B, S, w), BF16)
                   for i, w in enumerate(out_w)]
        + [jax.ShapeDtypeStruct((B, S // TM_SUB, 8, LANES), F32)],
        scratch_shapes=[pltpu.VMEM((1, LANES), F32)],
        compiler_params=pltpu.CompilerParams(
            dimension_semantics=("arbitrary", "arbitrary"), vmem_limit_bytes=VMEM_LIMIT),
        name="in_proj",
    )(x, shift, scale, g_pre, w_perm, pos3, invf, bf_pad, tri, eqk, oq, ok, ind)


def _pack_item(qrow, j, state, head):
    return qrow | (j << 8) | (state << 16) | (head << 24)


def _unpack_item(word):
    return word & 0xFF, (word >> 8) & 0xFF, (word >> 16) & 0xFF, word >> 24


def _fox_tables(nq):
    diag = [(qi, qi, hh * nq + qi, hh) for hh in range(2) for qi in range(nq)]
    tab = np.array([_pack_item(*it) for it in diag + [(1, 0, 2 * nq, 0)]], np.int32)
    return tab, len(diag), nq * (nq - 1)


def _fox_kernel(tab_ref, st_ref, q_ref, k_ref, v_ref, g_ref, o_ref,
                off_tab, gen_tab, s0, s1, p0, p1, al0, al1, m_st, acc_st, *, n_diag, n_off):
    tq = TQ_FOX
    tk = TK_FOX
    nq = q_ref.shape[2] // tq
    s_buf = (s0, s1)
    p_buf = (p0, p1)
    al_buf = (al0, al1)

    def run(tab, n_items, dummy, diag):
        half = (n_items + 1) // 2
        lens = (half, n_items - half)
        hk, hq = tk // 2, tq // 2

        def item(stream, t):
            return jnp.where(t < lens[stream], stream * half + t, dummy)

        def stage_qk(t, slot):
            for sm in range(2):
                qrow, j, _, head = _unpack_item(tab[item(sm, t)])
                qoff = pl.multiple_of(qrow * tq, tq)
                koff = pl.multiple_of(j * tk, tk)
                hoff = pl.multiple_of(head * LANES, LANES)
                qt = q_ref[0, pl.ds(hoff, LANES), pl.ds(qoff, tq)]
                k = k_ref[0, pl.ds(koff, tk), pl.ds(hoff, LANES)]
                if diag:
                    s_buf[slot][sm, :hk, :] = jnp.dot(k[:hk], qt, preferred_element_type=F32)
                    s_buf[slot][sm, hk:, hq:] = jnp.dot(k[hk:], qt[:, hq:],
                                                        preferred_element_type=F32)
                else:
                    s_buf[slot][sm] = jnp.dot(k, qt, preferred_element_type=F32)

        def stage_softmax(t, slot):
            for sm in range(2):
                st = _unpack_item(tab[item(sm, t)])[2]
                if diag:
                    kr = lax.broadcasted_iota(jnp.int32, (hk, tq), 0)
                    qc = lax.broadcasted_iota(jnp.int32, (hk, tq), 1)
                    top = jnp.where(kr <= qc, s_buf[slot][sm, :hk, :], NEG_BIG)
                    kr2 = lax.broadcasted_iota(jnp.int32, (tk - hk, tq - hq), 0)
                    qc2 = lax.broadcasted_iota(jnp.int32, (tk - hk, tq - hq), 1)
                    low = jnp.where(kr2 <= qc2, s_buf[slot][sm, hk:, hq:], NEG_BIG)
                    top_a, top_b = top[:, :hq], top[:, hq:]
                    m_a = jnp.max(top_a, axis=0, keepdims=True)
                    m_b = jnp.maximum(jnp.max(top_b, axis=0, keepdims=True),
                                      jnp.max(low, axis=0, keepdims=True))
                    m_next = jnp.concatenate([m_a, m_b], axis=1)
                    p_buf[slot][sm, :hk, :hq] = jnp.exp2(top_a - m_a).astype(BF16)
                    p_buf[slot][sm, :hk, hq:] = jnp.exp2(top_b - m_b).astype(BF16)
                    p_buf[slot][sm, hk:, hq:] = jnp.exp2(low - m_b).astype(BF16)
                else:
                    s = s_buf[slot][sm]
                    m_prev = m_st[st]
                    m_next = jnp.maximum(m_prev, jnp.max(s, axis=0, keepdims=True))
                    al_buf[slot][sm] = jnp.exp2(m_prev - m_next)
                    p_buf[slot][sm] = jnp.exp2(s - m_next).astype(BF16)
                m_st[st] = m_next

        def stage_pv(t, slot):
            for sm in range(2):
                _, j, st, head = _unpack_item(tab[item(sm, t)])
                koff = pl.multiple_of(j * tk, tk)
                hoff = pl.multiple_of(head * LANES, LANES)
                vt = v_ref[0, pl.ds(hoff, LANES), pl.ds(koff, tk)]
                if diag:
                    acc_st[st] = jnp.concatenate(
                        [jnp.dot(vt[:, :hk], p_buf[slot][sm, :hk, :hq],
                                 preferred_element_type=F32),
                         jnp.dot(vt, p_buf[slot][sm, :, hq:], preferred_element_type=F32)],
                        axis=1)
                else:
                    pv = jnp.dot(vt, p_buf[slot][sm], preferred_element_type=F32)
                    acc_st[st] = al_buf[slot][sm] * acc_st[st] + pv

        stage_qk(0, 0)
        stage_qk(1, 1)
        stage_softmax(0, 0)

        unroll = DIAG_UNROLL if diag else 2

        def body(u, carry):
            for d in range(unroll):
                t = unroll * u + 1 + d
                stage_qk(t + 1, d % 2)
                stage_softmax(t, (d + 1) % 2)
                stage_pv(t - 1, d % 2)
            return carry

        lax.fori_loop(0, (half + unroll - 1) // unroll, body, 0)

    def run_fixed(tab, n_items, dummy):
        half = (n_items + 1) // 2
        lens = (half, n_items - half)

        def item(stream, t):
            return jnp.where(t < lens[stream], stream * half + t, dummy)

        def stage_probs(t, slot):
            for sm in range(2):
                qrow, j, st, head = _unpack_item(tab[item(sm, t)])
                qoff = pl.multiple_of(qrow * tq, tq)
                koff = pl.multiple_of(j * tk, tk)
                hoff = pl.multiple_of(head * LANES, LANES)
                qt = q_ref[0, pl.ds(hoff, LANES), pl.ds(qoff, tq)]
                k = k_ref[0, pl.ds(koff, tk), pl.ds(hoff, LANES)]
                s = jnp.dot(k, qt, preferred_element_type=F32)
                p_buf[slot][sm] = jnp.exp2(s - m_st[st]).astype(BF16)

        def stage_pv(t, slot):
            for sm in range(2):
                _, j, st, head = _unpack_item(tab[item(sm, t)])
                koff = pl.multiple_of(j * tk, tk)
                hoff = pl.multiple_of(head * LANES, LANES)
                vt = v_ref[0, pl.ds(hoff, LANES), pl.ds(koff, tk)]
                acc_st[st] = acc_st[st] + jnp.dot(vt, p_buf[slot][sm],
                                                  preferred_element_type=F32)

        stage_probs(0, 0)

        def trips(first, unroll):
            def body(u, carry):
                for d in range(unroll):
                    t = first + unroll * u + 1 + d
                    stage_probs(t, (d + 1) % 2)
                    stage_pv(t - 1, d % 2)
                return carry
            return body

        n_long = half // FIXED_UNROLL
        done = n_long * FIXED_UNROLL
        lax.fori_loop(0, n_long, trips(0, FIXED_UNROLL), 0)
        lax.fori_loop(0, (half - done + 1) // 2, trips(done, 2), 0)

    assert FIXED_UNROLL % 2 == 0 and tab_ref.shape[0] == n_diag + 1
    run(tab_ref, n_diag, n_diag, True)

    b = pl.program_id(0)
    pr = pl.program_id(1)

    n_fix = jnp.int32(0)
    n_gen = jnp.int32(0)
    for hh in range(2):
        hd = 2 * pr + hh

        kmax = lax.fori_loop(
            0, nq, lambda j, m, hd=hd: jnp.maximum(m, st_ref[b, j, 0, FOX_HEADS + hd]),
            jnp.float32(0.0))

        def list_row(qi, counts, hh=hh, hd=hd, kmax=kmax):
            base = (st_ref[b, qi, 0, hd] * (st_ref[b, qi, 0, FOX_HEADS + hd] + kmax)
                    + st_ref[b, qi, 1, hd])
            to_fixed = (base - st_ref[b, qi - 1, 2, hd] <= FIXED_REF_T).astype(jnp.int32)
            row_word = _pack_item(qi, 0, hh * nq + qi, hh)

            def contributes(carry):
                j = carry[0]
                bound = base - st_ref[b, jnp.maximum(j, 0), 2, hd]
                return jnp.logical_and(j >= 0, bound > -PRUNE_T)

            def take(carry):
                j, n_f, n_g = carry
                word = row_word | (j << 8)
                off_tab[n_f] = word
                gen_tab[n_g] = word
                return j - 1, n_f + to_fixed, n_g + 1 - to_fixed

            return lax.while_loop(contributes, take, (qi - 1,) + tuple(counts))[1:]

        n_fix, n_gen = lax.fori_loop(1, nq, list_row, (n_fix, n_gen))
    off_tab[n_off] = _pack_item(1, 0, 2 * nq, 0)
    gen_tab[n_off] = _pack_item(1, 0, 2 * nq, 0)
    run_fixed(off_tab, n_fix, n_off)

    @pl.when(n_gen > 0)
    def _():
        run(gen_tab, n_gen, n_off, False)

    def finish(qi, carry):
        a0 = acc_st[qi]
        a1 = acc_st[nq + qi]
        ot = jnp.concatenate([a0[:HEAD_DIM] / a0[HEAD_DIM:HEAD_DIM + 1],
                              a1[HEAD_DIM:] / a1[0:1]], axis=0)
        rows = pl.ds(pl.multiple_of(qi * tq, tq), tq)
        o_ref[0, rows, :] = (ot.T * g_ref[0, rows, :].astype(F32)).astype(BF16)
        return carry

    lax.fori_loop(0, nq, finish, 0, unroll=8)


def _fox_call(qaug_t, kaug, vaug_t, ga, stats):
    B, S, _ = kaug.shape
    tq, tk = TQ_FOX, TK_FOX
    assert tq == tk
    nq = S // tq
    tab, n_diag, n_off = _fox_tables(nq)
    pair = lambda b, p: (b, 0, p)
    return pl.pallas_call(
        functools.partial(_fox_kernel, n_diag=n_diag, n_off=n_off),
        grid=(B, FOX_HEADS // 2),
        in_specs=[pl.BlockSpec(memory_space=pltpu.SMEM),
                  pl.BlockSpec(memory_space=pltpu.SMEM),
                  pl.BlockSpec((1, 2 * LANES, S), lambda b, p: (b, p, 0)),
                  pl.BlockSpec((1, S, 2 * LANES), pair),
                  pl.BlockSpec((1, 2 * LANES, S), lambda b, p: (b, p, 0)),
                  pl.BlockSpec((1, S, LANES), pair)],
        out_specs=pl.BlockSpec((1, S, LANES), pair),
        out_shape=jax.ShapeDtypeStruct((B, S, FOX_W), BF16),
        scratch_shapes=[pltpu.SMEM((n_off + 1,), jnp.int32),
                        pltpu.SMEM((n_off + 1,), jnp.int32),
                        pltpu.VMEM((2, tk, tq), F32), pltpu.VMEM((2, tk, tq), F32),
                        pltpu.VMEM((2, tk, tq), BF16), pltpu.VMEM((2, tk, tq), BF16),
                        pltpu.VMEM((2, 1, tq), F32), pltpu.VMEM((2, 1, tq), F32),
                        pltpu.VMEM((2 * nq + 1, 1, tq), F32),
                        pltpu.VMEM((2 * nq + 1, LANES, tq), F32)],
        compiler_params=pltpu.CompilerParams(
            dimension_semantics=("arbitrary", "arbitrary"), vmem_limit_bytes=VMEM_LIMIT),
        name="fox",
    )(jnp.asarray(tab), stats, qaug_t, kaug, vaug_t, ga)


def _swa_out_kernel(sinks_ref, q_ref, kp_ref, kc_ref, vtp_ref, vtc_ref, g_ref,
                    oa_ref, x_ref, gate_ref, gpost_ref, wa_ref, wb_ref, out_ref,
                    s_sc, p_sc, e_sc, ob_sc):
    i = pl.program_id(1)
    nsub = TQ_SWA // WINDOW
    group = SWA_Q_HEADS // SWA_KV_HEADS
    ncol = group * WINDOW
    kall = jnp.concatenate([kp_ref[0], kc_ref[0]], axis=0)
    vtall = jnp.concatenate([vtp_ref[0], vtc_ref[0]], axis=1)
    lane = lax.broadcasted_iota(jnp.int32, (WINDOW, LANES), 1)
    lo = lane < HEAD_DIM
    zero = jnp.zeros((WINDOW, LANES), BF16)
    kj_ = lax.broadcasted_iota(jnp.int32, (2 * WINDOW, ncol), 0)
    qi_ = lax.broadcasted_iota(jnp.int32, (2 * WINDOW, ncol), 1) & (WINDOW - 1)
    rel = qi_ + WINDOW - kj_
    band = (rel >= 0) & (rel < WINDOW)
    nt = (((1,), (1,)), ((), ()))
    probs = [(r, g) for r in range(nsub) for g in range(SWA_KV_HEADS)]

    def stage_qk(n):
        r, g = probs[n]
        rows = slice(r * WINDOW, (r + 1) * WINDOW)
        kg = kall[r * WINDOW:(r + 2) * WINDOW, g * LANES:(g + 1) * LANES]
        c0 = g * group * HEAD_DIM
        qp0 = q_ref[0, rows, c0:c0 + LANES]
        qp1 = q_ref[0, rows, c0 + LANES:c0 + 2 * LANES]
        qs = jnp.concatenate([jnp.where(lo, qp0, zero), jnp.where(lo, qp1, zero),
                              jnp.where(lo, zero, qp0), jnp.where(lo, zero, qp1)], axis=0)
        s_sc[n] = lax.dot_general(kg, qs, nt, preferred_element_type=F32)

    def stage_softmax(n):
        r, g = probs[n]
        valid = band & (kj_ >= jnp.where(i == 0, WINDOW, 0)) if r == 0 else band
        st = jnp.where(valid, s_sc[n], NEG_BIG)
        heads = (group * g, group * g + 2, group * g + 1, group * g + 3)
        sink = jnp.concatenate(
            [jnp.full((1, WINDOW), sinks_ref[hd] * LOG2E, F32) for hd in heads], axis=1)
        m = jnp.maximum(jnp.max(st, axis=0, keepdims=True), sink)
        p_sc[n] = jnp.exp2(st - m).astype(BF16)
        e_sc[n] = jnp.exp2(sink - m)

    def stage_pv(n):
        r, g = probs[n]
        rows = slice(r * WINDOW, (r + 1) * WINDOW)
        c0 = g * group * HEAD_DIM
        vtg = vtall[g * LANES:(g + 1) * LANES, r * WINDOW:(r + 2) * WINDOW]
        acc = jnp.dot(vtg, p_sc[n], preferred_element_type=F32)
        l = acc[HEAD_DIM:HEAD_DIM + 1] + e_sc[n]
        on = acc[:HEAD_DIM] * (1.0 / l)
        pair0 = jnp.concatenate([on[:, 0:WINDOW], on[:, 2 * WINDOW:3 * WINDOW]], axis=0).T
        pair1 = jnp.concatenate([on[:, WINDOW:2 * WINDOW], on[:, 3 * WINDOW:]], axis=0).T
        ob_sc[rows, c0:c0 + LANES] = (
            pair0 * g_ref[0, rows, c0:c0 + LANES].astype(F32)).astype(BF16)
        ob_sc[rows, c0 + LANES:c0 + 2 * LANES] = (
            pair1 * g_ref[0, rows, c0 + LANES:c0 + 2 * LANES].astype(F32)).astype(BF16)

    for t in range(len(probs) + 2):
        if t < len(probs):
            stage_qk(t)
        if 1 <= t <= len(probs):
            stage_softmax(t - 1)
        if t >= 2:
            stage_pv(t - 2)

    y = (jnp.dot(oa_ref[0], wa_ref[...], preferred_element_type=F32)
         + jnp.dot(ob_sc[...], wb_ref[...], preferred_element_type=F32))
    ms = jnp.mean(y * y, axis=-1, keepdims=True)
    yn = y * lax.rsqrt(ms + RMS_EPS) * gpost_ref[...]
    out_ref[0] = x_ref[0] + gate_ref[0] * yn


def _swa_out_call(sinks, qb, kd, vdt, gb, oa, x, gate, g_post, wa, wb):
    B, S, D = x.shape
    tq = TQ_SWA
    nsub = tq // WINDOW
    cur = lambda b, i: (b, i, 0)
    prev = lambda b, i: (b, jnp.maximum(i * nsub - 1, 0), 0)
    cur_t = lambda b, i: (b, 0, i)
    prev_t = lambda b, i: (b, 0, jnp.maximum(i * nsub - 1, 0))
    const2 = lambda b, i: (0, 0)
    kvw = 2 * SWA_KV_W
    nprob = nsub * SWA_KV_HEADS
    ncol = SWA_Q_HEADS // SWA_KV_HEADS * WINDOW
    return pl.pallas_call(
        _swa_out_kernel,
        grid=(B, S // tq),
        in_specs=[pl.BlockSpec(memory_space=pltpu.SMEM),
                  pl.BlockSpec((1, tq, SWA_W), cur),
                  pl.BlockSpec((1, WINDOW, kvw), prev),
                  pl.BlockSpec((1, tq, kvw), cur),
                  pl.BlockSpec((1, kvw, WINDOW), prev_t),
                  pl.BlockSpec((1, kvw, tq), cur_t),
                  pl.BlockSpec((1, tq, SWA_W), cur),
                  pl.BlockSpec((1, tq, FOX_W), cur),
                  pl.BlockSpec((1, tq, D), cur),
                  pl.BlockSpec((1, 1, D), lambda b, i: (b, 0, 0)),
                  pl.BlockSpec((1, D), const2),
                  pl.BlockSpec((FOX_W, D), const2),
                  pl.BlockSpec((SWA_W, D), const2)],
        out_specs=pl.BlockSpec((1, tq, D), cur),
        out_shape=jax.ShapeDtypeStruct((B, S, D), F32),
        scratch_shapes=[pltpu.VMEM((nprob, 2 * WINDOW, ncol), F32),
                        pltpu.VMEM((nprob, 2 * WINDOW, ncol), BF16),
                        pltpu.VMEM((nprob, 1, ncol), F32),
                        pltpu.VMEM((tq, SWA_W), BF16)],
        compiler_params=pltpu.CompilerParams(
            dimension_semantics=("arbitrary", "arbitrary"), vmem_limit_bytes=VMEM_LIMIT),
        name="swa_out",
    )(sinks, qb, kd, kd, vdt, vdt, gb, oa, x, gate, g_post, wa, wb)


def _wprep_kernel(wt_ref, o_ref):
    sc2 = HEAD_DIM ** -0.5 * LOG2E
    src = {}
    o = 0
    for name, width in (("qa", FOX_W), ("ka", FOX_W), ("va", FOX_W), ("fa", FOX_HEADS),
                        ("za", FOX_W), ("qb", SWA_W), ("kb", SWA_KV_W), ("vb", SWA_KV_W),
                        ("zb", SWA_W)):
        src[name] = (o, width)
        o += width
    for name, dst, scale in (("qa", OFF_QA, sc2), ("ka", OFF_KA, None), ("va", OFF_VA, None),
                             ("za", OFF_ZA, None), ("qb", OFF_QB, sc2), ("zb", OFF_ZB, None),
                             ("kb", OFF_KD, None), ("vb", OFF_VD, None)):
        start, width = src[name]
        rows = wt_ref[start:start + width, :]
        o_ref[dst:dst + width, :] = (rows if scale is None else rows * scale).astype(BF16)
    fa = wt_ref[src["fa"][0]:src["fa"][0] + FOX_HEADS, :]
    o_ref[OFF_F:OFF_F + LANES, :] = jnp.concatenate(
        [fa, fa, fa, jnp.zeros((LANES - 3 * FOX_HEADS, fa.shape[1]), F32)], axis=0).astype(BF16)


def _perm_w_in(w):
    wt = w.T
    return pl.pallas_call(
        _wprep_kernel,
        out_shape=jax.ShapeDtypeStruct((IN_NP, wt.shape[1]), BF16),
        compiler_params=pltpu.CompilerParams(vmem_limit_bytes=VMEM_LIMIT),
        name="w_prep",
    )(wt)


def _aug_constants():
    eq = np.zeros((LANES, AUG_W), np.float32)
    ek = np.zeros((LANES, AUG_W), np.float32)
    oq = np.zeros((1, AUG_W), np.float32)
    ok = np.zeros((1, AUG_W), np.float32)
    for hd in range(FOX_HEADS):
        base = LANES * hd + (HEAD_DIM if hd % 2 == 0 else 0)
        for part in range(3):
            eq[part * FOX_HEADS + hd, base + part] = 1.0
            ok[0, base + part] = 1.0
            ek[part * FOX_HEADS + hd, base + 3 + part] = -1.0
            oq[0, base + 3 + part] = 1.0
    return jnp.asarray(eq + ek, BF16), jnp.asarray(oq), jnp.asarray(ok)


def kernel(x, c, positions, w_ada, b_ada, g_pre, w_in, b_fgate, sinks, w_out, g_post):
    B, S, D = x.shape
    depth = w_ada.shape[0]
    assert TM_SUB == TQ_FOX == TK_FOX
    half = HEAD_DIM // 2
    inv_freq = ROPE_THETA ** (-jnp.arange(half, dtype=F32) / half)
    invf = inv_freq[:, None]
    pos3 = positions[:, None, :]
    ind = np.zeros((2 * FOX_W, LANES), np.float32)
    ind[np.arange(2 * FOX_W), np.arange(2 * FOX_W) // HEAD_DIM] = 1.0
    ind = jnp.asarray(ind, BF16)
    tri = jnp.asarray(np.tril(np.ones((TM_SUB, TM_SUB), np.float32)), BF16)
    eqk, oq, ok = _aug_constants()
    c_pad = jnp.zeros((8, D), F32).at[:B].set(c)
    for l in range(depth):
        mod = _mod_call(c_pad, w_ada[l], b_ada[l][None, :])[:B]
        shift = mod[:, None, 0:D]
        scale = mod[:, None, D:2 * D]
        gate = mod[:, None, 2 * D:3 * D]
        bf_pad = jnp.concatenate(
            [b_fgate[l]] * 3 + [jnp.zeros((LANES - 3 * FOX_HEADS,), F32)])[None, :]
        qaug, kaug, vaug, ga, qb, kd, vd, gb, stats = _inproj_call(
            x, shift, scale, g_pre[l][None, :], _perm_w_in(w_in[l]), pos3, invf, bf_pad,
            tri, eqk, oq, ok, ind)
        oa = _fox_call(qaug, kaug, vaug, ga, stats[:, :, 0:3, 0:2 * FOX_HEADS])
        wo = w_out[l].astype(BF16)
        x = _swa_out_call(sinks[l], qb, kd, vd, gb, oa, x, gate, g_post[l][None, :],
                          wo[:FOX_W], wo[FOX_W:])
    return x
```

```python
import functools

import jax
import jax.numpy as jnp
import numpy as np
from jax import lax
from jax.experimental import pallas as pl
from jax.experimental.pallas import tpu as pltpu

D_MODEL = 1024
HEAD_DIM = 64
FOX_HEADS = 8
SWA_Q_HEADS = 8
SWA_KV_HEADS = 2
WINDOW = 128
ROPE_THETA = 10000.0
RMS_EPS = 1e-6
FOX_W = FOX_HEADS * HEAD_DIM
SWA_W = SWA_Q_HEADS * HEAD_DIM
SWA_KV_W = SWA_KV_HEADS * HEAD_DIM

LANES = 128
AUG_W = FOX_HEADS * LANES

OFF_QA = 0
OFF_KA = OFF_QA + FOX_W
OFF_VA = OFF_KA + FOX_W
OFF_ZA = OFF_VA + FOX_W
OFF_QB = OFF_ZA + FOX_W
OFF_ZB = OFF_QB + SWA_W
OFF_KD = OFF_ZB + SWA_W
OFF_VD = OFF_KD + SWA_KV_W
OFF_F = OFF_VD + SWA_KV_W
IN_NP = OFF_F + LANES

TM_IN = 1024
TM_SUB = 512
TQ_FOX = 512
TK_FOX = 512
FIXED_UNROLL = 8
DIAG_UNROLL = 4
TQ_SWA = 1024
NEG_BIG = -1e30
LOG2E = 1.4426950408889634
PRUNE_T = 140.0
NORM_MARGIN = 1.01
FIXED_REF_T = 100.0
VMEM_LIMIT = 56 * 1024 * 1024

F32 = jnp.float32
BF16 = jnp.bfloat16


def _split3(a):
    hi = a.astype(BF16)
    r = a - hi.astype(F32)
    mid = r.astype(BF16)
    lo = (r - mid.astype(F32)).astype(BF16)
    return hi, mid, lo


def _silu(z):
    hz = 0.5 * z
    return hz + hz * jnp.tanh(hz)


def _mod_kernel(c_ref, w_ref, b_ref, o_ref):
    c = c_ref[...]
    sc = c * (1.0 / (1.0 + jnp.exp(-c)))
    o_ref[...] = jnp.dot(sc, w_ref[...], precision=lax.Precision.HIGHEST,
                         preferred_element_type=F32) + b_ref[...]


def _mod_call(c_pad, w_ada, b_ada):
    rows = c_pad.shape[0]
    n = w_ada.shape[1]
    bn = D_MODEL
    return pl.pallas_call(
        _mod_kernel,
        grid=(n // bn,),
        in_specs=[pl.BlockSpec((rows, D_MODEL), lambda j: (0, 0)),
                  pl.BlockSpec((D_MODEL, bn), lambda j: (0, j)),
                  pl.BlockSpec((1, bn), lambda j: (0, j))],
        out_specs=pl.BlockSpec((rows, bn), lambda j: (0, j)),
        out_shape=jax.ShapeDtypeStruct((rows, n), F32),
        name="mod",
    )(c_pad, w_ada, b_ada)


def _inproj_kernel(x_ref, shift_ref, scale_ref, gpre_ref, w_ref, pos_ref, invf_ref, bf_ref,
                   tri_ref, eqk_ref, oq_ref, ok_ref, ind_ref,
                   qaug_ref, kaug_ref, vaug_ref, ga_ref, qb_ref, kd_ref, vd_ref, gb_ref, st_ref,
                   carry_ref):
    tm = TM_SUB

    @pl.when(pl.program_id(1) == 0)
    def _():
        carry_ref[...] = jnp.zeros_like(carry_ref)

    for sub in range(x_ref.shape[1] // tm):
        _inproj_subtile(sub, slice(sub * tm, (sub + 1) * tm),
                        x_ref, shift_ref, scale_ref, gpre_ref, w_ref, pos_ref, invf_ref, bf_ref,
                        tri_ref, eqk_ref, oq_ref, ok_ref, ind_ref,
                        qaug_ref, kaug_ref, vaug_ref, ga_ref, qb_ref, kd_ref, vd_ref, gb_ref,
                        st_ref, carry_ref)


def _inproj_subtile(sub, rows, x_ref, shift_ref, scale_ref, gpre_ref, w_ref, pos_ref, invf_ref,
                    bf_ref, tri_ref, eqk_ref, oq_ref, ok_ref, ind_ref,
                    qaug_ref, kaug_ref, vaug_ref, ga_ref, qb_ref, kd_ref, vd_ref, gb_ref,
                    st_ref, carry_ref):
    tm = TM_SUB
    x = x_ref[0, rows, :]
    ms = jnp.mean(x * x, axis=-1, keepdims=True)
    gain = gpre_ref[...] * (1.0 + scale_ref[0])
    h = (x * lax.rsqrt(ms + RMS_EPS) * gain + shift_ref[0]).astype(BF16)

    def proj(off, width):
        return lax.dot_general(h, w_ref[off:off + width, :], (((1,), (1,)), ((), ())),
                               preferred_element_type=F32)

    tail = proj(OFF_KD, 3 * LANES)
    f = tail[:, OFF_F - OFF_KD:] + bf_ref[...]
    ls = jnp.minimum(f, 0.0) - jnp.log1p(jnp.exp(-jnp.abs(f)))
    lane = lax.broadcasted_iota(jnp.int32, (tm, LANES), 1)

    def by_group(a, b_, c_):
        return jnp.where(lane < FOX_HEADS, a, jnp.where(lane < 2 * FOX_HEADS, b_, c_))

    part = by_group(*(t.astype(F32) for t in _split3(ls))).astype(BF16)
    psum = jnp.dot(tri_ref[...], part, preferred_element_type=F32)
    cum = (psum + pltpu.roll(psum, LANES - FOX_HEADS, 1)
           + pltpu.roll(psum, LANES - 2 * FOX_HEADS, 1))
    cum = cum + carry_ref[...]
    carry_ref[...] = cum[tm - 1:tm, :]

    cum2 = cum * LOG2E
    cum2 = by_group(cum2, pltpu.roll(cum2, FOX_HEADS, 1), pltpu.roll(cum2, 2 * FOX_HEADS, 1))
    cs = by_group(*(t.astype(F32) for t in _split3(cum2))).astype(BF16)
    placed = jnp.dot(cs, eqk_ref[...], preferred_element_type=F32)
    augq = placed * ok_ref[...] + oq_ref[...]
    augk = placed * oq_ref[...] + ok_ref[...]

    lane_w = lax.broadcasted_iota(jnp.int32, (tm, AUG_W), 1)
    data = (((lane_w >> 6) ^ (lane_w >> 7)) & 1) == 0

    def rep(a):
        return jnp.concatenate(
            [a[:, LANES * (hd // 2):LANES * (hd // 2 + 1)] for hd in range(FOX_HEADS)], axis=1)

    qa = proj(OFF_QA, FOX_W)
    ka = proj(OFF_KA, FOX_W)
    qaug_ref[0, :, rows] = jnp.where(data, rep(qa), augq).T.astype(BF16)
    kaug_ref[0, rows, :] = jnp.where(data, rep(ka), augk).astype(BF16)

    sq = jnp.concatenate([qa * qa, ka * ka], axis=1).astype(BF16)
    nrm2 = jnp.dot(sq, ind_ref[...], preferred_element_type=F32)
    nmax = jnp.sqrt(jnp.max(nrm2, axis=0, keepdims=True)) * NORM_MARGIN
    st_ref[0, sub] = jnp.concatenate(
        [nmax, cum2[0:1], cum2[tm - 1:tm], jnp.zeros((5, LANES), F32)], axis=0)
    vaug_ref[0, :, rows] = jnp.where(data, rep(proj(OFF_VA, FOX_W)), 1.0).T.astype(BF16)

    za = proj(OFF_ZA, FOX_W)
    ga_ref[0, rows, :] = _silu(za).astype(BF16)
    zb = proj(OFF_ZB, SWA_W)
    gb_ref[0, rows, :] = _silu(zb).astype(BF16)

    ang_t = invf_ref[...] * pos_ref[0, :, rows].astype(F32)
    reps = LANES // (HEAD_DIM // 2)
    cosv = jnp.concatenate([jnp.cos(ang_t)] * reps, axis=0).T
    sinv = jnp.concatenate([jnp.sin(ang_t)] * reps, axis=0).T
    first = (lane & (HEAD_DIM // 2)) == 0
    sin_signed = jnp.where(first, -sinv, sinv)

    def rope(a):
        outs = []
        for cidx in range(a.shape[1] // LANES):
            blk = a[:, cidx * LANES:(cidx + 1) * LANES]
            other = jnp.where(first, pltpu.roll(blk, LANES - HEAD_DIM // 2, 1),
                              pltpu.roll(blk, HEAD_DIM // 2, 1))
            outs.append(blk * cosv + other * sin_signed)
        return jnp.concatenate(outs, axis=1)

    qb_ref[0, rows, :] = rope(proj(OFF_QB, SWA_W)).astype(BF16)
    kr = rope(tail[:, :LANES])
    ks = pltpu.roll(kr, HEAD_DIM, 1)
    low = lane < HEAD_DIM
    kd_ref[0, rows, :] = jnp.concatenate([jnp.where(low, kr, ks), jnp.where(low, ks, kr)],
                                axis=1).astype(BF16)
    vt = tail[:, OFF_VD - OFF_KD:OFF_F - OFF_KD].T
    ones = jnp.ones((HEAD_DIM, tm), F32)
    vd_ref[0, :, rows] = jnp.concatenate([vt[:HEAD_DIM], ones, vt[HEAD_DIM:], ones],
                                axis=0).astype(BF16)


def _inproj_call(x, shift, scale, g_pre, w_perm, pos3, invf, bf_pad, tri, eqk, oq, ok, ind):
    B, S, D = x.shape
    tm = TM_IN
    row = lambda b, t: (b, t, 0)
    per_b = lambda b, t: (b, 0, 0)
    const2 = lambda b, t: (0, 0)
    out_w = (AUG_W, AUG_W, AUG_W, FOX_W, SWA_W, 2 * SWA_KV_W, 2 * SWA_KV_W, SWA_W)
    transposed = (0, 2, 6)
    return pl.pallas_call(
        _inproj_kernel,
        grid=(B, S // tm),
        in_specs=[pl.BlockSpec((1, tm, D), row),
                  pl.BlockSpec((1, 1, D), per_b),
                  pl.BlockSpec((1, 1, D), per_b),
                  pl.BlockSpec((1, D), const2),
                  pl.BlockSpec((IN_NP, D), const2),
                  pl.BlockSpec((1, 1, tm), lambda b, t: (b, 0, t)),
                  pl.BlockSpec((HEAD_DIM // 2, 1), const2),
                  pl.BlockSpec((1, LANES), const2),
                  pl.BlockSpec((TM_SUB, TM_SUB), const2),
                  pl.BlockSpec((LANES, AUG_W), const2),
                  pl.BlockSpec((1, AUG_W), const2),
                  pl.BlockSpec((1, AUG_W), const2),
                  pl.BlockSpec((2 * FOX_W, LANES), const2)],
        out_specs=[pl.BlockSpec((1, w, tm), lambda b, t: (b, 0, t)) if i in transposed
                   else pl.BlockSpec((1, tm, w), row) for i, w in enumerate(out_w)]
        + [pl.BlockSpec((1, tm // TM_SUB, 8, LANES), lambda b, t: (b, t, 0, 0))],
        out_shape=[jax.ShapeDtypeStruct((B, w, S) if i in transposed else (B, S, w), BF16)
                   for i, w in enumerate(out_w)]
        + [jax.ShapeDtypeStruct((B, S // TM_SUB, 8, LANES), F32)],
        scratch_shapes=[pltpu.VMEM((1, LANES), F32)],
        compiler_params=pltpu.CompilerParams(
            dimension_semantics=("arbitrary", "arbitrary"), vmem_limit_bytes=VMEM_LIMIT),
        name="in_proj",
    )(x, shift, scale, g_pre, w_perm, pos3, invf, bf_pad, tri, eqk, oq, ok, ind)


def _pack_item(qrow, j, state, head):
    return qrow | (j << 8) | (state << 16) | (head << 24)


def _unpack_item(word):
    return word & 0xFF, (word >> 8) & 0xFF, (word >> 16) & 0xFF, word >> 24


def _fox_kernel(st_ref, q_ref, k_ref, v_ref, g_ref, o_ref,
                dfix_tab, dgen_tab, off_tab, gen_tab, r_tab,
                s0, s1, p0, p1, al0, al1, m_st, acc_st, *, n_off):
    tq = TQ_FOX
    tk = TK_FOX
    nq = q_ref.shape[2] // tq
    s_buf = (s0, s1)
    p_buf = (p0, p1)
    al_buf = (al0, al1)

    def run(tab, n_items, dummy, diag):
        half = (n_items + 1) // 2
        lens = (half, n_items - half)
        hk, hq = tk // 2, tq // 2

        def item(stream, t):
            return jnp.where(t < lens[stream], stream * half + t, dummy)

        def stage_qk(t, slot):
            for sm in range(2):
                qrow, j, _, head = _unpack_item(tab[item(sm, t)])
                qoff = pl.multiple_of(qrow * tq, tq)
                koff = pl.multiple_of(j * tk, tk)
                hoff = pl.multiple_of(head * LANES, LANES)
                qt = q_ref[0, pl.ds(hoff, LANES), pl.ds(qoff, tq)]
                k = k_ref[0, pl.ds(koff, tk), pl.ds(hoff, LANES)]
                if diag:
                    s_buf[slot][sm, :hk, :] = jnp.dot(k[:hk], qt, preferred_element_type=F32)
                    s_buf[slot][sm, hk:, hq:] = jnp.dot(k[hk:], qt[:, hq:],
                                                        preferred_element_type=F32)
                else:
                    s_buf[slot][sm] = jnp.dot(k, qt, preferred_element_type=F32)

        def stage_softmax(t, slot):
            for sm in range(2):
                st = _unpack_item(tab[item(sm, t)])[2]
                if diag:
                    kr = lax.broadcasted_iota(jnp.int32, (hk, tq), 0)
                    qc = lax.broadcasted_iota(jnp.int32, (hk, tq), 1)
                    top = jnp.where(kr <= qc, s_buf[slot][sm, :hk, :], NEG_BIG)
                    kr2 = lax.broadcasted_iota(jnp.int32, (tk - hk, tq - hq), 0)
                    qc2 = lax.broadcasted_iota(jnp.int32, (tk - hk, tq - hq), 1)
                    low = jnp.where(kr2 <= qc2, s_buf[slot][sm, hk:, hq:], NEG_BIG)
                    top_a, top_b = top[:, :hq], top[:, hq:]
                    m_a = jnp.max(top_a, axis=0, keepdims=True)
                    m_b = jnp.maximum(jnp.max(top_b, axis=0, keepdims=True),
                                      jnp.max(low, axis=0, keepdims=True))
                    m_next = jnp.concatenate([m_a, m_b], axis=1)
                    p_buf[slot][sm, :hk, :hq] = jnp.exp2(top_a - m_a).astype(BF16)
                    p_buf[slot][sm, :hk, hq:] = jnp.exp2(top_b - m_b).astype(BF16)
                    p_buf[slot][sm, hk:, hq:] = jnp.exp2(low - m_b).astype(BF16)
                else:
                    s = s_buf[slot][sm]
                    m_prev = m_st[st]
                    m_next = jnp.maximum(m_prev, jnp.max(s, axis=0, keepdims=True))
                    al_buf[slot][sm] = jnp.exp2(m_prev - m_next)
                    p_buf[slot][sm] = jnp.exp2(s - m_next).astype(BF16)
                m_st[st] = m_next

        def stage_pv(t, slot):
            for sm in range(2):
                _, j, st, head = _unpack_item(tab[item(sm, t)])
                koff = pl.multiple_of(j * tk, tk)
                hoff = pl.multiple_of(head * LANES, LANES)
                vt = v_ref[0, pl.ds(hoff, LANES), pl.ds(koff, tk)]
                if diag:
                    acc_st[st] = jnp.concatenate(
                        [jnp.dot(vt[:, :hk], p_buf[slot][sm, :hk, :hq],
                                 preferred_element_type=F32),
                         jnp.dot(vt, p_buf[slot][sm, :, hq:], preferred_element_type=F32)],
                        axis=1)
                else:
                    pv = jnp.dot(vt, p_buf[slot][sm], preferred_element_type=F32)
                    acc_st[st] = al_buf[slot][sm] * acc_st[st] + pv

        stage_qk(0, 0)
        stage_qk(1, 1)
        stage_softmax(0, 0)

        unroll = DIAG_UNROLL if diag else 2

        def body(u, carry):
            for d in range(unroll):
                t = unroll * u + 1 + d
                stage_qk(t + 1, d % 2)
                stage_softmax(t, (d + 1) % 2)
                stage_pv(t - 1, d % 2)
            return carry

        lax.fori_loop(0, (half + unroll - 1) // unroll, body, 0)

    def run_fixed(tab, n_items, dummy, diag):
        half = (n_items + 1) // 2
        lens = (half, n_items - half)
        hk, hq = tk // 2, tq // 2

        def item(stream, t):
            return jnp.where(t < lens[stream], stream * half + t, dummy)

        def stage_probs(t, slot):
            for sm in range(2):
                qrow, j, st, head = _unpack_item(tab[item(sm, t)])
                qoff = pl.multiple_of(qrow * tq, tq)
                koff = pl.multiple_of(j * tk, tk)
                hoff = pl.multiple_of(head * LANES, LANES)
                qt = q_ref[0, pl.ds(hoff, LANES), pl.ds(qoff, tq)]
                k = k_ref[0, pl.ds(koff, tk), pl.ds(hoff, LANES)]
                ref = r_tab[st]
                if diag:
                    kr = lax.broadcasted_iota(jnp.int32, (hk, tq), 0)
                    qc = lax.broadcasted_iota(jnp.int32, (hk, tq), 1)
                    top = jnp.where(kr <= qc, jnp.dot(k[:hk], qt, preferred_element_type=F32),
                                    NEG_BIG)
                    kr2 = lax.broadcasted_iota(jnp.int32, (tk - hk, tq - hq), 0)
                    qc2 = lax.broadcasted_iota(jnp.int32, (tk - hk, tq - hq), 1)
                    low = jnp.where(kr2 <= qc2,
                                    jnp.dot(k[hk:], qt[:, hq:], preferred_element_type=F32),
                                    NEG_BIG)
                    p_buf[slot][sm, :hk, :] = jnp.exp2(top - ref).astype(BF16)
                    p_buf[slot][sm, hk:, hq:] = jnp.exp2(low - ref).astype(BF16)
                else:
                    s = jnp.dot(k, qt, preferred_element_type=F32)
                    p_buf[slot][sm] = jnp.exp2(s - ref).astype(BF16)

        def stage_pv(t, slot):
            for sm in range(2):
                _, j, st, head = _unpack_item(tab[item(sm, t)])
                koff = pl.multiple_of(j * tk, tk)
                hoff = pl.multiple_of(head * LANES, LANES)
                vt = v_ref[0, pl.ds(hoff, LANES), pl.ds(koff, tk)]
                if diag:
                    acc_st[st] = jnp.concatenate(
                        [jnp.dot(vt[:, :hk], p_buf[slot][sm, :hk, :hq],
                                 preferred_element_type=F32),
                         jnp.dot(vt, p_buf[slot][sm, :, hq:], preferred_element_type=F32)],
                        axis=1)
                else:
                    acc_st[st] = acc_st[st] + jnp.dot(vt, p_buf[slot][sm],
                                                      preferred_element_type=F32)

        stage_probs(0, 0)

        def trips(first, unroll):
            def body(u, carry):
                for d in range(unroll):
                    t = first + unroll * u + 1 + d
                    stage_probs(t, (d + 1) % 2)
                    stage_pv(t - 1, d % 2)
                return carry
            return body

        long = DIAG_UNROLL if diag else FIXED_UNROLL
        n_long = half // long
        done = n_long * long
        lax.fori_loop(0, n_long, trips(0, long), 0)
        lax.fori_loop(0, (half - done + 1) // 2, trips(done, 2), 0)

    assert FIXED_UNROLL % 2 == 0 and DIAG_UNROLL % 2 == 0

    b = pl.program_id(0)
    pr = pl.program_id(1)
    counts = (jnp.int32(0),) * 4
    for hh in range(2):
        hd = 2 * pr + hh

        kmax = lax.fori_loop(
            0, nq, lambda j, m, hd=hd: jnp.maximum(m, st_ref[b, j, 0, FOX_HEADS + hd]),
            jnp.float32(0.0))

        def list_row(qi, counts, hh=hh, hd=hd, kmax=kmax):
            qn = st_ref[b, qi, 0, hd]
            spread = qn * (st_ref[b, qi, 0, FOX_HEADS + hd] + kmax)
            to_fixed = (spread <= FIXED_REF_T).astype(jnp.int32)
            base = spread + st_ref[b, qi, 1, hd]
            state = hh * nq + qi
            r_tab[state] = qn * kmax
            row_word = _pack_item(qi, 0, state, hh)
            n_df, n_dg = counts[0], counts[1]
            dfix_tab[n_df] = row_word | (qi << 8)
            dgen_tab[n_dg] = row_word | (qi << 8)

            def contributes(carry):
                j = carry[0]
                bound = base - st_ref[b, jnp.maximum(j, 0), 2, hd]
                return jnp.logical_and(j >= 0, bound > -PRUNE_T)

            def take(carry):
                j, n_f, n_g = carry
                word = row_word | (j << 8)
                off_tab[n_f] = word
                gen_tab[n_g] = word
                return j - 1, n_f + to_fixed, n_g + 1 - to_fixed

            below = lax.while_loop(contributes, take, (qi - 1, counts[2], counts[3]))[1:]
            return (n_df + to_fixed, n_dg + 1 - to_fixed) + tuple(below)

        counts = lax.fori_loop(0, nq, list_row, counts)
    n_dfix, n_dgen, n_fix, n_gen = counts
    for tab in (dfix_tab, dgen_tab, off_tab, gen_tab):
        tab[n_off] = _pack_item(1, 0, 2 * nq, 0)
    r_tab[2 * nq] = jnp.float32(0.0)

    run_fixed(dfix_tab, n_dfix, n_off, True)

    @pl.when(n_dgen > 0)
    def _():
        run(dgen_tab, n_dgen, n_off, True)

    run_fixed(off_tab, n_fix, n_off, False)

    @pl.when(n_gen > 0)
    def _():
        run(gen_tab, n_gen, n_off, False)

    def finish(qi, carry):
        a0 = acc_st[qi]
        a1 = acc_st[nq + qi]
        ot = jnp.concatenate([a0[:HEAD_DIM] / a0[HEAD_DIM:HEAD_DIM + 1],
                              a1[HEAD_DIM:] / a1[0:1]], axis=0)
        rows = pl.ds(pl.multiple_of(qi * tq, tq), tq)
        o_ref[0, rows, :] = (ot.T * g_ref[0, rows, :].astype(F32)).astype(BF16)
        return carry

    lax.fori_loop(0, nq, finish, 0, unroll=8)


def _fox_call(qaug_t, kaug, vaug_t, ga, stats):
    B, S, _ = kaug.shape
    tq, tk = TQ_FOX, TK_FOX
    assert tq == tk
    nq = S // tq
    n_off = nq * (nq - 1)
    pair = lambda b, p: (b, 0, p)
    return pl.pallas_call(
        functools.partial(_fox_kernel, n_off=n_off),
        grid=(B, FOX_HEADS // 2),
        in_specs=[pl.BlockSpec(memory_space=pltpu.SMEM),
                  pl.BlockSpec((1, 2 * LANES, S), lambda b, p: (b, p, 0)),
                  pl.BlockSpec((1, S, 2 * LANES), pair),
                  pl.BlockSpec((1, 2 * LANES, S), lambda b, p: (b, p, 0)),
                  pl.BlockSpec((1, S, LANES), pair)],
        out_specs=pl.BlockSpec((1, S, LANES), pair),
        out_shape=jax.ShapeDtypeStruct((B, S, FOX_W), BF16),
        scratch_shapes=[pltpu.SMEM((n_off + 1,), jnp.int32),
                        pltpu.SMEM((n_off + 1,), jnp.int32),
                        pltpu.SMEM((n_off + 1,), jnp.int32),
                        pltpu.SMEM((n_off + 1,), jnp.int32),
                        pltpu.SMEM((2 * nq + 1,), F32),
                        pltpu.VMEM((2, tk, tq), F32), pltpu.VMEM((2, tk, tq), F32),
                        pltpu.VMEM((2, tk, tq), BF16), pltpu.VMEM((2, tk, tq), BF16),
                        pltpu.VMEM((2, 1, tq), F32), pltpu.VMEM((2, 1, tq), F32),
                        pltpu.VMEM((2 * nq + 1, 1, tq), F32),
                        pltpu.VMEM((2 * nq + 1, LANES, tq), F32)],
        compiler_params=pltpu.CompilerParams(
            dimension_semantics=("arbitrary", "arbitrary"), vmem_limit_bytes=VMEM_LIMIT),
        name="fox",
    )(stats, qaug_t, kaug, vaug_t, ga)


def _swa_out_kernel(sinks_ref, q_ref, kp_ref, kc_ref, vtp_ref, vtc_ref, g_ref,
                    oa_ref, x_ref, gate_ref, gpost_ref, wa_ref, wb_ref, out_ref,
                    s_sc, p_sc, e_sc, ob_sc):
    i = pl.program_id(1)
    nsub = TQ_SWA // WINDOW
    group = SWA_Q_HEADS // SWA_KV_HEADS
    ncol = group * WINDOW
    kall = jnp.concatenate([kp_ref[0], kc_ref[0]], axis=0)
    vtall = jnp.concatenate([vtp_ref[0], vtc_ref[0]], axis=1)
    lane = lax.broadcasted_iota(jnp.int32, (WINDOW, LANES), 1)
    lo = lane < HEAD_DIM
    zero = jnp.zeros((WINDOW, LANES), BF16)
    kj_ = lax.broadcasted_iota(jnp.int32, (2 * WINDOW, ncol), 0)
    qi_ = lax.broadcasted_iota(jnp.int32, (2 * WINDOW, ncol), 1) & (WINDOW - 1)
    rel = qi_ + WINDOW - kj_
    band = (rel >= 0) & (rel < WINDOW)
    nt = (((1,), (1,)), ((), ()))
    probs = [(r, g) for r in range(nsub) for g in range(SWA_KV_HEADS)]

    def stage_qk(n):
        r, g = probs[n]
        rows = slice(r * WINDOW, (r + 1) * WINDOW)
        kg = kall[r * WINDOW:(r + 2) * WINDOW, g * LANES:(g + 1) * LANES]
        c0 = g * group * HEAD_DIM
        qp0 = q_ref[0, rows, c0:c0 + LANES]
        qp1 = q_ref[0, rows, c0 + LANES:c0 + 2 * LANES]
        qs = jnp.concatenate([jnp.where(lo, qp0, zero), jnp.where(lo, qp1, zero),
                              jnp.where(lo, zero, qp0), jnp.where(lo, zero, qp1)], axis=0)
        s_sc[n] = lax.dot_general(kg, qs, nt, preferred_element_type=F32)

    def stage_softmax(n):
        r, g = probs[n]
        valid = band & (kj_ >= jnp.where(i == 0, WINDOW, 0)) if r == 0 else band
        st = jnp.where(valid, s_sc[n], NEG_BIG)
        heads = (group * g, group * g + 2, group * g + 1, group * g + 3)
        sink = jnp.concatenate(
            [jnp.full((1, WINDOW), sinks_ref[hd] * LOG2E, F32) for hd in heads], axis=1)
        m = jnp.maximum(jnp.max(st, axis=0, keepdims=True), sink)
        p_sc[n] = jnp.exp2(st - m).astype(BF16)
        e_sc[n] = jnp.exp2(sink - m)

    def stage_pv(n):
        r, g = probs[n]
        rows = slice(r * WINDOW, (r + 1) * WINDOW)
        c0 = g * group * HEAD_DIM
        vtg = vtall[g * LANES:(g + 1) * LANES, r * WINDOW:(r + 2) * WINDOW]
        acc = jnp.dot(vtg, p_sc[n], preferred_element_type=F32)
        l = acc[HEAD_DIM:HEAD_DIM + 1] + e_sc[n]
        on = acc[:HEAD_DIM] * (1.0 / l)
        pair0 = jnp.concatenate([on[:, 0:WINDOW], on[:, 2 * WINDOW:3 * WINDOW]], axis=0).T
        pair1 = jnp.concatenate([on[:, WINDOW:2 * WINDOW], on[:, 3 * WINDOW:]], axis=0).T
        ob_sc[rows, c0:c0 + LANES] = (
            pair0 * g_ref[0, rows, c0:c0 + LANES].astype(F32)).astype(BF16)
        ob_sc[rows, c0 + LANES:c0 + 2 * LANES] = (
            pair1 * g_ref[0, rows, c0 + LANES:c0 + 2 * LANES].astype(F32)).astype(BF16)

    for t in range(len(probs) + 2):
        if t < len(probs):
            stage_qk(t)
        if 1 <= t <= len(probs):
            stage_softmax(t - 1)
        if t >= 2:
            stage_pv(t - 2)

    y = (jnp.dot(oa_ref[0], wa_ref[...], preferred_element_type=F32)
         + jnp.dot(ob_sc[...], wb_ref[...], preferred_element_type=F32))
    ms = jnp.mean(y * y, axis=-1, keepdims=True)
    yn = y * lax.rsqrt(ms + RMS_EPS) * gpost_ref[...]
    out_ref[0] = x_ref[0] + gate_ref[0] * yn


def _swa_out_call(sinks, qb, kd, vdt, gb, oa, x, gate, g_post, wa, wb):
    B, S, D = x.shape
    tq = TQ_SWA
    nsub = tq // WINDOW
    cur = lambda b, i: (b, i, 0)
    prev = lambda b, i: (b, jnp.maximum(i * nsub - 1, 0), 0)
    cur_t = lambda b, i: (b, 0, i)
    prev_t = lambda b, i: (b, 0, jnp.maximum(i * nsub - 1, 0))
    const2 = lambda b, i: (0, 0)
    kvw = 2 * SWA_KV_W
    nprob = nsub * SWA_KV_HEADS
    ncol = SWA_Q_HEADS // SWA_KV_HEADS * WINDOW
    return pl.pallas_call(
        _swa_out_kernel,
        grid=(B, S // tq),
        in_specs=[pl.BlockSpec(memory_space=pltpu.SMEM),
                  pl.BlockSpec((1, tq, SWA_W), cur),
                  pl.BlockSpec((1, WINDOW, kvw), prev),
                  pl.BlockSpec((1, tq, kvw), cur),
                  pl.BlockSpec((1, kvw, WINDOW), prev_t),
                  pl.BlockSpec((1, kvw, tq), cur_t),
                  pl.BlockSpec((1, tq, SWA_W), cur),
                  pl.BlockSpec((1, tq, FOX_W), cur),
                  pl.BlockSpec((1, tq, D), cur),
                  pl.BlockSpec((1, 1, D), lambda b, i: (b, 0, 0)),
                  pl.BlockSpec((1, D), const2),
                  pl.BlockSpec((FOX_W, D), const2),
                  pl.BlockSpec((SWA_W, D), const2)],
        out_specs=pl.BlockSpec((1, tq, D), cur),
        out_shape=jax.ShapeDtypeStruct((B, S, D), F32),
        scratch_shapes=[pltpu.VMEM((nprob, 2 * WINDOW, ncol), F32),
                        pltpu.VMEM((nprob, 2 * WINDOW, ncol), BF16),
                        pltpu.VMEM((nprob, 1, ncol), F32),
                        pltpu.VMEM((tq, SWA_W), BF16)],
        compiler_params=pltpu.CompilerParams(
            dimension_semantics=("arbitrary", "arbitrary"), vmem_limit_bytes=VMEM_LIMIT),
        name="swa_out",
    )(sinks, qb, kd, kd, vdt, vdt, gb, oa, x, gate, g_post, wa, wb)


def _wprep_kernel(wt_ref, o_ref):
    sc2 = HEAD_DIM ** -0.5 * LOG2E
    src = {}
    o = 0
    for name, width in (("qa", FOX_W), ("ka", FOX_W), ("va", FOX_W), ("fa", FOX_HEADS),
                        ("za", FOX_W), ("qb", SWA_W), ("kb", SWA_KV_W), ("vb", SWA_KV_W),
                        ("zb", SWA_W)):
        src[name] = (o, width)
        o += width
    for name, dst, scale in (("qa", OFF_QA, sc2), ("ka", OFF_KA, None), ("va", OFF_VA, None),
                             ("za", OFF_ZA, None), ("qb", OFF_QB, sc2), ("zb", OFF_ZB, None),
                             ("kb", OFF_KD, None), ("vb", OFF_VD, None)):
        start, width = src[name]
        rows = wt_ref[start:start + width, :]
        o_ref[dst:dst + width, :] = (rows if scale is None else rows * scale).astype(BF16)
    fa = wt_ref[src["fa"][0]:src["fa"][0] + FOX_HEADS, :]
    o_ref[OFF_F:OFF_F + LANES, :] = jnp.concatenate(
        [fa, fa, fa, jnp.zeros((LANES - 3 * FOX_HEADS, fa.shape[1]), F32)], axis=0).astype(BF16)


def _perm_w_in(w):
    wt = w.T
    return pl.pallas_call(
        _wprep_kernel,
        out_shape=jax.ShapeDtypeStruct((IN_NP, wt.shape[1]), BF16),
        compiler_params=pltpu.CompilerParams(vmem_limit_bytes=VMEM_LIMIT),
        name="w_prep",
    )(wt)


def _aug_constants():
    eq = np.zeros((LANES, AUG_W), np.float32)
    ek = np.zeros((LANES, AUG_W), np.float32)
    oq = np.zeros((1, AUG_W), np.float32)
    ok = np.zeros((1, AUG_W), np.float32)
    for hd in range(FOX_HEADS):
        base = LANES * hd + (HEAD_DIM if hd % 2 == 0 else 0)
        for part in range(3):
            eq[part * FOX_HEADS + hd, base + part] = 1.0
            ok[0, base + part] = 1.0
            ek[part * FOX_HEADS + hd, base + 3 + part] = -1.0
            oq[0, base + 3 + part] = 1.0
    return jnp.asarray(eq + ek, BF16), jnp.asarray(oq), jnp.asarray(ok)


def kernel(x, c, positions, w_ada, b_ada, g_pre, w_in, b_fgate, sinks, w_out, g_post):
    B, S, D = x.shape
    depth = w_ada.shape[0]
    assert TM_SUB == TQ_FOX == TK_FOX
    half = HEAD_DIM // 2
    inv_freq = ROPE_THETA ** (-jnp.arange(half, dtype=F32) / half)
    invf = inv_freq[:, None]
    pos3 = positions[:, None, :]
    ind = np.zeros((2 * FOX_W, LANES), np.float32)
    ind[np.arange(2 * FOX_W), np.arange(2 * FOX_W) // HEAD_DIM] = 1.0
    ind = jnp.asarray(ind, BF16)
    tri = jnp.asarray(np.tril(np.ones((TM_SUB, TM_SUB), np.float32)), BF16)
    eqk, oq, ok = _aug_constants()
    c_pad = jnp.zeros((8, D), F32).at[:B].set(c)
    for l in range(depth):
        mod = _mod_call(c_pad, w_ada[l], b_ada[l][None, :])[:B]
        shift = mod[:, None, 0:D]
        scale = mod[:, None, D:2 * D]
        gate = mod[:, None, 2 * D:3 * D]
        bf_pad = jnp.concatenate(
            [b_fgate[l]] * 3 + [jnp.zeros((LANES - 3 * FOX_HEADS,), F32)])[None, :]
        qaug, kaug, vaug, ga, qb, kd, vd, gb, stats = _inproj_call(
            x, shift, scale, g_pre[l][None, :], _perm_w_in(w_in[l]), pos3, invf, bf_pad,
            tri, eqk, oq, ok, ind)
        oa = _fox_call(qaug, kaug, vaug, ga, stats[:, :, 0:3, 0:2 * FOX_HEADS])
        wo = w_out[l].astype(BF16)
        x = _swa_out_call(sinks[l], qb, kd, vd, gb, oa, x, gate, g_post[l][None, :],
                          wo[:FOX_W], wo[FOX_W:])
    return x
```

```python
import functools

import jax
import jax.numpy as jnp
import numpy as np
from jax import lax
from jax.experimental import pallas as pl
from jax.experimental.pallas import tpu as pltpu

D_MODEL = 1024
HEAD_DIM = 64
FOX_HEADS = 8
SWA_Q_HEADS = 8
SWA_KV_HEADS = 2
WINDOW = 128
ROPE_THETA = 10000.0
RMS_EPS = 1e-6
FOX_W = FOX_HEADS * HEAD_DIM
SWA_W = SWA_Q_HEADS * HEAD_DIM
SWA_KV_W = SWA_KV_HEADS * HEAD_DIM

LANES = 128
AUG_W = FOX_HEADS * LANES

OFF_QA = 0
OFF_KA = OFF_QA + FOX_W
OFF_VA = OFF_KA + FOX_W
OFF_ZA = OFF_VA + FOX_W
OFF_QB = OFF_ZA + FOX_W
OFF_ZB = OFF_QB + SWA_W
OFF_KD = OFF_ZB + SWA_W
OFF_VD = OFF_KD + SWA_KV_W
OFF_F = OFF_VD + SWA_KV_W
IN_NP = OFF_F + LANES

TM_IN = 1024
TM_SUB = 512
TQ_FOX = 512
TK_FOX = 512
FIXED_UNROLL = 8
DIAG_UNROLL = 4
TQ_SWA = 1024
NEG_BIG = -1e30
LOG2E = 1.4426950408889634
PRUNE_T = 140.0
NORM_MARGIN = 1.01
FIXED_REF_T = 100.0
VMEM_LIMIT = 56 * 1024 * 1024

F32 = jnp.float32
BF16 = jnp.bfloat16


def _split3(a):
    hi = a.astype(BF16)
    r = a - hi.astype(F32)
    mid = r.astype(BF16)
    lo = (r - mid.astype(F32)).astype(BF16)
    return hi, mid, lo


def _silu(z):
    hz = 0.5 * z
    return hz + hz * jnp.tanh(hz)


def _mod_kernel(c_ref, w_ref, b_ref, o_ref):
    c = c_ref[...]
    sc = c * (1.0 / (1.0 + jnp.exp(-c)))
    o_ref[...] = jnp.dot(sc, w_ref[...], precision=lax.Precision.HIGHEST,
                         preferred_element_type=F32) + b_ref[...]


def _mod_call(c_pad, w_ada, b_ada):
    rows = c_pad.shape[0]
    n = w_ada.shape[1]
    bn = D_MODEL
    return pl.pallas_call(
        _mod_kernel,
        grid=(n // bn,),
        in_specs=[pl.BlockSpec((rows, D_MODEL), lambda j: (0, 0)),
                  pl.BlockSpec((D_MODEL, bn), lambda j: (0, j)),
                  pl.BlockSpec((1, bn), lambda j: (0, j))],
        out_specs=pl.BlockSpec((rows, bn), lambda j: (0, j)),
        out_shape=jax.ShapeDtypeStruct((rows, n), F32),
        name="mod",
    )(c_pad, w_ada, b_ada)


def _inproj_kernel(x_ref, shift_ref, scale_ref, gpre_ref, w_ref, pos_ref, invf_ref, bf_ref,
                   tri_ref, eqk_ref, oq_ref, ok_ref, ind_ref,
                   qaug_ref, kaug_ref, vaug_ref, ga_ref, qb_ref, kd_ref, vd_ref, gb_ref, st_ref,
                   carry_ref):
    tm = TM_SUB

    @pl.when(pl.program_id(1) == 0)
    def _():
        carry_ref[...] = jnp.zeros_like(carry_ref)

    for sub in range(x_ref.shape[1] // tm):
        _inproj_subtile(sub, slice(sub * tm, (sub + 1) * tm),
                        x_ref, shift_ref, scale_ref, gpre_ref, w_ref, pos_ref, invf_ref, bf_ref,
                        tri_ref, eqk_ref, oq_ref, ok_ref, ind_ref,
                        qaug_ref, kaug_ref, vaug_ref, ga_ref, qb_ref, kd_ref, vd_ref, gb_ref,
                        st_ref, carry_ref)


def _inproj_subtile(sub, rows, x_ref, shift_ref, scale_ref, gpre_ref, w_ref, pos_ref, invf_ref,
                    bf_ref, tri_ref, eqk_ref, oq_ref, ok_ref, ind_ref,
                    qaug_ref, kaug_ref, vaug_ref, ga_ref, qb_ref, kd_ref, vd_ref, gb_ref,
                    st_ref, carry_ref):
    tm = TM_SUB
    x = x_ref[0, rows, :]
    ms = jnp.mean(x * x, axis=-1, keepdims=True)
    gain = gpre_ref[...] * (1.0 + scale_ref[0])
    h = (x * lax.rsqrt(ms + RMS_EPS) * gain + shift_ref[0]).astype(BF16)

    def proj(off, width):
        return lax.dot_general(h, w_ref[off:off + width, :], (((1,), (1,)), ((), ())),
                               preferred_element_type=F32)

    tail = proj(OFF_KD, 3 * LANES)
    f = tail[:, OFF_F - OFF_KD:] + bf_ref[...]
    ls = jnp.minimum(f, 0.0) - jnp.log1p(jnp.exp(-jnp.abs(f)))
    lane = lax.broadcasted_iota(jnp.int32, (tm, LANES), 1)

    def by_group(a, b_, c_):
        return jnp.where(lane < FOX_HEADS, a, jnp.where(lane < 2 * FOX_HEADS, b_, c_))

    part = by_group(*(t.astype(F32) for t in _split3(ls))).astype(BF16)
    psum = jnp.dot(tri_ref[...], part, preferred_element_type=F32)
    cum = (psum + pltpu.roll(psum, LANES - FOX_HEADS, 1)
           + pltpu.roll(psum, LANES - 2 * FOX_HEADS, 1))
    cum = cum + carry_ref[...]
    carry_ref[...] = cum[tm - 1:tm, :]

    cum2 = cum * LOG2E
    cum2 = by_group(cum2, pltpu.roll(cum2, FOX_HEADS, 1), pltpu.roll(cum2, 2 * FOX_HEADS, 1))
    cs = by_group(*(t.astype(F32) for t in _split3(cum2))).astype(BF16)
    placed = jnp.dot(cs, eqk_ref[...], preferred_element_type=F32)
    augq = placed * ok_ref[...] + oq_ref[...]
    augk = placed * oq_ref[...] + ok_ref[...]

    lane_w = lax.broadcasted_iota(jnp.int32, (tm, AUG_W), 1)
    data = (((lane_w >> 6) ^ (lane_w >> 7)) & 1) == 0

    def rep(a):
        return jnp.concatenate(
            [a[:, LANES * (hd // 2):LANES * (hd // 2 + 1)] for hd in range(FOX_HEADS)], axis=1)

    qa = proj(OFF_QA, FOX_W)
    ka = proj(OFF_KA, FOX_W)
    qaug_ref[0, :, rows] = jnp.where(data, rep(qa), augq).T.astype(BF16)
    kaug_ref[0, rows, :] = jnp.where(data, rep(ka), augk).astype(BF16)

    sq = jnp.concatenate([qa * qa, ka * ka], axis=1).astype(BF16)
    nrm2 = jnp.dot(sq, ind_ref[...], preferred_element_type=F32)
    nmax = jnp.sqrt(jnp.max(nrm2, axis=0, keepdims=True)) * NORM_MARGIN
    st_ref[0, sub] = jnp.concatenate(
        [nmax, cum2[0:1], cum2[tm - 1:tm], jnp.zeros((5, LANES), F32)], axis=0)
    vaug_ref[0, :, rows] = jnp.where(data, rep(proj(OFF_VA, FOX_W)), 1.0).T.astype(BF16)

    za = proj(OFF_ZA, FOX_W)
    ga_ref[0, rows, :] = _silu(za).astype(BF16)
    zb = proj(OFF_ZB, SWA_W)
    gb_ref[0, rows, :] = _silu(zb).astype(BF16)

    ang_t = invf_ref[...] * pos_ref[0, :, rows].astype(F32)
    reps = LANES // (HEAD_DIM // 2)
    cosv = jnp.concatenate([jnp.cos(ang_t)] * reps, axis=0).T
    sinv = jnp.concatenate([jnp.sin(ang_t)] * reps, axis=0).T
    first = (lane & (HEAD_DIM // 2)) == 0
    sin_signed = jnp.where(first, -sinv, sinv)

    def rope(a):
        outs = []
        for cidx in range(a.shape[1] // LANES):
            blk = a[:, cidx * LANES:(cidx + 1) * LANES]
            other = jnp.where(first, pltpu.roll(blk, LANES - HEAD_DIM // 2, 1),
                              pltpu.roll(blk, HEAD_DIM // 2, 1))
            outs.append(blk * cosv + other * sin_signed)
        return jnp.concatenate(outs, axis=1)

    qb_ref[0, rows, :] = rope(proj(OFF_QB, SWA_W)).astype(BF16)
    kr = rope(tail[:, :LANES])
    ks = pltpu.roll(kr, HEAD_DIM, 1)
    low = lane < HEAD_DIM
    kd_ref[0, rows, :] = jnp.concatenate([jnp.where(low, kr, ks), jnp.where(low, ks, kr)],
                                axis=1).astype(BF16)
    vt = tail[:, OFF_VD - OFF_KD:OFF_F - OFF_KD].T
    ones = jnp.ones((HEAD_DIM, tm), F32)
    vd_ref[0, :, rows] = jnp.concatenate([vt[:HEAD_DIM], ones, vt[HEAD_DIM:], ones],
                                axis=0).astype(BF16)


def _inproj_call(x, shift, scale, g_pre, w_perm, pos3, invf, bf_pad, tri, eqk, oq, ok, ind):
    B, S, D = x.shape
    tm = TM_IN
    row = lambda b, t: (b, t, 0)
    per_b = lambda b, t: (b, 0, 0)
    const2 = lambda b, t: (0, 0)
    out_w = (AUG_W, AUG_W, AUG_W, FOX_W, SWA_W, 2 * SWA_KV_W, 2 * SWA_KV_W, SWA_W)
    transposed = (0, 2, 6)
    return pl.pallas_call(
        _inproj_kernel,
        grid=(B, S // tm),
        in_specs=[pl.BlockSpec((1, tm, D), row),
                  pl.BlockSpec((1, 1, D), per_b),
                  pl.BlockSpec((1, 1, D), per_b),
                  pl.BlockSpec((1, D), const2),
                  pl.BlockSpec((IN_NP, D), const2),
                  pl.BlockSpec((1, 1, tm), lambda b, t: (b, 0, t)),
                  pl.BlockSpec((HEAD_DIM // 2, 1), const2),
                  pl.BlockSpec((1, LANES), const2),
                  pl.BlockSpec((TM_SUB, TM_SUB), const2),
                  pl.BlockSpec((LANES, AUG_W), const2),
                  pl.BlockSpec((1, AUG_W), const2),
                  pl.BlockSpec((1, AUG_W), const2),
                  pl.BlockSpec((2 * FOX_W, LANES), const2)],
        out_specs=[pl.BlockSpec((1, w, tm), lambda b, t: (b, 0, t)) if i in transposed
                   else pl.BlockSpec((1, tm, w), row) for i, w in enumerate(out_w)]
        + [pl.BlockSpec((1, tm // TM_SUB, 8, LANES), lambda b, t: (b, t, 0, 0))],
        out_shape=[jax.ShapeDtypeStruct((B, w, S) if i in transposed else (B, S, w), BF16)
                   for i, w in enumerate(out_w)]
        + [jax.ShapeDtypeStruct((B, S // TM_SUB, 8, LANES), F32)],
        scratch_shapes=[pltpu.VMEM((1, LANES), F32)],
        compiler_params=pltpu.CompilerParams(
            dimension_semantics=("arbitrary", "arbitrary"), vmem_limit_bytes=VMEM_LIMIT),
        name="in_proj",
    )(x, shift, scale, g_pre, w_perm, pos3, invf, bf_pad, tri, eqk, oq, ok, ind)


def _pack_item(qrow, j, state, head):
    return qrow | (j << 8) | (state << 16) | (head << 24)


def _unpack_item(word):
    return word & 0xFF, (word >> 8) & 0xFF, (word >> 16) & 0xFF, word >> 24


def _fox_kernel(st_ref, q_ref, k_ref, v_ref, g_ref, o_ref,
                dfix_tab, dgen_tab, off_tab, gen_tab, r_tab,
                s0, s1, p0, p1, al0, al1, m_st, acc_st, *, n_off):
    tq = TQ_FOX
    tk = TK_FOX
    nq = q_ref.shape[2] // tq
    s_buf = (s0, s1)
    p_buf = (p0, p1)
    al_buf = (al0, al1)

    def run(tab, n_items, dummy, diag):
        half = (n_items + 1) // 2
        lens = (half, n_items - half)
        hk, hq = tk // 2, tq // 2

        def item(stream, t):
            return jnp.where(t < lens[stream], stream * half + t, dummy)

        def stage_qk(t, slot):
            for sm in range(2):
                qrow, j, _, head = _unpack_item(tab[item(sm, t)])
                qoff = pl.multiple_of(qrow * tq, tq)
                koff = pl.multiple_of(j * tk, tk)
                hoff = pl.multiple_of(head * LANES, LANES)
                qt = q_ref[0, pl.ds(hoff, LANES), pl.ds(qoff, tq)]
                k = k_ref[0, pl.ds(koff, tk), pl.ds(hoff, LANES)]
                if diag:
                    s_buf[slot][sm, :hk, :] = jnp.dot(k[:hk], qt, preferred_element_type=F32)
                    s_buf[slot][sm, hk:, hq:] = jnp.dot(k[hk:], qt[:, hq:],
                                                        preferred_element_type=F32)
                else:
                    s_buf[slot][sm] = jnp.dot(k, qt, preferred_element_type=F32)

        def stage_softmax(t, slot):
            for sm in range(2):
                st = _unpack_item(tab[item(sm, t)])[2]
                if diag:
                    kr = lax.broadcasted_iota(jnp.int32, (hk, tq), 0)
                    qc = lax.broadcasted_iota(jnp.int32, (hk, tq), 1)
                    top = jnp.where(kr <= qc, s_buf[slot][sm, :hk, :], NEG_BIG)
                    kr2 = lax.broadcasted_iota(jnp.int32, (tk - hk, tq - hq), 0)
                    qc2 = lax.broadcasted_iota(jnp.int32, (tk - hk, tq - hq), 1)
                    low = jnp.where(kr2 <= qc2, s_buf[slot][sm, hk:, hq:], NEG_BIG)
                    top_a, top_b = top[:, :hq], top[:, hq:]
                    m_a = jnp.max(top_a, axis=0, keepdims=True)
                    m_b = jnp.maximum(jnp.max(top_b, axis=0, keepdims=True),
                                      jnp.max(low, axis=0, keepdims=True))
                    m_next = jnp.concatenate([m_a, m_b], axis=1)
                    p_buf[slot][sm, :hk, :hq] = jnp.exp2(top_a - m_a).astype(BF16)
                    p_buf[slot][sm, :hk, hq:] = jnp.exp2(top_b - m_b).astype(BF16)
                    p_buf[slot][sm, hk:, hq:] = jnp.exp2(low - m_b).astype(BF16)
                else:
                    s = s_buf[slot][sm]
                    m_prev = m_st[st]
                    m_next = jnp.maximum(m_prev, jnp.max(s, axis=0, keepdims=True))
                    al_buf[slot][sm] = jnp.exp2(m_prev - m_next)
                    p_buf[slot][sm] = jnp.exp2(s - m_next).astype(BF16)
                m_st[st] = m_next

        def stage_pv(t, slot):
            for sm in range(2):
                _, j, st, head = _unpack_item(tab[item(sm, t)])
                koff = pl.multiple_of(j * tk, tk)
                hoff = pl.multiple_of(head * LANES, LANES)
                vt = v_ref[0, pl.ds(hoff, LANES), pl.ds(koff, tk)]
                if diag:
                    acc_st[st] = jnp.concatenate(
                        [jnp.dot(vt[:, :hk], p_buf[slot][sm, :hk, :hq],
                                 preferred_element_type=F32),
                         jnp.dot(vt, p_buf[slot][sm, :, hq:], preferred_element_type=F32)],
                        axis=1)
                else:
                    pv = jnp.dot(vt, p_buf[slot][sm], preferred_element_type=F32)
                    acc_st[st] = al_buf[slot][sm] * acc_st[st] + pv

        stage_qk(0, 0)
        stage_qk(1, 1)
        stage_softmax(0, 0)

        unroll = DIAG_UNROLL if diag else 2

        def body(u, carry):
            for d in range(unroll):
                t = unroll * u + 1 + d
                stage_qk(t + 1, d % 2)
                stage_softmax(t, (d + 1) % 2)
                stage_pv(t - 1, d % 2)
            return carry

        lax.fori_loop(0, (half + unroll - 1) // unroll, body, 0)

    def run_fixed(tab, n_items, dummy, diag):
        half = (n_items + 1) // 2
        lens = (half, n_items - half)
        hk, hq = tk // 2, tq // 2

        def item(stream, t):
            return jnp.where(t < lens[stream], stream * half + t, dummy)

        def stage_probs(t, slot):
            for sm in range(2):
                qrow, j, st, head = _unpack_item(tab[item(sm, t)])
                qoff = pl.multiple_of(qrow * tq, tq)
                koff = pl.multiple_of(j * tk, tk)
                hoff = pl.multiple_of(head * LANES, LANES)
                qt = q_ref[0, pl.ds(hoff, LANES), pl.ds(qoff, tq)]
                k = k_ref[0, pl.ds(koff, tk), pl.ds(hoff, LANES)]
                ref = r_tab[st]
                if diag:
                    kr = lax.broadcasted_iota(jnp.int32, (hk, tq), 0)
                    qc = lax.broadcasted_iota(jnp.int32, (hk, tq), 1)
                    top = jnp.where(kr <= qc, jnp.dot(k[:hk], qt, preferred_element_type=F32),
                                    NEG_BIG)
                    kr2 = lax.broadcasted_iota(jnp.int32, (tk - hk, tq - hq), 0)
                    qc2 = lax.broadcasted_iota(jnp.int32, (tk - hk, tq - hq), 1)
                    low = jnp.where(kr2 <= qc2,
                                    jnp.dot(k[hk:], qt[:, hq:], preferred_element_type=F32),
                                    NEG_BIG)
                    p_buf[slot][sm, :hk, :] = jnp.exp2(top - ref).astype(BF16)
                    p_buf[slot][sm, hk:, hq:] = jnp.exp2(low - ref).astype(BF16)
                else:
                    s = jnp.dot(k, qt, preferred_element_type=F32)
                    p_buf[slot][sm] = jnp.exp2(s - ref).astype(BF16)

        def stage_pv(t, slot):
            for sm in range(2):
                _, j, st, head = _unpack_item(tab[item(sm, t)])
                koff = pl.multiple_of(j * tk, tk)
                hoff = pl.multiple_of(head * LANES, LANES)
                vt = v_ref[0, pl.ds(hoff, LANES), pl.ds(koff, tk)]
                if diag:
                    acc_st[st] = jnp.concatenate(
                        [jnp.dot(vt[:, :hk], p_buf[slot][sm, :hk, :hq],
                                 preferred_element_type=F32),
                         jnp.dot(vt, p_buf[slot][sm, :, hq:], preferred_element_type=F32)],
                        axis=1)
                else:
                    acc_st[st] = acc_st[st] + jnp.dot(vt, p_buf[slot][sm],
                                                      preferred_element_type=F32)

        stage_probs(0, 0)

        def trips(first, unroll):
            def body(u, carry):
                for d in range(unroll):
                    t = first + unroll * u + 1 + d
                    stage_probs(t, (d + 1) % 2)
                    stage_pv(t - 1, d % 2)
                return carry
            return body

        long = FIXED_UNROLL
        n_long = half // long
        done = n_long * long
        lax.fori_loop(0, n_long, trips(0, long), 0)
        lax.fori_loop(0, (half - done + 1) // 2, trips(done, 2), 0)

    assert FIXED_UNROLL % 2 == 0 and DIAG_UNROLL % 2 == 0

    b = pl.program_id(0)
    pr = pl.program_id(1)
    counts = (jnp.int32(0),) * 4
    for hh in range(2):
        hd = 2 * pr + hh

        kmax = lax.fori_loop(
            0, nq, lambda j, m, hd=hd: jnp.maximum(m, st_ref[b, j, 0, FOX_HEADS + hd]),
            jnp.float32(0.0))

        def list_row(qi, counts, hh=hh, hd=hd, kmax=kmax):
            qn = st_ref[b, qi, 0, hd]
            spread = qn * (st_ref[b, qi, 0, FOX_HEADS + hd] + kmax)
            to_fixed = (spread <= FIXED_REF_T).astype(jnp.int32)
            base = spread + st_ref[b, qi, 1, hd]
            state = hh * nq + qi
            r_tab[state] = qn * kmax
            row_word = _pack_item(qi, 0, state, hh)
            n_df, n_dg = counts[0], counts[1]
            dfix_tab[n_df] = row_word | (qi << 8)
            dgen_tab[n_dg] = row_word | (qi << 8)

            def contributes(carry):
                j = carry[0]
                bound = base - st_ref[b, jnp.maximum(j, 0), 2, hd]
                return jnp.logical_and(j >= 0, bound > -PRUNE_T)

            def take(carry):
                j, n_f, n_g = carry
                word = row_word | (j << 8)
                off_tab[n_f] = word
                gen_tab[n_g] = word
                return j - 1, n_f + to_fixed, n_g + 1 - to_fixed

            below = lax.while_loop(contributes, take, (qi - 1, counts[2], counts[3]))[1:]
            return (n_df + to_fixed, n_dg + 1 - to_fixed) + tuple(below)

        counts = lax.fori_loop(0, nq, list_row, counts)
    n_dfix, n_dgen, n_fix, n_gen = counts
    for tab in (dfix_tab, dgen_tab, off_tab, gen_tab):
        tab[n_off] = _pack_item(1, 0, 2 * nq, 0)
    r_tab[2 * nq] = jnp.float32(0.0)

    run_fixed(dfix_tab, n_dfix, n_off, True)

    @pl.when(n_dgen > 0)
    def _():
        run(dgen_tab, n_dgen, n_off, True)

    run_fixed(off_tab, n_fix, n_off, False)

    @pl.when(n_gen > 0)
    def _():
        run(gen_tab, n_gen, n_off, False)

    def finish(qi, carry):
        a0 = acc_st[qi]
        a1 = acc_st[nq + qi]
        ot = jnp.concatenate([a0[:HEAD_DIM] / a0[HEAD_DIM:HEAD_DIM + 1],
                              a1[HEAD_DIM:] / a1[0:1]], axis=0)
        rows = pl.ds(pl.multiple_of(qi * tq, tq), tq)
        o_ref[0, rows, :] = (ot.T * g_ref[0, rows, :].astype(F32)).astype(BF16)
        return carry

    lax.fori_loop(0, nq, finish, 0, unroll=8)


def _fox_call(qaug_t, kaug, vaug_t, ga, stats):
    B, S, _ = kaug.shape
    tq, tk = TQ_FOX, TK_FOX
    assert tq == tk
    nq = S // tq
    n_off = nq * (nq - 1)
    pair = lambda b, p: (b, 0, p)
    return pl.pallas_call(
        functools.partial(_fox_kernel, n_off=n_off),
        grid=(B, FOX_HEADS // 2),
        in_specs=[pl.BlockSpec(memory_space=pltpu.SMEM),
                  pl.BlockSpec((1, 2 * LANES, S), lambda b, p: (b, p, 0)),
                  pl.BlockSpec((1, S, 2 * LANES), pair),
                  pl.BlockSpec((1, 2 * LANES, S), lambda b, p: (b, p, 0)),
                  pl.BlockSpec((1, S, LANES), pair)],
        out_specs=pl.BlockSpec((1, S, LANES), pair),
        out_shape=jax.ShapeDtypeStruct((B, S, FOX_W), BF16),
        scratch_shapes=[pltpu.SMEM((n_off + 1,), jnp.int32),
                        pltpu.SMEM((n_off + 1,), jnp.int32),
                        pltpu.SMEM((n_off + 1,), jnp.int32),
                        pltpu.SMEM((n_off + 1,), jnp.int32),
                        pltpu.SMEM((2 * nq + 1,), F32),
                        pltpu.VMEM((2, tk, tq), F32), pltpu.VMEM((2, tk, tq), F32),
                        pltpu.VMEM((2, tk, tq), BF16), pltpu.VMEM((2, tk, tq), BF16),
                        pltpu.VMEM((2, 1, tq), F32), pltpu.VMEM((2, 1, tq), F32),
                        pltpu.VMEM((2 * nq + 1, 1, tq), F32),
                        pltpu.VMEM((2 * nq + 1, LANES, tq), F32)],
        compiler_params=pltpu.CompilerParams(
            dimension_semantics=("arbitrary", "arbitrary"), vmem_limit_bytes=VMEM_LIMIT),
        name="fox",
    )(stats, qaug_t, kaug, vaug_t, ga)


def _swa_out_kernel(sinks_ref, q_ref, kp_ref, kc_ref, vtp_ref, vtc_ref, g_ref,
                    oa_ref, x_ref, gate_ref, gpost_ref, wa_ref, wb_ref, out_ref,
                    s_sc, p_sc, e_sc, ob_sc):
    i = pl.program_id(1)
    nsub = TQ_SWA // WINDOW
    group = SWA_Q_HEADS // SWA_KV_HEADS
    ncol = group * WINDOW
    kall = jnp.concatenate([kp_ref[0], kc_ref[0]], axis=0)
    vtall = jnp.concatenate([vtp_ref[0], vtc_ref[0]], axis=1)
    lane = lax.broadcasted_iota(jnp.int32, (WINDOW, LANES), 1)
    lo = lane < HEAD_DIM
    zero = jnp.zeros((WINDOW, LANES), BF16)
    kj_ = lax.broadcasted_iota(jnp.int32, (2 * WINDOW, ncol), 0)
    qi_ = lax.broadcasted_iota(jnp.int32, (2 * WINDOW, ncol), 1) & (WINDOW - 1)
    rel = qi_ + WINDOW - kj_
    band = (rel >= 0) & (rel < WINDOW)
    nt = (((1,), (1,)), ((), ()))
    probs = [(r, g) for r in range(nsub) for g in range(SWA_KV_HEADS)]

    def stage_qk(n):
        r, g = probs[n]
        rows = slice(r * WINDOW, (r + 1) * WINDOW)
        kg = kall[r * WINDOW:(r + 2) * WINDOW, g * LANES:(g + 1) * LANES]
        c0 = g * group * HEAD_DIM
        qp0 = q_ref[0, rows, c0:c0 + LANES]
        qp1 = q_ref[0, rows, c0 + LANES:c0 + 2 * LANES]
        qs = jnp.concatenate([jnp.where(lo, qp0, zero), jnp.where(lo, qp1, zero),
                              jnp.where(lo, zero, qp0), jnp.where(lo, zero, qp1)], axis=0)
        s_sc[n] = lax.dot_general(kg, qs, nt, preferred_element_type=F32)

    def stage_softmax(n):
        r, g = probs[n]
        valid = band & (kj_ >= jnp.where(i == 0, WINDOW, 0)) if r == 0 else band
        st = jnp.where(valid, s_sc[n], NEG_BIG)
        heads = (group * g, group * g + 2, group * g + 1, group * g + 3)
        sink = jnp.concatenate(
            [jnp.full((1, WINDOW), sinks_ref[hd] * LOG2E, F32) for hd in heads], axis=1)
        m = jnp.maximum(jnp.max(st, axis=0, keepdims=True), sink)
        p_sc[n] = jnp.exp2(st - m).astype(BF16)
        e_sc[n] = jnp.exp2(sink - m)

    def stage_pv(n):
        r, g = probs[n]
        rows = slice(r * WINDOW, (r + 1) * WINDOW)
        c0 = g * group * HEAD_DIM
        vtg = vtall[g * LANES:(g + 1) * LANES, r * WINDOW:(r + 2) * WINDOW]
        acc = jnp.dot(vtg, p_sc[n], preferred_element_type=F32)
        l = acc[HEAD_DIM:HEAD_DIM + 1] + e_sc[n]
        on = acc[:HEAD_DIM] * (1.0 / l)
        pair0 = jnp.concatenate([on[:, 0:WINDOW], on[:, 2 * WINDOW:3 * WINDOW]], axis=0).T
        pair1 = jnp.concatenate([on[:, WINDOW:2 * WINDOW], on[:, 3 * WINDOW:]], axis=0).T
        ob_sc[rows, c0:c0 + LANES] = (
            pair0 * g_ref[0, rows, c0:c0 + LANES].astype(F32)).astype(BF16)
        ob_sc[rows, c0 + LANES:c0 + 2 * LANES] = (
            pair1 * g_ref[0, rows, c0 + LANES:c0 + 2 * LANES].astype(F32)).astype(BF16)

    for t in range(len(probs) + 2):
        if t < len(probs):
            stage_qk(t)
        if 1 <= t <= len(probs):
            stage_softmax(t - 1)
        if t >= 2:
            stage_pv(t - 2)

    y = (jnp.dot(oa_ref[0], wa_ref[...], preferred_element_type=F32)
         + jnp.dot(ob_sc[...], wb_ref[...], preferred_element_type=F32))
    ms = jnp.mean(y * y, axis=-1, keepdims=True)
    yn = y * lax.rsqrt(ms + RMS_EPS) * gpost_ref[...]
    out_ref[0] = x_ref[0] + gate_ref[0] * yn


def _swa_out_call(sinks, qb, kd, vdt, gb, oa, x, gate, g_post, wa, wb):
    B, S, D = x.shape
    tq = TQ_SWA
    nsub = tq // WINDOW
    cur = lambda b, i: (b, i, 0)
    prev = lambda b, i: (b, jnp.maximum(i * nsub - 1, 0), 0)
    cur_t = lambda b, i: (b, 0, i)
    prev_t = lambda b, i: (b, 0, jnp.maximum(i * nsub - 1, 0))
    const2 = lambda b, i: (0, 0)
    kvw = 2 * SWA_KV_W
    nprob = nsub * SWA_KV_HEADS
    ncol = SWA_Q_HEADS // SWA_KV_HEADS * WINDOW
    return pl.pallas_call(
        _swa_out_kernel,
        grid=(B, S // tq),
        in_specs=[pl.BlockSpec(memory_space=pltpu.SMEM),
                  pl.BlockSpec((1, tq, SWA_W), cur),
                  pl.BlockSpec((1, WINDOW, kvw), prev),
                  pl.BlockSpec((1, tq, kvw), cur),
                  pl.BlockSpec((1, kvw, WINDOW), prev_t),
                  pl.BlockSpec((1, kvw, tq), cur_t),
                  pl.BlockSpec((1, tq, SWA_W), cur),
                  pl.BlockSpec((1, tq, FOX_W), cur),
                  pl.BlockSpec((1, tq, D), cur),
                  pl.BlockSpec((1, 1, D), lambda b, i: (b, 0, 0)),
                  pl.BlockSpec((1, D), const2),
                  pl.BlockSpec((FOX_W, D), const2),
                  pl.BlockSpec((SWA_W, D), const2)],
        out_specs=pl.BlockSpec((1, tq, D), cur),
        out_shape=jax.ShapeDtypeStruct((B, S, D), F32),
        scratch_shapes=[pltpu.VMEM((nprob, 2 * WINDOW, ncol), F32),
                        pltpu.VMEM((nprob, 2 * WINDOW, ncol), BF16),
                        pltpu.VMEM((nprob, 1, ncol), F32),
                        pltpu.VMEM((tq, SWA_W), BF16)],
        compiler_params=pltpu.CompilerParams(
            dimension_semantics=("arbitrary", "arbitrary"), vmem_limit_bytes=VMEM_LIMIT),
        name="swa_out",
    )(sinks, qb, kd, kd, vdt, vdt, gb, oa, x, gate, g_post, wa, wb)


def _wprep_kernel(wt_ref, o_ref):
    sc2 = HEAD_DIM ** -0.5 * LOG2E
    src = {}
    o = 0
    for name, width in (("qa", FOX_W), ("ka", FOX_W), ("va", FOX_W), ("fa", FOX_HEADS),
                        ("za", FOX_W), ("qb", SWA_W), ("kb", SWA_KV_W), ("vb", SWA_KV_W),
                        ("zb", SWA_W)):
        src[name] = (o, width)
        o += width
    for name, dst, scale in (("qa", OFF_QA, sc2), ("ka", OFF_KA, None), ("va", OFF_VA, None),
                             ("za", OFF_ZA, None), ("qb", OFF_QB, sc2), ("zb", OFF_ZB, None),
                             ("kb", OFF_KD, None), ("vb", OFF_VD, None)):
        start, width = src[name]
        rows = wt_ref[start:start + width, :]
        o_ref[dst:dst + width, :] = (rows if scale is None else rows * scale).astype(BF16)
    fa = wt_ref[src["fa"][0]:src["fa"][0] + FOX_HEADS, :]
    o_ref[OFF_F:OFF_F + LANES, :] = jnp.concatenate(
        [fa, fa, fa, jnp.zeros((LANES - 3 * FOX_HEADS, fa.shape[1]), F32)], axis=0).astype(BF16)


def _perm_w_in(w):
    wt = w.T
    return pl.pallas_call(
        _wprep_kernel,
        out_shape=jax.ShapeDtypeStruct((IN_NP, wt.shape[1]), BF16),
        compiler_params=pltpu.CompilerParams(vmem_limit_bytes=VMEM_LIMIT),
        name="w_prep",
    )(wt)


def _aug_constants():
    eq = np.zeros((LANES, AUG_W), np.float32)
    ek = np.zeros((LANES, AUG_W), np.float32)
    oq = np.zeros((1, AUG_W), np.float32)
    ok = np.zeros((1, AUG_W), np.float32)
    for hd in range(FOX_HEADS):
        base = LANES * hd + (HEAD_DIM if hd % 2 == 0 else 0)
        for part in range(3):
            eq[part * FOX_HEADS + hd, base + part] = 1.0
            ok[0, base + part] = 1.0
            ek[part * FOX_HEADS + hd, base + 3 + part] = -1.0
            oq[0, base + 3 + part] = 1.0
    return jnp.asarray(eq + ek, BF16), jnp.asarray(oq), jnp.asarray(ok)


def kernel(x, c, positions, w_ada, b_ada, g_pre, w_in, b_fgate, sinks, w_out, g_post):
    B, S, D = x.shape
    depth = w_ada.shape[0]
    assert TM_SUB == TQ_FOX == TK_FOX
    half = HEAD_DIM // 2
    inv_freq = ROPE_THETA ** (-jnp.arange(half, dtype=F32) / half)
    invf = inv_freq[:, None]
    pos3 = positions[:, None, :]
    ind = np.zeros((2 * FOX_W, LANES), np.float32)
    ind[np.arange(2 * FOX_W), np.arange(2 * FOX_W) // HEAD_DIM] = 1.0
    ind = jnp.asarray(ind, BF16)
    tri = jnp.asarray(np.tril(np.ones((TM_SUB, TM_SUB), np.float32)), BF16)
    eqk, oq, ok = _aug_constants()
    c_pad = jnp.zeros((8, D), F32).at[:B].set(c)
    for l in range(depth):
        mod = _mod_call(c_pad, w_ada[l], b_ada[l][None, :])[:B]
        shift = mod[:, None, 0:D]
        scale = mod[:, None, D:2 * D]
        gate = mod[:, None, 2 * D:3 * D]
        bf_pad = jnp.concatenate(
            [b_fgate[l]] * 3 + [jnp.zeros((LANES - 3 * FOX_HEADS,), F32)])[None, :]
        qaug, kaug, vaug, ga, qb, kd, vd, gb, stats = _inproj_call(
            x, shift, scale, g_pre[l][None, :], _perm_w_in(w_in[l]), pos3, invf, bf_pad,
            tri, eqk, oq, ok, ind)
        oa = _fox_call(qaug, kaug, vaug, ga, stats[:, :, 0:3, 0:2 * FOX_HEADS])
        wo = w_out[l].astype(BF16)
        x = _swa_out_call(sinks[l], qb, kd, vd, gb, oa, x, gate, g_post[l][None, :],
                          wo[:FOX_W], wo[FOX_W:])
    return x
```

```python
import functools

import jax
import jax.numpy as jnp
import numpy as np
from jax import lax
from jax.experimental import pallas as pl
from jax.experimental.pallas import tpu as pltpu

D_MODEL = 1024
HEAD_DIM = 64
FOX_HEADS = 8
SWA_Q_HEADS = 8
SWA_KV_HEADS = 2
WINDOW = 128
ROPE_THETA = 10000.0
RMS_EPS = 1e-6
FOX_W = FOX_HEADS * HEAD_DIM
SWA_W = SWA_Q_HEADS * HEAD_DIM
SWA_KV_W = SWA_KV_HEADS * HEAD_DIM

LANES = 128
AUG_W = FOX_HEADS * LANES

OFF_QA = 0
OFF_KA = OFF_QA + FOX_W
OFF_VA = OFF_KA + FOX_W
OFF_ZA = OFF_VA + FOX_W
OFF_QB = OFF_ZA + FOX_W
OFF_ZB = OFF_QB + SWA_W
OFF_KD = OFF_ZB + SWA_W
OFF_VD = OFF_KD + SWA_KV_W
OFF_F = OFF_VD + SWA_KV_W
IN_NP = OFF_F + LANES

TM_IN = 1024
TM_SUB = 512
TQ_FOX = 512
TK_FOX = 512
FIXED_UNROLL = 8
DIAG_UNROLL = 4
TQ_SWA = 1024
NEG_BIG = -1e30
LOG2E = 1.4426950408889634
PRUNE_T = 140.0
NORM_MARGIN = 1.01
FIXED_REF_T = 100.0
VMEM_LIMIT = 56 * 1024 * 1024

F32 = jnp.float32
BF16 = jnp.bfloat16


def _split3(a):
    hi = a.astype(BF16)
    r = a - hi.astype(F32)
    mid = r.astype(BF16)
    lo = (r - mid.astype(F32)).astype(BF16)
    return hi, mid, lo


def _silu(z):
    hz = 0.5 * z
    return hz + hz * jnp.tanh(hz)


def _mod_kernel(c_ref, w_ref, b_ref, o_ref):
    c = c_ref[...]
    sc = c * (1.0 / (1.0 + jnp.exp(-c)))
    o_ref[...] = jnp.dot(sc, w_ref[...], precision=lax.Precision.HIGHEST,
                         preferred_element_type=F32) + b_ref[...]


def _mod_call(c_pad, w_ada, b_ada):
    rows = c_pad.shape[0]
    n = w_ada.shape[1]
    bn = D_MODEL
    return pl.pallas_call(
        _mod_kernel,
        grid=(n // bn,),
        in_specs=[pl.BlockSpec((rows, D_MODEL), lambda j: (0, 0)),
                  pl.BlockSpec((D_MODEL, bn), lambda j: (0, j)),
                  pl.BlockSpec((1, bn), lambda j: (0, j))],
        out_specs=pl.BlockSpec((rows, bn), lambda j: (0, j)),
        out_shape=jax.ShapeDtypeStruct((rows, n), F32),
        name="mod",
    )(c_pad, w_ada, b_ada)


def _inproj_kernel(x_ref, shift_ref, scale_ref, gpre_ref, w_ref, pos_ref, invf_ref, bf_ref,
                   tri_ref, eqk_ref, oq_ref, ok_ref, ind_ref,
                   qaug_ref, kaug_ref, vaug_ref, ga_ref, qb_ref, kd_ref, vd_ref, gb_ref, st_ref,
                   carry_ref):
    tm = TM_SUB

    @pl.when(pl.program_id(1) == 0)
    def _():
        carry_ref[...] = jnp.zeros_like(carry_ref)

    for sub in range(x_ref.shape[1] // tm):
        _inproj_subtile(sub, slice(sub * tm, (sub + 1) * tm),
                        x_ref, shift_ref, scale_ref, gpre_ref, w_ref, pos_ref, invf_ref, bf_ref,
                        tri_ref, eqk_ref, oq_ref, ok_ref, ind_ref,
                        qaug_ref, kaug_ref, vaug_ref, ga_ref, qb_ref, kd_ref, vd_ref, gb_ref,
                        st_ref, carry_ref)


def _inproj_subtile(sub, rows, x_ref, shift_ref, scale_ref, gpre_ref, w_ref, pos_ref, invf_ref,
                    bf_ref, tri_ref, eqk_ref, oq_ref, ok_ref, ind_ref,
                    qaug_ref, kaug_ref, vaug_ref, ga_ref, qb_ref, kd_ref, vd_ref, gb_ref,
                    st_ref, carry_ref):
    tm = TM_SUB
    x = x_ref[0, rows, :]
    ms = jnp.mean(x * x, axis=-1, keepdims=True)
    gain = gpre_ref[...] * (1.0 + scale_ref[0])
    h = (x * lax.rsqrt(ms + RMS_EPS) * gain + shift_ref[0]).astype(BF16)

    def proj(off, width):
        return lax.dot_general(h, w_ref[off:off + width, :], (((1,), (1,)), ((), ())),
                               preferred_element_type=F32)

    tail = proj(OFF_KD, 3 * LANES)
    f = tail[:, OFF_F - OFF_KD:] + bf_ref[...]
    ls = jnp.minimum(f, 0.0) - jnp.log1p(jnp.exp(-jnp.abs(f)))
    lane = lax.broadcasted_iota(jnp.int32, (tm, LANES), 1)

    def by_group(a, b_, c_):
        return jnp.where(lane < FOX_HEADS, a, jnp.where(lane < 2 * FOX_HEADS, b_, c_))

    part = by_group(*(t.astype(F32) for t in _split3(ls))).astype(BF16)
    psum = jnp.dot(tri_ref[...], part, preferred_element_type=F32)
    cum = (psum + pltpu.roll(psum, LANES - FOX_HEADS, 1)
           + pltpu.roll(psum, LANES - 2 * FOX_HEADS, 1))
    cum = cum + carry_ref[...]
    carry_ref[...] = cum[tm - 1:tm, :]

    cum2 = cum * LOG2E
    cum2 = by_group(cum2, pltpu.roll(cum2, FOX_HEADS, 1), pltpu.roll(cum2, 2 * FOX_HEADS, 1))
    cs = by_group(*(t.astype(F32) for t in _split3(cum2))).astype(BF16)
    placed = jnp.dot(cs, eqk_ref[...], preferred_element_type=F32)
    augq = placed * ok_ref[...] + oq_ref[...]
    augk = placed * oq_ref[...] + ok_ref[...]

    lane_w = lax.broadcasted_iota(jnp.int32, (tm, AUG_W), 1)
    data = (((lane_w >> 6) ^ (lane_w >> 7)) & 1) == 0

    def rep(a):
        return jnp.concatenate(
            [a[:, LANES * (hd // 2):LANES * (hd // 2 + 1)] for hd in range(FOX_HEADS)], axis=1)

    qa = proj(OFF_QA, FOX_W)
    ka = proj(OFF_KA, FOX_W)
    qaug_ref[0, :, rows] = jnp.where(data, rep(qa), augq).T.astype(BF16)
    kaug_ref[0, rows, :] = jnp.where(data, rep(ka), augk).astype(BF16)

    sq = jnp.concatenate([qa * qa, ka * ka], axis=1).astype(BF16)
    nrm2 = jnp.dot(sq, ind_ref[...], preferred_element_type=F32)
    nmax = jnp.sqrt(jnp.max(nrm2, axis=0, keepdims=True)) * NORM_MARGIN
    st_ref[0, sub] = jnp.concatenate(
        [nmax, cum2[0:1], cum2[tm - 1:tm], jnp.zeros((5, LANES), F32)], axis=0)
    vaug_ref[0, :, rows] = jnp.where(data, rep(proj(OFF_VA, FOX_W)), 1.0).T.astype(BF16)

    za = proj(OFF_ZA, FOX_W)
    ga_ref[0, rows, :] = _silu(za).astype(BF16)
    zb = proj(OFF_ZB, SWA_W)
    gb_ref[0, rows, :] = _silu(zb).astype(BF16)

    ang_t = invf_ref[...] * pos_ref[0, :, rows].astype(F32)
    reps = LANES // (HEAD_DIM // 2)
    cosv = jnp.concatenate([jnp.cos(ang_t)] * reps, axis=0).T
    sinv = jnp.concatenate([jnp.sin(ang_t)] * reps, axis=0).T
    first = (lane & (HEAD_DIM // 2)) == 0
    sin_signed = jnp.where(first, -sinv, sinv)

    def rope(a):
        outs = []
        for cidx in range(a.shape[1] // LANES):
            blk = a[:, cidx * LANES:(cidx + 1) * LANES]
            other = jnp.where(first, pltpu.roll(blk, LANES - HEAD_DIM // 2, 1),
                              pltpu.roll(blk, HEAD_DIM // 2, 1))
            outs.append(blk * cosv + other * sin_signed)
        return jnp.concatenate(outs, axis=1)

    qb_ref[0, rows, :] = rope(proj(OFF_QB, SWA_W)).astype(BF16)
    kr = rope(tail[:, :LANES])
    ks = pltpu.roll(kr, HEAD_DIM, 1)
    low = lane < HEAD_DIM
    kd_ref[0, rows, :] = jnp.concatenate([jnp.where(low, kr, ks), jnp.where(low, ks, kr)],
                                axis=1).astype(BF16)
    vt = tail[:, OFF_VD - OFF_KD:OFF_F - OFF_KD].T
    ones = jnp.ones((HEAD_DIM, tm), F32)
    vd_ref[0, :, rows] = jnp.concatenate([vt[:HEAD_DIM], ones, vt[HEAD_DIM:], ones],
                                axis=0).astype(BF16)


def _inproj_call(x, shift, scale, g_pre, w_perm, pos3, invf, bf_pad, tri, eqk, oq, ok, ind):
    B, S, D = x.shape
    tm = TM_IN
    row = lambda b, t: (b, t, 0)
    per_b = lambda b, t: (b, 0, 0)
    const2 = lambda b, t: (0, 0)
    out_w = (AUG_W, AUG_W, AUG_W, FOX_W, SWA_W, 2 * SWA_KV_W, 2 * SWA_KV_W, SWA_W)
    transposed = (0, 2, 6)
    return pl.pallas_call(
        _inproj_kernel,
        grid=(B, S // tm),
        in_specs=[pl.BlockSpec((1, tm, D), row),
                  pl.BlockSpec((1, 1, D), per_b),
                  pl.BlockSpec((1, 1, D), per_b),
                  pl.BlockSpec((1, D), const2),
                  pl.BlockSpec((IN_NP, D), const2),
                  pl.BlockSpec((1, 1, tm), lambda b, t: (b, 0, t)),
                  pl.BlockSpec((HEAD_DIM // 2, 1), const2),
                  pl.BlockSpec((1, LANES), const2),
                  pl.BlockSpec((TM_SUB, TM_SUB), const2),
                  pl.BlockSpec((LANES, AUG_W), const2),
                  pl.BlockSpec((1, AUG_W), const2),
                  pl.BlockSpec((1, AUG_W), const2),
                  pl.BlockSpec((2 * FOX_W, LANES), const2)],
        out_specs=[pl.BlockSpec((1, w, tm), lambda b, t: (b, 0, t)) if i in transposed
                   else pl.BlockSpec((1, tm, w), row) for i, w in enumerate(out_w)]
        + [pl.BlockSpec((1, tm // TM_SUB, 8, LANES), lambda b, t: (b, t, 0, 0))],
        out_shape=[jax.ShapeDtypeStruct((B, w, S) if i in transposed else (B, S, w), BF16)
                   for i, w in enumerate(out_w)]
        + [jax.ShapeDtypeStruct((B, S // TM_SUB, 8, LANES), F32)],
        scratch_shapes=[pltpu.VMEM((1, LANES), F32)],
        compiler_params=pltpu.CompilerParams(
            dimension_semantics=("arbitrary", "arbitrary"), vmem_limit_bytes=VMEM_LIMIT),
        name="in_proj",
    )(x, shift, scale, g_pre, w_perm, pos3, invf, bf_pad, tri, eqk, oq, ok, ind)


def _pack_item(qrow, j, state, head):
    return qrow | (j << 8) | (state << 16) | (head << 24)


def _unpack_item(word):
    return word & 0xFF, (word >> 8) & 0xFF, (word >> 16) & 0xFF, word >> 24


def _fox_kernel(st_ref, q_ref, k_ref, v_ref, g_ref, o_ref,
                dfix_tab, dgen_tab, off_tab, gen_tab, r_tab,
                s0, s1, p0, p1, al0, al1, m_st, acc_st, *, n_off):
    tq = TQ_FOX
    tk = TK_FOX
    nq = q_ref.shape[2] // tq
    s_buf = (s0, s1)
    p_buf = (p0, p1)
    al_buf = (al0, al1)

    def run(tab, n_items, dummy, diag):
        half = (n_items + 1) // 2
        lens = (half, n_items - half)
        hk, hq = tk // 2, tq // 2

        def item(stream, t):
            return jnp.where(t < lens[stream], stream * half + t, dummy)

        def stage_qk(t, slot):
            for sm in range(2):
                qrow, j, _, head = _unpack_item(tab[item(sm, t)])
                qoff = pl.multiple_of(qrow * tq, tq)
                koff = pl.multiple_of(j * tk, tk)
                hoff = pl.multiple_of(head * LANES, LANES)
                qt = q_ref[0, pl.ds(hoff, LANES), pl.ds(qoff, tq)]
                k = k_ref[0, pl.ds(koff, tk), pl.ds(hoff, LANES)]
                if diag:
                    s_buf[slot][sm, :hk, :] = jnp.dot(k[:hk], qt, preferred_element_type=F32)
                    s_buf[slot][sm, hk:, hq:] = jnp.dot(k[hk:], qt[:, hq:],
                                                        preferred_element_type=F32)
                else:
                    s_buf[slot][sm] = jnp.dot(k, qt, preferred_element_type=F32)

        def stage_softmax(t, slot):
            for sm in range(2):
                st = _unpack_item(tab[item(sm, t)])[2]
                if diag:
                    kr = lax.broadcasted_iota(jnp.int32, (hk, tq), 0)
                    qc = lax.broadcasted_iota(jnp.int32, (hk, tq), 1)
                    top = jnp.where(kr <= qc, s_buf[slot][sm, :hk, :], NEG_BIG)
                    kr2 = lax.broadcasted_iota(jnp.int32, (tk - hk, tq - hq), 0)
                    qc2 = lax.broadcasted_iota(jnp.int32, (tk - hk, tq - hq), 1)
                    low = jnp.where(kr2 <= qc2, s_buf[slot][sm, hk:, hq:], NEG_BIG)
                    top_a, top_b = top[:, :hq], top[:, hq:]
                    m_a = jnp.max(top_a, axis=0, keepdims=True)
                    m_b = jnp.maximum(jnp.max(top_b, axis=0, keepdims=True),
                                      jnp.max(low, axis=0, keepdims=True))
                    m_next = jnp.concatenate([m_a, m_b], axis=1)
                    p_buf[slot][sm, :hk, :hq] = jnp.exp2(top_a - m_a).astype(BF16)
                    p_buf[slot][sm, :hk, hq:] = jnp.exp2(top_b - m_b).astype(BF16)
                    p_buf[slot][sm, hk:, hq:] = jnp.exp2(low - m_b).astype(BF16)
                else:
                    s = s_buf[slot][sm]
                    m_prev = m_st[st]
                    m_next = jnp.maximum(m_prev, jnp.max(s, axis=0, keepdims=True))
                    al_buf[slot][sm] = jnp.exp2(m_prev - m_next)
                    p_buf[slot][sm] = jnp.exp2(s - m_next).astype(BF16)
                m_st[st] = m_next

        def stage_pv(t, slot):
            for sm in range(2):
                _, j, st, head = _unpack_item(tab[item(sm, t)])
                koff = pl.multiple_of(j * tk, tk)
                hoff = pl.multiple_of(head * LANES, LANES)
                vt = v_ref[0, pl.ds(hoff, LANES), pl.ds(koff, tk)]
                if diag:
                    acc_st[st] = jnp.concatenate(
                        [jnp.dot(vt[:, :hk], p_buf[slot][sm, :hk, :hq],
                                 preferred_element_type=F32),
                         jnp.dot(vt, p_buf[slot][sm, :, hq:], preferred_element_type=F32)],
                        axis=1)
                else:
                    pv = jnp.dot(vt, p_buf[slot][sm], preferred_element_type=F32)
                    acc_st[st] = al_buf[slot][sm] * acc_st[st] + pv

        stage_qk(0, 0)
        stage_qk(1, 1)
        stage_softmax(0, 0)

        unroll = DIAG_UNROLL if diag else 2

        def body(u, carry):
            for d in range(unroll):
                t = unroll * u + 1 + d
                stage_qk(t + 1, d % 2)
                stage_softmax(t, (d + 1) % 2)
                stage_pv(t - 1, d % 2)
            return carry

        lax.fori_loop(0, (half + unroll - 1) // unroll, body, 0)

    def run_fixed(tab, n_items, dummy, diag):
        half = (n_items + 1) // 2
        lens = (half, n_items - half)
        hk, hq = tk // 2, tq // 2

        def item(stream, t):
            return jnp.where(t < lens[stream], stream * half + t, dummy)

        def stage_probs(t, slot):
            for sm in range(2):
                qrow, j, st, head = _unpack_item(tab[item(sm, t)])
                qoff = pl.multiple_of(qrow * tq, tq)
                koff = pl.multiple_of(j * tk, tk)
                hoff = pl.multiple_of(head * LANES, LANES)
                qt = q_ref[0, pl.ds(hoff, LANES), pl.ds(qoff, tq)]
                k = k_ref[0, pl.ds(koff, tk), pl.ds(hoff, LANES)]
                ref = r_tab[st]
                if diag:
                    kr = lax.broadcasted_iota(jnp.int32, (hk, tq), 0)
                    qc = lax.broadcasted_iota(jnp.int32, (hk, tq), 1)
                    top = jnp.where(kr <= qc, jnp.dot(k[:hk], qt, preferred_element_type=F32),
                                    NEG_BIG)
                    kr2 = lax.broadcasted_iota(jnp.int32, (tk - hk, tq - hq), 0)
                    qc2 = lax.broadcasted_iota(jnp.int32, (tk - hk, tq - hq), 1)
                    low = jnp.where(kr2 <= qc2,
                                    jnp.dot(k[hk:], qt[:, hq:], preferred_element_type=F32),
                                    NEG_BIG)
                    p_buf[slot][sm, :hk, :] = jnp.exp2(top - ref).astype(BF16)
                    p_buf[slot][sm, hk:, hq:] = jnp.exp2(low - ref).astype(BF16)
                else:
                    s = jnp.dot(k, qt, preferred_element_type=F32)
                    p_buf[slot][sm] = jnp.exp2(s - ref).astype(BF16)

        def stage_pv(t, slot):
            for sm in range(2):
                _, j, st, head = _unpack_item(tab[item(sm, t)])
                koff = pl.multiple_of(j * tk, tk)
                hoff = pl.multiple_of(head * LANES, LANES)
                vt = v_ref[0, pl.ds(hoff, LANES), pl.ds(koff, tk)]
                if diag:
                    acc_st[st] = jnp.concatenate(
                        [jnp.dot(vt[:, :hk], p_buf[slot][sm, :hk, :hq],
                                 preferred_element_type=F32),
                         jnp.dot(vt, p_buf[slot][sm, :, hq:], preferred_element_type=F32)],
                        axis=1)
                else:
                    acc_st[st] = acc_st[st] + jnp.dot(vt, p_buf[slot][sm],
                                                      preferred_element_type=F32)

        stage_probs(0, 0)

        def trips(first, unroll):
            def body(u, carry):
                for d in range(unroll):
                    t = first + unroll * u + 1 + d
                    stage_probs(t, (d + 1) % 2)
                    stage_pv(t - 1, d % 2)
                return carry
            return body

        done = 0
        for unroll in (FIXED_UNROLL, FIXED_UNROLL // 2):
            n_trips = (half - done) // unroll
            lax.fori_loop(0, n_trips, trips(done, unroll), 0)
            done = done + n_trips * unroll
        lax.fori_loop(0, (half - done + 1) // 2, trips(done, 2), 0)

    assert FIXED_UNROLL % 4 == 0 and DIAG_UNROLL % 2 == 0

    b = pl.program_id(0)
    pr = pl.program_id(1)
    counts = (jnp.int32(0),) * 4
    for hh in range(2):
        hd = 2 * pr + hh

        kmax = lax.fori_loop(
            0, nq, lambda j, m, hd=hd: jnp.maximum(m, st_ref[b, j, 0, FOX_HEADS + hd]),
            jnp.float32(0.0))

        def list_row(qi, counts, hh=hh, hd=hd, kmax=kmax):
            qn = st_ref[b, qi, 0, hd]
            spread = qn * (st_ref[b, qi, 0, FOX_HEADS + hd] + kmax)
            to_fixed = (spread <= FIXED_REF_T).astype(jnp.int32)
            base = spread + st_ref[b, qi, 1, hd]
            state = hh * nq + qi
            r_tab[state] = qn * kmax
            row_word = _pack_item(qi, 0, state, hh)
            n_df, n_dg = counts[0], counts[1]
            dfix_tab[n_df] = row_word | (qi << 8)
            dgen_tab[n_dg] = row_word | (qi << 8)

            def contributes(carry):
                j = carry[0]
                bound = base - st_ref[b, jnp.maximum(j, 0), 2, hd]
                return jnp.logical_and(j >= 0, bound > -PRUNE_T)

            def take(carry):
                j, n_f, n_g = carry
                word = row_word | (j << 8)
                off_tab[n_f] = word
                gen_tab[n_g] = word
                return j - 1, n_f + to_fixed, n_g + 1 - to_fixed

            below = lax.while_loop(contributes, take, (qi - 1, counts[2], counts[3]))[1:]
            return (n_df + to_fixed, n_dg + 1 - to_fixed) + tuple(below)

        counts = lax.fori_loop(0, nq, list_row, counts)
    n_dfix, n_dgen, n_fix, n_gen = counts
    for tab in (dfix_tab, dgen_tab, off_tab, gen_tab):
        tab[n_off] = _pack_item(1, 0, 2 * nq, 0)
    r_tab[2 * nq] = jnp.float32(0.0)

    run_fixed(dfix_tab, n_dfix, n_off, True)

    @pl.when(n_dgen > 0)
    def _():
        run(dgen_tab, n_dgen, n_off, True)

    run_fixed(off_tab, n_fix, n_off, False)

    @pl.when(n_gen > 0)
    def _():
        run(gen_tab, n_gen, n_off, False)

    def finish(qi, carry):
        a0 = acc_st[qi]
        a1 = acc_st[nq + qi]
        ot = jnp.concatenate([a0[:HEAD_DIM] / a0[HEAD_DIM:HEAD_DIM + 1],
                              a1[HEAD_DIM:] / a1[0:1]], axis=0)
        rows = pl.ds(pl.multiple_of(qi * tq, tq), tq)
        o_ref[0, rows, :] = (ot.T * g_ref[0, rows, :].astype(F32)).astype(BF16)
        return carry

    lax.fori_loop(0, nq, finish, 0, unroll=8)


def _fox_call(qaug_t, kaug, vaug_t, ga, stats):
    B, S, _ = kaug.shape
    tq, tk = TQ_FOX, TK_FOX
    assert tq == tk
    nq = S // tq
    n_off = nq * (nq - 1)
    pair = lambda b, p: (b, 0, p)
    return pl.pallas_call(
        functools.partial(_fox_kernel, n_off=n_off),
        grid=(B, FOX_HEADS // 2),
        in_specs=[pl.BlockSpec(memory_space=pltpu.SMEM),
                  pl.BlockSpec((1, 2 * LANES, S), lambda b, p: (b, p, 0)),
                  pl.BlockSpec((1, S, 2 * LANES), pair),
                  pl.BlockSpec((1, 2 * LANES, S), lambda b, p: (b, p, 0)),
                  pl.BlockSpec((1, S, LANES), pair)],
        out_specs=pl.BlockSpec((1, S, LANES), pair),
        out_shape=jax.ShapeDtypeStruct((B, S, FOX_W), BF16),
        scratch_shapes=[pltpu.SMEM((n_off + 1,), jnp.int32),
                        pltpu.SMEM((n_off + 1,), jnp.int32),
                        pltpu.SMEM((n_off + 1,), jnp.int32),
                        pltpu.SMEM((n_off + 1,), jnp.int32),
                        pltpu.SMEM((2 * nq + 1,), F32),
                        pltpu.VMEM((2, tk, tq), F32), pltpu.VMEM((2, tk, tq), F32),
                        pltpu.VMEM((2, tk, tq), BF16), pltpu.VMEM((2, tk, tq), BF16),
                        pltpu.VMEM((2, 1, tq), F32), pltpu.VMEM((2, 1, tq), F32),
                        pltpu.VMEM((2 * nq + 1, 1, tq), F32),
                        pltpu.VMEM((2 * nq + 1, LANES, tq), F32)],
        compiler_params=pltpu.CompilerParams(
            dimension_semantics=("arbitrary", "arbitrary"), vmem_limit_bytes=VMEM_LIMIT),
        name="fox",
    )(stats, qaug_t, kaug, vaug_t, ga)


def _swa_out_kernel(sinks_ref, q_ref, kp_ref, kc_ref, vtp_ref, vtc_ref, g_ref,
                    oa_ref, x_ref, gate_ref, gpost_ref, wa_ref, wb_ref, out_ref,
                    s_sc, p_sc, e_sc, ob_sc):
    i = pl.program_id(1)
    nsub = TQ_SWA // WINDOW
    group = SWA_Q_HEADS // SWA_KV_HEADS
    ncol = group * WINDOW
    kall = jnp.concatenate([kp_ref[0], kc_ref[0]], axis=0)
    vtall = jnp.concatenate([vtp_ref[0], vtc_ref[0]], axis=1)
    lane = lax.broadcasted_iota(jnp.int32, (WINDOW, LANES), 1)
    lo = lane < HEAD_DIM
    zero = jnp.zeros((WINDOW, LANES), BF16)
    kj_ = lax.broadcasted_iota(jnp.int32, (2 * WINDOW, ncol), 0)
    qi_ = lax.broadcasted_iota(jnp.int32, (2 * WINDOW, ncol), 1) & (WINDOW - 1)
    rel = qi_ + WINDOW - kj_
    band = (rel >= 0) & (rel < WINDOW)
    nt = (((1,), (1,)), ((), ()))
    probs = [(r, g) for r in range(nsub) for g in range(SWA_KV_HEADS)]

    def stage_qk(n):
        r, g = probs[n]
        rows = slice(r * WINDOW, (r + 1) * WINDOW)
        kg = kall[r * WINDOW:(r + 2) * WINDOW, g * LANES:(g + 1) * LANES]
        c0 = g * group * HEAD_DIM
        qp0 = q_ref[0, rows, c0:c0 + LANES]
        qp1 = q_ref[0, rows, c0 + LANES:c0 + 2 * LANES]
        qs = jnp.concatenate([jnp.where(lo, qp0, zero), jnp.where(lo, qp1, zero),
                              jnp.where(lo, zero, qp0), jnp.where(lo, zero, qp1)], axis=0)
        s_sc[n] = lax.dot_general(kg, qs, nt, preferred_element_type=F32)

    def stage_softmax(n):
        r, g = probs[n]
        valid = band & (kj_ >= jnp.where(i == 0, WINDOW, 0)) if r == 0 else band
        st = jnp.where(valid, s_sc[n], NEG_BIG)
        heads = (group * g, group * g + 2, group * g + 1, group * g + 3)
        sink = jnp.concatenate(
            [jnp.full((1, WINDOW), sinks_ref[hd] * LOG2E, F32) for hd in heads], axis=1)
        m = jnp.maximum(jnp.max(st, axis=0, keepdims=True), sink)
        p_sc[n] = jnp.exp2(st - m).astype(BF16)
        e_sc[n] = jnp.exp2(sink - m)

    def stage_pv(n):
        r, g = probs[n]
        rows = slice(r * WINDOW, (r + 1) * WINDOW)
        c0 = g * group * HEAD_DIM
        vtg = vtall[g * LANES:(g + 1) * LANES, r * WINDOW:(r + 2) * WINDOW]
        acc = jnp.dot(vtg, p_sc[n], preferred_element_type=F32)
        l = acc[HEAD_DIM:HEAD_DIM + 1] + e_sc[n]
        on = acc[:HEAD_DIM] * (1.0 / l)
        pair0 = jnp.concatenate([on[:, 0:WINDOW], on[:, 2 * WINDOW:3 * WINDOW]], axis=0).T
        pair1 = jnp.concatenate([on[:, WINDOW:2 * WINDOW], on[:, 3 * WINDOW:]], axis=0).T
        ob_sc[rows, c0:c0 + LANES] = (
            pair0 * g_ref[0, rows, c0:c0 + LANES].astype(F32)).astype(BF16)
        ob_sc[rows, c0 + LANES:c0 + 2 * LANES] = (
            pair1 * g_ref[0, rows, c0 + LANES:c0 + 2 * LANES].astype(F32)).astype(BF16)

    for t in range(len(probs) + 2):
        if t < len(probs):
            stage_qk(t)
        if 1 <= t <= len(probs):
            stage_softmax(t - 1)
        if t >= 2:
            stage_pv(t - 2)

    y = (jnp.dot(oa_ref[0], wa_ref[...], preferred_element_type=F32)
         + jnp.dot(ob_sc[...], wb_ref[...], preferred_element_type=F32))
    ms = jnp.mean(y * y, axis=-1, keepdims=True)
    yn = y * lax.rsqrt(ms + RMS_EPS) * gpost_ref[...]
    out_ref[0] = x_ref[0] + gate_ref[0] * yn


def _swa_out_call(sinks, qb, kd, vdt, gb, oa, x, gate, g_post, wa, wb):
    B, S, D = x.shape
    tq = TQ_SWA
    nsub = tq // WINDOW
    cur = lambda b, i: (b, i, 0)
    prev = lambda b, i: (b, jnp.maximum(i * nsub - 1, 0), 0)
    cur_t = lambda b, i: (b, 0, i)
    prev_t = lambda b, i: (b, 0, jnp.maximum(i * nsub - 1, 0))
    const2 = lambda b, i: (0, 0)
    kvw = 2 * SWA_KV_W
    nprob = nsub * SWA_KV_HEADS
    ncol = SWA_Q_HEADS // SWA_KV_HEADS * WINDOW
    return pl.pallas_call(
        _swa_out_kernel,
        grid=(B, S // tq),
        in_specs=[pl.BlockSpec(memory_space=pltpu.SMEM),
                  pl.BlockSpec((1, tq, SWA_W), cur),
                  pl.BlockSpec((1, WINDOW, kvw), prev),
                  pl.BlockSpec((1, tq, kvw), cur),
                  pl.BlockSpec((1, kvw, WINDOW), prev_t),
                  pl.BlockSpec((1, kvw, tq), cur_t),
                  pl.BlockSpec((1, tq, SWA_W), cur),
                  pl.BlockSpec((1, tq, FOX_W), cur),
                  pl.BlockSpec((1, tq, D), cur),
                  pl.BlockSpec((1, 1, D), lambda b, i: (b, 0, 0)),
                  pl.BlockSpec((1, D), const2),
                  pl.BlockSpec((FOX_W, D), const2),
                  pl.BlockSpec((SWA_W, D), const2)],
        out_specs=pl.BlockSpec((1, tq, D), cur),
        out_shape=jax.ShapeDtypeStruct((B, S, D), F32),
        scratch_shapes=[pltpu.VMEM((nprob, 2 * WINDOW, ncol), F32),
                        pltpu.VMEM((nprob, 2 * WINDOW, ncol), BF16),
                        pltpu.VMEM((nprob, 1, ncol), F32),
                        pltpu.VMEM((tq, SWA_W), BF16)],
        compiler_params=pltpu.CompilerParams(
            dimension_semantics=("arbitrary", "arbitrary"), vmem_limit_bytes=VMEM_LIMIT),
        name="swa_out",
    )(sinks, qb, kd, kd, vdt, vdt, gb, oa, x, gate, g_post, wa, wb)


def _wprep_kernel(wt_ref, o_ref):
    sc2 = HEAD_DIM ** -0.5 * LOG2E
    src = {}
    o = 0
    for name, width in (("qa", FOX_W), ("ka", FOX_W), ("va", FOX_W), ("fa", FOX_HEADS),
                        ("za", FOX_W), ("qb", SWA_W), ("kb", SWA_KV_W), ("vb", SWA_KV_W),
                        ("zb", SWA_W)):
        src[name] = (o, width)
        o += width
    for name, dst, scale in (("qa", OFF_QA, sc2), ("ka", OFF_KA, None), ("va", OFF_VA, None),
                             ("za", OFF_ZA, None), ("qb", OFF_QB, sc2), ("zb", OFF_ZB, None),
                             ("kb", OFF_KD, None), ("vb", OFF_VD, None)):
        start, width = src[name]
        rows = wt_ref[start:start + width, :]
        o_ref[dst:dst + width, :] = (rows if scale is None else rows * scale).astype(BF16)
    fa = wt_ref[src["fa"][0]:src["fa"][0] + FOX_HEADS, :]
    o_ref[OFF_F:OFF_F + LANES, :] = jnp.concatenate(
        [fa, fa, fa, jnp.zeros((LANES - 3 * FOX_HEADS, fa.shape[1]), F32)], axis=0).astype(BF16)


def _perm_w_in(w):
    wt = w.T
    return pl.pallas_call(
        _wprep_kernel,
        out_shape=jax.ShapeDtypeStruct((IN_NP, wt.shape[1]), BF16),
        compiler_params=pltpu.CompilerParams(vmem_limit_bytes=VMEM_LIMIT),
        name="w_prep",
    )(wt)


def _aug_constants():
    eq = np.zeros((LANES, AUG_W), np.float32)
    ek = np.zeros((LANES, AUG_W), np.float32)
    oq = np.zeros((1, AUG_W), np.float32)
    ok = np.zeros((1, AUG_W), np.float32)
    for hd in range(FOX_HEADS):
        base = LANES * hd + (HEAD_DIM if hd % 2 == 0 else 0)
        for part in range(3):
            eq[part * FOX_HEADS + hd, base + part] = 1.0
            ok[0, base + part] = 1.0
            ek[part * FOX_HEADS + hd, base + 3 + part] = -1.0
            oq[0, base + 3 + part] = 1.0
    return jnp.asarray(eq + ek, BF16), jnp.asarray(oq), jnp.asarray(ok)


def kernel(x, c, positions, w_ada, b_ada, g_pre, w_in, b_fgate, sinks, w_out, g_post):
    B, S, D = x.shape
    depth = w_ada.shape[0]
    assert TM_SUB == TQ_FOX == TK_FOX
    half = HEAD_DIM // 2
    inv_freq = ROPE_THETA ** (-jnp.arange(half, dtype=F32) / half)
    invf = inv_freq[:, None]
    pos3 = positions[:, None, :]
    ind = np.zeros((2 * FOX_W, LANES), np.float32)
    ind[np.arange(2 * FOX_W), np.arange(2 * FOX_W) // HEAD_DIM] = 1.0
    ind = jnp.asarray(ind, BF16)
    tri = jnp.asarray(np.tril(np.ones((TM_SUB, TM_SUB), np.float32)), BF16)
    eqk, oq, ok = _aug_constants()
    c_pad = jnp.zeros((8, D), F32).at[:B].set(c)
    for l in range(depth):
        mod = _mod_call(c_pad, w_ada[l], b_ada[l][None, :])[:B]
        shift = mod[:, None, 0:D]
        scale = mod[:, None, D:2 * D]
        gate = mod[:, None, 2 * D:3 * D]
        bf_pad = jnp.concatenate(
            [b_fgate[l]] * 3 + [jnp.zeros((LANES - 3 * FOX_HEADS,), F32)])[None, :]
        qaug, kaug, vaug, ga, qb, kd, vd, gb, stats = _inproj_call(
            x, shift, scale, g_pre[l][None, :], _perm_w_in(w_in[l]), pos3, invf, bf_pad,
            tri, eqk, oq, ok, ind)
        oa = _fox_call(qaug, kaug, vaug, ga, stats[:, :, 0:3, 0:2 * FOX_HEADS])
        wo = w_out[l].astype(BF16)
        x = _swa_out_call(sinks[l], qb, kd, vd, gb, oa, x, gate, g_post[l][None, :],
                          wo[:FOX_W], wo[FOX_W:])
    return x
```

```python
import functools

import jax
import jax.numpy as jnp
import numpy as np
from jax import lax
from jax.experimental import pallas as pl
from jax.experimental.pallas import tpu as pltpu

D_MODEL = 1024
HEAD_DIM = 64
FOX_HEADS = 8
SWA_Q_HEADS = 8
SWA_KV_HEADS = 2
WINDOW = 128
ROPE_THETA = 10000.0
RMS_EPS = 1e-6
FOX_W = FOX_HEADS * HEAD_DIM
SWA_W = SWA_Q_HEADS * HEAD_DIM
SWA_KV_W = SWA_KV_HEADS * HEAD_DIM

LANES = 128
AUG_W = FOX_HEADS * LANES

OFF_QA = 0
OFF_KA = OFF_QA + FOX_W
OFF_VA = OFF_KA + FOX_W
OFF_ZA = OFF_VA + FOX_W
OFF_QB = OFF_ZA + FOX_W
OFF_ZB = OFF_QB + SWA_W
OFF_KD = OFF_ZB + SWA_W
OFF_VD = OFF_KD + SWA_KV_W
OFF_F = OFF_VD + SWA_KV_W
IN_NP = OFF_F + LANES

TM_IN = 1024
TM_SUB = 512
TQ_FOX = 512
TK_FOX = 512
FIXED_UNROLL = 8
DIAG_UNROLL = 4
TQ_SWA = 1024
NEG_BIG = -1e30
LOG2E = 1.4426950408889634
PRUNE_T = 140.0
NORM_MARGIN = 1.01
FIXED_REF_T = 100.0
VMEM_LIMIT = 56 * 1024 * 1024

F32 = jnp.float32
BF16 = jnp.bfloat16


def _split3(a):
    hi = a.astype(BF16)
    r = a - hi.astype(F32)
    mid = r.astype(BF16)
    lo = (r - mid.astype(F32)).astype(BF16)
    return hi, mid, lo


def _silu(z):
    hz = 0.5 * z
    return hz + hz * jnp.tanh(hz)


def _mod_kernel(c_ref, w_ref, b_ref, o_ref):
    c = c_ref[...]
    sc = c * (1.0 / (1.0 + jnp.exp(-c)))
    o_ref[...] = jnp.dot(sc, w_ref[...], precision=lax.Precision.HIGHEST,
                         preferred_element_type=F32) + b_ref[...]


def _mod_call(c_pad, w_ada, b_ada):
    rows = c_pad.shape[0]
    n = w_ada.shape[1]
    bn = D_MODEL
    return pl.pallas_call(
        _mod_kernel,
        grid=(n // bn,),
        in_specs=[pl.BlockSpec((rows, D_MODEL), lambda j: (0, 0)),
                  pl.BlockSpec((D_MODEL, bn), lambda j: (0, j)),
                  pl.BlockSpec((1, bn), lambda j: (0, j))],
        out_specs=pl.BlockSpec((rows, bn), lambda j: (0, j)),
        out_shape=jax.ShapeDtypeStruct((rows, n), F32),
        name="mod",
    )(c_pad, w_ada, b_ada)


def _inproj_kernel(x_ref, shift_ref, scale_ref, xn_ref, shiftn_ref, scalen_ref, gpre_ref, w_ref,
                   pos_ref, invf_ref, bf_ref, tri_ref, eqk_ref, oq_ref, ok_ref, ind_ref,
                   qaug_ref, kaug_ref, vaug_ref, ga_ref, qb_ref, kd_ref, vd_ref, gb_ref, st_ref,
                   carry_ref, h0_sc):
    tm = TM_SUB

    def prenorm(x, shift_r, scale_r):
        ms = jnp.mean(x * x, axis=-1, keepdims=True)
        gain = gpre_ref[...] * (1.0 + scale_r[0])
        return (x * lax.rsqrt(ms + RMS_EPS) * gain + shift_r[0]).astype(BF16)

    @pl.when(pl.program_id(1) == 0)
    def _():
        carry_ref[...] = jnp.zeros_like(carry_ref)

    @pl.when(jnp.logical_and(pl.program_id(0) == 0, pl.program_id(1) == 0))
    def _():
        h0_sc[...] = prenorm(x_ref[0, 0:tm, :], shift_ref, scale_ref)

    for sub in range(x_ref.shape[1] // tm):
        rows = slice(sub * tm, (sub + 1) * tm)
        h = h0_sc[...] if sub == 0 else prenorm(x_ref[0, rows, :], shift_ref, scale_ref)
        _inproj_subtile(sub, rows, h, w_ref, pos_ref, invf_ref, bf_ref,
                        tri_ref, eqk_ref, oq_ref, ok_ref, ind_ref,
                        qaug_ref, kaug_ref, vaug_ref, ga_ref, qb_ref, kd_ref, vd_ref, gb_ref,
                        st_ref, carry_ref)

    h0_sc[...] = prenorm(xn_ref[0], shiftn_ref, scalen_ref)


def _inproj_subtile(sub, rows, h, w_ref, pos_ref, invf_ref,
                    bf_ref, tri_ref, eqk_ref, oq_ref, ok_ref, ind_ref,
                    qaug_ref, kaug_ref, vaug_ref, ga_ref, qb_ref, kd_ref, vd_ref, gb_ref,
                    st_ref, carry_ref):
    tm = TM_SUB

    def proj(off, width):
        return lax.dot_general(h, w_ref[off:off + width, :], (((1,), (1,)), ((), ())),
                               preferred_element_type=F32)

    tail = proj(OFF_KD, 3 * LANES)
    f = tail[:, OFF_F - OFF_KD:] + bf_ref[...]
    ls = jnp.minimum(f, 0.0) - jnp.log1p(jnp.exp(-jnp.abs(f)))
    lane = lax.broadcasted_iota(jnp.int32, (tm, LANES), 1)

    def by_group(a, b_, c_):
        return jnp.where(lane < FOX_HEADS, a, jnp.where(lane < 2 * FOX_HEADS, b_, c_))

    part = by_group(*(t.astype(F32) for t in _split3(ls))).astype(BF16)
    psum = jnp.dot(tri_ref[...], part, preferred_element_type=F32)
    cum = (psum + pltpu.roll(psum, LANES - FOX_HEADS, 1)
           + pltpu.roll(psum, LANES - 2 * FOX_HEADS, 1))
    cum = cum + carry_ref[...]
    carry_ref[...] = cum[tm - 1:tm, :]

    cum2 = cum * LOG2E
    cum2 = by_group(cum2, pltpu.roll(cum2, FOX_HEADS, 1), pltpu.roll(cum2, 2 * FOX_HEADS, 1))
    cs = by_group(*(t.astype(F32) for t in _split3(cum2))).astype(BF16)
    placed = jnp.dot(cs, eqk_ref[...], preferred_element_type=F32)
    augq = placed * ok_ref[...] + oq_ref[...]
    augk = placed * oq_ref[...] + ok_ref[...]

    lane_w = lax.broadcasted_iota(jnp.int32, (tm, AUG_W), 1)
    data = (((lane_w >> 6) ^ (lane_w >> 7)) & 1) == 0

    def rep(a):
        return jnp.concatenate(
            [a[:, LANES * (hd // 2):LANES * (hd // 2 + 1)] for hd in range(FOX_HEADS)], axis=1)

    qa = proj(OFF_QA, FOX_W)
    ka = proj(OFF_KA, FOX_W)
    qaug_ref[0, :, rows] = jnp.where(data, rep(qa), augq).T.astype(BF16)
    kaug_ref[0, rows, :] = jnp.where(data, rep(ka), augk).astype(BF16)

    sq = jnp.concatenate([qa * qa, ka * ka], axis=1).astype(BF16)
    nrm2 = jnp.dot(sq, ind_ref[...], preferred_element_type=F32)
    nmax = jnp.sqrt(jnp.max(nrm2, axis=0, keepdims=True)) * NORM_MARGIN
    st_ref[0, sub] = jnp.concatenate(
        [nmax, cum2[0:1], cum2[tm - 1:tm], jnp.zeros((5, LANES), F32)], axis=0)
    vaug_ref[0, :, rows] = jnp.where(data, rep(proj(OFF_VA, FOX_W)), 1.0).T.astype(BF16)

    za = proj(OFF_ZA, FOX_W)
    ga_ref[0, rows, :] = _silu(za).astype(BF16)
    zb = proj(OFF_ZB, SWA_W)
    gb_ref[0, rows, :] = _silu(zb).astype(BF16)

    ang_t = invf_ref[...] * pos_ref[0, :, rows].astype(F32)
    reps = LANES // (HEAD_DIM // 2)
    cosv = jnp.concatenate([jnp.cos(ang_t)] * reps, axis=0).T
    sinv = jnp.concatenate([jnp.sin(ang_t)] * reps, axis=0).T
    first = (lane & (HEAD_DIM // 2)) == 0
    sin_signed = jnp.where(first, -sinv, sinv)

    def rope(a):
        outs = []
        for cidx in range(a.shape[1] // LANES):
            blk = a[:, cidx * LANES:(cidx + 1) * LANES]
            other = jnp.where(first, pltpu.roll(blk, LANES - HEAD_DIM // 2, 1),
                              pltpu.roll(blk, HEAD_DIM // 2, 1))
            outs.append(blk * cosv + other * sin_signed)
        return jnp.concatenate(outs, axis=1)

    qb_ref[0, rows, :] = rope(proj(OFF_QB, SWA_W)).astype(BF16)
    kr = rope(tail[:, :LANES])
    ks = pltpu.roll(kr, HEAD_DIM, 1)
    low = lane < HEAD_DIM
    kd_ref[0, rows, :] = jnp.concatenate([jnp.where(low, kr, ks), jnp.where(low, ks, kr)],
                                axis=1).astype(BF16)
    vt = tail[:, OFF_VD - OFF_KD:OFF_F - OFF_KD].T
    ones = jnp.ones((HEAD_DIM, tm), F32)
    vd_ref[0, :, rows] = jnp.concatenate([vt[:HEAD_DIM], ones, vt[HEAD_DIM:], ones],
                                axis=0).astype(BF16)


def _inproj_call(x, shift, scale, g_pre, w_perm, pos3, invf, bf_pad, tri, eqk, oq, ok, ind):
    B, S, D = x.shape
    tm = TM_IN
    row = lambda b, t: (b, t, 0)
    per_b = lambda b, t: (b, 0, 0)
    const2 = lambda b, t: (0, 0)
    out_w = (AUG_W, AUG_W, AUG_W, FOX_W, SWA_W, 2 * SWA_KV_W, 2 * SWA_KV_W, SWA_W)
    transposed = (0, 2, 6)
    nt = S // tm
    next_b = lambda b, t: jnp.where(t + 1 < nt, b, jnp.minimum(b + 1, B - 1))
    next_blk = lambda t: jnp.where(t + 1 < nt, (t + 1) * (tm // TM_SUB), 0)
    return pl.pallas_call(
        _inproj_kernel,
        grid=(B, S // tm),
        in_specs=[pl.BlockSpec((1, tm, D), row),
                  pl.BlockSpec((1, 1, D), per_b),
                  pl.BlockSpec((1, 1, D), per_b),
                  pl.BlockSpec((1, TM_SUB, D), lambda b, t: (next_b(b, t), next_blk(t), 0)),
                  pl.BlockSpec((1, 1, D), lambda b, t: (next_b(b, t), 0, 0)),
                  pl.BlockSpec((1, 1, D), lambda b, t: (next_b(b, t), 0, 0)),
                  pl.BlockSpec((1, D), const2),
                  pl.BlockSpec((IN_NP, D), const2),
                  pl.BlockSpec((1, 1, tm), lambda b, t: (b, 0, t)),
                  pl.BlockSpec((HEAD_DIM // 2, 1), const2),
                  pl.BlockSpec((1, LANES), const2),
                  pl.BlockSpec((TM_SUB, TM_SUB), const2),
                  pl.BlockSpec((LANES, AUG_W), const2),
                  pl.BlockSpec((1, AUG_W), const2),
                  pl.BlockSpec((1, AUG_W), const2),
                  pl.BlockSpec((2 * FOX_W, LANES), const2)],
        out_specs=[pl.BlockSpec((1, w, tm), lambda b, t: (b, 0, t)) if i in transposed
                   else pl.BlockSpec((1, tm, w), row) for i, w in enumerate(out_w)]
        + [pl.BlockSpec((1, tm // TM_SUB, 8, LANES), lambda b, t: (b, t, 0, 0))],
        out_shape=[jax.ShapeDtypeStruct((B, w, S) if i in transposed else (B, S, w), BF16)
                   for i, w in enumerate(out_w)]
        + [jax.ShapeDtypeStruct((B, S // TM_SUB, 8, LANES), F32)],
        scratch_shapes=[pltpu.VMEM((1, LANES), F32), pltpu.VMEM((TM_SUB, D), BF16)],
        compiler_params=pltpu.CompilerParams(
            dimension_semantics=("arbitrary", "arbitrary"), vmem_limit_bytes=VMEM_LIMIT),
        name="in_proj",
    )(x, shift, scale, x, shift, scale, g_pre, w_perm, pos3, invf, bf_pad, tri, eqk, oq, ok, ind)


def _pack_item(qrow, j, state, head):
    return qrow | (j << 8) | (state << 16) | (head << 24)


def _unpack_item(word):
    return word & 0xFF, (word >> 8) & 0xFF, (word >> 16) & 0xFF, word >> 24


def _fox_kernel(st_ref, q_ref, k_ref, v_ref, g_ref, o_ref,
                dfix_tab, dgen_tab, off_tab, gen_tab, r_tab,
                s0, s1, p0, p1, al0, al1, m_st, acc_st, *, n_off):
    tq = TQ_FOX
    tk = TK_FOX
    nq = q_ref.shape[2] // tq
    s_buf = (s0, s1)
    p_buf = (p0, p1)
    al_buf = (al0, al1)

    def run(tab, n_items, dummy, diag):
        half = (n_items + 1) // 2
        lens = (half, n_items - half)
        hk, hq = tk // 2, tq // 2

        def item(stream, t):
            return jnp.where(t < lens[stream], stream * half + t, dummy)

        def stage_qk(t, slot):
            for sm in range(2):
                qrow, j, _, head = _unpack_item(tab[item(sm, t)])
                qoff = pl.multiple_of(qrow * tq, tq)
                koff = pl.multiple_of(j * tk, tk)
                hoff = pl.multiple_of(head * LANES, LANES)
                qt = q_ref[0, pl.ds(hoff, LANES), pl.ds(qoff, tq)]
                k = k_ref[0, pl.ds(koff, tk), pl.ds(hoff, LANES)]
                if diag:
                    s_buf[slot][sm, :hk, :] = jnp.dot(k[:hk], qt, preferred_element_type=F32)
                    s_buf[slot][sm, hk:, hq:] = jnp.dot(k[hk:], qt[:, hq:],
                                                        preferred_element_type=F32)
                else:
                    s_buf[slot][sm] = jnp.dot(k, qt, preferred_element_type=F32)

        def stage_softmax(t, slot):
            for sm in range(2):
                st = _unpack_item(tab[item(sm, t)])[2]
                if diag:
                    kr = lax.broadcasted_iota(jnp.int32, (hk, tq), 0)
                    qc = lax.broadcasted_iota(jnp.int32, (hk, tq), 1)
                    top = jnp.where(kr <= qc, s_buf[slot][sm, :hk, :], NEG_BIG)
                    kr2 = lax.broadcasted_iota(jnp.int32, (tk - hk, tq - hq), 0)
                    qc2 = lax.broadcasted_iota(jnp.int32, (tk - hk, tq - hq), 1)
                    low = jnp.where(kr2 <= qc2, s_buf[slot][sm, hk:, hq:], NEG_BIG)
                    top_a, top_b = top[:, :hq], top[:, hq:]
                    m_a = jnp.max(top_a, axis=0, keepdims=True)
                    m_b = jnp.maximum(jnp.max(top_b, axis=0, keepdims=True),
                                      jnp.max(low, axis=0, keepdims=True))
                    m_next = jnp.concatenate([m_a, m_b], axis=1)
                    p_buf[slot][sm, :hk, :hq] = jnp.exp2(top_a - m_a).astype(BF16)
                    p_buf[slot][sm, :hk, hq:] = jnp.exp2(top_b - m_b).astype(BF16)
                    p_buf[slot][sm, hk:, hq:] = jnp.exp2(low - m_b).astype(BF16)
                else:
                    s = s_buf[slot][sm]
                    m_prev = m_st[st]
                    m_next = jnp.maximum(m_prev, jnp.max(s, axis=0, keepdims=True))
                    al_buf[slot][sm] = jnp.exp2(m_prev - m_next)
                    p_buf[slot][sm] = jnp.exp2(s - m_next).astype(BF16)
                m_st[st] = m_next

        def stage_pv(t, slot):
            for sm in range(2):
                _, j, st, head = _unpack_item(tab[item(sm, t)])
                koff = pl.multiple_of(j * tk, tk)
                hoff = pl.multiple_of(head * LANES, LANES)
                vt = v_ref[0, pl.ds(hoff, LANES), pl.ds(koff, tk)]
                if diag:
                    acc_st[st] = jnp.concatenate(
                        [jnp.dot(vt[:, :hk], p_buf[slot][sm, :hk, :hq],
                                 preferred_element_type=F32),
                         jnp.dot(vt, p_buf[slot][sm, :, hq:], preferred_element_type=F32)],
                        axis=1)
                else:
                    pv = jnp.dot(vt, p_buf[slot][sm], preferred_element_type=F32)
                    acc_st[st] = al_buf[slot][sm] * acc_st[st] + pv

        stage_qk(0, 0)
        stage_qk(1, 1)
        stage_softmax(0, 0)

        unroll = DIAG_UNROLL if diag else 2

        def body(u, carry):
            for d in range(unroll):
                t = unroll * u + 1 + d
                stage_qk(t + 1, d % 2)
                stage_softmax(t, (d + 1) % 2)
                stage_pv(t - 1, d % 2)
            return carry

        lax.fori_loop(0, (half + unroll - 1) // unroll, body, 0)

    def run_fixed(tab, n_items, dummy, diag):
        half = (n_items + 1) // 2
        lens = (half, n_items - half)
        hk, hq = tk // 2, tq // 2

        def item(stream, t):
            return jnp.where(t < lens[stream], stream * half + t, dummy)

        def stage_probs(t, slot):
            for sm in range(2):
                qrow, j, st, head = _unpack_item(tab[item(sm, t)])
                qoff = pl.multiple_of(qrow * tq, tq)
                koff = pl.multiple_of(j * tk, tk)
                hoff = pl.multiple_of(head * LANES, LANES)
                qt = q_ref[0, pl.ds(hoff, LANES), pl.ds(qoff, tq)]
                k = k_ref[0, pl.ds(koff, tk), pl.ds(hoff, LANES)]
                ref = r_tab[st]
                if diag:
                    kr = lax.broadcasted_iota(jnp.int32, (hk, tq), 0)
                    qc = lax.broadcasted_iota(jnp.int32, (hk, tq), 1)
                    top = jnp.where(kr <= qc, jnp.dot(k[:hk], qt, preferred_element_type=F32),
                                    NEG_BIG)
                    kr2 = lax.broadcasted_iota(jnp.int32, (tk - hk, tq - hq), 0)
                    qc2 = lax.broadcasted_iota(jnp.int32, (tk - hk, tq - hq), 1)
                    low = jnp.where(kr2 <= qc2,
                                    jnp.dot(k[hk:], qt[:, hq:], preferred_element_type=F32),
                                    NEG_BIG)
                    p_buf[slot][sm, :hk, :] = jnp.exp2(top - ref).astype(BF16)
                    p_buf[slot][sm, hk:, hq:] = jnp.exp2(low - ref).astype(BF16)
                else:
                    s = jnp.dot(k, qt, preferred_element_type=F32)
                    p_buf[slot][sm] = jnp.exp2(s - ref).astype(BF16)

        def stage_pv(t, slot):
            for sm in range(2):
                _, j, st, head = _unpack_item(tab[item(sm, t)])
                koff = pl.multiple_of(j * tk, tk)
                hoff = pl.multiple_of(head * LANES, LANES)
                vt = v_ref[0, pl.ds(hoff, LANES), pl.ds(koff, tk)]
                if diag:
                    acc_st[st] = jnp.concatenate(
                        [jnp.dot(vt[:, :hk], p_buf[slot][sm, :hk, :hq],
                                 preferred_element_type=F32),
                         jnp.dot(vt, p_buf[slot][sm, :, hq:], preferred_element_type=F32)],
                        axis=1)
                else:
                    acc_st[st] = acc_st[st] + jnp.dot(vt, p_buf[slot][sm],
                                                      preferred_element_type=F32)

        stage_probs(0, 0)

        def trips(first, unroll):
            def body(u, carry):
                for d in range(unroll):
                    t = first + unroll * u + 1 + d
                    stage_probs(t, (d + 1) % 2)
                    stage_pv(t - 1, d % 2)
                return carry
            return body

        long = FIXED_UNROLL
        n_long = half // long
        done = n_long * long
        lax.fori_loop(0, n_long, trips(0, long), 0)
        lax.fori_loop(0, (half - done + 1) // 2, trips(done, 2), 0)

    assert FIXED_UNROLL % 2 == 0 and DIAG_UNROLL % 2 == 0

    b = pl.program_id(0)
    pr = pl.program_id(1)
    counts = (jnp.int32(0),) * 4
    for hh in range(2):
        hd = 2 * pr + hh

        kmax = lax.fori_loop(
            0, nq, lambda j, m, hd=hd: jnp.maximum(m, st_ref[b, j, 0, FOX_HEADS + hd]),
            jnp.float32(0.0))

        def list_row(qi, counts, hh=hh, hd=hd, kmax=kmax):
            qn = st_ref[b, qi, 0, hd]
            spread = qn * (st_ref[b, qi, 0, FOX_HEADS + hd] + kmax)
            to_fixed = (spread <= FIXED_REF_T).astype(jnp.int32)
            base = spread + st_ref[b, qi, 1, hd]
            state = hh * nq + qi
            r_tab[state] = qn * kmax
            row_word = _pack_item(qi, 0, state, hh)
            n_df, n_dg = counts[0], counts[1]
            dfix_tab[n_df] = row_word | (qi << 8)
            dgen_tab[n_dg] = row_word | (qi << 8)

            def contributes(carry):
                j = carry[0]
                bound = base - st_ref[b, jnp.maximum(j, 0), 2, hd]
                return jnp.logical_and(j >= 0, bound > -PRUNE_T)

            def take(carry):
                j, n_f, n_g = carry
                word = row_word | (j << 8)
                off_tab[n_f] = word
                gen_tab[n_g] = word
                return j - 1, n_f + to_fixed, n_g + 1 - to_fixed

            below = lax.while_loop(contributes, take, (qi - 1, counts[2], counts[3]))[1:]
            return (n_df + to_fixed, n_dg + 1 - to_fixed) + tuple(below)

        counts = lax.fori_loop(0, nq, list_row, counts)
    n_dfix, n_dgen, n_fix, n_gen = counts
    for tab in (dfix_tab, dgen_tab, off_tab, gen_tab):
        tab[n_off] = _pack_item(1, 0, 2 * nq, 0)
    r_tab[2 * nq] = jnp.float32(0.0)

    run_fixed(dfix_tab, n_dfix, n_off, True)

    @pl.when(n_dgen > 0)
    def _():
        run(dgen_tab, n_dgen, n_off, True)

    run_fixed(off_tab, n_fix, n_off, False)

    @pl.when(n_gen > 0)
    def _():
        run(gen_tab, n_gen, n_off, False)

    def finish(qi, carry):
        a0 = acc_st[qi]
        a1 = acc_st[nq + qi]
        ot = jnp.concatenate([a0[:HEAD_DIM] / a0[HEAD_DIM:HEAD_DIM + 1],
                              a1[HEAD_DIM:] / a1[0:1]], axis=0)
        rows = pl.ds(pl.multiple_of(qi * tq, tq), tq)
        o_ref[0, rows, :] = (ot.T * g_ref[0, rows, :].astype(F32)).astype(BF16)
        return carry

    lax.fori_loop(0, nq, finish, 0, unroll=8)


def _fox_call(qaug_t, kaug, vaug_t, ga, stats):
    B, S, _ = kaug.shape
    tq, tk = TQ_FOX, TK_FOX
    assert tq == tk
    nq = S // tq
    n_off = nq * (nq - 1)
    pair = lambda b, p: (b, 0, p)
    return pl.pallas_call(
        functools.partial(_fox_kernel, n_off=n_off),
        grid=(B, FOX_HEADS // 2),
        in_specs=[pl.BlockSpec(memory_space=pltpu.SMEM),
                  pl.BlockSpec((1, 2 * LANES, S), lambda b, p: (b, p, 0)),
                  pl.BlockSpec((1, S, 2 * LANES), pair),
                  pl.BlockSpec((1, 2 * LANES, S), lambda b, p: (b, p, 0)),
                  pl.BlockSpec((1, S, LANES), pair)],
        out_specs=pl.BlockSpec((1, S, LANES), pair),
        out_shape=jax.ShapeDtypeStruct((B, S, FOX_W), BF16),
        scratch_shapes=[pltpu.SMEM((n_off + 1,), jnp.int32),
                        pltpu.SMEM((n_off + 1,), jnp.int32),
                        pltpu.SMEM((n_off + 1,), jnp.int32),
                        pltpu.SMEM((n_off + 1,), jnp.int32),
                        pltpu.SMEM((2 * nq + 1,), F32),
                        pltpu.VMEM((2, tk, tq), F32), pltpu.VMEM((2, tk, tq), F32),
                        pltpu.VMEM((2, tk, tq), BF16), pltpu.VMEM((2, tk, tq), BF16),
                        pltpu.VMEM((2, 1, tq), F32), pltpu.VMEM((2, 1, tq), F32),
                        pltpu.VMEM((2 * nq + 1, 1, tq), F32),
                        pltpu.VMEM((2 * nq + 1, LANES, tq), F32)],
        compiler_params=pltpu.CompilerParams(
            dimension_semantics=("arbitrary", "arbitrary"), vmem_limit_bytes=VMEM_LIMIT),
        name="fox",
    )(stats, qaug_t, kaug, vaug_t, ga)


def _swa_out_kernel(sinks_ref, q_ref, kp_ref, kc_ref, vtp_ref, vtc_ref, g_ref,
                    oa_ref, x_ref, gate_ref, gpost_ref, wa_ref, wb_ref, out_ref,
                    s_sc, p_sc, e_sc, ob_sc):
    i = pl.program_id(1)
    nsub = TQ_SWA // WINDOW
    group = SWA_Q_HEADS // SWA_KV_HEADS
    ncol = group * WINDOW
    kall = jnp.concatenate([kp_ref[0], kc_ref[0]], axis=0)
    vtall = jnp.concatenate([vtp_ref[0], vtc_ref[0]], axis=1)
    lane = lax.broadcasted_iota(jnp.int32, (WINDOW, LANES), 1)
    lo = lane < HEAD_DIM
    zero = jnp.zeros((WINDOW, LANES), BF16)
    kj_ = lax.broadcasted_iota(jnp.int32, (2 * WINDOW, ncol), 0)
    qi_ = lax.broadcasted_iota(jnp.int32, (2 * WINDOW, ncol), 1) & (WINDOW - 1)
    rel = qi_ + WINDOW - kj_
    band = (rel >= 0) & (rel < WINDOW)
    nt = (((1,), (1,)), ((), ()))
    probs = [(r, g) for r in range(nsub) for g in range(SWA_KV_HEADS)]

    def stage_qk(n):
        r, g = probs[n]
        rows = slice(r * WINDOW, (r + 1) * WINDOW)
        kg = kall[r * WINDOW:(r + 2) * WINDOW, g * LANES:(g + 1) * LANES]
        c0 = g * group * HEAD_DIM
        qp0 = q_ref[0, rows, c0:c0 + LANES]
        qp1 = q_ref[0, rows, c0 + LANES:c0 + 2 * LANES]
        qs = jnp.concatenate([jnp.where(lo, qp0, zero), jnp.where(lo, qp1, zero),
                              jnp.where(lo, zero, qp0), jnp.where(lo, zero, qp1)], axis=0)
        s_sc[n] = lax.dot_general(kg, qs, nt, preferred_element_type=F32)

    def stage_softmax(n):
        r, g = probs[n]
        valid = band & (kj_ >= jnp.where(i == 0, WINDOW, 0)) if r == 0 else band
        st = jnp.where(valid, s_sc[n], NEG_BIG)
        heads = (group * g, group * g + 2, group * g + 1, group * g + 3)
        sink = jnp.concatenate(
            [jnp.full((1, WINDOW), sinks_ref[hd] * LOG2E, F32) for hd in heads], axis=1)
        m = jnp.maximum(jnp.max(st, axis=0, keepdims=True), sink)
        p_sc[n] = jnp.exp2(st - m).astype(BF16)
        e_sc[n] = jnp.exp2(sink - m)

    def stage_pv(n):
        r, g = probs[n]
        rows = slice(r * WINDOW, (r + 1) * WINDOW)
        c0 = g * group * HEAD_DIM
        vtg = vtall[g * LANES:(g + 1) * LANES, r * WINDOW:(r + 2) * WINDOW]
        acc = jnp.dot(vtg, p_sc[n], preferred_element_type=F32)
        l = acc[HEAD_DIM:HEAD_DIM + 1] + e_sc[n]
        on = acc[:HEAD_DIM] * (1.0 / l)
        pair0 = jnp.concatenate([on[:, 0:WINDOW], on[:, 2 * WINDOW:3 * WINDOW]], axis=0).T
        pair1 = jnp.concatenate([on[:, WINDOW:2 * WINDOW], on[:, 3 * WINDOW:]], axis=0).T
        ob_sc[rows, c0:c0 + LANES] = (
            pair0 * g_ref[0, rows, c0:c0 + LANES].astype(F32)).astype(BF16)
        ob_sc[rows, c0 + LANES:c0 + 2 * LANES] = (
            pair1 * g_ref[0, rows, c0 + LANES:c0 + 2 * LANES].astype(F32)).astype(BF16)

    for t in range(len(probs) + 2):
        if t < len(probs):
            stage_qk(t)
        if 1 <= t <= len(probs):
            stage_softmax(t - 1)
        if t >= 2:
            stage_pv(t - 2)

    y = (jnp.dot(oa_ref[0], wa_ref[...], preferred_element_type=F32)
         + jnp.dot(ob_sc[...], wb_ref[...], preferred_element_type=F32))
    ms = jnp.mean(y * y, axis=-1, keepdims=True)
    yn = y * lax.rsqrt(ms + RMS_EPS) * gpost_ref[...]
    out_ref[0] = x_ref[0] + gate_ref[0] * yn


def _swa_out_call(sinks, qb, kd, vdt, gb, oa, x, gate, g_post, wa, wb):
    B, S, D = x.shape
    tq = TQ_SWA
    nsub = tq // WINDOW
    cur = lambda b, i: (b, i, 0)
    prev = lambda b, i: (b, jnp.maximum(i * nsub - 1, 0), 0)
    cur_t = lambda b, i: (b, 0, i)
    prev_t = lambda b, i: (b, 0, jnp.maximum(i * nsub - 1, 0))
    const2 = lambda b, i: (0, 0)
    kvw = 2 * SWA_KV_W
    nprob = nsub * SWA_KV_HEADS
    ncol = SWA_Q_HEADS // SWA_KV_HEADS * WINDOW
    return pl.pallas_call(
        _swa_out_kernel,
        grid=(B, S // tq),
        in_specs=[pl.BlockSpec(memory_space=pltpu.SMEM),
                  pl.BlockSpec((1, tq, SWA_W), cur),
                  pl.BlockSpec((1, WINDOW, kvw), prev),
                  pl.BlockSpec((1, tq, kvw), cur),
                  pl.BlockSpec((1, kvw, WINDOW), prev_t),
                  pl.BlockSpec((1, kvw, tq), cur_t),
                  pl.BlockSpec((1, tq, SWA_W), cur),
                  pl.BlockSpec((1, tq, FOX_W), cur),
                  pl.BlockSpec((1, tq, D), cur),
                  pl.BlockSpec((1, 1, D), lambda b, i: (b, 0, 0)),
                  pl.BlockSpec((1, D), const2),
                  pl.BlockSpec((FOX_W, D), const2),
                  pl.BlockSpec((SWA_W, D), const2)],
        out_specs=pl.BlockSpec((1, tq, D), cur),
        out_shape=jax.ShapeDtypeStruct((B, S, D), F32),
        scratch_shapes=[pltpu.VMEM((nprob, 2 * WINDOW, ncol), F32),
                        pltpu.VMEM((nprob, 2 * WINDOW, ncol), BF16),
                        pltpu.VMEM((nprob, 1, ncol), F32),
                        pltpu.VMEM((tq, SWA_W), BF16)],
        compiler_params=pltpu.CompilerParams(
            dimension_semantics=("arbitrary", "arbitrary"), vmem_limit_bytes=VMEM_LIMIT),
        name="swa_out",
    )(sinks, qb, kd, kd, vdt, vdt, gb, oa, x, gate, g_post, wa, wb)


def _wprep_kernel(wt_ref, o_ref):
    sc2 = HEAD_DIM ** -0.5 * LOG2E
    src = {}
    o = 0
    for name, width in (("qa", FOX_W), ("ka", FOX_W), ("va", FOX_W), ("fa", FOX_HEADS),
                        ("za", FOX_W), ("qb", SWA_W), ("kb", SWA_KV_W), ("vb", SWA_KV_W),
                        ("zb", SWA_W)):
        src[name] = (o, width)
        o += width
    for name, dst, scale in (("qa", OFF_QA, sc2), ("ka", OFF_KA, None), ("va", OFF_VA, None),
                             ("za", OFF_ZA, None), ("qb", OFF_QB, sc2), ("zb", OFF_ZB, None),
                             ("kb", OFF_KD, None), ("vb", OFF_VD, None)):
        start, width = src[name]
        rows = wt_ref[start:start + width, :]
        o_ref[dst:dst + width, :] = (rows if scale is None else rows * scale).astype(BF16)
    fa = wt_ref[src["fa"][0]:src["fa"][0] + FOX_HEADS, :]
    o_ref[OFF_F:OFF_F + LANES, :] = jnp.concatenate(
        [fa, fa, fa, jnp.zeros((LANES - 3 * FOX_HEADS, fa.shape[1]), F32)], axis=0).astype(BF16)


def _perm_w_in(w):
    wt = w.T
    return pl.pallas_call(
        _wprep_kernel,
        out_shape=jax.ShapeDtypeStruct((IN_NP, wt.shape[1]), BF16),
        compiler_params=pltpu.CompilerParams(vmem_limit_bytes=VMEM_LIMIT),
        name="w_prep",
    )(wt)


def _aug_constants():
    eq = np.zeros((LANES, AUG_W), np.float32)
    ek = np.zeros((LANES, AUG_W), np.float32)
    oq = np.zeros((1, AUG_W), np.float32)
    ok = np.zeros((1, AUG_W), np.float32)
    for hd in range(FOX_HEADS):
        base = LANES * hd + (HEAD_DIM if hd % 2 == 0 else 0)
        for part in range(3):
            eq[part * FOX_HEADS + hd, base + part] = 1.0
            ok[0, base + part] = 1.0
            ek[part * FOX_HEADS + hd, base + 3 + part] = -1.0
            oq[0, base + 3 + part] = 1.0
    return jnp.asarray(eq + ek, BF16), jnp.asarray(oq), jnp.asarray(ok)


def kernel(x, c, positions, w_ada, b_ada, g_pre, w_in, b_fgate, sinks, w_out, g_post):
    B, S, D = x.shape
    depth = w_ada.shape[0]
    assert TM_SUB == TQ_FOX == TK_FOX
    half = HEAD_DIM // 2
    inv_freq = ROPE_THETA ** (-jnp.arange(half, dtype=F32) / half)
    invf = inv_freq[:, None]
    pos3 = positions[:, None, :]
    ind = np.zeros((2 * FOX_W, LANES), np.float32)
    ind[np.arange(2 * FOX_W), np.arange(2 * FOX_W) // HEAD_DIM] = 1.0
    ind = jnp.asarray(ind, BF16)
    tri = jnp.asarray(np.tril(np.ones((TM_SUB, TM_SUB), np.float32)), BF16)
    eqk, oq, ok = _aug_constants()
    c_pad = jnp.zeros((8, D), F32).at[:B].set(c)
    for l in range(depth):
        mod = _mod_call(c_pad, w_ada[l], b_ada[l][None, :])[:B]
        shift = mod[:, None, 0:D]
        scale = mod[:, None, D:2 * D]
        gate = mod[:, None, 2 * D:3 * D]
        bf_pad = jnp.concatenate(
            [b_fgate[l]] * 3 + [jnp.zeros((LANES - 3 * FOX_HEADS,), F32)])[None, :]
        qaug, kaug, vaug, ga, qb, kd, vd, gb, stats = _inproj_call(
            x, shift, scale, g_pre[l][None, :], _perm_w_in(w_in[l]), pos3, invf, bf_pad,
            tri, eqk, oq, ok, ind)
        oa = _fox_call(qaug, kaug, vaug, ga, stats[:, :, 0:3, 0:2 * FOX_HEADS])
        wo = w_out[l].astype(BF16)
        x = _swa_out_call(sinks[l], qb, kd, vd, gb, oa, x, gate, g_post[l][None, :],
                          wo[:FOX_W], wo[FOX_W:])
    return x
```
